```python
import math
import jax, jax.numpy as jnp
from jax import lax
import numpy as np

D_MODEL = 1024
BATCH = 8
SEQ = 2048
DEPTH = 1

HEAD_DIM = 64
N_HEADS_A = 8
N_HEADS_B = 8
D_A = N_HEADS_A * HEAD_DIM
D_B = N_HEADS_B * HEAD_DIM
D_MIX = D_A + D_B
N_IN = 3 * D_A + 3 * D_B + N_HEADS_B
DILATED_PATTERNS = ((128, 1), (512, 4), (2048, 16))
ROPE_THETA = 500000.0
ROT_DIM = HEAD_DIM // 4
Q_BLOCK = 128
N_GROUPS = 4
EXPERTS_PER_GROUP = 8
N_EXPERTS = N_GROUPS * EXPERTS_PER_GROUP
TOP_K = 2
D_EXPERT = 512
PLE_DIM = 256
EPS = 1e-6
NEG = -1e30
FORGET_BIAS = 2.5

kernel_name = "hymba_dilated_fox_hmoe_block"


def rmsnorm(x, g):
    xf = x.astype(jnp.float32)
    y = xf * lax.rsqrt(jnp.mean(xf * xf, axis=-1, keepdims=True) + EPS)
    return (y * g.astype(jnp.float32)).astype(x.dtype)


def rope_tables(positions):
    pos = positions.astype(jnp.float32)
    inv = ROPE_THETA ** (-jnp.arange(0, ROT_DIM, 2, dtype=jnp.float32) / ROT_DIM)
    ang = pos[..., None] * inv
    return jnp.cos(ang)[:, :, None, :], jnp.sin(ang)[:, :, None, :]


def partial_rope(t, cos, sin):
    half = ROT_DIM // 2
    tf = t.astype(jnp.float32)
    t1, t2, rest = tf[..., :half], tf[..., half:ROT_DIM], tf[..., ROT_DIM:]
    out = jnp.concatenate([t1 * cos - t2 * sin, t2 * cos + t1 * sin, rest], axis=-1)
    return out.astype(t.dtype)


def dilated_window_attention(q, k, v, window, dilation):
    B, S, H, Dh = q.shape
    d = dilation
    W = window // d
    L = S // d
    nb = -(-L // W)
    pad = nb * W - L

    def to_blocks(a):
        a = a.reshape(B, L, d, H, Dh).transpose(0, 2, 1, 3, 4).reshape(B * d, L, H, Dh)
        a = jnp.pad(a, ((0, 0), (0, pad), (0, 0), (0, 0)))
        return a.reshape(B * d, nb, W, H, Dh)

    def with_prev(a):
        prev = jnp.pad(a, ((0, 0), (1, 0), (0, 0), (0, 0), (0, 0)))[:, :-1]
        return jnp.concatenate([prev, a], axis=2)

    qb, kb, vb = to_blocks(q), to_blocks(k), to_blocks(v)
    kc, vc = with_prev(kb), with_prev(vb)
    s = jnp.einsum('znqhd,znkhd->znhqk', qb, kc).astype(jnp.float32) / math.sqrt(Dh)
    qi = jnp.arange(W)[:, None]
    kj = jnp.arange(2 * W)[None, :]
    dist = qi + W - kj
    band = (dist >= 0) & (dist <= W)
    valid = (jnp.arange(nb)[:, None, None] > 0) | (kj[None] >= W)
    mask = band[None] & valid
    s = jnp.where(mask[None, :, None], s, NEG)
    m = jnp.max(s, axis=-1, keepdims=True)
    e = jnp.exp(s - m)
    den = jnp.sum(e, axis=-1)
    o = jnp.einsum('znhqk,znkhd->znqhd', e, vc.astype(jnp.float32))
    o = o / jnp.transpose(den, (0, 1, 3, 2))[..., None]
    lse = jnp.transpose(m[..., 0] + jnp.log(den), (0, 1, 3, 2))
    o = o.reshape(B * d, nb * W, H, Dh)[:, :L]
    o = o.reshape(B, d, L, H, Dh).transpose(0, 2, 1, 3, 4).reshape(B, S, H, Dh)
    lse = lse.reshape(B * d, nb * W, H)[:, :L]
    lse = lse.reshape(B, d, L, H).transpose(0, 2, 1, 3).reshape(B, S, H)
    return o, lse


def forgetting_attention(q, k, v, logf):
    B, S, H, Dh = q.shape
    c = jnp.cumsum(logf, axis=1)
    cT = jnp.transpose(c, (0, 2, 1))
    nq = S // Q_BLOCK
    qblk = q.reshape(B, nq, Q_BLOCK, H, Dh).transpose(1, 0, 2, 3, 4)
    cblk = cT.reshape(B, H, nq, Q_BLOCK).transpose(2, 0, 1, 3)
    starts = jnp.arange(nq, dtype=jnp.int32) * Q_BLOCK
    kpos = jnp.arange(S, dtype=jnp.int32)
    vf = v.astype(jnp.float32)
    scale = 1.0 / math.sqrt(Dh)

    def block(args):
        qi, ci, st = args
        s = jnp.einsum('bqhd,bkhd->bhqk', qi, k).astype(jnp.float32) * scale
        s = s + (ci[..., :, None] - cT[:, :, None, :])
        qpos = st + jnp.arange(Q_BLOCK, dtype=jnp.int32)
        s = jnp.where(kpos[None, :] <= qpos[:, None], s, NEG)
        pr = jax.nn.softmax(s, axis=-1)
        return jnp.einsum('bhqk,bkhd->bqhd', pr, vf)

    o = lax.map(block, (qblk, cblk, starts))
    return o.transpose(1, 0, 2, 3, 4).reshape(B, S, H, Dh)


def hierarchical_moe(m, w_rg, b_rg, w_re, b_re, w_up, w_down):
    lg = (jnp.einsum('bsd,dg->bsg', m, w_rg) + b_rg).astype(jnp.float32)
    pg = jax.nn.softmax(lg, axis=-1)
    gidx = jnp.argmax(lg, axis=-1)
    gw = jnp.take_along_axis(pg, gidx[..., None], axis=-1)[..., 0]
    le = (jnp.einsum('bsd,gde->bsge', m, w_re) + b_re).astype(jnp.float32)
    le_sel = jnp.take_along_axis(le, gidx[..., None, None], axis=2)[:, :, 0]
    top_v, top_i = lax.top_k(le_sel, TOP_K)
    tw = jax.nn.softmax(top_v, axis=-1) * gw[..., None]
    eid = gidx[..., None] * EXPERTS_PER_GROUP + top_i
    gate = jnp.sum(jax.nn.one_hot(eid, N_EXPERTS, dtype=jnp.float32) * tw[..., None], axis=-2)
    y = jnp.zeros(m.shape, jnp.float32)
    for e in range(N_EXPERTS):
        hu = jnp.einsum('bsd,df->bsf', m, w_up[e])
        hid = jax.nn.silu(hu[..., :D_EXPERT]) * hu[..., D_EXPERT:]
        y = y + gate[..., e:e + 1] * jnp.einsum('bsf,fd->bsd', hid, w_down[e]).astype(jnp.float32)
    return y.astype(m.dtype)


def setup_inputs(seed: int = 0) -> dict:
    key = jax.random.key(seed)
    ks = jax.random.split(key, 24)
    f32 = jnp.float32

    def nrm(k, shape, fan_in):
        return jax.random.normal(k, shape, f32) * (fan_in ** -0.5)

    def gain(k, shape):
        return 1.0 + 0.1 * jax.random.normal(k, shape, f32)

    x = jax.random.normal(ks[0], (BATCH, SEQ, D_MODEL), f32)
    p = jax.random.normal(ks[1], (DEPTH, BATCH, SEQ, PLE_DIM), f32)
    offset = jax.random.randint(ks[2], (BATCH, 1), 0, 4096, dtype=jnp.int32)
    positions = (offset + jnp.arange(SEQ, dtype=jnp.int32)[None, :]).astype(jnp.int32)
    return {
        "x": x,
        "p": p,
        "positions": positions,
        "g_mix": gain(ks[3], (DEPTH, D_MODEL)),
        "w_in": nrm(ks[4], (DEPTH, D_MODEL, N_IN), D_MODEL),
        "b_f": FORGET_BIAS + 0.1 * jax.random.normal(ks[5], (DEPTH, N_HEADS_B), f32),
        "qn_a": gain(ks[6], (DEPTH, HEAD_DIM)),
        "kn_a": gain(ks[7], (DEPTH, HEAD_DIM)),
        "qn_b": gain(ks[8], (DEPTH, HEAD_DIM)),
        "kn_b": gain(ks[9], (DEPTH, HEAD_DIM)),
        "w_o": nrm(ks[10], (DEPTH, D_MIX, D_MODEL), D_MIX),
        "g_ffn": gain(ks[11], (DEPTH, D_MODEL)),
        "w_rg": nrm(ks[12], (DEPTH, D_MODEL, N_GROUPS), D_MODEL),
        "b_rg": 0.01 * jax.random.normal(ks[13], (DEPTH, N_GROUPS), f32),
        "w_re": nrm(ks[14], (DEPTH, N_GROUPS, D_MODEL, EXPERTS_PER_GROUP), D_MODEL),
        "b_re": 0.01 * jax.random.normal(ks[15], (DEPTH, N_GROUPS, EXPERTS_PER_GROUP), f32),
        "w_up": nrm(ks[16], (DEPTH, N_EXPERTS, D_MODEL, 2 * D_EXPERT), D_MODEL),
        "w_down": nrm(ks[17], (DEPTH, N_EXPERTS, D_EXPERT, D_MODEL), D_EXPERT),
        "g_ple": gain(ks[18], (DEPTH, D_MODEL)),
        "w_ple_gate": nrm(ks[19], (DEPTH, D_MODEL, D_MODEL), D_MODEL),
        "w_ple_proj": nrm(ks[20], (DEPTH, PLE_DIM, D_MODEL), PLE_DIM),
    }


def reference(x, p, positions, g_mix, w_in, b_f, qn_a, kn_a, qn_b, kn_b, w_o,
              g_ffn, w_rg, b_rg, w_re, b_re, w_up, w_down, g_ple, w_ple_gate, w_ple_proj):
    B, S, _ = x.shape
    cos, sin = rope_tables(positions)
    h = x
    for i in range(DEPTH):
        a = rmsnorm(h, g_mix[i])
        proj = jnp.einsum('bsd,dn->bsn', a, w_in[i])
        o0 = 0
        parts = []
        for (width, heads) in ((D_A, N_HEADS_A),) * 3 + ((D_B, N_HEADS_B),) * 3:
            parts.append(proj[..., o0:o0 + width].reshape(B, S, heads, HEAD_DIM))
            o0 += width
        qa, ka, va, qb, kb, vb = parts
        f_logit = proj[..., o0:o0 + N_HEADS_B]

        qa = partial_rope(rmsnorm(qa, qn_a[i]), cos, sin)
        ka = partial_rope(rmsnorm(ka, kn_a[i]), cos, sin)
        outs, lses = [], []
        for (window, dilation) in DILATED_PATTERNS:
            o_p, l_p = dilated_window_attention(qa, ka, va, window, dilation)
            outs.append(o_p)
            lses.append(l_p)
        alpha = jax.nn.softmax(jnp.stack(lses, axis=0), axis=0)
        oa = jnp.sum(alpha[..., None] * jnp.stack(outs, axis=0), axis=0)

        qb = rmsnorm(qb, qn_b[i])
        kb = rmsnorm(kb, kn_b[i])
        logf = jax.nn.log_sigmoid(f_logit.astype(jnp.float32) + b_f[i].astype(jnp.float32))
        ob = forgetting_attention(qb, kb, vb, logf)

        mix = jnp.concatenate([oa, ob], axis=2).reshape(B, S, D_MIX).astype(h.dtype)
        h = h + jnp.einsum('bsm,md->bsd', mix, w_o[i])

        h = h + hierarchical_moe(rmsnorm(h, g_ffn[i]), w_rg[i], b_rg[i], w_re[i], b_re[i],
                                 w_up[i], w_down[i])

        gate = jax.nn.sigmoid(jnp.einsum('bsd,de->bse', rmsnorm(h, g_ple[i]), w_ple_gate[i]).astype(jnp.float32))
        ple = jnp.einsum('bsc,cd->bsd', p[i], w_ple_proj[i]).astype(jnp.float32)
        h = h + (gate * ple).astype(h.dtype)
    return h
```

```python
import functools
import math

import jax
import jax.numpy as jnp
from jax import lax
from jax.experimental import pallas as pl
from jax.experimental.pallas import tpu as pltpu

D_MODEL = 1024
HEAD_DIM = 64
N_HEADS = 8
D_SEC = N_HEADS * HEAD_DIM
N_QKV = 6 * D_SEC
ROT_DIM = HEAD_DIM // 4
ROPE_THETA = 500000.0
N_GROUPS = 4
EXPERTS_PER_GROUP = 8
N_EXPERTS = N_GROUPS * EXPERTS_PER_GROUP
D_EXPERT = 512
PLE_DIM = 256
EPS = 1e-6
NEG = -1e30
WINDOW = 128

LANES = 128
SUBLANES = 8
VMEM_LIMIT = 48 * 1024 * 1024

TM_PROJ = 512
TQ = 128
TK = 256
N_PAIRS = N_HEADS // 2
TM_MOE = 256
TM_DISPATCH = 512
TM_COMBINE = 256
ROUTER_LANE0 = N_GROUPS

F32 = jnp.float32
BF16 = jnp.bfloat16
NT_DIMS = (((1,), (1,)), ((), ()))


def _cparams(*sem):
    return pltpu.CompilerParams(dimension_semantics=sem, vmem_limit_bytes=VMEM_LIMIT)


def _in_proj_kernel(x_ref, g_ref, w_ref, wft_ref, qkv_ref, ft_ref):
    x = x_ref[...]
    ms = jnp.mean(x * x, axis=-1, keepdims=True)
    a = (x * lax.rsqrt(ms + EPS) * g_ref[...]).astype(BF16)
    qkv_ref[...] = jnp.dot(a, w_ref[...], preferred_element_type=F32).astype(BF16)
    ft_ref[...] = lax.dot_general(wft_ref[...], a, NT_DIMS, preferred_element_type=F32)


def _in_proj(x2, g_mix, w_qkv, wft):
    T = x2.shape[0]
    return pl.pallas_call(
        _in_proj_kernel,
        grid=(T // TM_PROJ,),
        in_specs=[
            pl.BlockSpec((TM_PROJ, D_MODEL), lambda i: (i, 0)),
            pl.BlockSpec((1, D_MODEL), lambda i: (0, 0)),
            pl.BlockSpec((D_MODEL, N_QKV), lambda i: (0, 0)),
            pl.BlockSpec((4 * SUBLANES, D_MODEL), lambda i: (0, 0)),
        ],
        out_specs=[
            pl.BlockSpec((TM_PROJ, N_QKV), lambda i: (i, 0)),
            pl.BlockSpec((4 * SUBLANES, TM_PROJ), lambda i: (0, i)),
        ],
        out_shape=[
            jax.ShapeDtypeStruct((T, N_QKV), BF16),
            jax.ShapeDtypeStruct((4 * SUBLANES, T), F32),
        ],
        compiler_params=_cparams("parallel"),
        name="in_proj",
    )(x2, g_mix, w_qkv, wft)


def _forget_scan_kernel(ft_ref, bf_ref, c_ref, *, seq):
    f = ft_ref[...] + bf_ref[...]
    c = jnp.minimum(f, 0.0) - jnp.log1p(jnp.exp(-jnp.abs(f)))
    lane = lax.broadcasted_iota(jnp.int32, c.shape, 1)
    k = 1
    while k < seq:
        c = c + jnp.where(lane >= k, pltpu.roll(c, k, axis=1), 0.0)
        k *= 2
    for pair in range(N_PAIRS):
        for j in range(seq // TK):
            c_ref[0, pair, j] = c[pair * SUBLANES:(pair + 1) * SUBLANES, j * TK:(j + 1) * TK]


def _forget_scan(ft, bft, batch, seq):
    return pl.pallas_call(
        functools.partial(_forget_scan_kernel, seq=seq),
        grid=(batch,),
        in_specs=[
            pl.BlockSpec((4 * SUBLANES, seq), lambda b: (0, b)),
            pl.BlockSpec((4 * SUBLANES, 1), lambda b: (0, 0)),
        ],
        out_specs=pl.BlockSpec((1, N_PAIRS, seq // TK, SUBLANES, TK), lambda b: (b, 0, 0, 0, 0)),
        out_shape=jax.ShapeDtypeStruct((batch, N_PAIRS, seq // TK, SUBLANES, TK), F32),
        compiler_params=_cparams("parallel"),
        name="forget_scan",
    )(ft, bft)


def _head_sumsq_matrix():
    r = lax.broadcasted_iota(jnp.int32, (LANES, LANES), 0) // HEAD_DIM
    c = lax.broadcasted_iota(jnp.int32, (LANES, LANES), 1) // HEAD_DIM
    return (r == c).astype(BF16)


def _qk_norm(x, gain, gmat):
    ss = jnp.dot((x * x).astype(BF16), gmat, preferred_element_type=F32)
    return x * lax.rsqrt(ss * (1.0 / HEAD_DIM) + EPS) * gain


def _stack_heads(qb, head0):
    zero = jnp.zeros_like(qb)
    return jnp.concatenate([jnp.where(head0, qb, zero), jnp.where(head0, zero, qb)], axis=0)


def _unstack(a, head0):
    return jnp.where(head0, a[:TQ], a[TQ:])


def _dilated_kernel(q_ref, k_ref, v_ref, cos_ref, sin_ref, gq_ref, gk_ref, o_ref,
                    qn_s, kn_s, v_s, acc_s, m_s, l_s, *, seq):
    gmat = _head_sumsq_matrix()
    lane = lax.broadcasted_iota(jnp.int32, (TQ, LANES), 1)
    head0 = lane < HEAD_DIM
    chunk = 512
    lane_c = lax.broadcasted_iota(jnp.int32, (chunk, LANES), 1) % HEAD_DIM
    first_half = lane_c < ROT_DIM // 2

    def rope(t, cs, sn):
        partner = jnp.where(first_half, pltpu.roll(t, LANES - ROT_DIM // 2, axis=1),
                            pltpu.roll(t, ROT_DIM // 2, axis=1))
        return t * cs + partner * sn

    for c0 in range(0, seq, chunk):
        rows = pl.ds(c0, chunk)
        cs = cos_ref[0, rows, :]
        sn = sin_ref[0, rows, :]
        qn = _qk_norm(q_ref[0, rows, :].astype(F32), gq_ref[...], gmat)
        qn_s[rows, :] = rope(qn, cs, sn) * (1.0 / math.sqrt(HEAD_DIM))
        kn = _qk_norm(k_ref[0, rows, :].astype(F32), gk_ref[...], gmat)
        kn_s[rows, :] = rope(kn, cs, sn)
        v_s[rows, :] = v_ref[0, rows, :].astype(F32)

    def block(p, q_rows, k_rows, nk, dist0):
        qst = _stack_heads(qn_s[q_rows, :].astype(BF16), head0)
        kb = kn_s[k_rows, :].astype(BF16)
        vb = v_s[k_rows, :].astype(BF16)
        s = lax.dot_general(qst, kb, NT_DIMS, preferred_element_type=F32)
        u = lax.broadcasted_iota(jnp.int32, (2 * TQ, nk), 0) % TQ
        c = lax.broadcasted_iota(jnp.int32, (2 * TQ, nk), 1)
        dist = dist0 + u - c
        s = jnp.where((dist >= 0) & (dist <= WINDOW), s, NEG)
        m = jnp.max(s, axis=-1, keepdims=True)
        pr = jnp.exp(s - m)
        l = jnp.sum(pr, axis=-1, keepdims=True)
        acc = jnp.dot(pr.astype(BF16), vb, preferred_element_type=F32)
        acc_s[p, q_rows, :] = _unstack(acc, head0)
        m_s[p, q_rows, :] = _unstack(jnp.broadcast_to(m, (2 * TQ, LANES)), head0)
        l_s[p, q_rows, :] = _unstack(jnp.broadcast_to(l, (2 * TQ, LANES)), head0)

    def p0(i, carry):
        kb0 = jnp.maximum(i - 1, 0)
        block(0, pl.ds(pl.multiple_of(i * TQ, TQ), TQ),
              pl.ds(pl.multiple_of(kb0 * TQ, TQ), 2 * TQ), 2 * TQ, (i - kb0) * TQ)
        return carry
    lax.fori_loop(0, seq // TQ, p0, 0)

    def p1(idx, carry):
        r = idx // 4
        n = idx % 4
        kb0 = jnp.maximum(n - 1, 0)
        block(1, pl.ds(4 * TQ * n + r, TQ, stride=4),
              pl.ds(4 * TQ * kb0 + r, 2 * TQ, stride=4), 2 * TQ, (n - kb0) * TQ)
        return carry
    lax.fori_loop(0, 16, p1, 0)

    def p2(r, carry):
        rows = pl.ds(r, TQ, stride=16)
        block(2, rows, rows, TQ, 0)
        return carry
    lax.fori_loop(0, 16, p2, 0)

    for c0 in range(0, seq, chunk):
        rows = pl.ds(c0, chunk)
        m0, m1, m2 = m_s[0, rows, :], m_s[1, rows, :], m_s[2, rows, :]
        mm = jnp.maximum(jnp.maximum(m0, m1), m2)
        e0, e1, e2 = jnp.exp(m0 - mm), jnp.exp(m1 - mm), jnp.exp(m2 - mm)
        num = acc_s[0, rows, :] * e0 + acc_s[1, rows, :] * e1 + acc_s[2, rows, :] * e2
        den = l_s[0, rows, :] * e0 + l_s[1, rows, :] * e1 + l_s[2, rows, :] * e2
        o_ref[0, rows, :] = (num / den).astype(BF16)


def _dilated_attn(qkv3, cos_t, sin_t, gq, gk):
    batch, seq, _ = qkv3.shape
    sec = D_SEC // LANES
    blk = lambda s: pl.BlockSpec((1, seq, LANES), lambda b, h, s=s: (b, 0, s * sec + h))
    tab = pl.BlockSpec((1, seq, LANES), lambda b, h: (b, 0, 0))
    vec = pl.BlockSpec((1, LANES), lambda b, h: (0, 0))
    return pl.pallas_call(
        functools.partial(_dilated_kernel, seq=seq),
        grid=(batch, N_PAIRS),
        in_specs=[blk(0), blk(1), blk(2), tab, tab, vec, vec],
        out_specs=pl.BlockSpec((1, seq, LANES), lambda b, h: (b, 0, h)),
        out_shape=jax.ShapeDtypeStruct((batch, seq, D_SEC), BF16),
        scratch_shapes=[
            pltpu.VMEM((seq, LANES), F32), pltpu.VMEM((seq, LANES), F32), pltpu.VMEM((seq, LANES), F32),
            pltpu.VMEM((3, seq, LANES), F32), pltpu.VMEM((3, seq, LANES), F32),
            pltpu.VMEM((3, seq, LANES), F32),
        ],
        compiler_params=_cparams("parallel", "parallel"),
        name="dilated_attn",
    )(qkv3, qkv3, qkv3, cos_t, sin_t, gq, gk)


def _fox_kernel(q_ref, k_ref, v_ref, c_ref, gq_ref, gk_ref, o_ref, qn_s, kn_s, *, seq):
    gmat = _head_sumsq_matrix()
    lane = lax.broadcasted_iota(jnp.int32, (TQ, LANES), 1)
    head0 = lane < HEAD_DIM
    chunk = 512
    for c0 in range(0, seq, chunk):
        rows = pl.ds(c0, chunk)
        qn = _qk_norm(q_ref[0, rows, :].astype(F32), gq_ref[...], gmat)
        qn_s[rows, :] = (qn * (1.0 / math.sqrt(HEAD_DIM))).astype(BF16)
        kn_s[rows, :] = _qk_norm(k_ref[0, rows, :].astype(F32), gk_ref[...], gmat).astype(BF16)

    u = lax.broadcasted_iota(jnp.int32, (2 * TQ, TK), 0) % TQ
    col = lax.broadcasted_iota(jnp.int32, (2 * TQ, TK), 1)
    top = lax.broadcasted_iota(jnp.int32, (2 * TQ, TK), 0) < TQ

    def q_block(qi, carry):
        q_rows = pl.ds(pl.multiple_of(qi * TQ, TQ), TQ)
        qst = _stack_heads(qn_s[q_rows, :], head0)

        def k_chunk(j, st):
            m, l, acc = st
            k_rows = pl.ds(pl.multiple_of(j * TK, TK), TK)
            s = lax.dot_general(qst, kn_s[k_rows, :], NT_DIMS, preferred_element_type=F32)
            cj = c_ref[0, 0, j]
            ck = jnp.where(top, jnp.broadcast_to(cj[0:1, :], (2 * TQ, TK)),
                           jnp.broadcast_to(cj[1:2, :], (2 * TQ, TK)))
            t = jnp.where(j * TK + col <= qi * TQ + u, s - ck, NEG)
            m_new = jnp.maximum(m, jnp.max(t, axis=-1, keepdims=True))
            alpha = jnp.exp(m - m_new)
            pr = jnp.exp(t - m_new)
            l = alpha * l + jnp.sum(pr, axis=-1, keepdims=True)
            acc = alpha * acc + jnp.dot(pr.astype(BF16), v_ref[0, k_rows, :], preferred_element_type=F32)
            return m_new, l, acc

        init = (jnp.full((2 * TQ, 1), NEG, F32), jnp.zeros((2 * TQ, 1), F32), jnp.zeros((2 * TQ, LANES), F32))
        n_chunks = (qi * TQ + TQ + TK - 1) // TK
        m, l, acc = lax.fori_loop(0, n_chunks, k_chunk, init)
        o_ref[0, q_rows, :] = _unstack(acc / l, head0).astype(BF16)
        return carry

    lax.fori_loop(0, seq // TQ, q_block, 0)


def _fox_attn(qkv3, c5, gq, gk):
    batch, seq, _ = qkv3.shape
    sec = D_SEC // LANES
    blk = lambda s: pl.BlockSpec((1, seq, LANES), lambda b, h, s=s: (b, 0, s * sec + h))
    vec = pl.BlockSpec((1, LANES), lambda b, h: (0, 0))
    return pl.pallas_call(
        functools.partial(_fox_kernel, seq=seq),
        grid=(batch, N_PAIRS),
        in_specs=[blk(3), blk(4), blk(5),
                  pl.BlockSpec((1, 1, seq // TK, SUBLANES, TK), lambda b, h: (b, h, 0, 0, 0)),
                  vec, vec],
        out_specs=pl.BlockSpec((1, seq, LANES), lambda b, h: (b, 0, h)),
        out_shape=jax.ShapeDtypeStruct((batch, seq, D_SEC), BF16),
        scratch_shapes=[pltpu.VMEM((seq, LANES), BF16), pltpu.VMEM((seq, LANES), BF16)],
        compiler_params=_cparams("parallel", "parallel"),
        name="fox_attn",
    )(qkv3, qkv3, qkv3, c5, gq, gk)


def _out_router_kernel(oa_ref, ob_ref, x_ref, wo_ref, g_ref, wr_ref, br_ref,
                       h1_ref, m_ref, route_ref, cnt_ref, carry_s):
    i = pl.program_id(0)
    tm = TM_PROJ

    @pl.when(i == 0)
    def _():
        carry_s[...] = jnp.zeros_like(carry_s)

    h1 = (x_ref[...]
          + jnp.dot(oa_ref[...], wo_ref[0:D_SEC, :], preferred_element_type=F32)
          + jnp.dot(ob_ref[...], wo_ref[D_SEC:2 * D_SEC, :], preferred_element_type=F32))
    h1_ref[...] = h1
    ms = jnp.mean(h1 * h1, axis=-1, keepdims=True)
    mn = h1 * lax.rsqrt(ms + EPS) * g_ref[...]
    m_ref[...] = mn

    logits = jnp.dot(mn.astype(BF16), wr_ref[...], preferred_element_type=F32) + br_ref[...]
    lane = lax.broadcasted_iota(jnp.int32, (tm, LANES), 1).astype(F32)
    big = float(LANES)

    def first_argmax(vals):
        vmax = jnp.max(vals, axis=-1, keepdims=True)
        idx = jnp.min(jnp.where(vals == vmax, lane, big), axis=-1, keepdims=True)
        return vmax, idx

    lg = jnp.where(lane < N_GROUPS, logits, -jnp.inf)
    gmax, gidx = first_argmax(lg)
    gw = 1.0 / jnp.sum(jnp.exp(lg - gmax), axis=-1, keepdims=True)
    lo = ROUTER_LANE0 + EXPERTS_PER_GROUP * gidx
    le = jnp.where((lane >= lo) & (lane < lo + EXPERTS_PER_GROUP), logits, -jnp.inf)
    v0, i0 = first_argmax(le)
    v1, i1 = first_argmax(jnp.where(lane == i0, -jnp.inf, le))
    ex = jnp.exp(v1 - v0)
    w0 = gw / (1.0 + ex)
    w1 = gw * ex / (1.0 + ex)

    sel0 = lane == i0
    sel1 = lane == i1
    onehot = jnp.where(sel0 | sel1, 1.0, 0.0)
    r = lax.broadcasted_iota(jnp.int32, (tm, tm), 0)
    c = lax.broadcasted_iota(jnp.int32, (tm, tm), 1)
    ltri = (c <= r).astype(BF16)
    incl = jnp.dot(ltri, onehot.astype(BF16), preferred_element_type=F32)
    excl = incl - onehot + carry_s[0:1, :]
    rank0 = jnp.sum(jnp.where(sel0, excl, 0.0), axis=-1, keepdims=True)
    rank1 = jnp.sum(jnp.where(sel1, excl, 0.0), axis=-1, keepdims=True)
    new_carry = carry_s[0:1, :] + incl[tm - 1:tm, :]
    carry_s[...] = jnp.broadcast_to(new_carry, carry_s.shape)
    cnt_ref[...] = jnp.broadcast_to(new_carry, cnt_ref.shape)

    rec = jnp.zeros((tm, LANES), F32)
    for pos, val in enumerate((i0 - ROUTER_LANE0, i1 - ROUTER_LANE0, rank0, rank1, w0, w1)):
        rec = jnp.where(lane == float(pos), val, rec)
    route_ref[...] = rec[:, 0:SUBLANES]


def _out_router(oa, ob, x2, w_o, g_ffn, w_r, b_r):
    T = x2.shape[0]
    tm = TM_PROJ
    row = lambda w: pl.BlockSpec((tm, w), lambda i: (i, 0))
    const = lambda a, b: pl.BlockSpec((a, b), lambda i: (0, 0))
    return pl.pallas_call(
        _out_router_kernel,
        grid=(T // tm,),
        in_specs=[row(D_SEC), row(D_SEC), row(D_MODEL), const(D_MODEL, D_MODEL), const(1, D_MODEL),
                  const(D_MODEL, LANES), const(1, LANES)],
        out_specs=[row(D_MODEL), row(D_MODEL), row(SUBLANES), const(SUBLANES, LANES)],
        out_shape=[
            jax.ShapeDtypeStruct((T, D_MODEL), F32),
            jax.ShapeDtypeStruct((T, D_MODEL), F32),
            jax.ShapeDtypeStruct((T, SUBLANES), F32),
            jax.ShapeDtypeStruct((SUBLANES, LANES), F32),
        ],
        scratch_shapes=[pltpu.VMEM((SUBLANES, LANES), F32)],
        compiler_params=_cparams("arbitrary"),
        name="out_router",
    )(oa, ob, x2, w_o, g_ffn, w_r, b_r)


def _row_copy(src, src_row, dst, dst_row, sem):
    return pltpu.make_async_copy(src.at[pl.ds(src_row, 1)], dst.at[pl.ds(dst_row, 1)], sem)


def _zero_fill(off_ref, cnt_ref, zero_s, xs_ref, zsem, wait):
    def tile_copy(t):
        copy = pltpu.make_async_copy(zero_s, xs_ref.at[pl.ds(pl.multiple_of(t * TM_MOE, TM_MOE), TM_MOE)], zsem)
        if wait:
            copy.wait()
        else:
            copy.start()

    def per_expert(e, carry):
        @pl.when(cnt_ref[e] % TM_MOE != 0)
        def _():
            tile_copy(off_ref[e + 1] // TM_MOE - 1)
        return carry
    lax.fori_loop(0, N_EXPERTS, per_expert, 0)

    def per_tile(t, carry):
        tile_copy(t)
        return carry
    lax.fori_loop(off_ref[N_EXPERTS] // TM_MOE, xs_ref.shape[0] // TM_MOE, per_tile, 0)


def _dispatch_kernel(p0_ref, p1_ref, off_ref, cnt_ref, m_ref, xs_ref, zero_s, sem, zsem):
    i = pl.program_id(0)
    tm = TM_DISPATCH

    @pl.when(i == 0)
    def _():
        zero_s[...] = jnp.zeros_like(zero_s)
        _zero_fill(off_ref, cnt_ref, zero_s, xs_ref, zsem, wait=False)
        _zero_fill(off_ref, cnt_ref, zero_s, xs_ref, zsem, wait=True)

    base = i * tm

    def start(t, carry):
        g = base + t
        _row_copy(m_ref, t, xs_ref, p0_ref[g], sem).start()
        _row_copy(m_ref, t, xs_ref, p1_ref[g], sem).start()
        return carry
    lax.fori_loop(0, tm, start, 0)

    def wait(t, carry):
        _row_copy(m_ref, t, xs_ref, 0, sem).wait()
        _row_copy(m_ref, t, xs_ref, 0, sem).wait()
        return carry
    lax.fori_loop(0, tm, wait, 0)


def _dispatch(pos0, pos1, off, cnt, mn, n_rows):
    T = mn.shape[0]
    grid_spec = pltpu.PrefetchScalarGridSpec(
        num_scalar_prefetch=4,
        grid=(T // TM_DISPATCH,),
        in_specs=[pl.BlockSpec((TM_DISPATCH, D_MODEL), lambda i, *_: (i, 0))],
        out_specs=pl.BlockSpec(memory_space=pl.ANY),
        scratch_shapes=[pltpu.VMEM((TM_MOE, D_MODEL), F32), pltpu.SemaphoreType.DMA, pltpu.SemaphoreType.DMA],
    )
    return pl.pallas_call(
        _dispatch_kernel,
        grid_spec=grid_spec,
        out_shape=jax.ShapeDtypeStruct((n_rows, D_MODEL), F32),
        compiler_params=_cparams("arbitrary"),
        name="dispatch",
    )(pos0, pos1, off, cnt, mn)


def _moe_kernel(tile_e_ref, nused_ref, xs_ref, wup_ref, wdn_ref, ys_ref, wup_s, wdn_s):
    i = pl.program_id(0)

    @pl.when(i < nused_ref[0])
    def _():
        prev = tile_e_ref[jnp.maximum(i - 1, 0)]

        @pl.when((i == 0) | (tile_e_ref[i] != prev))
        def _():
            wup_s[...] = wup_ref[0].astype(BF16)
            wdn_s[...] = wdn_ref[0].astype(BF16)

        hu = jnp.dot(xs_ref[...].astype(BF16), wup_s[...], preferred_element_type=F32)
        gate = hu[:, :D_EXPERT]
        hid = gate * (1.0 / (1.0 + jnp.exp(-gate))) * hu[:, D_EXPERT:]
        ys_ref[...] = jnp.dot(hid.astype(BF16), wdn_s[...], preferred_element_type=F32)

    @pl.when(i >= nused_ref[0])
    def _():
        ys_ref[...] = jnp.zeros_like(ys_ref)


def _moe_experts(tile_e, nused, xs, w_up, w_down):
    n_rows = xs.shape[0]
    n_tiles = n_rows // TM_MOE
    row_map = lambda i, te, nu: (jnp.minimum(i, nu[0] - 1), 0)
    grid_spec = pltpu.PrefetchScalarGridSpec(
        num_scalar_prefetch=2,
        grid=(n_tiles,),
        in_specs=[
            pl.BlockSpec((TM_MOE, D_MODEL), row_map),
            pl.BlockSpec((1, D_MODEL, 2 * D_EXPERT), lambda i, te, nu: (te[i], 0, 0)),
            pl.BlockSpec((1, D_EXPERT, D_MODEL), lambda i, te, nu: (te[i], 0, 0)),
        ],
        out_specs=pl.BlockSpec((TM_MOE, D_MODEL), lambda i, te, nu: (i, 0)),
        scratch_shapes=[pltpu.VMEM((D_MODEL, 2 * D_EXPERT), BF16), pltpu.VMEM((D_EXPERT, D_MODEL), BF16)],
    )
    return pl.pallas_call(
        _moe_kernel,
        grid_spec=grid_spec,
        out_shape=jax.ShapeDtypeStruct((n_rows, D_MODEL), F32),
        compiler_params=_cparams("arbitrary"),
        name="moe_experts",
    )(tile_e, nused, xs, w_up, w_down)


def _combine_kernel(p0_ref, p1_ref, ys_ref, h1_ref, route_ref, p_ref,
                    g_ref, wg_ref, wp_ref, o_ref, gath_s, sem):
    i = pl.program_id(0)
    tm = TM_COMBINE
    base = i * tm

    def start(t, carry):
        g = base + t
        _row_copy(ys_ref, p0_ref[g], gath_s.at[0], t, sem).start()
        _row_copy(ys_ref, p1_ref[g], gath_s.at[1], t, sem).start()
        return carry
    lax.fori_loop(0, tm, start, 0)

    def wait(t, carry):
        _row_copy(ys_ref, 0, gath_s.at[0], t, sem).wait()
        _row_copy(ys_ref, 0, gath_s.at[1], t, sem).wait()
        return carry
    lax.fori_loop(0, tm, wait, 0)

    route = route_ref[...]
    y = route[:, 4:5] * gath_s[0] + route[:, 5:6] * gath_s[1]
    h2 = h1_ref[...] + y
    ms = jnp.mean(h2 * h2, axis=-1, keepdims=True)
    n = (h2 * lax.rsqrt(ms + EPS) * g_ref[...]).astype(BF16)
    z = jnp.dot(n, wg_ref[...], preferred_element_type=F32)
    gate = 1.0 / (1.0 + jnp.exp(-z))
    ple = jnp.dot(p_ref[...].astype(BF16), wp_ref[...], preferred_element_type=F32)
    o_ref[...] = h2 + gate * ple


def _combine_ple(pos0, pos1, ys, h1, route, p2, g_ple, w_gate, w_proj):
    T = h1.shape[0]
    tm = TM_COMBINE
    row = lambda w: pl.BlockSpec((tm, w), lambda i, *_: (i, 0))
    const = lambda a, b: pl.BlockSpec((a, b), lambda i, *_: (0, 0))
    grid_spec = pltpu.PrefetchScalarGridSpec(
        num_scalar_prefetch=2,
        grid=(T // tm,),
        in_specs=[pl.BlockSpec(memory_space=pl.ANY), row(D_MODEL), row(SUBLANES), row(PLE_DIM),
                  const(1, D_MODEL), const(D_MODEL, D_MODEL), const(PLE_DIM, D_MODEL)],
        out_specs=row(D_MODEL),
        scratch_shapes=[pltpu.VMEM((2, tm, D_MODEL), F32), pltpu.SemaphoreType.DMA],
    )
    return pl.pallas_call(
        _combine_kernel,
        grid_spec=grid_spec,
        out_shape=jax.ShapeDtypeStruct((T, D_MODEL), F32),
        compiler_params=_cparams("arbitrary"),
        name="combine_ple",
    )(pos0, pos1, ys, h1, route, p2, g_ple, w_gate, w_proj)


def _rope_tables(positions):
    half = ROT_DIM // 2
    inv = ROPE_THETA ** (-jnp.arange(0, ROT_DIM, 2, dtype=F32) / ROT_DIM)
    ang = positions.astype(F32)[..., None] * inv
    cos, sin = jnp.cos(ang), jnp.sin(ang)
    ones = jnp.ones(ang.shape[:-1] + (HEAD_DIM - ROT_DIM,), F32)
    cos_h = jnp.concatenate([cos, cos, ones], axis=-1)
    sin_h = jnp.concatenate([-sin, sin, 0.0 * ones], axis=-1)
    return jnp.tile(cos_h, (1, 1, 2)), jnp.tile(sin_h, (1, 1, 2))


def _layer(i, h, p, cos_t, sin_t, g_mix, w_in, b_f, qn_a, kn_a, qn_b, kn_b, w_o, g_ffn, w_rg, b_rg,
           w_re, b_re, w_up, w_down, g_ple, w_ple_gate, w_ple_proj):
    B, S, _ = h.shape
    T = B * S
    x2 = h.reshape(T, D_MODEL)

    w_qkv = w_in[i][:, :N_QKV].astype(BF16)
    wf = w_in[i][:, N_QKV:]
    wft = jnp.zeros((N_PAIRS, SUBLANES, D_MODEL), F32).at[:, :2, :].set(wf.T.reshape(N_PAIRS, 2, D_MODEL))
    wft = wft.reshape(N_PAIRS * SUBLANES, D_MODEL).astype(BF16)
    bft = jnp.zeros((N_PAIRS, SUBLANES), F32).at[:, :2].set(b_f[i].reshape(N_PAIRS, 2)).reshape(-1, 1)
    pair = lambda g: jnp.tile(g, 2).reshape(1, LANES)
    w_r = jnp.zeros((D_MODEL, LANES), F32)
    w_r = w_r.at[:, :N_GROUPS].set(w_rg[i])
    w_r = w_r.at[:, ROUTER_LANE0:ROUTER_LANE0 + N_EXPERTS].set(
        jnp.transpose(w_re[i], (1, 0, 2)).reshape(D_MODEL, N_EXPERTS))
    b_r = jnp.zeros((1, LANES), F32).at[0, :N_GROUPS].set(b_rg[i])
    b_r = b_r.at[0, ROUTER_LANE0:ROUTER_LANE0 + N_EXPERTS].set(b_re[i].reshape(-1))

    qkv, ft = _in_proj(x2, g_mix[i].reshape(1, -1), w_qkv, wft)
    c5 = _forget_scan(ft, bft, B, S)
    qkv3 = qkv.reshape(B, S, N_QKV)
    oa = _dilated_attn(qkv3, cos_t, sin_t, pair(qn_a[i]), pair(kn_a[i]))
    ob = _fox_attn(qkv3, c5, pair(qn_b[i]), pair(kn_b[i]))

    h1, mn, route, cnt = _out_router(oa.reshape(T, D_SEC), ob.reshape(T, D_SEC), x2, w_o[i].astype(BF16),
                                     g_ffn[i].reshape(1, -1), w_r.astype(BF16), b_r)

    counts = cnt[0, ROUTER_LANE0:ROUTER_LANE0 + N_EXPERTS].astype(jnp.int32)
    tiles_per_e = (counts + TM_MOE - 1) // TM_MOE
    tile_end = jnp.cumsum(tiles_per_e)
    off = jnp.concatenate([jnp.zeros((1,), jnp.int32), tile_end * TM_MOE]).astype(jnp.int32)
    n_tiles = (2 * T) // TM_MOE + N_EXPERTS
    nused = tile_end[-1:].astype(jnp.int32)
    tile_ids = jnp.minimum(jnp.arange(n_tiles, dtype=jnp.int32), nused[0] - 1)
    tile_e = jnp.sum((tile_ids[:, None] >= tile_end[None, :]).astype(jnp.int32), axis=1).astype(jnp.int32)
    e0, e1, r0, r1 = (route[:, j].astype(jnp.int32) for j in range(4))
    pos0 = jnp.take(off, e0) + r0
    pos1 = jnp.take(off, e1) + r1

    xs = _dispatch(pos0, pos1, off, counts, mn, n_tiles * TM_MOE)
    ys = _moe_experts(tile_e, nused, xs, w_up[i], w_down[i])
    out = _combine_ple(pos0, pos1, ys, h1, route, p[i].reshape(T, PLE_DIM),
                       g_ple[i].reshape(1, -1), w_ple_gate[i].astype(BF16), w_ple_proj[i].astype(BF16))
    return out.reshape(B, S, D_MODEL)


def kernel(x, p, positions, g_mix, w_in, b_f, qn_a, kn_a, qn_b, kn_b, w_o, g_ffn, w_rg, b_rg, w_re, b_re,
           w_up, w_down, g_ple, w_ple_gate, w_ple_proj):
    cos_t, sin_t = _rope_tables(positions)
    h = x
    for i in range(p.shape[0]):
        h = _layer(i, h, p, cos_t, sin_t, g_mix, w_in, b_f, qn_a, kn_a, qn_b, kn_b, w_o, g_ffn, w_rg, b_rg,
                   w_re, b_re, w_up, w_down, g_ple, w_ple_gate, w_ple_proj)
    return h
```

```python
import functools
import math

import jax
import jax.numpy as jnp
from jax import lax
from jax.experimental import pallas as pl
from jax.experimental.pallas import tpu as pltpu

D_MODEL = 1024
HEAD_DIM = 64
N_HEADS = 8
D_SEC = N_HEADS * HEAD_DIM
N_QKV = 6 * D_SEC
ROT_DIM = HEAD_DIM // 4
ROPE_THETA = 500000.0
N_GROUPS = 4
EXPERTS_PER_GROUP = 8
N_EXPERTS = N_GROUPS * EXPERTS_PER_GROUP
D_EXPERT = 512
PLE_DIM = 256
EPS = 1e-6
NEG = -1e30
WINDOW = 128

LANES = 128
SUBLANES = 8
VMEM_LIMIT = 48 * 1024 * 1024

TM_PROJ = 512
TQ = 128
TK = 256
N_PAIRS = N_HEADS // 2
TM_MOE = 256
TM_DISPATCH = 512
TM_COMBINE = 256
ROUTER_LANE0 = N_GROUPS

Q_SCALE_LOG2 = math.log2(math.e) / math.sqrt(HEAD_DIM)

F32 = jnp.float32
BF16 = jnp.bfloat16
NT_DIMS = (((1,), (1,)), ((), ()))


def _cparams(*sem):
    return pltpu.CompilerParams(dimension_semantics=sem, vmem_limit_bytes=VMEM_LIMIT)


def _in_proj_kernel(x_ref, g_ref, w_ref, wf_ref, qkv_ref, f_ref):
    x = x_ref[...]
    ms = jnp.mean(x * x, axis=-1, keepdims=True)
    a = (x * lax.rsqrt(ms + EPS) * g_ref[...]).astype(BF16)
    qkv_ref[...] = jnp.dot(a, w_ref[...], preferred_element_type=F32).astype(BF16)
    f_ref[...] = jnp.dot(a, wf_ref[...], preferred_element_type=F32)


def _in_proj(x2, g_mix, w_qkv, wf):
    T = x2.shape[0]
    return pl.pallas_call(
        _in_proj_kernel,
        grid=(T // TM_PROJ,),
        in_specs=[
            pl.BlockSpec((TM_PROJ, D_MODEL), lambda i: (i, 0)),
            pl.BlockSpec((1, D_MODEL), lambda i: (0, 0)),
            pl.BlockSpec((D_MODEL, N_QKV), lambda i: (0, 0)),
            pl.BlockSpec((D_MODEL, LANES), lambda i: (0, 0)),
        ],
        out_specs=[
            pl.BlockSpec((TM_PROJ, N_QKV), lambda i: (i, 0)),
            pl.BlockSpec((TM_PROJ, LANES), lambda i: (i, 0)),
        ],
        out_shape=[
            jax.ShapeDtypeStruct((T, N_QKV), BF16),
            jax.ShapeDtypeStruct((T, LANES), F32),
        ],
        compiler_params=_cparams("parallel"),
        name="in_proj",
    )(x2, g_mix, w_qkv, wf)


BIAS_TERMS = 3


def _bias_lane(head):
    return (head // 2) * LANES + (HEAD_DIM if head % 2 == 0 else 0)


def _forget_scan_kernel(f_ref, bf_ref, kb_ref, *, seq):
    f = f_ref[...] + bf_ref[...]
    c = jnp.minimum(f, 0.0) - jnp.log1p(jnp.exp(-jnp.abs(f)))
    row = lax.broadcasted_iota(jnp.int32, c.shape, 0)
    k = 1
    while k < seq:
        c = c + jnp.where(row >= k, pltpu.roll(c, k, axis=0), 0.0)
        k *= 2
    rest = c * (-math.log2(math.e))
    r_idx = lax.broadcasted_iota(jnp.int32, (LANES, D_SEC), 0)
    c_idx = lax.broadcasted_iota(jnp.int32, (LANES, D_SEC), 1)
    base = (r_idx // 2) * LANES + jnp.where(r_idx % 2 == 0, HEAD_DIM, 0)
    out = jnp.zeros((seq, D_SEC), F32)
    for t in range(BIAS_TERMS):
        term = rest.astype(BF16)
        rest = rest - term.astype(F32)
        place = ((r_idx < N_HEADS) & (c_idx == base + t)).astype(BF16)
        out = out + jnp.dot(term, place, preferred_element_type=F32)
    kb_ref[0] = out.astype(BF16)


def _forget_scan(f, bf, batch, seq):
    return pl.pallas_call(
        functools.partial(_forget_scan_kernel, seq=seq),
        grid=(batch,),
        in_specs=[
            pl.BlockSpec((seq, LANES), lambda b: (b, 0)),
            pl.BlockSpec((1, LANES), lambda b: (0, 0)),
        ],
        out_specs=pl.BlockSpec((1, seq, D_SEC), lambda b: (b, 0, 0)),
        out_shape=jax.ShapeDtypeStruct((batch, seq, D_SEC), BF16),
        compiler_params=_cparams("parallel"),
        name="forget_scan",
    )(f, bf)


def _head_sumsq_matrix():
    r = lax.broadcasted_iota(jnp.int32, (LANES, LANES), 0) // HEAD_DIM
    c = lax.broadcasted_iota(jnp.int32, (LANES, LANES), 1) // HEAD_DIM
    return (r == c).astype(BF16)


def _qk_norm(x, gain, gmat):
    ss = jnp.dot((x * x).astype(BF16), gmat, preferred_element_type=F32)
    return x * lax.rsqrt(ss * (1.0 / HEAD_DIM) + EPS) * gain


def _stack_heads(qb, head0):
    zero = jnp.zeros_like(qb)
    return jnp.concatenate([jnp.where(head0, qb, zero), jnp.where(head0, zero, qb)], axis=0)


def _unstack(a, head0):
    return jnp.where(head0, a[:TQ], a[TQ:])


def _dilated_kernel(q_ref, k_ref, v_ref, cos_ref, sin_ref, gq_ref, gk_ref, o_ref,
                    qn_s, kn_s, v_s, acc_s, m_s, l_s, bias_s, *, seq):
    gmat = _head_sumsq_matrix()
    lane = lax.broadcasted_iota(jnp.int32, (TQ, LANES), 1)
    head0 = lane < HEAD_DIM
    chunk = 512
    lane_c = lax.broadcasted_iota(jnp.int32, (chunk, LANES), 1) % HEAD_DIM
    first_half = lane_c < ROT_DIM // 2

    def rope(t, cs, sn):
        partner = jnp.where(first_half, pltpu.roll(t, LANES - ROT_DIM // 2, axis=1),
                            pltpu.roll(t, ROT_DIM // 2, axis=1))
        return t * cs + partner * sn

    for c0 in range(0, seq, chunk):
        rows = pl.ds(c0, chunk)
        cs = cos_ref[0, rows, :]
        sn = sin_ref[0, rows, :]
        qn = _qk_norm(q_ref[0, rows, :].astype(F32), gq_ref[...], gmat)
        qn_s[rows, :] = rope(qn, cs, sn) * Q_SCALE_LOG2
        kn = _qk_norm(k_ref[0, rows, :].astype(F32), gk_ref[...], gmat)
        kn_s[rows, :] = rope(kn, cs, sn)
        v_s[rows, :] = v_ref[0, rows, :].astype(F32)

    u = lax.broadcasted_iota(jnp.int32, (2 * TQ, 2 * TQ), 0) % TQ
    c = lax.broadcasted_iota(jnp.int32, (2 * TQ, 2 * TQ), 1)
    for slot, dist0 in enumerate((0, TQ)):
        dist = dist0 + u - c
        bias_s[slot] = jnp.where((dist >= 0) & (dist <= WINDOW), 0.0, NEG)

    def scores(blk):
        p, q_rows, k_rows, nk, dist0 = blk
        qst = _stack_heads(qn_s[q_rows, :].astype(BF16), head0)
        s = lax.dot_general(qst, kn_s[k_rows, :].astype(BF16), NT_DIMS, preferred_element_type=F32)
        return s + bias_s[dist0 // TQ, :, 0:nk]

    def finish(blk, s):
        p, q_rows, k_rows, nk, dist0 = blk
        m = jnp.max(s, axis=-1, keepdims=True)
        pr = jnp.exp2(s - m)
        l = jnp.sum(pr, axis=-1, keepdims=True)
        acc = jnp.dot(pr.astype(BF16), v_s[k_rows, :].astype(BF16), preferred_element_type=F32)
        acc_s[p, q_rows, :] = _unstack(acc, head0)
        m_s[p, q_rows, :] = _unstack(jnp.broadcast_to(m, (2 * TQ, LANES)), head0)
        l_s[p, q_rows, :] = _unstack(jnp.broadcast_to(l, (2 * TQ, LANES)), head0)

    blocks = []
    for i in range(seq // TQ):
        kb0 = max(i - 1, 0)
        blocks.append((0, pl.ds(i * TQ, TQ), pl.ds(kb0 * TQ, 2 * TQ), 2 * TQ, (i - kb0) * TQ))
    for r in range(4):
        for n in range(seq // (4 * TQ)):
            kb0 = max(n - 1, 0)
            blocks.append((1, pl.ds(4 * TQ * n + r, TQ, stride=4), pl.ds(4 * TQ * kb0 + r, 2 * TQ, stride=4),
                           2 * TQ, (n - kb0) * TQ))
    for r in range(16):
        rows = pl.ds(r, TQ, stride=16)
        blocks.append((2, rows, rows, TQ, 0))

    ahead = 2
    pending = [scores(b) for b in blocks[:ahead]]
    for idx, b in enumerate(blocks):
        s = pending.pop(0)
        if idx + ahead < len(blocks):
            pending.append(scores(blocks[idx + ahead]))
        finish(b, s)

    for c0 in range(0, seq, chunk):
        rows = pl.ds(c0, chunk)
        m0, m1, m2 = m_s[0, rows, :], m_s[1, rows, :], m_s[2, rows, :]
        mm = jnp.maximum(jnp.maximum(m0, m1), m2)
        e0, e1, e2 = jnp.exp2(m0 - mm), jnp.exp2(m1 - mm), jnp.exp2(m2 - mm)
        num = acc_s[0, rows, :] * e0 + acc_s[1, rows, :] * e1 + acc_s[2, rows, :] * e2
        den = l_s[0, rows, :] * e0 + l_s[1, rows, :] * e1 + l_s[2, rows, :] * e2
        o_ref[0, rows, :] = (num / den).astype(BF16)


def _dilated_attn(qkv3, cos_t, sin_t, gq, gk):
    batch, seq, _ = qkv3.shape
    sec = D_SEC // LANES
    blk = lambda s: pl.BlockSpec((1, seq, LANES), lambda b, h, s=s: (b, 0, s * sec + h))
    tab = pl.BlockSpec((1, seq, LANES), lambda b, h: (b, 0, 0))
    vec = pl.BlockSpec((1, LANES), lambda b, h: (0, 0))
    return pl.pallas_call(
        functools.partial(_dilated_kernel, seq=seq),
        grid=(batch, N_PAIRS),
        in_specs=[blk(0), blk(1), blk(2), tab, tab, vec, vec],
        out_specs=pl.BlockSpec((1, seq, LANES), lambda b, h: (b, 0, h)),
        out_shape=jax.ShapeDtypeStruct((batch, seq, D_SEC), BF16),
        scratch_shapes=[
            pltpu.VMEM((seq, LANES), F32), pltpu.VMEM((seq, LANES), F32), pltpu.VMEM((seq, LANES), F32),
            pltpu.VMEM((3, seq, LANES), F32), pltpu.VMEM((3, seq, LANES), F32),
            pltpu.VMEM((3, seq, LANES), F32), pltpu.VMEM((2, 2 * TQ, 2 * TQ), F32),
        ],
        compiler_params=_cparams("parallel", "parallel"),
        name="dilated_attn",
    )(qkv3, qkv3, qkv3, cos_t, sin_t, gq, gk)


def _fox_kernel(q_ref, k_ref, v_ref, kb_ref, gq_ref, gk_ref, o_ref, qat_s, ka_s, vt_s, acc_s, tri_s, *, seq):
    gmat = _head_sumsq_matrix()
    chunk = 512
    lane = lax.broadcasted_iota(jnp.int32, (chunk, LANES), 1)
    low = lane < HEAD_DIM
    sel = jnp.where((lane >= HEAD_DIM) & (lane < HEAD_DIM + BIAS_TERMS), 1.0, 0.0)

    for c0 in range(0, seq, chunk):
        rows = pl.ds(c0, chunk)
        for g in range(N_PAIRS):
            lanes = slice(g * LANES, (g + 1) * LANES)
            qn = _qk_norm(q_ref[0, rows, lanes].astype(F32), gq_ref[...], gmat) * Q_SCALE_LOG2
            for h, qa in ((2 * g, jnp.where(low, qn, sel)),
                          (2 * g + 1, jnp.where(low, pltpu.roll(qn, HEAD_DIM, axis=1), sel))):
                qt = qa.T
                for cc in range(chunk // TK):
                    qat_s[c0 // TK + cc, h * LANES:(h + 1) * LANES, :] = qt[:, cc * TK:(cc + 1) * TK].astype(BF16)
            kn =_qk_norm(k_ref[0, rows, lanes].astype(F32), gk_ref[...], gmat)
            kb = kb_ref[0, rows, lanes].astype(F32)
            ka_s[2 * g, rows, :] = jnp.where(low, kn, kb).astype(BF16)
            ka_s[2 * g + 1, rows, :] = pltpu.roll(jnp.where(low, kb, kn), HEAD_DIM, axis=1).astype(BF16)
            vt = v_ref[0, rows, lanes].astype(F32).T
            for cc in range(chunk // TK):
                vt_s[c0 // TK + cc, lanes, :] = vt[:, cc * TK:(cc + 1) * TK].astype(BF16)

    r = lax.broadcasted_iota(jnp.int32, (TK, TK), 0)
    c = lax.broadcasted_iota(jnp.int32, (TK, TK), 1)
    tri_s[...] = jnp.where(r <= c, 0.0, NEG)

    def q_block(qi, carry):
        q_rows = pl.ds(pl.multiple_of(qi * TK, TK), TK)
        for h in range(N_HEADS):
            acc_s[h] = jnp.zeros((HEAD_DIM, TK), F32)

        def step(j, st, diagonal):
            k_rows = pl.ds(pl.multiple_of(j * TK, TK), TK)

            def scores(h):
                s = jnp.dot(ka_s[h, k_rows, :], qat_s[qi, h * LANES:(h + 1) * LANES, :], preferred_element_type=F32)
                return s + tri_s[...] if diagonal else s

            ahead = 2
            pending = [scores(h) for h in range(ahead)]
            new = []
            for h in range(N_HEADS):
                m, l = st[h]
                s = pending.pop(0)
                if h + ahead < N_HEADS:
                    pending.append(scores(h + ahead))
                m_new = jnp.maximum(m, jnp.max(s, axis=0, keepdims=True))
                alpha = jnp.exp2(m - m_new)
                pr = jnp.exp2(s - m_new)
                l_new = alpha * l + jnp.sum(pr, axis=0, keepdims=True)
                pv = jnp.dot(vt_s[j, h * HEAD_DIM:(h + 1) * HEAD_DIM, :], pr.astype(BF16),
                             preferred_element_type=F32)
                acc_s[h] = alpha * acc_s[h] + pv
                new.append((m_new, l_new))
            return tuple(new)

        init = tuple((jnp.full((1, TK), NEG, F32), jnp.zeros((1, TK), F32)) for _ in range(N_HEADS))
        st = lax.fori_loop(0, qi, lambda j, st: step(j, st, False), init)
        st = step(qi, st, True)
        for g in range(N_PAIRS):
            o2 = jnp.concatenate([acc_s[2 * g] / st[2 * g][1], acc_s[2 * g + 1] / st[2 * g + 1][1]], axis=0)
            o_ref[0, q_rows, g * LANES:(g + 1) * LANES] = o2.T.astype(BF16)
        return carry

    lax.fori_loop(0, seq // TK, q_block, 0)


def _fox_attn(qkv3, kbias, gq, gk):
    batch, seq, _ = qkv3.shape
    blk = lambda s: pl.BlockSpec((1, seq, D_SEC), lambda b, s=s: (b, 0, s))
    vec = pl.BlockSpec((1, LANES), lambda b: (0, 0))
    return pl.pallas_call(
        functools.partial(_fox_kernel, seq=seq),
        grid=(batch,),
        in_specs=[blk(3), blk(4), blk(5), pl.BlockSpec((1, seq, D_SEC), lambda b: (b, 0, 0)), vec, vec],
        out_specs=pl.BlockSpec((1, seq, D_SEC), lambda b: (b, 0, 0)),
        out_shape=jax.ShapeDtypeStruct((batch, seq, D_SEC), BF16),
        scratch_shapes=[
            pltpu.VMEM((seq // TK, N_HEADS * LANES, TK), BF16),
            pltpu.VMEM((N_HEADS, seq, LANES), BF16),
            pltpu.VMEM((seq // TK, D_SEC, TK), BF16),
            pltpu.VMEM((N_HEADS, HEAD_DIM, TK), F32),
            pltpu.VMEM((TK, TK), F32),
        ],
        compiler_params=_cparams("parallel"),
        name="fox_attn",
    )(qkv3, qkv3, qkv3, kbias, gq, gk)


def _out_router_kernel(oa_ref, ob_ref, x_ref, wo_ref, g_ref, wr_ref, br_ref,
                       h1_ref, m_ref, route_ref, cnt_ref, carry_s):
    i = pl.program_id(0)
    tm = TM_PROJ

    @pl.when(i == 0)
    def _():
        carry_s[...] = jnp.zeros_like(carry_s)

    h1 = (x_ref[...]
          + jnp.dot(oa_ref[...], wo_ref[0:D_SEC, :], preferred_element_type=F32)
          + jnp.dot(ob_ref[...], wo_ref[D_SEC:2 * D_SEC, :], preferred_element_type=F32))
    h1_ref[...] = h1
    ms = jnp.mean(h1 * h1, axis=-1, keepdims=True)
    mn = h1 * lax.rsqrt(ms + EPS) * g_ref[...]
    m_ref[...] = mn

    logits = jnp.dot(mn.astype(BF16), wr_ref[...], preferred_element_type=F32) + br_ref[...]
    lane = lax.broadcasted_iota(jnp.int32, (tm, LANES), 1).astype(F32)
    big = float(LANES)

    def first_argmax(vals):
        vmax = jnp.max(vals, axis=-1, keepdims=True)
        idx = jnp.min(jnp.where(vals == vmax, lane, big), axis=-1, keepdims=True)
        return vmax, idx

    lg = jnp.where(lane < N_GROUPS, logits, -jnp.inf)
    gmax, gidx = first_argmax(lg)
    gw = 1.0 / jnp.sum(jnp.exp(lg - gmax), axis=-1, keepdims=True)
    lo = ROUTER_LANE0 + EXPERTS_PER_GROUP * gidx
    le = jnp.where((lane >= lo) & (lane < lo + EXPERTS_PER_GROUP), logits, -jnp.inf)
    v0, i0 = first_argmax(le)
    v1, i1 = first_argmax(jnp.where(lane == i0, -jnp.inf, le))
    ex = jnp.exp(v1 - v0)
    w0 = gw / (1.0 + ex)
    w1 = gw * ex / (1.0 + ex)

    sel0 = lane == i0
    sel1 = lane == i1
    onehot = jnp.where(sel0 | sel1, 1.0, 0.0)
    r = lax.broadcasted_iota(jnp.int32, (tm, tm), 0)
    c = lax.broadcasted_iota(jnp.int32, (tm, tm), 1)
    ltri = (c <= r).astype(BF16)
    incl = jnp.dot(ltri, onehot.astype(BF16), preferred_element_type=F32)
    excl = incl - onehot + carry_s[0:1, :]
    rank0 = jnp.sum(jnp.where(sel0, excl, 0.0), axis=-1, keepdims=True)
    rank1 = jnp.sum(jnp.where(sel1, excl, 0.0), axis=-1, keepdims=True)
    new_carry = carry_s[0:1, :] + incl[tm - 1:tm, :]
    carry_s[...] = jnp.broadcast_to(new_carry, carry_s.shape)
    cnt_ref[...] = jnp.broadcast_to(new_carry, cnt_ref.shape)

    rec = jnp.zeros((tm, LANES), F32)
    for pos, val in enumerate((i0 - ROUTER_LANE0, i1 - ROUTER_LANE0, rank0, rank1, w0, w1)):
        rec = jnp.where(lane == float(pos), val, rec)
    route_ref[...] = rec[:, 0:SUBLANES]


def _out_router(oa, ob, x2, w_o, g_ffn, w_r, b_r):
    T = x2.shape[0]
    tm = TM_PROJ
    row = lambda w: pl.BlockSpec((tm, w), lambda i: (i, 0))
    const = lambda a, b: pl.BlockSpec((a, b), lambda i: (0, 0))
    return pl.pallas_call(
        _out_router_kernel,
        grid=(T // tm,),
        in_specs=[row(D_SEC), row(D_SEC), row(D_MODEL), const(D_MODEL, D_MODEL), const(1, D_MODEL),
                  const(D_MODEL, LANES), const(1, LANES)],
        out_specs=[row(D_MODEL), row(D_MODEL), row(SUBLANES), const(SUBLANES, LANES)],
        out_shape=[
            jax.ShapeDtypeStruct((T, D_MODEL), F32),
            jax.ShapeDtypeStruct((T, D_MODEL), F32),
            jax.ShapeDtypeStruct((T, SUBLANES), F32),
            jax.ShapeDtypeStruct((SUBLANES, LANES), F32),
        ],
        scratch_shapes=[pltpu.VMEM((SUBLANES, LANES), F32)],
        compiler_params=_cparams("arbitrary"),
        name="out_router",
    )(oa, ob, x2, w_o, g_ffn, w_r, b_r)


def _row_copy(src, src_row, dst, dst_row, sem):
    return pltpu.make_async_copy(src.at[pl.ds(src_row, 1)], dst.at[pl.ds(dst_row, 1)], sem)


def _zero_fill(off_ref, cnt_ref, zero_s, xs_ref, zsem, wait):
    def tile_copy(t):
        copy = pltpu.make_async_copy(zero_s, xs_ref.at[pl.ds(pl.multiple_of(t * TM_MOE, TM_MOE), TM_MOE)], zsem)
        if wait:
            copy.wait()
        else:
            copy.start()

    def per_expert(e, carry):
        @pl.when(cnt_ref[e] % TM_MOE != 0)
        def _():
            tile_copy(off_ref[e + 1] // TM_MOE - 1)
        return carry
    lax.fori_loop(0, N_EXPERTS, per_expert, 0)

    def per_tile(t, carry):
        tile_copy(t)
        return carry
    lax.fori_loop(off_ref[N_EXPERTS] // TM_MOE, xs_ref.shape[0] // TM_MOE, per_tile, 0)


def _dispatch_kernel(p0_ref, p1_ref, off_ref, cnt_ref, m_ref, xs_ref, zero_s, sem, zsem):
    i = pl.program_id(0)
    tm = TM_DISPATCH

    @pl.when(i == 0)
    def _():
        zero_s[...] = jnp.zeros_like(zero_s)
        _zero_fill(off_ref, cnt_ref, zero_s, xs_ref, zsem, wait=False)
        _zero_fill(off_ref, cnt_ref, zero_s, xs_ref, zsem, wait=True)

    base = i * tm

    def start(t, carry):
        g = base + t
        _row_copy(m_ref, t, xs_ref, p0_ref[g], sem).start()
        _row_copy(m_ref, t, xs_ref, p1_ref[g], sem).start()
        return carry
    lax.fori_loop(0, tm, start, 0)

    def wait(t, carry):
        _row_copy(m_ref, t, xs_ref, 0, sem).wait()
        _row_copy(m_ref, t, xs_ref, 0, sem).wait()
        return carry
    lax.fori_loop(0, tm, wait, 0)


def _dispatch(pos0, pos1, off, cnt, mn, n_rows):
    T = mn.shape[0]
    grid_spec = pltpu.PrefetchScalarGridSpec(
        num_scalar_prefetch=4,
        grid=(T // TM_DISPATCH,),
        in_specs=[pl.BlockSpec((TM_DISPATCH, D_MODEL), lambda i, *_: (i, 0))],
        out_specs=pl.BlockSpec(memory_space=pl.ANY),
        scratch_shapes=[pltpu.VMEM((TM_MOE, D_MODEL), F32), pltpu.SemaphoreType.DMA, pltpu.SemaphoreType.DMA],
    )
    return pl.pallas_call(
        _dispatch_kernel,
        grid_spec=grid_spec,
        out_shape=jax.ShapeDtypeStruct((n_rows, D_MODEL), F32),
        compiler_params=_cparams("arbitrary"),
        name="dispatch",
    )(pos0, pos1, off, cnt, mn)


def _moe_kernel(tile_e_ref, nused_ref, xs_ref, wup_ref, wdn_ref, ys_ref, wup_s, wdn_s):
    i = pl.program_id(0)

    @pl.when(i < nused_ref[0])
    def _():
        prev = tile_e_ref[jnp.maximum(i - 1, 0)]

        @pl.when((i == 0) | (tile_e_ref[i] != prev))
        def _():
            wup_s[...] = wup_ref[0].astype(BF16)
            wdn_s[...] = wdn_ref[0].astype(BF16)

        hu = jnp.dot(xs_ref[...].astype(BF16), wup_s[...], preferred_element_type=F32)
        gate = hu[:, :D_EXPERT]
        hid = gate * (1.0 / (1.0 + jnp.exp(-gate))) * hu[:, D_EXPERT:]
        ys_ref[...] = jnp.dot(hid.astype(BF16), wdn_s[...], preferred_element_type=F32)

    @pl.when(i >= nused_ref[0])
    def _():
        ys_ref[...] = jnp.zeros_like(ys_ref)


def _moe_experts(tile_e, nused, xs, w_up, w_down):
    n_rows = xs.shape[0]
    n_tiles = n_rows // TM_MOE
    row_map = lambda i, te, nu: (jnp.minimum(i, nu[0] - 1), 0)
    grid_spec = pltpu.PrefetchScalarGridSpec(
        num_scalar_prefetch=2,
        grid=(n_tiles,),
        in_specs=[
            pl.BlockSpec((TM_MOE, D_MODEL), row_map),
            pl.BlockSpec((1, D_MODEL, 2 * D_EXPERT), lambda i, te, nu: (te[i], 0, 0)),
            pl.BlockSpec((1, D_EXPERT, D_MODEL), lambda i, te, nu: (te[i], 0, 0)),
        ],
        out_specs=pl.BlockSpec((TM_MOE, D_MODEL), lambda i, te, nu: (i, 0)),
        scratch_shapes=[pltpu.VMEM((D_MODEL, 2 * D_EXPERT), BF16), pltpu.VMEM((D_EXPERT, D_MODEL), BF16)],
    )
    return pl.pallas_call(
        _moe_kernel,
        grid_spec=grid_spec,
        out_shape=jax.ShapeDtypeStruct((n_rows, D_MODEL), F32),
        compiler_params=_cparams("arbitrary"),
        name="moe_experts",
    )(tile_e, nused, xs, w_up, w_down)


def _combine_kernel(p0_ref, p1_ref, ys_ref, h1_ref, route_ref, p_ref,
                    g_ref, wg_ref, wp_ref, o_ref, gath_s, sem):
    i = pl.program_id(0)
    tm = TM_COMBINE
    base = i * tm

    def start(t, carry):
        g = base + t
        _row_copy(ys_ref, p0_ref[g], gath_s.at[0], t, sem).start()
        _row_copy(ys_ref, p1_ref[g], gath_s.at[1], t, sem).start()
        return carry
    lax.fori_loop(0, tm, start, 0)

    def wait(t, carry):
        _row_copy(ys_ref, 0, gath_s.at[0], t, sem).wait()
        _row_copy(ys_ref, 0, gath_s.at[1], t, sem).wait()
        return carry
    lax.fori_loop(0, tm, wait, 0)

    route = route_ref[...]
    y = route[:, 4:5] * gath_s[0] + route[:, 5:6] * gath_s[1]
    h2 = h1_ref[...] + y
    ms = jnp.mean(h2 * h2, axis=-1, keepdims=True)
    n = (h2 * lax.rsqrt(ms + EPS) * g_ref[...]).astype(BF16)
    z = jnp.dot(n, wg_ref[...], preferred_element_type=F32)
    gate = 1.0 / (1.0 + jnp.exp(-z))
    ple = jnp.dot(p_ref[...].astype(BF16), wp_ref[...], preferred_element_type=F32)
    o_ref[...] = h2 + gate * ple


def _combine_ple(pos0, pos1, ys, h1, route, p2, g_ple, w_gate, w_proj):
    T = h1.shape[0]
    tm = TM_COMBINE
    row = lambda w: pl.BlockSpec((tm, w), lambda i, *_: (i, 0))
    const = lambda a, b: pl.BlockSpec((a, b), lambda i, *_: (0, 0))
    grid_spec = pltpu.PrefetchScalarGridSpec(
        num_scalar_prefetch=2,
        grid=(T // tm,),
        in_specs=[pl.BlockSpec(memory_space=pl.ANY), row(D_MODEL), row(SUBLANES), row(PLE_DIM),
                  const(1, D_MODEL), const(D_MODEL, D_MODEL), const(PLE_DIM, D_MODEL)],
        out_specs=row(D_MODEL),
        scratch_shapes=[pltpu.VMEM((2, tm, D_MODEL), F32), pltpu.SemaphoreType.DMA],
    )
    return pl.pallas_call(
        _combine_kernel,
        grid_spec=grid_spec,
        out_shape=jax.ShapeDtypeStruct((T, D_MODEL), F32),
        compiler_params=_cparams("arbitrary"),
        name="combine_ple",
    )(pos0, pos1, ys, h1, route, p2, g_ple, w_gate, w_proj)


def _rope_tables(positions):
    half = ROT_DIM // 2
    inv = ROPE_THETA ** (-jnp.arange(0, ROT_DIM, 2, dtype=F32) / ROT_DIM)
    ang = positions.astype(F32)[..., None] * inv
    cos, sin = jnp.cos(ang), jnp.sin(ang)
    ones = jnp.ones(ang.shape[:-1] + (HEAD_DIM - ROT_DIM,), F32)
    cos_h = jnp.concatenate([cos, cos, ones], axis=-1)
    sin_h = jnp.concatenate([-sin, sin, 0.0 * ones], axis=-1)
    return jnp.tile(cos_h, (1, 1, 2)), jnp.tile(sin_h, (1, 1, 2))


def _layer(i, h, p, cos_t, sin_t, g_mix, w_in, b_f, qn_a, kn_a, qn_b, kn_b, w_o, g_ffn, w_rg, b_rg,
           w_re, b_re, w_up, w_down, g_ple, w_ple_gate, w_ple_proj):
    B, S, _ = h.shape
    T = B * S
    x2 = h.reshape(T, D_MODEL)

    w_qkv = w_in[i][:, :N_QKV].astype(BF16)
    wf = jnp.zeros((D_MODEL, LANES), F32).at[:, :N_HEADS].set(w_in[i][:, N_QKV:]).astype(BF16)
    bf = jnp.zeros((1, LANES), F32).at[0, :N_HEADS].set(b_f[i])
    pair = lambda g: jnp.tile(g, 2).reshape(1, LANES)
    w_r = jnp.zeros((D_MODEL, LANES), F32)
    w_r = w_r.at[:, :N_GROUPS].set(w_rg[i])
    w_r = w_r.at[:, ROUTER_LANE0:ROUTER_LANE0 + N_EXPERTS].set(
        jnp.transpose(w_re[i], (1, 0, 2)).reshape(D_MODEL, N_EXPERTS))
    b_r = jnp.zeros((1, LANES), F32).at[0, :N_GROUPS].set(b_rg[i])
    b_r = b_r.at[0, ROUTER_LANE0:ROUTER_LANE0 + N_EXPERTS].set(b_re[i].reshape(-1))

    qkv, f = _in_proj(x2, g_mix[i].reshape(1, -1), w_qkv, wf)
    kbias = _forget_scan(f, bf, B, S)
    qkv3 = qkv.reshape(B, S, N_QKV)
    oa = _dilated_attn(qkv3, cos_t, sin_t, pair(qn_a[i]), pair(kn_a[i]))
    ob = _fox_attn(qkv3, kbias, pair(qn_b[i]), pair(kn_b[i]))

    h1, mn, route, cnt = _out_router(oa.reshape(T, D_SEC), ob.reshape(T, D_SEC), x2, w_o[i].astype(BF16),
                                     g_ffn[i].reshape(1, -1), w_r.astype(BF16), b_r)

    counts = cnt[0, ROUTER_LANE0:ROUTER_LANE0 + N_EXPERTS].astype(jnp.int32)
    tiles_per_e = (counts + TM_MOE - 1) // TM_MOE
    tile_end = jnp.cumsum(tiles_per_e)
    off = jnp.concatenate([jnp.zeros((1,), jnp.int32), tile_end * TM_MOE]).astype(jnp.int32)
    n_tiles = (2 * T) // TM_MOE + N_EXPERTS
    nused = tile_end[-1:].astype(jnp.int32)
    tile_ids = jnp.minimum(jnp.arange(n_tiles, dtype=jnp.int32), nused[0] - 1)
    tile_e = jnp.sum((tile_ids[:, None] >= tile_end[None, :]).astype(jnp.int32), axis=1).astype(jnp.int32)
    e0, e1, r0, r1 = (route[:, j].astype(jnp.int32) for j in range(4))
    pos0 = jnp.take(off, e0) + r0
    pos1 = jnp.take(off, e1) + r1

    xs = _dispatch(pos0, pos1, off, counts, mn, n_tiles * TM_MOE)
    ys = _moe_experts(tile_e, nused, xs, w_up[i], w_down[i])
    out = _combine_ple(pos0, pos1, ys, h1, route, p[i].reshape(T, PLE_DIM),
                       g_ple[i].reshape(1, -1), w_ple_gate[i].astype(BF16), w_ple_proj[i].astype(BF16))
    return out.reshape(B, S, D_MODEL)


def kernel(x, p, positions, g_mix, w_in, b_f, qn_a, kn_a, qn_b, kn_b, w_o, g_ffn, w_rg, b_rg, w_re, b_re,
           w_up, w_down, g_ple, w_ple_gate, w_ple_proj):
    cos_t, sin_t = _rope_tables(positions)
    h = x
    for i in range(p.shape[0]):
        h = _layer(i, h, p, cos_t, sin_t, g_mix, w_in, b_f, qn_a, kn_a, qn_b, kn_b, w_o, g_ffn, w_rg, b_rg,
                   w_re, b_re, w_up, w_down, g_ple, w_ple_gate, w_ple_proj)
    return h
```

```python
import functools
import math

import jax
import jax.numpy as jnp
from jax import lax
from jax.experimental import pallas as pl
from jax.experimental.pallas import tpu as pltpu

D_MODEL = 1024
HEAD_DIM = 64
N_HEADS = 8
D_SEC = N_HEADS * HEAD_DIM
N_QKV = 6 * D_SEC
ROT_DIM = HEAD_DIM // 4
ROPE_THETA = 500000.0
N_GROUPS = 4
EXPERTS_PER_GROUP = 8
N_EXPERTS = N_GROUPS * EXPERTS_PER_GROUP
D_EXPERT = 512
PLE_DIM = 256
EPS = 1e-6
NEG = -1e30
WINDOW = 128

LANES = 128
SUBLANES = 8
VMEM_LIMIT = 48 * 1024 * 1024

TM_PROJ = 512
TQ = 128
TK = 256
N_PAIRS = N_HEADS // 2
TM_MOE = 256
TM_TOK = 512
SEG_ALIGN = 2 * SUBLANES
SORT_ROWS = 2 * TM_TOK + N_EXPERTS * SEG_ALIGN
SEG_SIZES = tuple(TM_TOK >> k for k in range(TM_TOK.bit_length()) if TM_TOK >> k >= SEG_ALIGN)
ROUTER_LANE0 = N_GROUPS

Q_SCALE_LOG2 = math.log2(math.e) / math.sqrt(HEAD_DIM)

F32 = jnp.float32
BF16 = jnp.bfloat16
NT_DIMS = (((1,), (1,)), ((), ()))


def _cparams(*sem):
    return pltpu.CompilerParams(dimension_semantics=sem, vmem_limit_bytes=VMEM_LIMIT)


def _in_proj_kernel(x_ref, g_ref, w_ref, wf_ref, qkv_ref, f_ref):
    x = x_ref[...]
    ms = jnp.mean(x * x, axis=-1, keepdims=True)
    a = (x * lax.rsqrt(ms + EPS) * g_ref[...]).astype(BF16)
    qkv_ref[...] = jnp.dot(a, w_ref[...], preferred_element_type=F32).astype(BF16)
    f_ref[...] = jnp.dot(a, wf_ref[...], preferred_element_type=F32)


def _in_proj(x2, g_mix, w_qkv, wf):
    T = x2.shape[0]
    return pl.pallas_call(
        _in_proj_kernel,
        grid=(T // TM_PROJ,),
        in_specs=[
            pl.BlockSpec((TM_PROJ, D_MODEL), lambda i: (i, 0)),
            pl.BlockSpec((1, D_MODEL), lambda i: (0, 0)),
            pl.BlockSpec((D_MODEL, N_QKV), lambda i: (0, 0)),
            pl.BlockSpec((D_MODEL, LANES), lambda i: (0, 0)),
        ],
        out_specs=[
            pl.BlockSpec((TM_PROJ, N_QKV), lambda i: (i, 0)),
            pl.BlockSpec((TM_PROJ, LANES), lambda i: (i, 0)),
        ],
        out_shape=[
            jax.ShapeDtypeStruct((T, N_QKV), BF16),
            jax.ShapeDtypeStruct((T, LANES), F32),
        ],
        compiler_params=_cparams("parallel"),
        name="in_proj",
    )(x2, g_mix, w_qkv, wf)


BIAS_TERMS = 3


def _bias_lane(head):
    return (head // 2) * LANES + (HEAD_DIM if head % 2 == 0 else 0)


def _forget_scan_kernel(f_ref, bf_ref, kb_ref, *, seq):
    f = f_ref[...] + bf_ref[...]
    c = jnp.minimum(f, 0.0) - jnp.log1p(jnp.exp(-jnp.abs(f)))
    row = lax.broadcasted_iota(jnp.int32, c.shape, 0)
    k = 1
    while k < seq:
        c = c + jnp.where(row >= k, pltpu.roll(c, k, axis=0), 0.0)
        k *= 2
    rest = c * (-math.log2(math.e))
    r_idx = lax.broadcasted_iota(jnp.int32, (LANES, D_SEC), 0)
    c_idx = lax.broadcasted_iota(jnp.int32, (LANES, D_SEC), 1)
    base = (r_idx // 2) * LANES + jnp.where(r_idx % 2 == 0, HEAD_DIM, 0)
    out = jnp.zeros((seq, D_SEC), F32)
    for t in range(BIAS_TERMS):
        term = rest.astype(BF16)
        rest = rest - term.astype(F32)
        place = ((r_idx < N_HEADS) & (c_idx == base + t)).astype(BF16)
        out = out + jnp.dot(term, place, preferred_element_type=F32)
    kb_ref[0] = out.astype(BF16)


def _forget_scan(f, bf, batch, seq):
    return pl.pallas_call(
        functools.partial(_forget_scan_kernel, seq=seq),
        grid=(batch,),
        in_specs=[
            pl.BlockSpec((seq, LANES), lambda b: (b, 0)),
            pl.BlockSpec((1, LANES), lambda b: (0, 0)),
        ],
        out_specs=pl.BlockSpec((1, seq, D_SEC), lambda b: (b, 0, 0)),
        out_shape=jax.ShapeDtypeStruct((batch, seq, D_SEC), BF16),
        compiler_params=_cparams("parallel"),
        name="forget_scan",
    )(f, bf)


def _head_sumsq_matrix():
    r = lax.broadcasted_iota(jnp.int32, (LANES, LANES), 0) // HEAD_DIM
    c = lax.broadcasted_iota(jnp.int32, (LANES, LANES), 1) // HEAD_DIM
    return (r == c).astype(BF16)


def _qk_norm(x, gain, gmat):
    ss = jnp.dot((x * x).astype(BF16), gmat, preferred_element_type=F32)
    return x * lax.rsqrt(ss * (1.0 / HEAD_DIM) + EPS) * gain


def _stack_heads(qb, head0):
    zero = jnp.zeros_like(qb)
    return jnp.concatenate([jnp.where(head0, qb, zero), jnp.where(head0, zero, qb)], axis=0)


def _unstack(a, head0):
    return jnp.where(head0, a[:TQ], a[TQ:])


def _dilated_kernel(q_ref, k_ref, v_ref, cos_ref, sin_ref, gq_ref, gk_ref, o_ref,
                    qn_s, kn_s, v_s, acc_s, m_s, l_s, bias_s, *, seq):
    gmat = _head_sumsq_matrix()
    lane = lax.broadcasted_iota(jnp.int32, (TQ, LANES), 1)
    head0 = lane < HEAD_DIM
    chunk = 512
    lane_c = lax.broadcasted_iota(jnp.int32, (chunk, LANES), 1) % HEAD_DIM
    first_half = lane_c < ROT_DIM // 2

    def rope(t, cs, sn):
        partner = jnp.where(first_half, pltpu.roll(t, LANES - ROT_DIM // 2, axis=1),
                            pltpu.roll(t, ROT_DIM // 2, axis=1))
        return t * cs + partner * sn

    for c0 in range(0, seq, chunk):
        rows = pl.ds(c0, chunk)
        cs = cos_ref[0, rows, :]
        sn = sin_ref[0, rows, :]
        qn = _qk_norm(q_ref[0, rows, :].astype(F32), gq_ref[...], gmat)
        qn_s[rows, :] = rope(qn, cs, sn) * Q_SCALE_LOG2
        kn = _qk_norm(k_ref[0, rows, :].astype(F32), gk_ref[...], gmat)
        kn_s[rows, :] = rope(kn, cs, sn)
        v_s[rows, :] = v_ref[0, rows, :].astype(F32)

    u = lax.broadcasted_iota(jnp.int32, (2 * TQ, 2 * TQ), 0) % TQ
    c = lax.broadcasted_iota(jnp.int32, (2 * TQ, 2 * TQ), 1)
    for slot, dist0 in enumerate((0, TQ)):
        dist = dist0 + u - c
        bias_s[slot] = jnp.where((dist >= 0) & (dist <= WINDOW), 0.0, NEG)

    def scores(blk):
        p, q_rows, k_rows, nk, dist0 = blk
        qst = _stack_heads(qn_s[q_rows, :].astype(BF16), head0)
        s = lax.dot_general(qst, kn_s[k_rows, :].astype(BF16), NT_DIMS, preferred_element_type=F32)
        return s + bias_s[dist0 // TQ, :, 0:nk]

    def finish(blk, s):
        p, q_rows, k_rows, nk, dist0 = blk
        m = jnp.max(s, axis=-1, keepdims=True)
        pr = jnp.exp2(s - m)
        l = jnp.sum(pr, axis=-1, keepdims=True)
        acc = jnp.dot(pr.astype(BF16), v_s[k_rows, :].astype(BF16), preferred_element_type=F32)
        acc_s[p, q_rows, :] = _unstack(acc, head0)
        m_s[p, q_rows, :] = _unstack(jnp.broadcast_to(m, (2 * TQ, LANES)), head0)
        l_s[p, q_rows, :] = _unstack(jnp.broadcast_to(l, (2 * TQ, LANES)), head0)

    blocks = []
    for i in range(seq // TQ):
        kb0 = max(i - 1, 0)
        blocks.append((0, pl.ds(i * TQ, TQ), pl.ds(kb0 * TQ, 2 * TQ), 2 * TQ, (i - kb0) * TQ))
    for r in range(4):
        for n in range(seq // (4 * TQ)):
            kb0 = max(n - 1, 0)
            blocks.append((1, pl.ds(4 * TQ * n + r, TQ, stride=4), pl.ds(4 * TQ * kb0 + r, 2 * TQ, stride=4),
                           2 * TQ, (n - kb0) * TQ))
    for r in range(16):
        rows = pl.ds(r, TQ, stride=16)
        blocks.append((2, rows, rows, TQ, 0))

    ahead = 2
    pending = [scores(b) for b in blocks[:ahead]]
    for idx, b in enumerate(blocks):
        s = pending.pop(0)
        if idx + ahead < len(blocks):
            pending.append(scores(blocks[idx + ahead]))
        finish(b, s)

    for c0 in range(0, seq, chunk):
        rows = pl.ds(c0, chunk)
        m0, m1, m2 = m_s[0, rows, :], m_s[1, rows, :], m_s[2, rows, :]
        mm = jnp.maximum(jnp.maximum(m0, m1), m2)
        e0, e1, e2 = jnp.exp2(m0 - mm), jnp.exp2(m1 - mm), jnp.exp2(m2 - mm)
        num = acc_s[0, rows, :] * e0 + acc_s[1, rows, :] * e1 + acc_s[2, rows, :] * e2
        den = l_s[0, rows, :] * e0 + l_s[1, rows, :] * e1 + l_s[2, rows, :] * e2
        o_ref[0, rows, :] = (num / den).astype(BF16)


def _dilated_attn(qkv3, cos_t, sin_t, gq, gk):
    batch, seq, _ = qkv3.shape
    sec = D_SEC // LANES
    blk = lambda s: pl.BlockSpec((1, seq, LANES), lambda b, h, s=s: (b, 0, s * sec + h))
    tab = pl.BlockSpec((1, seq, LANES), lambda b, h: (b, 0, 0))
    vec = pl.BlockSpec((1, LANES), lambda b, h: (0, 0))
    return pl.pallas_call(
        functools.partial(_dilated_kernel, seq=seq),
        grid=(batch, N_PAIRS),
        in_specs=[blk(0), blk(1), blk(2), tab, tab, vec, vec],
        out_specs=pl.BlockSpec((1, seq, LANES), lambda b, h: (b, 0, h)),
        out_shape=jax.ShapeDtypeStruct((batch, seq, D_SEC), BF16),
        scratch_shapes=[
            pltpu.VMEM((seq, LANES), F32), pltpu.VMEM((seq, LANES), F32), pltpu.VMEM((seq, LANES), F32),
            pltpu.VMEM((3, seq, LANES), F32), pltpu.VMEM((3, seq, LANES), F32),
            pltpu.VMEM((3, seq, LANES), F32), pltpu.VMEM((2, 2 * TQ, 2 * TQ), F32),
        ],
        compiler_params=_cparams("parallel", "parallel"),
        name="dilated_attn",
    )(qkv3, qkv3, qkv3, cos_t, sin_t, gq, gk)


def _fox_kernel(q_ref, k_ref, v_ref, kb_ref, gq_ref, gk_ref, o_ref, qat_s, ka_s, vt_s, acc_s, tri_s, *, seq):
    gmat = _head_sumsq_matrix()
    chunk = 512
    lane = lax.broadcasted_iota(jnp.int32, (chunk, LANES), 1)
    low = lane < HEAD_DIM
    sel = jnp.where((lane >= HEAD_DIM) & (lane < HEAD_DIM + BIAS_TERMS), 1.0, 0.0)

    for c0 in range(0, seq, chunk):
        rows = pl.ds(c0, chunk)
        for g in range(N_PAIRS):
            lanes = slice(g * LANES, (g + 1) * LANES)
            qn = _qk_norm(q_ref[0, rows, lanes].astype(F32), gq_ref[...], gmat) * Q_SCALE_LOG2
            for h, qa in ((2 * g, jnp.where(low, qn, sel)),
                          (2 * g + 1, jnp.where(low, pltpu.roll(qn, HEAD_DIM, axis=1), sel))):
                qt = qa.T
                for cc in range(chunk // TK):
                    qat_s[c0 // TK + cc, h * LANES:(h + 1) * LANES, :] = qt[:, cc * TK:(cc + 1) * TK].astype(BF16)
            kn =_qk_norm(k_ref[0, rows, lanes].astype(F32), gk_ref[...], gmat)
            kb = kb_ref[0, rows, lanes].astype(F32)
            ka_s[2 * g, rows, :] = jnp.where(low, kn, kb).astype(BF16)
            ka_s[2 * g + 1, rows, :] = pltpu.roll(jnp.where(low, kb, kn), HEAD_DIM, axis=1).astype(BF16)
            vt = v_ref[0, rows, lanes].astype(F32).T
            for cc in range(chunk // TK):
                vt_s[c0 // TK + cc, lanes, :] = vt[:, cc * TK:(cc + 1) * TK].astype(BF16)

    r = lax.broadcasted_iota(jnp.int32, (TK, TK), 0)
    c = lax.broadcasted_iota(jnp.int32, (TK, TK), 1)
    tri_s[...] = jnp.where(r <= c, 0.0, NEG)

    def q_block(qi, carry):
        q_rows = pl.ds(pl.multiple_of(qi * TK, TK), TK)
        for h in range(N_HEADS):
            acc_s[h] = jnp.zeros((HEAD_DIM, TK), F32)

        def step(j, st, diagonal):
            k_rows = pl.ds(pl.multiple_of(j * TK, TK), TK)

            def scores(h):
                s = jnp.dot(ka_s[h, k_rows, :], qat_s[qi, h * LANES:(h + 1) * LANES, :], preferred_element_type=F32)
                return s + tri_s[...] if diagonal else s

            ahead = 2
            pending = [scores(h) for h in range(ahead)]
            new = []
            for h in range(N_HEADS):
                m, l = st[h]
                s = pending.pop(0)
                if h + ahead < N_HEADS:
                    pending.append(scores(h + ahead))
                m_new = jnp.maximum(m, jnp.max(s, axis=0, keepdims=True))
                alpha = jnp.exp2(m - m_new)
                pr = jnp.exp2(s - m_new)
                l_new = alpha * l + jnp.sum(pr, axis=0, keepdims=True)
                pv = jnp.dot(vt_s[j, h * HEAD_DIM:(h + 1) * HEAD_DIM, :], pr.astype(BF16),
                             preferred_element_type=F32)
                acc_s[h] = alpha * acc_s[h] + pv
                new.append((m_new, l_new))
            return tuple(new)

        init = tuple((jnp.full((1, TK), NEG, F32), jnp.zeros((1, TK), F32)) for _ in range(N_HEADS))
        st = lax.fori_loop(0, qi, lambda j, st: step(j, st, False), init)
        st = step(qi, st, True)
        for g in range(N_PAIRS):
            o2 = jnp.concatenate([acc_s[2 * g] / st[2 * g][1], acc_s[2 * g + 1] / st[2 * g + 1][1]], axis=0)
            o_ref[0, q_rows, g * LANES:(g + 1) * LANES] = o2.T.astype(BF16)
        return carry

    lax.fori_loop(0, seq // TK, q_block, 0)


def _fox_attn(qkv3, kbias, gq, gk):
    batch, seq, _ = qkv3.shape
    blk = lambda s: pl.BlockSpec((1, seq, D_SEC), lambda b, s=s: (b, 0, s))
    vec = pl.BlockSpec((1, LANES), lambda b: (0, 0))
    return pl.pallas_call(
        functools.partial(_fox_kernel, seq=seq),
        grid=(batch,),
        in_specs=[blk(3), blk(4), blk(5), pl.BlockSpec((1, seq, D_SEC), lambda b: (b, 0, 0)), vec, vec],
        out_specs=pl.BlockSpec((1, seq, D_SEC), lambda b: (b, 0, 0)),
        out_shape=jax.ShapeDtypeStruct((batch, seq, D_SEC), BF16),
        scratch_shapes=[
            pltpu.VMEM((seq // TK, N_HEADS * LANES, TK), BF16),
            pltpu.VMEM((N_HEADS, seq, LANES), BF16),
            pltpu.VMEM((seq // TK, D_SEC, TK), BF16),
            pltpu.VMEM((N_HEADS, HEAD_DIM, TK), F32),
            pltpu.VMEM((TK, TK), F32),
        ],
        compiler_params=_cparams("parallel"),
        name="fox_attn",
    )(qkv3, qkv3, qkv3, kbias, gq, gk)


def _out_router_kernel(oa_ref, ob_ref, x_ref, wo_ref, g_ref, wr_ref, br_ref,
                       h1_ref, msort_ref, route_ref, cnt_ref):
    tm = TM_TOK
    h1 = (x_ref[...]
          + jnp.dot(oa_ref[...], wo_ref[0:D_SEC, :], preferred_element_type=F32)
          + jnp.dot(ob_ref[...], wo_ref[D_SEC:2 * D_SEC, :], preferred_element_type=F32))
    h1_ref[...] = h1
    ms = jnp.mean(h1 * h1, axis=-1, keepdims=True)
    mn = h1 * lax.rsqrt(ms + EPS) * g_ref[...]

    logits = jnp.dot(mn.astype(BF16), wr_ref[...], preferred_element_type=F32) + br_ref[...]
    lane = lax.broadcasted_iota(jnp.int32, (tm, LANES), 1).astype(F32)
    big = float(LANES)

    def first_argmax(vals):
        vmax = jnp.max(vals, axis=-1, keepdims=True)
        idx = jnp.min(jnp.where(vals == vmax, lane, big), axis=-1, keepdims=True)
        return vmax, idx

    lg = jnp.where(lane < N_GROUPS, logits, -jnp.inf)
    gmax, gidx = first_argmax(lg)
    gw = 1.0 / jnp.sum(jnp.exp(lg - gmax), axis=-1, keepdims=True)
    lo = ROUTER_LANE0 + EXPERTS_PER_GROUP * gidx
    le = jnp.where((lane >= lo) & (lane < lo + EXPERTS_PER_GROUP), logits, -jnp.inf)
    v0, i0 = first_argmax(le)
    v1, i1 = first_argmax(jnp.where(lane == i0, -jnp.inf, le))
    ex = jnp.exp(v1 - v0)
    w0 = gw / (1.0 + ex)
    w1 = gw * ex / (1.0 + ex)

    sel0 = lane == i0
    sel1 = lane == i1
    onehot = jnp.where(sel0 | sel1, 1.0, 0.0)
    r = lax.broadcasted_iota(jnp.int32, (tm, tm), 0)
    c = lax.broadcasted_iota(jnp.int32, (tm, tm), 1)
    ltri = (c <= r).astype(BF16)
    incl = jnp.dot(ltri, onehot.astype(BF16), preferred_element_type=F32)
    excl = incl - onehot
    counts = incl[tm - 1:tm, :]
    cnt_ref[0] = jnp.broadcast_to(counts, (SUBLANES, LANES))

    seg_rows = jnp.floor((counts + (SEG_ALIGN - 1.0)) * (1.0 / SEG_ALIGN)) * SEG_ALIGN
    seg_rows = jnp.broadcast_to(seg_rows, (tm, LANES))
    lp0 = jnp.sum(jnp.where(lane < i0, seg_rows, 0.0) + jnp.where(sel0, excl, 0.0), axis=-1, keepdims=True)
    lp1 = jnp.sum(jnp.where(lane < i1, seg_rows, 0.0) + jnp.where(sel1, excl, 0.0), axis=-1, keepdims=True)

    rec = jnp.zeros((tm, LANES), F32)
    for pos, val in enumerate((i0 - ROUTER_LANE0, i1 - ROUTER_LANE0, lp0, lp1, w0, w1)):
        rec = jnp.where(lane == float(pos), val, rec)
    route_ref[...] = rec[:, 0:SUBLANES]

    rec_t = rec.T
    out_row = lax.broadcasted_iota(jnp.int32, (SORT_ROWS, tm), 0).astype(F32)
    pick = jnp.where((out_row == rec_t[2:3, :]) | (out_row == rec_t[3:4, :]), 1.0, 0.0).astype(BF16)
    msort_ref[...] = jnp.dot(pick, mn.astype(BF16), preferred_element_type=F32).astype(BF16)


def _out_router(oa, ob, x2, w_o, g_ffn, w_r, b_r):
    T = x2.shape[0]
    tm = TM_TOK
    row = lambda w: pl.BlockSpec((tm, w), lambda i: (i, 0))
    const = lambda a, b: pl.BlockSpec((a, b), lambda i: (0, 0))
    return pl.pallas_call(
        _out_router_kernel,
        grid=(T // tm,),
        in_specs=[row(D_SEC), row(D_SEC), row(D_MODEL), const(D_MODEL, D_MODEL), const(1, D_MODEL),
                  const(D_MODEL, LANES), const(1, LANES)],
        out_specs=[row(D_MODEL), pl.BlockSpec((SORT_ROWS, D_MODEL), lambda i: (i, 0)), row(SUBLANES),
                   pl.BlockSpec((1, SUBLANES, LANES), lambda i: (i, 0, 0))],
        out_shape=[
            jax.ShapeDtypeStruct((T, D_MODEL), F32),
            jax.ShapeDtypeStruct((T // tm * SORT_ROWS, D_MODEL), BF16),
            jax.ShapeDtypeStruct((T, SUBLANES), F32),
            jax.ShapeDtypeStruct((T // tm, SUBLANES, LANES), F32),
        ],
        compiler_params=_cparams("parallel"),
        name="out_router",
    )(oa, ob, x2, w_o, g_ffn, w_r, b_r)


def _for_segment_pieces(rows_ref, local_ref, global_ref, idx, fn):
    n = rows_ref[idx]
    for size in SEG_SIZES:
        done = n & (-2 * size)

        @pl.when((n & size) != 0)
        def _(done=done, size=size):
            fn(pl.multiple_of(local_ref[idx] + done, SEG_ALIGN), pl.multiple_of(global_ref[idx] + done, SEG_ALIGN),
               size)


def _dispatch_kernel(rows_ref, local_ref, global_ref, off_ref, cnt_ref, msort_ref, xs_ref, zero_s, sem, zsem):
    n_tok_tiles = msort_ref.shape[0] // SORT_ROWS

    def run(act):
        def per_tile(t, carry):
            def per_expert(e, carry):
                def piece(local_row, global_row, size):
                    src_row = pl.multiple_of(t * SORT_ROWS + local_row, SEG_ALIGN)
                    act(pltpu.make_async_copy(msort_ref.at[pl.ds(src_row, size)],
                                              xs_ref.at[pl.ds(global_row, size)], sem))
                _for_segment_pieces(rows_ref, local_ref, global_ref, t * N_EXPERTS + e, piece)
                return carry
            return lax.fori_loop(0, N_EXPERTS, per_expert, carry)
        lax.fori_loop(0, n_tok_tiles, per_tile, 0)

    def zero_fill(act):
        def tile_copy(t):
            act(pltpu.make_async_copy(zero_s, xs_ref.at[pl.ds(pl.multiple_of(t * TM_MOE, TM_MOE), TM_MOE)], zsem))

        def per_expert(e, carry):
            @pl.when(cnt_ref[e] % TM_MOE != 0)
            def _():
                tile_copy(off_ref[e + 1] // TM_MOE - 1)
            return carry
        lax.fori_loop(0, N_EXPERTS, per_expert, 0)

        def per_tile(t, carry):
            tile_copy(t)
            return carry
        lax.fori_loop(off_ref[N_EXPERTS] // TM_MOE, xs_ref.shape[0] // TM_MOE, per_tile, 0)

    zero_s[...] = jnp.zeros_like(zero_s)
    zero_fill(lambda c: c.start())
    zero_fill(lambda c: c.wait())
    run(lambda c: c.start())
    run(lambda c: c.wait())


def _dispatch(seg_rows, seg_local, seg_global, off, cnt, msort, n_rows):
    grid_spec = pltpu.PrefetchScalarGridSpec(
        num_scalar_prefetch=5,
        grid=(1,),
        in_specs=[pl.BlockSpec(memory_space=pl.ANY)],
        out_specs=pl.BlockSpec(memory_space=pl.ANY),
        scratch_shapes=[pltpu.VMEM((TM_MOE, D_MODEL), BF16), pltpu.SemaphoreType.DMA, pltpu.SemaphoreType.DMA],
    )
    return pl.pallas_call(
        _dispatch_kernel,
        grid_spec=grid_spec,
        out_shape=jax.ShapeDtypeStruct((n_rows, D_MODEL), BF16),
        compiler_params=_cparams("arbitrary"),
        name="dispatch",
    )(seg_rows, seg_local, seg_global, off, cnt, msort)


def _moe_kernel(tile_e_ref, nused_ref, xs_ref, wup_ref, wdn_ref, ys_ref, wup_s, wdn_s):
    i = pl.program_id(0)

    @pl.when(i < nused_ref[0])
    def _():
        prev = tile_e_ref[jnp.maximum(i - 1, 0)]

        @pl.when((i == 0) | (tile_e_ref[i] != prev))
        def _():
            wup_s[...] = wup_ref[0].astype(BF16)
            wdn_s[...] = wdn_ref[0].astype(BF16)

        hu = jnp.dot(xs_ref[...], wup_s[...], preferred_element_type=F32)
        gate = hu[:, :D_EXPERT]
        hid = gate * (1.0 / (1.0 + jnp.exp(-gate))) * hu[:, D_EXPERT:]
        ys_ref[...] = jnp.dot(hid.astype(BF16), wdn_s[...], preferred_element_type=F32).astype(BF16)

    @pl.when(i >= nused_ref[0])
    def _():
        ys_ref[...] = jnp.zeros_like(ys_ref)


def _moe_experts(tile_e, nused, xs, w_up, w_down):
    n_rows = xs.shape[0]
    n_tiles = n_rows // TM_MOE
    row_map = lambda i, te, nu: (jnp.minimum(i, nu[0] - 1), 0)
    grid_spec = pltpu.PrefetchScalarGridSpec(
        num_scalar_prefetch=2,
        grid=(n_tiles,),
        in_specs=[
            pl.BlockSpec((TM_MOE, D_MODEL), row_map),
            pl.BlockSpec((1, D_MODEL, 2 * D_EXPERT), lambda i, te, nu: (te[i], 0, 0)),
            pl.BlockSpec((1, D_EXPERT, D_MODEL), lambda i, te, nu: (te[i], 0, 0)),
        ],
        out_specs=pl.BlockSpec((TM_MOE, D_MODEL), lambda i, te, nu: (i, 0)),
        scratch_shapes=[pltpu.VMEM((D_MODEL, 2 * D_EXPERT), BF16), pltpu.VMEM((D_EXPERT, D_MODEL), BF16)],
    )
    return pl.pallas_call(
        _moe_kernel,
        grid_spec=grid_spec,
        out_shape=jax.ShapeDtypeStruct((n_rows, D_MODEL), BF16),
        compiler_params=_cparams("arbitrary"),
        name="moe_experts",
    )(tile_e, nused, xs, w_up, w_down)


def _combine_kernel(rows_ref, local_ref, global_ref, ys_ref, h1_ref, route_ref, p_ref,
                    g_ref, wg_ref, wp_ref, o_ref, gath_s, sem):
    i = pl.program_id(0)
    tm = TM_TOK

    @pl.when(i == 0)
    def _():
        gath_s[...] = jnp.zeros_like(gath_s)

    def run(act):
        def per_expert(e, carry):
            def piece(local_row, global_row, size):
                act(pltpu.make_async_copy(ys_ref.at[pl.ds(global_row, size)], gath_s.at[pl.ds(local_row, size)], sem))
            _for_segment_pieces(rows_ref, local_ref, global_ref, i * N_EXPERTS + e, piece)
            return carry
        lax.fori_loop(0, N_EXPERTS, per_expert, 0)

    run(lambda c: c.start())
    run(lambda c: c.wait())

    route = route_ref[...]
    col = lax.broadcasted_iota(jnp.int32, (tm, SORT_ROWS), 1).astype(F32)
    weights = (jnp.where(col == route[:, 2:3], route[:, 4:5], 0.0)
               + jnp.where(col == route[:, 3:4], route[:, 5:6], 0.0)).astype(BF16)
    y = jnp.dot(weights, gath_s[...], preferred_element_type=F32)
    h2 = h1_ref[...] + y
    ms = jnp.mean(h2 * h2, axis=-1, keepdims=True)
    n = (h2 * lax.rsqrt(ms + EPS) * g_ref[...]).astype(BF16)
    z = jnp.dot(n, wg_ref[...], preferred_element_type=F32)
    gate = 1.0 / (1.0 + jnp.exp(-z))
    ple = jnp.dot(p_ref[...].astype(BF16), wp_ref[...], preferred_element_type=F32)
    o_ref[...] = h2 + gate * ple


def _combine_ple(seg_rows, seg_local, seg_global, ys, h1, route, p2, g_ple, w_gate, w_proj):
    T = h1.shape[0]
    tm = TM_TOK
    row = lambda w: pl.BlockSpec((tm, w), lambda i, *_: (i, 0))
    const = lambda a, b: pl.BlockSpec((a, b), lambda i, *_: (0, 0))
    grid_spec = pltpu.PrefetchScalarGridSpec(
        num_scalar_prefetch=3,
        grid=(T // tm,),
        in_specs=[pl.BlockSpec(memory_space=pl.ANY), row(D_MODEL), row(SUBLANES), row(PLE_DIM),
                  const(1, D_MODEL), const(D_MODEL, D_MODEL), const(PLE_DIM, D_MODEL)],
        out_specs=row(D_MODEL),
        scratch_shapes=[pltpu.VMEM((SORT_ROWS, D_MODEL), BF16), pltpu.SemaphoreType.DMA],
    )
    return pl.pallas_call(
        _combine_kernel,
        grid_spec=grid_spec,
        out_shape=jax.ShapeDtypeStruct((T, D_MODEL), F32),
        compiler_params=_cparams("arbitrary"),
        name="combine_ple",
    )(seg_rows, seg_local, seg_global, ys, h1, route, p2, g_ple, w_gate, w_proj)


def _rope_tables(positions):
    half = ROT_DIM // 2
    inv = ROPE_THETA ** (-jnp.arange(0, ROT_DIM, 2, dtype=F32) / ROT_DIM)
    ang = positions.astype(F32)[..., None] * inv
    cos, sin = jnp.cos(ang), jnp.sin(ang)
    ones = jnp.ones(ang.shape[:-1] + (HEAD_DIM - ROT_DIM,), F32)
    cos_h = jnp.concatenate([cos, cos, ones], axis=-1)
    sin_h = jnp.concatenate([-sin, sin, 0.0 * ones], axis=-1)
    return jnp.tile(cos_h, (1, 1, 2)), jnp.tile(sin_h, (1, 1, 2))


def _layer(i, h, p, cos_t, sin_t, g_mix, w_in, b_f, qn_a, kn_a, qn_b, kn_b, w_o, g_ffn, w_rg, b_rg,
           w_re, b_re, w_up, w_down, g_ple, w_ple_gate, w_ple_proj):
    B, S, _ = h.shape
    T = B * S
    x2 = h.reshape(T, D_MODEL)

    w_qkv = w_in[i][:, :N_QKV].astype(BF16)
    wf = jnp.zeros((D_MODEL, LANES), F32).at[:, :N_HEADS].set(w_in[i][:, N_QKV:]).astype(BF16)
    bf = jnp.zeros((1, LANES), F32).at[0, :N_HEADS].set(b_f[i])
    pair = lambda g: jnp.tile(g, 2).reshape(1, LANES)
    w_r = jnp.zeros((D_MODEL, LANES), F32)
    w_r = w_r.at[:, :N_GROUPS].set(w_rg[i])
    w_r = w_r.at[:, ROUTER_LANE0:ROUTER_LANE0 + N_EXPERTS].set(
        jnp.transpose(w_re[i], (1, 0, 2)).reshape(D_MODEL, N_EXPERTS))
    b_r = jnp.zeros((1, LANES), F32).at[0, :N_GROUPS].set(b_rg[i])
    b_r = b_r.at[0, ROUTER_LANE0:ROUTER_LANE0 + N_EXPERTS].set(b_re[i].reshape(-1))

    qkv, f = _in_proj(x2, g_mix[i].reshape(1, -1), w_qkv, wf)
    kbias = _forget_scan(f, bf, B, S)
    qkv3 = qkv.reshape(B, S, N_QKV)
    oa = _dilated_attn(qkv3, cos_t, sin_t, pair(qn_a[i]), pair(kn_a[i]))
    ob = _fox_attn(qkv3, kbias, pair(qn_b[i]), pair(kn_b[i]))

    h1, msort, route, cnt = _out_router(oa.reshape(T, D_SEC), ob.reshape(T, D_SEC), x2, w_o[i].astype(BF16),
                                        g_ffn[i].reshape(1, -1), w_r.astype(BF16), b_r)

    n_tok_tiles = T // TM_TOK
    counts = cnt[:, 0, ROUTER_LANE0:ROUTER_LANE0 + N_EXPERTS].astype(jnp.int32)
    seg_rows = (counts + SEG_ALIGN - 1) // SEG_ALIGN * SEG_ALIGN
    seg_local = jnp.cumsum(seg_rows, axis=1) - seg_rows
    rows_e = jnp.sum(seg_rows, axis=0)
    tile_end = jnp.cumsum((rows_e + TM_MOE - 1) // TM_MOE)
    off = jnp.concatenate([jnp.zeros((1,), jnp.int32), tile_end * TM_MOE]).astype(jnp.int32)
    seg_global = off[None, :N_EXPERTS] + jnp.cumsum(seg_rows, axis=0) - seg_rows
    n_tiles = (2 * T + n_tok_tiles * N_EXPERTS * (SEG_ALIGN - 1)) // TM_MOE + N_EXPERTS
    nused = tile_end[-1:].astype(jnp.int32)
    tile_ids = jnp.minimum(jnp.arange(n_tiles, dtype=jnp.int32), nused[0] - 1)
    tile_e = jnp.sum((tile_ids[:, None] >= tile_end[None, :]).astype(jnp.int32), axis=1).astype(jnp.int32)
    seg = tuple(a.reshape(-1).astype(jnp.int32) for a in (seg_rows, seg_local, seg_global))

    xs = _dispatch(*seg, off, rows_e.astype(jnp.int32), msort, n_tiles * TM_MOE)
    ys = _moe_experts(tile_e, nused, xs, w_up[i], w_down[i])
    out = _combine_ple(*seg, ys, h1, route, p[i].reshape(T, PLE_DIM),
                       g_ple[i].reshape(1, -1), w_ple_gate[i].astype(BF16), w_ple_proj[i].astype(BF16))
    return out.reshape(B, S, D_MODEL)


def kernel(x, p, positions, g_mix, w_in, b_f, qn_a, kn_a, qn_b, kn_b, w_o, g_ffn, w_rg, b_rg, w_re, b_re,
           w_up, w_down, g_ple, w_ple_gate, w_ple_proj):
    cos_t, sin_t = _rope_tables(positions)
    h = x
    for i in range(p.shape[0]):
        h = _layer(i, h, p, cos_t, sin_t, g_mix, w_in, b_f, qn_a, kn_a, qn_b, kn_b, w_o, g_ffn, w_rg, b_rg,
                   w_re, b_re, w_up, w_down, g_ple, w_ple_gate, w_ple_proj)
    return h
```

```python
import functools
import math

import jax
import jax.numpy as jnp
from jax import lax
from jax.experimental import pallas as pl
from jax.experimental.pallas import tpu as pltpu

D_MODEL = 1024
HEAD_DIM = 64
N_HEADS = 8
D_SEC = N_HEADS * HEAD_DIM
N_QKV = 6 * D_SEC
ROT_DIM = HEAD_DIM // 4
ROPE_THETA = 500000.0
N_GROUPS = 4
EXPERTS_PER_GROUP = 8
N_EXPERTS = N_GROUPS * EXPERTS_PER_GROUP
D_EXPERT = 512
PLE_DIM = 256
EPS = 1e-6
NEG = -1e30
WINDOW = 128

LANES = 128
SUBLANES = 8
VMEM_LIMIT = 48 * 1024 * 1024

TM_PROJ = 512
TQ = 128
TK = 256
N_PAIRS = N_HEADS // 2
TM_MOE = 256
TM_TOK = 512
SEG_ALIGN = 2 * SUBLANES
SORT_ROWS = 2 * TM_TOK + N_EXPERTS * SEG_ALIGN
SEG_SIZES = tuple(TM_TOK >> k for k in range(TM_TOK.bit_length()) if TM_TOK >> k >= SEG_ALIGN)
ROUTER_LANE0 = N_GROUPS

Q_SCALE_LOG2 = math.log2(math.e) / math.sqrt(HEAD_DIM)

F32 = jnp.float32
BF16 = jnp.bfloat16
NT_DIMS = (((1,), (1,)), ((), ()))


def _cparams(*sem):
    return pltpu.CompilerParams(dimension_semantics=sem, vmem_limit_bytes=VMEM_LIMIT)


def _in_proj_kernel(x_ref, g_ref, w_ref, wf_ref, qkv_ref, f_ref):
    x = x_ref[...]
    ms = jnp.mean(x * x, axis=-1, keepdims=True)
    a = (x * lax.rsqrt(ms + EPS) * g_ref[...]).astype(BF16)
    qkv_ref[...] = jnp.dot(a, w_ref[...], preferred_element_type=F32).astype(BF16)
    f_ref[...] = jnp.dot(a, wf_ref[...], preferred_element_type=F32)


def _in_proj(x2, g_mix, w_qkv, wf):
    T = x2.shape[0]
    return pl.pallas_call(
        _in_proj_kernel,
        grid=(T // TM_PROJ,),
        in_specs=[
            pl.BlockSpec((TM_PROJ, D_MODEL), lambda i: (i, 0)),
            pl.BlockSpec((1, D_MODEL), lambda i: (0, 0)),
            pl.BlockSpec((D_MODEL, N_QKV), lambda i: (0, 0)),
            pl.BlockSpec((D_MODEL, LANES), lambda i: (0, 0)),
        ],
        out_specs=[
            pl.BlockSpec((TM_PROJ, N_QKV), lambda i: (i, 0)),
            pl.BlockSpec((TM_PROJ, LANES), lambda i: (i, 0)),
        ],
        out_shape=[
            jax.ShapeDtypeStruct((T, N_QKV), BF16),
            jax.ShapeDtypeStruct((T, LANES), F32),
        ],
        compiler_params=_cparams("parallel"),
        name="in_proj",
    )(x2, g_mix, w_qkv, wf)


BIAS_TERMS = 3


def _bias_lane(head):
    return (head // 2) * LANES + (HEAD_DIM if head % 2 == 0 else 0)


def _forget_scan_kernel(f_ref, bf_ref, kb_ref, *, seq):
    f = f_ref[...] + bf_ref[...]
    c = jnp.minimum(f, 0.0) - jnp.log1p(jnp.exp(-jnp.abs(f)))
    row = lax.broadcasted_iota(jnp.int32, c.shape, 0)
    k = 1
    while k < seq:
        c = c + jnp.where(row >= k, pltpu.roll(c, k, axis=0), 0.0)
        k *= 2
    rest = c * (-math.log2(math.e))
    r_idx = lax.broadcasted_iota(jnp.int32, (LANES, D_SEC), 0)
    c_idx = lax.broadcasted_iota(jnp.int32, (LANES, D_SEC), 1)
    base = (r_idx // 2) * LANES + jnp.where(r_idx % 2 == 0, HEAD_DIM, 0)
    out = jnp.zeros((seq, D_SEC), F32)
    for t in range(BIAS_TERMS):
        term = rest.astype(BF16)
        rest = rest - term.astype(F32)
        place = ((r_idx < N_HEADS) & (c_idx == base + t)).astype(BF16)
        out = out + jnp.dot(term, place, preferred_element_type=F32)
    kb_ref[0] = out.astype(BF16)


def _forget_scan(f, bf, batch, seq):
    return pl.pallas_call(
        functools.partial(_forget_scan_kernel, seq=seq),
        grid=(batch,),
        in_specs=[
            pl.BlockSpec((seq, LANES), lambda b: (b, 0)),
            pl.BlockSpec((1, LANES), lambda b: (0, 0)),
        ],
        out_specs=pl.BlockSpec((1, seq, D_SEC), lambda b: (b, 0, 0)),
        out_shape=jax.ShapeDtypeStruct((batch, seq, D_SEC), BF16),
        compiler_params=_cparams("parallel"),
        name="forget_scan",
    )(f, bf)


def _head_sumsq_matrix():
    r = lax.broadcasted_iota(jnp.int32, (LANES, LANES), 0) // HEAD_DIM
    c = lax.broadcasted_iota(jnp.int32, (LANES, LANES), 1) // HEAD_DIM
    return (r == c).astype(BF16)


def _qk_norm(x, gain, gmat):
    ss = jnp.dot((x * x).astype(BF16), gmat, preferred_element_type=F32)
    return x * lax.rsqrt(ss * (1.0 / HEAD_DIM) + EPS) * gain


def _stack_heads(qb, head0):
    zero = jnp.zeros_like(qb)
    return jnp.concatenate([jnp.where(head0, qb, zero), jnp.where(head0, zero, qb)], axis=0)


def _unstack(a, head0):
    return jnp.where(head0, a[:TQ], a[TQ:])


def _dilated_kernel(q_ref, k_ref, v_ref, cos_ref, sin_ref, gq_ref, gk_ref, o_ref,
                    qn_s, kn_s, v_s, acc_s, m_s, l_s, bias_s, *, seq):
    gmat = _head_sumsq_matrix()
    lane = lax.broadcasted_iota(jnp.int32, (TQ, LANES), 1)
    head0 = lane < HEAD_DIM
    chunk = 512
    lane_c = lax.broadcasted_iota(jnp.int32, (chunk, LANES), 1) % HEAD_DIM
    first_half = lane_c < ROT_DIM // 2

    def rope(t, cs, sn):
        partner = jnp.where(first_half, pltpu.roll(t, LANES - ROT_DIM // 2, axis=1),
                            pltpu.roll(t, ROT_DIM // 2, axis=1))
        return t * cs + partner * sn

    for c0 in range(0, seq, chunk):
        rows = pl.ds(c0, chunk)
        cs = cos_ref[0, rows, :]
        sn = sin_ref[0, rows, :]
        qn = _qk_norm(q_ref[0, rows, :].astype(F32), gq_ref[...], gmat)
        qn_s[rows, :] = rope(qn, cs, sn) * Q_SCALE_LOG2
        kn = _qk_norm(k_ref[0, rows, :].astype(F32), gk_ref[...], gmat)
        kn_s[rows, :] = rope(kn, cs, sn)
        v_s[rows, :] = v_ref[0, rows, :].astype(F32)

    u = lax.broadcasted_iota(jnp.int32, (2 * TQ, 2 * TQ), 0) % TQ
    c = lax.broadcasted_iota(jnp.int32, (2 * TQ, 2 * TQ), 1)
    for slot, dist0 in enumerate((0, TQ)):
        dist = dist0 + u - c
        bias_s[slot] = jnp.where((dist >= 0) & (dist <= WINDOW), 0.0, NEG)

    def scores(blk):
        p, q_rows, k_rows, nk, dist0 = blk
        qst = _stack_heads(qn_s[q_rows, :].astype(BF16), head0)
        s = lax.dot_general(qst, kn_s[k_rows, :].astype(BF16), NT_DIMS, preferred_element_type=F32)
        return s + bias_s[dist0 // TQ, :, 0:nk]

    def finish(blk, s):
        p, q_rows, k_rows, nk, dist0 = blk
        m = jnp.max(s, axis=-1, keepdims=True)
        pr = jnp.exp2(s - m)
        l = jnp.sum(pr, axis=-1, keepdims=True)
        acc = jnp.dot(pr.astype(BF16), v_s[k_rows, :].astype(BF16), preferred_element_type=F32)
        acc_s[p, q_rows, :] = _unstack(acc, head0)
        m_s[p, q_rows, :] = _unstack(jnp.broadcast_to(m, (2 * TQ, LANES)), head0)
        l_s[p, q_rows, :] = _unstack(jnp.broadcast_to(l, (2 * TQ, LANES)), head0)

    blocks = []
    for i in range(seq // TQ):
        kb0 = max(i - 1, 0)
        blocks.append((0, pl.ds(i * TQ, TQ), pl.ds(kb0 * TQ, 2 * TQ), 2 * TQ, (i - kb0) * TQ))
    for r in range(4):
        for n in range(seq // (4 * TQ)):
            kb0 = max(n - 1, 0)
            blocks.append((1, pl.ds(4 * TQ * n + r, TQ, stride=4), pl.ds(4 * TQ * kb0 + r, 2 * TQ, stride=4),
                           2 * TQ, (n - kb0) * TQ))
    for r in range(16):
        rows = pl.ds(r, TQ, stride=16)
        blocks.append((2, rows, rows, TQ, 0))

    ahead = 2
    pending = [scores(b) for b in blocks[:ahead]]
    for idx, b in enumerate(blocks):
        s = pending.pop(0)
        if idx + ahead < len(blocks):
            pending.append(scores(blocks[idx + ahead]))
        finish(b, s)

    for c0 in range(0, seq, chunk):
        rows = pl.ds(c0, chunk)
        m0, m1, m2 = m_s[0, rows, :], m_s[1, rows, :], m_s[2, rows, :]
        mm = jnp.maximum(jnp.maximum(m0, m1), m2)
        e0, e1, e2 = jnp.exp2(m0 - mm), jnp.exp2(m1 - mm), jnp.exp2(m2 - mm)
        num = acc_s[0, rows, :] * e0 + acc_s[1, rows, :] * e1 + acc_s[2, rows, :] * e2
        den = l_s[0, rows, :] * e0 + l_s[1, rows, :] * e1 + l_s[2, rows, :] * e2
        o_ref[0, rows, :] = (num / den).astype(BF16)


def _dilated_attn(qkv3, cos_t, sin_t, gq, gk):
    batch, seq, _ = qkv3.shape
    sec = D_SEC // LANES
    blk = lambda s: pl.BlockSpec((1, seq, LANES), lambda b, h, s=s: (b, 0, s * sec + h))
    tab = pl.BlockSpec((1, seq, LANES), lambda b, h: (b, 0, 0))
    vec = pl.BlockSpec((1, LANES), lambda b, h: (0, 0))
    return pl.pallas_call(
        functools.partial(_dilated_kernel, seq=seq),
        grid=(batch, N_PAIRS),
        in_specs=[blk(0), blk(1), blk(2), tab, tab, vec, vec],
        out_specs=pl.BlockSpec((1, seq, LANES), lambda b, h: (b, 0, h)),
        out_shape=jax.ShapeDtypeStruct((batch, seq, D_SEC), BF16),
        scratch_shapes=[
            pltpu.VMEM((seq, LANES), F32), pltpu.VMEM((seq, LANES), F32), pltpu.VMEM((seq, LANES), F32),
            pltpu.VMEM((3, seq, LANES), F32), pltpu.VMEM((3, seq, LANES), F32),
            pltpu.VMEM((3, seq, LANES), F32), pltpu.VMEM((2, 2 * TQ, 2 * TQ), F32),
        ],
        compiler_params=_cparams("parallel", "parallel"),
        name="dilated_attn",
    )(qkv3, qkv3, qkv3, cos_t, sin_t, gq, gk)


def _fox_kernel(q_ref, k_ref, v_ref, kb_ref, gq_ref, gk_ref, o_ref, qat_s, ka_s, vt_s, acc_s, tri_s, *, seq):
    gmat = _head_sumsq_matrix()
    chunk = 512
    lane = lax.broadcasted_iota(jnp.int32, (chunk, LANES), 1)
    low = lane < HEAD_DIM
    sel = jnp.where((lane >= HEAD_DIM) & (lane < HEAD_DIM + BIAS_TERMS), 1.0, 0.0)

    for c0 in range(0, seq, chunk):
        rows = pl.ds(c0, chunk)
        for g in range(N_PAIRS):
            lanes = slice(g * LANES, (g + 1) * LANES)
            qn = _qk_norm(q_ref[0, rows, lanes].astype(F32), gq_ref[...], gmat) * Q_SCALE_LOG2
            for h, qa in ((2 * g, jnp.where(low, qn, sel)),
                          (2 * g + 1, jnp.where(low, pltpu.roll(qn, HEAD_DIM, axis=1), sel))):
                qt = qa.T
                for cc in range(chunk // TK):
                    qat_s[c0 // TK + cc, h * LANES:(h + 1) * LANES, :] = qt[:, cc * TK:(cc + 1) * TK].astype(BF16)
            kn =_qk_norm(k_ref[0, rows, lanes].astype(F32), gk_ref[...], gmat)
            kb = kb_ref[0, rows, lanes].astype(F32)
            ka_s[2 * g, rows, :] = jnp.where(low, kn, kb).astype(BF16)
            ka_s[2 * g + 1, rows, :] = pltpu.roll(jnp.where(low, kb, kn), HEAD_DIM, axis=1).astype(BF16)
            vt = v_ref[0, rows, lanes].astype(F32).T
            for cc in range(chunk // TK):
                vt_s[c0 // TK + cc, lanes, :] = vt[:, cc * TK:(cc + 1) * TK].astype(BF16)

    r = lax.broadcasted_iota(jnp.int32, (TK, TK), 0)
    c = lax.broadcasted_iota(jnp.int32, (TK, TK), 1)
    tri_s[...] = jnp.where(r <= c, 0.0, NEG)

    def q_block(qi, carry):
        q_rows = pl.ds(pl.multiple_of(qi * TK, TK), TK)
        for h in range(N_HEADS):
            acc_s[h] = jnp.zeros((HEAD_DIM, TK), F32)

        def step(j, st, diagonal):
            k_rows = pl.ds(pl.multiple_of(j * TK, TK), TK)

            def scores(h):
                s = jnp.dot(ka_s[h, k_rows, :], qat_s[qi, h * LANES:(h + 1) * LANES, :], preferred_element_type=F32)
                return s + tri_s[...] if diagonal else s

            ahead = 2
            pending = [scores(h) for h in range(ahead)]
            new = []
            for h in range(N_HEADS):
                m, l = st[h]
                s = pending.pop(0)
                if h + ahead < N_HEADS:
                    pending.append(scores(h + ahead))
                m_new = jnp.maximum(m, jnp.max(s, axis=0, keepdims=True))
                alpha = jnp.exp2(m - m_new)
                pr = jnp.exp2(s - m_new)
                l_new = alpha * l + jnp.sum(pr, axis=0, keepdims=True)
                pv = jnp.dot(vt_s[j, h * HEAD_DIM:(h + 1) * HEAD_DIM, :], pr.astype(BF16),
                             preferred_element_type=F32)
                acc_s[h] = alpha * acc_s[h] + pv
                new.append((m_new, l_new))
            return tuple(new)

        init = tuple((jnp.full((1, TK), NEG, F32), jnp.zeros((1, TK), F32)) for _ in range(N_HEADS))
        st = lax.fori_loop(0, qi, lambda j, st: step(j, st, False), init)
        st = step(qi, st, True)
        for g in range(N_PAIRS):
            o2 = jnp.concatenate([acc_s[2 * g] / st[2 * g][1], acc_s[2 * g + 1] / st[2 * g + 1][1]], axis=0)
            o_ref[0, q_rows, g * LANES:(g + 1) * LANES] = o2.T.astype(BF16)
        return carry

    lax.fori_loop(0, seq // TK, q_block, 0)


def _fox_attn(qkv3, kbias, gq, gk):
    batch, seq, _ = qkv3.shape
    blk = lambda s: pl.BlockSpec((1, seq, D_SEC), lambda b, s=s: (b, 0, s))
    vec = pl.BlockSpec((1, LANES), lambda b: (0, 0))
    return pl.pallas_call(
        functools.partial(_fox_kernel, seq=seq),
        grid=(batch,),
        in_specs=[blk(3), blk(4), blk(5), pl.BlockSpec((1, seq, D_SEC), lambda b: (b, 0, 0)), vec, vec],
        out_specs=pl.BlockSpec((1, seq, D_SEC), lambda b: (b, 0, 0)),
        out_shape=jax.ShapeDtypeStruct((batch, seq, D_SEC), BF16),
        scratch_shapes=[
            pltpu.VMEM((seq // TK, N_HEADS * LANES, TK), BF16),
            pltpu.VMEM((N_HEADS, seq, LANES), BF16),
            pltpu.VMEM((seq // TK, D_SEC, TK), BF16),
            pltpu.VMEM((N_HEADS, HEAD_DIM, TK), F32),
            pltpu.VMEM((TK, TK), F32),
        ],
        compiler_params=_cparams("parallel"),
        name="fox_attn",
    )(qkv3, qkv3, qkv3, kbias, gq, gk)


def _out_router_kernel(oa_ref, ob_ref, x_ref, wo_ref, g_ref, wr_ref, br_ref,
                       h1_ref, msort_ref, route_ref, cnt_ref):
    tm = TM_TOK
    h1 = (x_ref[...]
          + jnp.dot(oa_ref[...], wo_ref[0:D_SEC, :], preferred_element_type=F32)
          + jnp.dot(ob_ref[...], wo_ref[D_SEC:2 * D_SEC, :], preferred_element_type=F32))
    h1_ref[...] = h1
    ms = jnp.mean(h1 * h1, axis=-1, keepdims=True)
    mn = h1 * lax.rsqrt(ms + EPS) * g_ref[...]

    logits = jnp.dot(mn.astype(BF16), wr_ref[...], preferred_element_type=F32) + br_ref[...]
    lane = lax.broadcasted_iota(jnp.int32, (tm, LANES), 1).astype(F32)
    big = float(LANES)

    def first_argmax(vals):
        vmax = jnp.max(vals, axis=-1, keepdims=True)
        idx = jnp.min(jnp.where(vals == vmax, lane, big), axis=-1, keepdims=True)
        return vmax, idx

    lg = jnp.where(lane < N_GROUPS, logits, -jnp.inf)
    gmax, gidx = first_argmax(lg)
    gw = 1.0 / jnp.sum(jnp.exp(lg - gmax), axis=-1, keepdims=True)
    lo = ROUTER_LANE0 + EXPERTS_PER_GROUP * gidx
    le = jnp.where((lane >= lo) & (lane < lo + EXPERTS_PER_GROUP), logits, -jnp.inf)
    v0, i0 = first_argmax(le)
    v1, i1 = first_argmax(jnp.where(lane == i0, -jnp.inf, le))
    ex = jnp.exp(v1 - v0)
    w0 = gw / (1.0 + ex)
    w1 = gw * ex / (1.0 + ex)

    sel0 = lane == i0
    sel1 = lane == i1
    onehot = jnp.where(sel0 | sel1, 1.0, 0.0)
    r = lax.broadcasted_iota(jnp.int32, (tm, tm), 0)
    c = lax.broadcasted_iota(jnp.int32, (tm, tm), 1)
    ltri = (c <= r).astype(BF16)
    incl = jnp.dot(ltri, onehot.astype(BF16), preferred_element_type=F32)
    excl = incl - onehot
    counts = incl[tm - 1:tm, :]
    cnt_ref[0] = jnp.broadcast_to(counts, (SUBLANES, LANES))

    seg_rows = jnp.floor((counts + (SEG_ALIGN - 1.0)) * (1.0 / SEG_ALIGN)) * SEG_ALIGN
    seg_rows = jnp.broadcast_to(seg_rows, (tm, LANES))
    lp0 = jnp.sum(jnp.where(lane < i0, seg_rows, 0.0) + jnp.where(sel0, excl, 0.0), axis=-1, keepdims=True)
    lp1 = jnp.sum(jnp.where(lane < i1, seg_rows, 0.0) + jnp.where(sel1, excl, 0.0), axis=-1, keepdims=True)

    rec = jnp.zeros((tm, LANES), F32)
    for pos, val in enumerate((i0 - ROUTER_LANE0, i1 - ROUTER_LANE0, lp0, lp1, w0, w1)):
        rec = jnp.where(lane == float(pos), val, rec)
    route_ref[...] = rec[:, 0:SUBLANES]

    rec_t = rec.T
    out_row = lax.broadcasted_iota(jnp.int32, (SORT_ROWS, tm), 0).astype(F32)
    pick = jnp.where((out_row == rec_t[2:3, :]) | (out_row == rec_t[3:4, :]), 1.0, 0.0).astype(BF16)
    msort_ref[...] = jnp.dot(pick, mn.astype(BF16), preferred_element_type=F32).astype(BF16)


def _out_router(oa, ob, x2, w_o, g_ffn, w_r, b_r):
    T = x2.shape[0]
    tm = TM_TOK
    row = lambda w: pl.BlockSpec((tm, w), lambda i: (i, 0))
    const = lambda a, b: pl.BlockSpec((a, b), lambda i: (0, 0))
    return pl.pallas_call(
        _out_router_kernel,
        grid=(T // tm,),
        in_specs=[row(D_SEC), row(D_SEC), row(D_MODEL), const(D_MODEL, D_MODEL), const(1, D_MODEL),
                  const(D_MODEL, LANES), const(1, LANES)],
        out_specs=[row(D_MODEL), pl.BlockSpec((SORT_ROWS, D_MODEL), lambda i: (i, 0)), row(SUBLANES),
                   pl.BlockSpec((1, SUBLANES, LANES), lambda i: (i, 0, 0))],
        out_shape=[
            jax.ShapeDtypeStruct((T, D_MODEL), F32),
            jax.ShapeDtypeStruct((T // tm * SORT_ROWS, D_MODEL), BF16),
            jax.ShapeDtypeStruct((T, SUBLANES), F32),
            jax.ShapeDtypeStruct((T // tm, SUBLANES, LANES), F32),
        ],
        compiler_params=_cparams("parallel"),
        name="out_router",
    )(oa, ob, x2, w_o, g_ffn, w_r, b_r)


def _for_segment_pieces(rows_ref, local_ref, global_ref, idx, fn):
    n = rows_ref[idx]
    for size in SEG_SIZES:
        done = n & (-2 * size)

        @pl.when((n & size) != 0)
        def _(done=done, size=size):
            fn(pl.multiple_of(local_ref[idx] + done, SEG_ALIGN), pl.multiple_of(global_ref[idx] + done, SEG_ALIGN),
               size)


def _dispatch_kernel(rows_ref, local_ref, global_ref, off_ref, cnt_ref, msort_ref, xs_ref, zero_s, sem, zsem):
    t = pl.program_id(0)

    def run(act):
        def per_expert(e, carry):
            def piece(local_row, global_row, size):
                act(pltpu.make_async_copy(msort_ref.at[pl.ds(local_row, size)],
                                          xs_ref.at[pl.ds(global_row, size)], sem))
            _for_segment_pieces(rows_ref, local_ref, global_ref, t * N_EXPERTS + e, piece)
            return carry
        lax.fori_loop(0, N_EXPERTS, per_expert, 0)

    def zero_fill(act):
        def tile_copy(t):
            act(pltpu.make_async_copy(zero_s, xs_ref.at[pl.ds(pl.multiple_of(t * TM_MOE, TM_MOE), TM_MOE)], zsem))

        def per_expert(e, carry):
            @pl.when(cnt_ref[e] % TM_MOE != 0)
            def _():
                tile_copy(off_ref[e + 1] // TM_MOE - 1)
            return carry
        lax.fori_loop(0, N_EXPERTS, per_expert, 0)

        def per_tile(t, carry):
            tile_copy(t)
            return carry
        lax.fori_loop(off_ref[N_EXPERTS] // TM_MOE, xs_ref.shape[0] // TM_MOE, per_tile, 0)

    @pl.when(t == 0)
    def _():
        zero_s[...] = jnp.zeros_like(zero_s)
        zero_fill(lambda c: c.start())
        zero_fill(lambda c: c.wait())

    run(lambda c: c.start())
    run(lambda c: c.wait())


def _dispatch(seg_rows, seg_local, seg_global, off, cnt, msort, n_rows):
    grid_spec = pltpu.PrefetchScalarGridSpec(
        num_scalar_prefetch=5,
        grid=(msort.shape[0] // SORT_ROWS,),
        in_specs=[pl.BlockSpec((SORT_ROWS, D_MODEL), lambda t, *_: (t, 0))],
        out_specs=pl.BlockSpec(memory_space=pl.ANY),
        scratch_shapes=[pltpu.VMEM((TM_MOE, D_MODEL), BF16), pltpu.SemaphoreType.DMA, pltpu.SemaphoreType.DMA],
    )
    return pl.pallas_call(
        _dispatch_kernel,
        grid_spec=grid_spec,
        out_shape=jax.ShapeDtypeStruct((n_rows, D_MODEL), BF16),
        compiler_params=_cparams("arbitrary"),
        name="dispatch",
    )(seg_rows, seg_local, seg_global, off, cnt, msort)


def _moe_kernel(tile_e_ref, nused_ref, xs_ref, wup_ref, wdn_ref, ys_ref, wup_s, wdn_s):
    i = pl.program_id(0)

    @pl.when(i < nused_ref[0])
    def _():
        prev = tile_e_ref[jnp.maximum(i - 1, 0)]

        @pl.when((i == 0) | (tile_e_ref[i] != prev))
        def _():
            wup_s[...] = wup_ref[0].astype(BF16)
            wdn_s[...] = wdn_ref[0].astype(BF16)

        hu = jnp.dot(xs_ref[...], wup_s[...], preferred_element_type=F32)
        gate = hu[:, :D_EXPERT]
        hid = gate * (1.0 / (1.0 + jnp.exp(-gate))) * hu[:, D_EXPERT:]
        ys_ref[...] = jnp.dot(hid.astype(BF16), wdn_s[...], preferred_element_type=F32).astype(BF16)

    @pl.when(i >= nused_ref[0])
    def _():
        ys_ref[...] = jnp.zeros_like(ys_ref)


def _moe_experts(tile_e, nused, xs, w_up, w_down):
    n_rows = xs.shape[0]
    n_tiles = n_rows // TM_MOE
    row_map = lambda i, te, nu: (jnp.minimum(i, nu[0] - 1), 0)
    grid_spec = pltpu.PrefetchScalarGridSpec(
        num_scalar_prefetch=2,
        grid=(n_tiles,),
        in_specs=[
            pl.BlockSpec((TM_MOE, D_MODEL), row_map),
            pl.BlockSpec((1, D_MODEL, 2 * D_EXPERT), lambda i, te, nu: (te[i], 0, 0)),
            pl.BlockSpec((1, D_EXPERT, D_MODEL), lambda i, te, nu: (te[i], 0, 0)),
        ],
        out_specs=pl.BlockSpec((TM_MOE, D_MODEL), lambda i, te, nu: (i, 0)),
        scratch_shapes=[pltpu.VMEM((D_MODEL, 2 * D_EXPERT), BF16), pltpu.VMEM((D_EXPERT, D_MODEL), BF16)],
    )
    return pl.pallas_call(
        _moe_kernel,
        grid_spec=grid_spec,
        out_shape=jax.ShapeDtypeStruct((n_rows, D_MODEL), BF16),
        compiler_params=_cparams("arbitrary"),
        name="moe_experts",
    )(tile_e, nused, xs, w_up, w_down)


def _combine_kernel(rows_ref, local_ref, global_ref, ys_ref, h1_ref, route_ref, p_ref,
                    g_ref, wg_ref, wp_ref, o_ref, gath_s, sem):
    i = pl.program_id(0)
    tm = TM_TOK

    @pl.when(i == 0)
    def _():
        gath_s[...] = jnp.zeros_like(gath_s)

    def run(act):
        def per_expert(e, carry):
            def piece(local_row, global_row, size):
                act(pltpu.make_async_copy(ys_ref.at[pl.ds(global_row, size)], gath_s.at[pl.ds(local_row, size)], sem))
            _for_segment_pieces(rows_ref, local_ref, global_ref, i * N_EXPERTS + e, piece)
            return carry
        lax.fori_loop(0, N_EXPERTS, per_expert, 0)

    run(lambda c: c.start())
    run(lambda c: c.wait())

    route = route_ref[...]
    col = lax.broadcasted_iota(jnp.int32, (tm, SORT_ROWS), 1).astype(F32)
    weights = (jnp.where(col == route[:, 2:3], route[:, 4:5], 0.0)
               + jnp.where(col == route[:, 3:4], route[:, 5:6], 0.0)).astype(BF16)
    y = jnp.dot(weights, gath_s[...], preferred_element_type=F32)
    h2 = h1_ref[...] + y
    ms = jnp.mean(h2 * h2, axis=-1, keepdims=True)
    n = (h2 * lax.rsqrt(ms + EPS) * g_ref[...]).astype(BF16)
    z = jnp.dot(n, wg_ref[...], preferred_element_type=F32)
    gate = 1.0 / (1.0 + jnp.exp(-z))
    ple = jnp.dot(p_ref[...].astype(BF16), wp_ref[...], preferred_element_type=F32)
    o_ref[...] = h2 + gate * ple


def _combine_ple(seg_rows, seg_local, seg_global, ys, h1, route, p2, g_ple, w_gate, w_proj):
    T = h1.shape[0]
    tm = TM_TOK
    row = lambda w: pl.BlockSpec((tm, w), lambda i, *_: (i, 0))
    const = lambda a, b: pl.BlockSpec((a, b), lambda i, *_: (0, 0))
    grid_spec = pltpu.PrefetchScalarGridSpec(
        num_scalar_prefetch=3,
        grid=(T // tm,),
        in_specs=[pl.BlockSpec(memory_space=pl.ANY), row(D_MODEL), row(SUBLANES), row(PLE_DIM),
                  const(1, D_MODEL), const(D_MODEL, D_MODEL), const(PLE_DIM, D_MODEL)],
        out_specs=row(D_MODEL),
        scratch_shapes=[pltpu.VMEM((SORT_ROWS, D_MODEL), BF16), pltpu.SemaphoreType.DMA],
    )
    return pl.pallas_call(
        _combine_kernel,
        grid_spec=grid_spec,
        out_shape=jax.ShapeDtypeStruct((T, D_MODEL), F32),
        compiler_params=_cparams("arbitrary"),
        name="combine_ple",
    )(seg_rows, seg_local, seg_global, ys, h1, route, p2, g_ple, w_gate, w_proj)


def _rope_tables(positions):
    half = ROT_DIM // 2
    inv = ROPE_THETA ** (-jnp.arange(0, ROT_DIM, 2, dtype=F32) / ROT_DIM)
    ang = positions.astype(F32)[..., None] * inv
    cos, sin = jnp.cos(ang), jnp.sin(ang)
    ones = jnp.ones(ang.shape[:-1] + (HEAD_DIM - ROT_DIM,), F32)
    cos_h = jnp.concatenate([cos, cos, ones], axis=-1)
    sin_h = jnp.concatenate([-sin, sin, 0.0 * ones], axis=-1)
    return jnp.tile(cos_h, (1, 1, 2)), jnp.tile(sin_h, (1, 1, 2))


def _layer(i, h, p, cos_t, sin_t, g_mix, w_in, b_f, qn_a, kn_a, qn_b, kn_b, w_o, g_ffn, w_rg, b_rg,
           w_re, b_re, w_up, w_down, g_ple, w_ple_gate, w_ple_proj):
    B, S, _ = h.shape
    T = B * S
    x2 = h.reshape(T, D_MODEL)

    w_qkv = w_in[i][:, :N_QKV].astype(BF16)
    wf = jnp.zeros((D_MODEL, LANES), F32).at[:, :N_HEADS].set(w_in[i][:, N_QKV:]).astype(BF16)
    bf = jnp.zeros((1, LANES), F32).at[0, :N_HEADS].set(b_f[i])
    pair = lambda g: jnp.tile(g, 2).reshape(1, LANES)
    w_r = jnp.zeros((D_MODEL, LANES), F32)
    w_r = w_r.at[:, :N_GROUPS].set(w_rg[i])
    w_r = w_r.at[:, ROUTER_LANE0:ROUTER_LANE0 + N_EXPERTS].set(
        jnp.transpose(w_re[i], (1, 0, 2)).reshape(D_MODEL, N_EXPERTS))
    b_r = jnp.zeros((1, LANES), F32).at[0, :N_GROUPS].set(b_rg[i])
    b_r = b_r.at[0, ROUTER_LANE0:ROUTER_LANE0 + N_EXPERTS].set(b_re[i].reshape(-1))

    qkv, f = _in_proj(x2, g_mix[i].reshape(1, -1), w_qkv, wf)
    kbias = _forget_scan(f, bf, B, S)
    qkv3 = qkv.reshape(B, S, N_QKV)
    oa = _dilated_attn(qkv3, cos_t, sin_t, pair(qn_a[i]), pair(kn_a[i]))
    ob = _fox_attn(qkv3, kbias, pair(qn_b[i]), pair(kn_b[i]))

    h1, msort, route, cnt = _out_router(oa.reshape(T, D_SEC), ob.reshape(T, D_SEC), x2, w_o[i].astype(BF16),
                                        g_ffn[i].reshape(1, -1), w_r.astype(BF16), b_r)

    n_tok_tiles = T // TM_TOK
    counts = cnt[:, 0, ROUTER_LANE0:ROUTER_LANE0 + N_EXPERTS].astype(jnp.int32)
    seg_rows = (counts + SEG_ALIGN - 1) // SEG_ALIGN * SEG_ALIGN
    seg_local = jnp.cumsum(seg_rows, axis=1) - seg_rows
    rows_e = jnp.sum(seg_rows, axis=0)
    tile_end = jnp.cumsum((rows_e + TM_MOE - 1) // TM_MOE)
    off = jnp.concatenate([jnp.zeros((1,), jnp.int32), tile_end * TM_MOE]).astype(jnp.int32)
    seg_global = off[None, :N_EXPERTS] + jnp.cumsum(seg_rows, axis=0) - seg_rows
    n_tiles = (2 * T + n_tok_tiles * N_EXPERTS * (SEG_ALIGN - 1)) // TM_MOE + N_EXPERTS
    nused = tile_end[-1:].astype(jnp.int32)
    tile_ids = jnp.minimum(jnp.arange(n_tiles, dtype=jnp.int32), nused[0] - 1)
    tile_e = jnp.sum((tile_ids[:, None] >= tile_end[None, :]).astype(jnp.int32), axis=1).astype(jnp.int32)
    seg = tuple(a.reshape(-1).astype(jnp.int32) for a in (seg_rows, seg_local, seg_global))

    xs = _dispatch(*seg, off, rows_e.astype(jnp.int32), msort, n_tiles * TM_MOE)
    ys = _moe_experts(tile_e, nused, xs, w_up[i], w_down[i])
    out = _combine_ple(*seg, ys, h1, route, p[i].reshape(T, PLE_DIM),
                       g_ple[i].reshape(1, -1), w_ple_gate[i].astype(BF16), w_ple_proj[i].astype(BF16))
    return out.reshape(B, S, D_MODEL)


def kernel(x, p, positions, g_mix, w_in, b_f, qn_a, kn_a, qn_b, kn_b, w_o, g_ffn, w_rg, b_rg, w_re, b_re,
           w_up, w_down, g_ple, w_ple_gate, w_ple_proj):
    cos_t, sin_t = _rope_tables(positions)
    h = x
    for i in range(p.shape[0]):
        h = _layer(i, h, p, cos_t, sin_t, g_mix, w_in, b_f, qn_a, kn_a, qn_b, kn_b, w_o, g_ffn, w_rg, b_rg,
                   w_re, b_re, w_up, w_down, g_ple, w_ple_gate, w_ple_proj)
    return h
```

```python
import functools
import math

import jax
import jax.numpy as jnp
from jax import lax
from jax.experimental import pallas as pl
from jax.experimental.pallas import tpu as pltpu

D_MODEL = 1024
HEAD_DIM = 64
N_HEADS = 8
D_SEC = N_HEADS * HEAD_DIM
N_QKV = 6 * D_SEC
ROT_DIM = HEAD_DIM // 4
ROPE_THETA = 500000.0
N_GROUPS = 4
EXPERTS_PER_GROUP = 8
N_EXPERTS = N_GROUPS * EXPERTS_PER_GROUP
D_EXPERT = 512
PLE_DIM = 256
EPS = 1e-6
NEG = -1e30
WINDOW = 128

LANES = 128
SUBLANES = 8
VMEM_LIMIT = 48 * 1024 * 1024

TM_PROJ = 512
TQ = 128
TK = 256
N_PAIRS = N_HEADS // 2
TM_MOE = 256
TM_TOK = 512
SEG_ALIGN = 2 * SUBLANES
SORT_ROWS = 2 * TM_TOK + N_EXPERTS * SEG_ALIGN
SORT_CHUNKS = SORT_ROWS // SEG_ALIGN
ROUTER_LANE0 = N_GROUPS

Q_SCALE_LOG2 = math.log2(math.e) / math.sqrt(HEAD_DIM)

F32 = jnp.float32
BF16 = jnp.bfloat16
NT_DIMS = (((1,), (1,)), ((), ()))


def _cparams(*sem):
    return pltpu.CompilerParams(dimension_semantics=sem, vmem_limit_bytes=VMEM_LIMIT)


def _in_proj_kernel(x_ref, g_ref, w_ref, wf_ref, qkv_ref, f_ref):
    x = x_ref[...]
    ms = jnp.mean(x * x, axis=-1, keepdims=True)
    a = (x * lax.rsqrt(ms + EPS) * g_ref[...]).astype(BF16)
    qkv_ref[...] = jnp.dot(a, w_ref[...], preferred_element_type=F32).astype(BF16)
    f_ref[...] = jnp.dot(a, wf_ref[...], preferred_element_type=F32)


def _in_proj(x2, g_mix, w_qkv, wf):
    T = x2.shape[0]
    return pl.pallas_call(
        _in_proj_kernel,
        grid=(T // TM_PROJ,),
        in_specs=[
            pl.BlockSpec((TM_PROJ, D_MODEL), lambda i: (i, 0)),
            pl.BlockSpec((1, D_MODEL), lambda i: (0, 0)),
            pl.BlockSpec((D_MODEL, N_QKV), lambda i: (0, 0)),
            pl.BlockSpec((D_MODEL, LANES), lambda i: (0, 0)),
        ],
        out_specs=[
            pl.BlockSpec((TM_PROJ, N_QKV), lambda i: (i, 0)),
            pl.BlockSpec((TM_PROJ, LANES), lambda i: (i, 0)),
        ],
        out_shape=[
            jax.ShapeDtypeStruct((T, N_QKV), BF16),
            jax.ShapeDtypeStruct((T, LANES), F32),
        ],
        compiler_params=_cparams("parallel"),
        name="in_proj",
    )(x2, g_mix, w_qkv, wf)


BIAS_TERMS = 3


def _bias_lane(head):
    return (head // 2) * LANES + (HEAD_DIM if head % 2 == 0 else 0)


def _forget_scan_kernel(f_ref, bf_ref, kb_ref, *, seq):
    f = f_ref[...] + bf_ref[...]
    c = jnp.minimum(f, 0.0) - jnp.log1p(jnp.exp(-jnp.abs(f)))
    row = lax.broadcasted_iota(jnp.int32, c.shape, 0)
    k = 1
    while k < seq:
        c = c + jnp.where(row >= k, pltpu.roll(c, k, axis=0), 0.0)
        k *= 2
    rest = c * (-math.log2(math.e))
    r_idx = lax.broadcasted_iota(jnp.int32, (LANES, D_SEC), 0)
    c_idx = lax.broadcasted_iota(jnp.int32, (LANES, D_SEC), 1)
    base = (r_idx // 2) * LANES + jnp.where(r_idx % 2 == 0, HEAD_DIM, 0)
    out = jnp.zeros((seq, D_SEC), F32)
    for t in range(BIAS_TERMS):
        term = rest.astype(BF16)
        rest = rest - term.astype(F32)
        place = ((r_idx < N_HEADS) & (c_idx == base + t)).astype(BF16)
        out = out + jnp.dot(term, place, preferred_element_type=F32)
    kb_ref[0] = out.astype(BF16)


def _forget_scan(f, bf, batch, seq):
    return pl.pallas_call(
        functools.partial(_forget_scan_kernel, seq=seq),
        grid=(batch,),
        in_specs=[
            pl.BlockSpec((seq, LANES), lambda b: (b, 0)),
            pl.BlockSpec((1, LANES), lambda b: (0, 0)),
        ],
        out_specs=pl.BlockSpec((1, seq, D_SEC), lambda b: (b, 0, 0)),
        out_shape=jax.ShapeDtypeStruct((batch, seq, D_SEC), BF16),
        compiler_params=_cparams("parallel"),
        name="forget_scan",
    )(f, bf)


def _head_sumsq_matrix():
    r = lax.broadcasted_iota(jnp.int32, (LANES, LANES), 0) // HEAD_DIM
    c = lax.broadcasted_iota(jnp.int32, (LANES, LANES), 1) // HEAD_DIM
    return (r == c).astype(BF16)


def _qk_norm(x, gain, gmat):
    ss = jnp.dot((x * x).astype(BF16), gmat, preferred_element_type=F32)
    return x * lax.rsqrt(ss * (1.0 / HEAD_DIM) + EPS) * gain


def _stack_heads(qb, head0):
    zero = jnp.zeros_like(qb)
    return jnp.concatenate([jnp.where(head0, qb, zero), jnp.where(head0, zero, qb)], axis=0)


def _unstack(a, head0):
    return jnp.where(head0, a[:TQ], a[TQ:])


def _dilated_kernel(q_ref, k_ref, v_ref, cos_ref, sin_ref, gq_ref, gk_ref, o_ref,
                    qn_s, kn_s, v_s, acc_s, m_s, l_s, bias_s, *, seq):
    gmat = _head_sumsq_matrix()
    lane = lax.broadcasted_iota(jnp.int32, (TQ, LANES), 1)
    head0 = lane < HEAD_DIM
    chunk = 512
    lane_c = lax.broadcasted_iota(jnp.int32, (chunk, LANES), 1) % HEAD_DIM
    first_half = lane_c < ROT_DIM // 2

    def rope(t, cs, sn):
        partner = jnp.where(first_half, pltpu.roll(t, LANES - ROT_DIM // 2, axis=1),
                            pltpu.roll(t, ROT_DIM // 2, axis=1))
        return t * cs + partner * sn

    for c0 in range(0, seq, chunk):
        rows = pl.ds(c0, chunk)
        cs = cos_ref[0, rows, :]
        sn = sin_ref[0, rows, :]
        qn = _qk_norm(q_ref[0, rows, :].astype(F32), gq_ref[...], gmat)
        qn_s[rows, :] = rope(qn, cs, sn) * Q_SCALE_LOG2
        kn = _qk_norm(k_ref[0, rows, :].astype(F32), gk_ref[...], gmat)
        kn_s[rows, :] = rope(kn, cs, sn)
        v_s[rows, :] = v_ref[0, rows, :].astype(F32)

    u = lax.broadcasted_iota(jnp.int32, (2 * TQ, 2 * TQ), 0) % TQ
    c = lax.broadcasted_iota(jnp.int32, (2 * TQ, 2 * TQ), 1)
    for slot, dist0 in enumerate((0, TQ)):
        dist = dist0 + u - c
        bias_s[slot] = jnp.where((dist >= 0) & (dist <= WINDOW), 0.0, NEG)

    def scores(blk):
        p, q_rows, k_rows, nk, dist0 = blk
        qst = _stack_heads(qn_s[q_rows, :].astype(BF16), head0)
        s = lax.dot_general(qst, kn_s[k_rows, :].astype(BF16), NT_DIMS, preferred_element_type=F32)
        return s + bias_s[dist0 // TQ, :, 0:nk]

    def finish(blk, s):
        p, q_rows, k_rows, nk, dist0 = blk
        m = jnp.max(s, axis=-1, keepdims=True)
        pr = jnp.exp2(s - m)
        l = jnp.sum(pr, axis=-1, keepdims=True)
        acc = jnp.dot(pr.astype(BF16), v_s[k_rows, :].astype(BF16), preferred_element_type=F32)
        acc_s[p, q_rows, :] = _unstack(acc, head0)
        m_s[p, q_rows, :] = _unstack(jnp.broadcast_to(m, (2 * TQ, LANES)), head0)
        l_s[p, q_rows, :] = _unstack(jnp.broadcast_to(l, (2 * TQ, LANES)), head0)

    blocks = []
    for i in range(seq // TQ):
        kb0 = max(i - 1, 0)
        blocks.append((0, pl.ds(i * TQ, TQ), pl.ds(kb0 * TQ, 2 * TQ), 2 * TQ, (i - kb0) * TQ))
    for r in range(4):
        for n in range(seq // (4 * TQ)):
            kb0 = max(n - 1, 0)
            blocks.append((1, pl.ds(4 * TQ * n + r, TQ, stride=4), pl.ds(4 * TQ * kb0 + r, 2 * TQ, stride=4),
                           2 * TQ, (n - kb0) * TQ))
    for r in range(16):
        rows = pl.ds(r, TQ, stride=16)
        blocks.append((2, rows, rows, TQ, 0))

    ahead = 2
    pending = [scores(b) for b in blocks[:ahead]]
    for idx, b in enumerate(blocks):
        s = pending.pop(0)
        if idx + ahead < len(blocks):
            pending.append(scores(blocks[idx + ahead]))
        finish(b, s)

    for c0 in range(0, seq, chunk):
        rows = pl.ds(c0, chunk)
        m0, m1, m2 = m_s[0, rows, :], m_s[1, rows, :], m_s[2, rows, :]
        mm = jnp.maximum(jnp.maximum(m0, m1), m2)
        e0, e1, e2 = jnp.exp2(m0 - mm), jnp.exp2(m1 - mm), jnp.exp2(m2 - mm)
        num = acc_s[0, rows, :] * e0 + acc_s[1, rows, :] * e1 + acc_s[2, rows, :] * e2
        den = l_s[0, rows, :] * e0 + l_s[1, rows, :] * e1 + l_s[2, rows, :] * e2
        o_ref[0, rows, :] = (num / den).astype(BF16)


def _dilated_attn(qkv3, cos_t, sin_t, gq, gk):
    batch, seq, _ = qkv3.shape
    sec = D_SEC // LANES
    blk = lambda s: pl.BlockSpec((1, seq, LANES), lambda b, h, s=s: (b, 0, s * sec + h))
    tab = pl.BlockSpec((1, seq, LANES), lambda b, h: (b, 0, 0))
    vec = pl.BlockSpec((1, LANES), lambda b, h: (0, 0))
    return pl.pallas_call(
        functools.partial(_dilated_kernel, seq=seq),
        grid=(batch, N_PAIRS),
        in_specs=[blk(0), blk(1), blk(2), tab, tab, vec, vec],
        out_specs=pl.BlockSpec((1, seq, LANES), lambda b, h: (b, 0, h)),
        out_shape=jax.ShapeDtypeStruct((batch, seq, D_SEC), BF16),
        scratch_shapes=[
            pltpu.VMEM((seq, LANES), F32), pltpu.VMEM((seq, LANES), F32), pltpu.VMEM((seq, LANES), F32),
            pltpu.VMEM((3, seq, LANES), F32), pltpu.VMEM((3, seq, LANES), F32),
            pltpu.VMEM((3, seq, LANES), F32), pltpu.VMEM((2, 2 * TQ, 2 * TQ), F32),
        ],
        compiler_params=_cparams("parallel", "parallel"),
        name="dilated_attn",
    )(qkv3, qkv3, qkv3, cos_t, sin_t, gq, gk)


def _fox_kernel(q_ref, k_ref, v_ref, kb_ref, gq_ref, gk_ref, o_ref, qat_s, ka_s, vt_s, acc_s, tri_s, *, seq):
    gmat = _head_sumsq_matrix()
    chunk = 512
    lane = lax.broadcasted_iota(jnp.int32, (chunk, LANES), 1)
    low = lane < HEAD_DIM
    sel = jnp.where((lane >= HEAD_DIM) & (lane < HEAD_DIM + BIAS_TERMS), 1.0, 0.0)

    for c0 in range(0, seq, chunk):
        rows = pl.ds(c0, chunk)
        for g in range(N_PAIRS):
            lanes = slice(g * LANES, (g + 1) * LANES)
            qn = _qk_norm(q_ref[0, rows, lanes].astype(F32), gq_ref[...], gmat) * Q_SCALE_LOG2
            for h, qa in ((2 * g, jnp.where(low, qn, sel)),
                          (2 * g + 1, jnp.where(low, pltpu.roll(qn, HEAD_DIM, axis=1), sel))):
                qt = qa.T
                for cc in range(chunk // TK):
                    qat_s[c0 // TK + cc, h * LANES:(h + 1) * LANES, :] = qt[:, cc * TK:(cc + 1) * TK].astype(BF16)
            kn =_qk_norm(k_ref[0, rows, lanes].astype(F32), gk_ref[...], gmat)
            kb = kb_ref[0, rows, lanes].astype(F32)
            ka_s[2 * g, rows, :] = jnp.where(low, kn, kb).astype(BF16)
            ka_s[2 * g + 1, rows, :] = pltpu.roll(jnp.where(low, kb, kn), HEAD_DIM, axis=1).astype(BF16)
            vt = v_ref[0, rows, lanes].astype(F32).T
            for cc in range(chunk // TK):
                vt_s[c0 // TK + cc, lanes, :] = vt[:, cc * TK:(cc + 1) * TK].astype(BF16)

    r = lax.broadcasted_iota(jnp.int32, (TK, TK), 0)
    c = lax.broadcasted_iota(jnp.int32, (TK, TK), 1)
    tri_s[...] = jnp.where(r <= c, 0.0, NEG)

    def q_block(qi, carry):
        q_rows = pl.ds(pl.multiple_of(qi * TK, TK), TK)
        for h in range(N_HEADS):
            acc_s[h] = jnp.zeros((HEAD_DIM, TK), F32)

        def step(j, st, diagonal):
            k_rows = pl.ds(pl.multiple_of(j * TK, TK), TK)

            def scores(h):
                s = jnp.dot(ka_s[h, k_rows, :], qat_s[qi, h * LANES:(h + 1) * LANES, :], preferred_element_type=F32)
                return s + tri_s[...] if diagonal else s

            ahead = 2
            pending = [scores(h) for h in range(ahead)]
            new = []
            for h in range(N_HEADS):
                m, l = st[h]
                s = pending.pop(0)
                if h + ahead < N_HEADS:
                    pending.append(scores(h + ahead))
                m_new = jnp.maximum(m, jnp.max(s, axis=0, keepdims=True))
                alpha = jnp.exp2(m - m_new)
                pr = jnp.exp2(s - m_new)
                l_new = alpha * l + jnp.sum(pr, axis=0, keepdims=True)
                pv = jnp.dot(vt_s[j, h * HEAD_DIM:(h + 1) * HEAD_DIM, :], pr.astype(BF16),
                             preferred_element_type=F32)
                acc_s[h] = alpha * acc_s[h] + pv
                new.append((m_new, l_new))
            return tuple(new)

        init = tuple((jnp.full((1, TK), NEG, F32), jnp.zeros((1, TK), F32)) for _ in range(N_HEADS))
        st = lax.fori_loop(0, qi, lambda j, st: step(j, st, False), init)
        st = step(qi, st, True)
        for g in range(N_PAIRS):
            o2 = jnp.concatenate([acc_s[2 * g] / st[2 * g][1], acc_s[2 * g + 1] / st[2 * g + 1][1]], axis=0)
            o_ref[0, q_rows, g * LANES:(g + 1) * LANES] = o2.T.astype(BF16)
        return carry

    lax.fori_loop(0, seq // TK, q_block, 0)


def _fox_attn(qkv3, kbias, gq, gk):
    batch, seq, _ = qkv3.shape
    blk = lambda s: pl.BlockSpec((1, seq, D_SEC), lambda b, s=s: (b, 0, s))
    vec = pl.BlockSpec((1, LANES), lambda b: (0, 0))
    return pl.pallas_call(
        functools.partial(_fox_kernel, seq=seq),
        grid=(batch,),
        in_specs=[blk(3), blk(4), blk(5), pl.BlockSpec((1, seq, D_SEC), lambda b: (b, 0, 0)), vec, vec],
        out_specs=pl.BlockSpec((1, seq, D_SEC), lambda b: (b, 0, 0)),
        out_shape=jax.ShapeDtypeStruct((batch, seq, D_SEC), BF16),
        scratch_shapes=[
            pltpu.VMEM((seq // TK, N_HEADS * LANES, TK), BF16),
            pltpu.VMEM((N_HEADS, seq, LANES), BF16),
            pltpu.VMEM((seq // TK, D_SEC, TK), BF16),
            pltpu.VMEM((N_HEADS, HEAD_DIM, TK), F32),
            pltpu.VMEM((TK, TK), F32),
        ],
        compiler_params=_cparams("parallel"),
        name="fox_attn",
    )(qkv3, qkv3, qkv3, kbias, gq, gk)


def _out_router_kernel(oa_ref, ob_ref, x_ref, wo_ref, g_ref, wr_ref, br_ref,
                       h1_ref, msort_ref, route_ref, cnt_ref):
    tm = TM_TOK
    h1 = (x_ref[...]
          + jnp.dot(oa_ref[...], wo_ref[0:D_SEC, :], preferred_element_type=F32)
          + jnp.dot(ob_ref[...], wo_ref[D_SEC:2 * D_SEC, :], preferred_element_type=F32))
    h1_ref[...] = h1
    ms = jnp.mean(h1 * h1, axis=-1, keepdims=True)
    mn = h1 * lax.rsqrt(ms + EPS) * g_ref[...]

    logits = jnp.dot(mn.astype(BF16), wr_ref[...], preferred_element_type=F32) + br_ref[...]
    lane = lax.broadcasted_iota(jnp.int32, (tm, LANES), 1).astype(F32)
    big = float(LANES)

    def first_argmax(vals):
        vmax = jnp.max(vals, axis=-1, keepdims=True)
        idx = jnp.min(jnp.where(vals == vmax, lane, big), axis=-1, keepdims=True)
        return vmax, idx

    lg = jnp.where(lane < N_GROUPS, logits, -jnp.inf)
    gmax, gidx = first_argmax(lg)
    gw = 1.0 / jnp.sum(jnp.exp(lg - gmax), axis=-1, keepdims=True)
    lo = ROUTER_LANE0 + EXPERTS_PER_GROUP * gidx
    le = jnp.where((lane >= lo) & (lane < lo + EXPERTS_PER_GROUP), logits, -jnp.inf)
    v0, i0 = first_argmax(le)
    v1, i1 = first_argmax(jnp.where(lane == i0, -jnp.inf, le))
    ex = jnp.exp(v1 - v0)
    w0 = gw / (1.0 + ex)
    w1 = gw * ex / (1.0 + ex)

    sel0 = lane == i0
    sel1 = lane == i1
    onehot = jnp.where(sel0 | sel1, 1.0, 0.0)
    r = lax.broadcasted_iota(jnp.int32, (tm, tm), 0)
    c = lax.broadcasted_iota(jnp.int32, (tm, tm), 1)
    ltri = (c <= r).astype(BF16)
    incl = jnp.dot(ltri, onehot.astype(BF16), preferred_element_type=F32)
    excl = incl - onehot
    counts = incl[tm - 1:tm, :]
    cnt_ref[0] = jnp.broadcast_to(counts, (SUBLANES, LANES))

    seg_rows = jnp.floor((counts + (SEG_ALIGN - 1.0)) * (1.0 / SEG_ALIGN)) * SEG_ALIGN
    seg_rows = jnp.broadcast_to(seg_rows, (tm, LANES))
    lp0 = jnp.sum(jnp.where(lane < i0, seg_rows, 0.0) + jnp.where(sel0, excl, 0.0), axis=-1, keepdims=True)
    lp1 = jnp.sum(jnp.where(lane < i1, seg_rows, 0.0) + jnp.where(sel1, excl, 0.0), axis=-1, keepdims=True)

    rec = jnp.zeros((tm, LANES), F32)
    for pos, val in enumerate((i0 - ROUTER_LANE0, i1 - ROUTER_LANE0, lp0, lp1, w0, w1)):
        rec = jnp.where(lane == float(pos), val, rec)
    route_ref[...] = rec[:, 0:SUBLANES]

    rec_t = rec.T
    out_row = lax.broadcasted_iota(jnp.int32, (SORT_ROWS, tm), 0).astype(F32)
    pick = jnp.where((out_row == rec_t[2:3, :]) | (out_row == rec_t[3:4, :]), 1.0, 0.0).astype(BF16)
    msort_ref[...] = jnp.dot(pick, mn.astype(BF16), preferred_element_type=F32).astype(BF16)


def _out_router(oa, ob, x2, w_o, g_ffn, w_r, b_r):
    T = x2.shape[0]
    tm = TM_TOK
    row = lambda w: pl.BlockSpec((tm, w), lambda i: (i, 0))
    const = lambda a, b: pl.BlockSpec((a, b), lambda i: (0, 0))
    return pl.pallas_call(
        _out_router_kernel,
        grid=(T // tm,),
        in_specs=[row(D_SEC), row(D_SEC), row(D_MODEL), const(D_MODEL, D_MODEL), const(1, D_MODEL),
                  const(D_MODEL, LANES), const(1, LANES)],
        out_specs=[row(D_MODEL), pl.BlockSpec((SORT_ROWS, D_MODEL), lambda i: (i, 0)), row(SUBLANES),
                   pl.BlockSpec((1, SUBLANES, LANES), lambda i: (i, 0, 0))],
        out_shape=[
            jax.ShapeDtypeStruct((T, D_MODEL), F32),
            jax.ShapeDtypeStruct((T // tm * SORT_ROWS, D_MODEL), BF16),
            jax.ShapeDtypeStruct((T, SUBLANES), F32),
            jax.ShapeDtypeStruct((T // tm, SUBLANES, LANES), F32),
        ],
        compiler_params=_cparams("parallel"),
        name="out_router",
    )(oa, ob, x2, w_o, g_ffn, w_r, b_r)


def _chunk_row(c):
    return pl.multiple_of(c * SEG_ALIGN, SEG_ALIGN)


def _dispatch_kernel(nchunk_ref, dst_ref, off_ref, cnt_ref, msort_ref, xs_ref, zero_s, sem, zsem):
    t = pl.program_id(0)

    def chunk_copy(c):
        dst = pl.multiple_of(dst_ref[t * SORT_CHUNKS + c], SEG_ALIGN)
        return pltpu.make_async_copy(msort_ref.at[pl.ds(_chunk_row(c), SEG_ALIGN)],
                                     xs_ref.at[pl.ds(dst, SEG_ALIGN)], sem)

    def run(act):
        lax.fori_loop(0, nchunk_ref[t], lambda c, carry: (act(chunk_copy(c)), carry)[1], 0)

    def zero_fill(act):
        def tile_copy(t):
            act(pltpu.make_async_copy(zero_s, xs_ref.at[pl.ds(pl.multiple_of(t * TM_MOE, TM_MOE), TM_MOE)], zsem))

        def per_expert(e, carry):
            @pl.when(cnt_ref[e] % TM_MOE != 0)
            def _():
                tile_copy(off_ref[e + 1] // TM_MOE - 1)
            return carry
        lax.fori_loop(0, N_EXPERTS, per_expert, 0)

        def per_tile(t, carry):
            tile_copy(t)
            return carry
        lax.fori_loop(off_ref[N_EXPERTS] // TM_MOE, xs_ref.shape[0] // TM_MOE, per_tile, 0)

    @pl.when(t == 0)
    def _():
        zero_s[...] = jnp.zeros_like(zero_s)
        zero_fill(lambda c: c.start())
        zero_fill(lambda c: c.wait())

    run(lambda c: c.start())
    run(lambda c: c.wait())


def _dispatch(n_chunks, chunk_dst, off, cnt, msort, n_rows):
    grid_spec = pltpu.PrefetchScalarGridSpec(
        num_scalar_prefetch=4,
        grid=(msort.shape[0] // SORT_ROWS,),
        in_specs=[pl.BlockSpec((SORT_ROWS, D_MODEL), lambda t, *_: (t, 0))],
        out_specs=pl.BlockSpec(memory_space=pl.ANY),
        scratch_shapes=[pltpu.VMEM((TM_MOE, D_MODEL), BF16), pltpu.SemaphoreType.DMA, pltpu.SemaphoreType.DMA],
    )
    return pl.pallas_call(
        _dispatch_kernel,
        grid_spec=grid_spec,
        out_shape=jax.ShapeDtypeStruct((n_rows, D_MODEL), BF16),
        compiler_params=_cparams("arbitrary"),
        name="dispatch",
    )(n_chunks, chunk_dst, off, cnt, msort)


def _moe_kernel(tile_e_ref, nused_ref, xs_ref, wup_ref, wdn_ref, ys_ref, wup_s, wdn_s):
    i = pl.program_id(0)

    @pl.when(i < nused_ref[0])
    def _():
        prev = tile_e_ref[jnp.maximum(i - 1, 0)]

        @pl.when((i == 0) | (tile_e_ref[i] != prev))
        def _():
            wup_s[...] = wup_ref[0].astype(BF16)
            wdn_s[...] = wdn_ref[0].astype(BF16)

        hu = jnp.dot(xs_ref[...], wup_s[...], preferred_element_type=F32)
        gate = hu[:, :D_EXPERT]
        hid = gate * (1.0 / (1.0 + jnp.exp(-gate))) * hu[:, D_EXPERT:]
        ys_ref[...] = jnp.dot(hid.astype(BF16), wdn_s[...], preferred_element_type=F32).astype(BF16)

    @pl.when(i >= nused_ref[0])
    def _():
        ys_ref[...] = jnp.zeros_like(ys_ref)


def _moe_experts(tile_e, nused, xs, w_up, w_down):
    n_rows = xs.shape[0]
    n_tiles = n_rows // TM_MOE
    row_map = lambda i, te, nu: (jnp.minimum(i, nu[0] - 1), 0)
    grid_spec = pltpu.PrefetchScalarGridSpec(
        num_scalar_prefetch=2,
        grid=(n_tiles,),
        in_specs=[
            pl.BlockSpec((TM_MOE, D_MODEL), row_map),
            pl.BlockSpec((1, D_MODEL, 2 * D_EXPERT), lambda i, te, nu: (te[i], 0, 0)),
            pl.BlockSpec((1, D_EXPERT, D_MODEL), lambda i, te, nu: (te[i], 0, 0)),
        ],
        out_specs=pl.BlockSpec((TM_MOE, D_MODEL), lambda i, te, nu: (i, 0)),
        scratch_shapes=[pltpu.VMEM((D_MODEL, 2 * D_EXPERT), BF16), pltpu.VMEM((D_EXPERT, D_MODEL), BF16)],
    )
    return pl.pallas_call(
        _moe_kernel,
        grid_spec=grid_spec,
        out_shape=jax.ShapeDtypeStruct((n_rows, D_MODEL), BF16),
        compiler_params=_cparams("arbitrary"),
        name="moe_experts",
    )(tile_e, nused, xs, w_up, w_down)


def _combine_kernel(nchunk_ref, src_ref, ys_ref, h1_ref, route_ref, p_ref,
                    g_ref, wg_ref, wp_ref, o_ref, gath_s, sems):
    i = pl.program_id(0)
    n_steps = pl.num_programs(0)
    tm = TM_TOK
    slot = i % 2

    def gather(t, s, act):
        def chunk(c, carry):
            src = pl.multiple_of(src_ref[t * SORT_CHUNKS + c], SEG_ALIGN)
            act(pltpu.make_async_copy(ys_ref.at[pl.ds(src, SEG_ALIGN)],
                                      gath_s.at[s, pl.ds(_chunk_row(c), SEG_ALIGN)], sems.at[s]))
            return carry
        lax.fori_loop(0, nchunk_ref[t], chunk, 0)

    @pl.when(i == 0)
    def _():
        gath_s[...] = jnp.zeros_like(gath_s)
        gather(0, 0, lambda c: c.start())

    @pl.when(i + 1 < n_steps)
    def _():
        gather(i + 1, 1 - slot, lambda c: c.start())

    gather(i, slot, lambda c: c.wait())

    route = route_ref[...]
    col = lax.broadcasted_iota(jnp.int32, (tm, SORT_ROWS), 1).astype(F32)
    weights = (jnp.where(col == route[:, 2:3], route[:, 4:5], 0.0)
               + jnp.where(col == route[:, 3:4], route[:, 5:6], 0.0)).astype(BF16)
    y = jnp.dot(weights, gath_s[slot], preferred_element_type=F32)
    h2 = h1_ref[...] + y
    ms = jnp.mean(h2 * h2, axis=-1, keepdims=True)
    n = (h2 * lax.rsqrt(ms + EPS) * g_ref[...]).astype(BF16)
    z = jnp.dot(n, wg_ref[...], preferred_element_type=F32)
    gate = 1.0 / (1.0 + jnp.exp(-z))
    ple = jnp.dot(p_ref[...].astype(BF16), wp_ref[...], preferred_element_type=F32)
    o_ref[...] = h2 + gate * ple


def _combine_ple(n_chunks, chunk_dst, ys, h1, route, p2, g_ple, w_gate, w_proj):
    T = h1.shape[0]
    tm = TM_TOK
    row = lambda w: pl.BlockSpec((tm, w), lambda i, *_: (i, 0))
    const = lambda a, b: pl.BlockSpec((a, b), lambda i, *_: (0, 0))
    grid_spec = pltpu.PrefetchScalarGridSpec(
        num_scalar_prefetch=2,
        grid=(T // tm,),
        in_specs=[pl.BlockSpec(memory_space=pl.ANY), row(D_MODEL), row(SUBLANES), row(PLE_DIM),
                  const(1, D_MODEL), const(D_MODEL, D_MODEL), const(PLE_DIM, D_MODEL)],
        out_specs=row(D_MODEL),
        scratch_shapes=[pltpu.VMEM((2, SORT_ROWS, D_MODEL), BF16), pltpu.SemaphoreType.DMA((2,))],
    )
    return pl.pallas_call(
        _combine_kernel,
        grid_spec=grid_spec,
        out_shape=jax.ShapeDtypeStruct((T, D_MODEL), F32),
        compiler_params=_cparams("arbitrary"),
        name="combine_ple",
    )(n_chunks, chunk_dst, ys, h1, route, p2, g_ple, w_gate, w_proj)


def _rope_tables(positions):
    half = ROT_DIM // 2
    inv = ROPE_THETA ** (-jnp.arange(0, ROT_DIM, 2, dtype=F32) / ROT_DIM)
    ang = positions.astype(F32)[..., None] * inv
    cos, sin = jnp.cos(ang), jnp.sin(ang)
    ones = jnp.ones(ang.shape[:-1] + (HEAD_DIM - ROT_DIM,), F32)
    cos_h = jnp.concatenate([cos, cos, ones], axis=-1)
    sin_h = jnp.concatenate([-sin, sin, 0.0 * ones], axis=-1)
    return jnp.tile(cos_h, (1, 1, 2)), jnp.tile(sin_h, (1, 1, 2))


def _layer(i, h, p, cos_t, sin_t, g_mix, w_in, b_f, qn_a, kn_a, qn_b, kn_b, w_o, g_ffn, w_rg, b_rg,
           w_re, b_re, w_up, w_down, g_ple, w_ple_gate, w_ple_proj):
    B, S, _ = h.shape
    T = B * S
    x2 = h.reshape(T, D_MODEL)

    w_qkv = w_in[i][:, :N_QKV].astype(BF16)
    wf = jnp.zeros((D_MODEL, LANES), F32).at[:, :N_HEADS].set(w_in[i][:, N_QKV:]).astype(BF16)
    bf = jnp.zeros((1, LANES), F32).at[0, :N_HEADS].set(b_f[i])
    pair = lambda g: jnp.tile(g, 2).reshape(1, LANES)
    w_r = jnp.zeros((D_MODEL, LANES), F32)
    w_r = w_r.at[:, :N_GROUPS].set(w_rg[i])
    w_r = w_r.at[:, ROUTER_LANE0:ROUTER_LANE0 + N_EXPERTS].set(
        jnp.transpose(w_re[i], (1, 0, 2)).reshape(D_MODEL, N_EXPERTS))
    b_r = jnp.zeros((1, LANES), F32).at[0, :N_GROUPS].set(b_rg[i])
    b_r = b_r.at[0, ROUTER_LANE0:ROUTER_LANE0 + N_EXPERTS].set(b_re[i].reshape(-1))

    qkv, f = _in_proj(x2, g_mix[i].reshape(1, -1), w_qkv, wf)
    kbias = _forget_scan(f, bf, B, S)
    qkv3 = qkv.reshape(B, S, N_QKV)
    oa = _dilated_attn(qkv3, cos_t, sin_t, pair(qn_a[i]), pair(kn_a[i]))
    ob = _fox_attn(qkv3, kbias, pair(qn_b[i]), pair(kn_b[i]))

    h1, msort, route, cnt = _out_router(oa.reshape(T, D_SEC), ob.reshape(T, D_SEC), x2, w_o[i].astype(BF16),
                                        g_ffn[i].reshape(1, -1), w_r.astype(BF16), b_r)

    n_tok_tiles = T // TM_TOK
    counts = cnt[:, 0, ROUTER_LANE0:ROUTER_LANE0 + N_EXPERTS].astype(jnp.int32)
    seg_rows = (counts + SEG_ALIGN - 1) // SEG_ALIGN * SEG_ALIGN
    seg_local = jnp.cumsum(seg_rows, axis=1) - seg_rows
    rows_e = jnp.sum(seg_rows, axis=0)
    tile_end = jnp.cumsum((rows_e + TM_MOE - 1) // TM_MOE)
    off = jnp.concatenate([jnp.zeros((1,), jnp.int32), tile_end * TM_MOE]).astype(jnp.int32)
    seg_global = off[None, :N_EXPERTS] + jnp.cumsum(seg_rows, axis=0) - seg_rows
    n_tiles = (2 * T + n_tok_tiles * N_EXPERTS * (SEG_ALIGN - 1)) // TM_MOE + N_EXPERTS
    nused = tile_end[-1:].astype(jnp.int32)
    tile_ids = jnp.minimum(jnp.arange(n_tiles, dtype=jnp.int32), nused[0] - 1)
    tile_e = jnp.sum((tile_ids[:, None] >= tile_end[None, :]).astype(jnp.int32), axis=1).astype(jnp.int32)
    chunk_row = jnp.arange(SORT_CHUNKS, dtype=jnp.int32)[None, :, None] * SEG_ALIGN
    lo, hi = seg_local[:, None, :], (seg_local + seg_rows)[:, None, :]
    chunk_dst = jnp.sum(jnp.where((chunk_row >= lo) & (chunk_row < hi), seg_global[:, None, :] + chunk_row - lo, 0),
                        axis=2).reshape(-1).astype(jnp.int32)
    n_chunks = (jnp.sum(seg_rows, axis=1) // SEG_ALIGN).astype(jnp.int32)

    xs = _dispatch(n_chunks, chunk_dst, off, rows_e.astype(jnp.int32), msort, n_tiles * TM_MOE)
    ys = _moe_experts(tile_e, nused, xs, w_up[i], w_down[i])
    out = _combine_ple(n_chunks, chunk_dst, ys, h1, route, p[i].reshape(T, PLE_DIM),
                       g_ple[i].reshape(1, -1), w_ple_gate[i].astype(BF16), w_ple_proj[i].astype(BF16))
    return out.reshape(B, S, D_MODEL)


def kernel(x, p, positions, g_mix, w_in, b_f, qn_a, kn_a, qn_b, kn_b, w_o, g_ffn, w_rg, b_rg, w_re, b_re,
           w_up, w_down, g_ple, w_ple_gate, w_ple_proj):
    cos_t, sin_t = _rope_tables(positions)
    h = x
    for i in range(p.shape[0]):
        h = _layer(i, h, p, cos_t, sin_t, g_mix, w_in, b_f, qn_a, kn_a, qn_b, kn_b, w_o, g_ffn, w_rg, b_rg,
                   w_re, b_re, w_up, w_down, g_ple, w_ple_gate, w_ple_proj)
    return h
```

```python
import functools
import math

import jax
import jax.numpy as jnp
from jax import lax
from jax.experimental import pallas as pl
from jax.experimental.pallas import tpu as pltpu

D_MODEL = 1024
HEAD_DIM = 64
N_HEADS = 8
D_SEC = N_HEADS * HEAD_DIM
N_QKV = 6 * D_SEC
ROT_DIM = HEAD_DIM // 4
ROPE_THETA = 500000.0
N_GROUPS = 4
EXPERTS_PER_GROUP = 8
N_EXPERTS = N_GROUPS * EXPERTS_PER_GROUP
D_EXPERT = 512
PLE_DIM = 256
EPS = 1e-6
NEG = -1e30
WINDOW = 128

LANES = 128
SUBLANES = 8
VMEM_LIMIT = 48 * 1024 * 1024

TM_PROJ = 512
TQ = 128
TK = 256
N_PAIRS = N_HEADS // 2
TM_MOE = 256
TM_TOK = 512
SEG_ALIGN = 2 * SUBLANES
SORT_ROWS = 2 * TM_TOK + N_EXPERTS * SEG_ALIGN
SORT_CHUNKS = SORT_ROWS // SEG_ALIGN
ROUTER_LANE0 = N_GROUPS

Q_SCALE_LOG2 = math.log2(math.e) / math.sqrt(HEAD_DIM)

F32 = jnp.float32
BF16 = jnp.bfloat16
NT_DIMS = (((1,), (1,)), ((), ()))


def _cparams(*sem):
    return pltpu.CompilerParams(dimension_semantics=sem, vmem_limit_bytes=VMEM_LIMIT)


def _in_proj_kernel(x_ref, g_ref, w_ref, wf_ref, qkv_ref, f_ref):
    x = x_ref[...]
    ms = jnp.mean(x * x, axis=-1, keepdims=True)
    a = (x * lax.rsqrt(ms + EPS) * g_ref[...]).astype(BF16)
    qkv_ref[...] = jnp.dot(a, w_ref[...], preferred_element_type=F32).astype(BF16)
    f_ref[...] = jnp.dot(a, wf_ref[...], preferred_element_type=F32)


def _in_proj(x2, g_mix, w_qkv, wf):
    T = x2.shape[0]
    return pl.pallas_call(
        _in_proj_kernel,
        grid=(T // TM_PROJ,),
        in_specs=[
            pl.BlockSpec((TM_PROJ, D_MODEL), lambda i: (i, 0)),
            pl.BlockSpec((1, D_MODEL), lambda i: (0, 0)),
            pl.BlockSpec((D_MODEL, N_QKV), lambda i: (0, 0)),
            pl.BlockSpec((D_MODEL, LANES), lambda i: (0, 0)),
        ],
        out_specs=[
            pl.BlockSpec((TM_PROJ, N_QKV), lambda i: (i, 0)),
            pl.BlockSpec((TM_PROJ, LANES), lambda i: (i, 0)),
        ],
        out_shape=[
            jax.ShapeDtypeStruct((T, N_QKV), BF16),
            jax.ShapeDtypeStruct((T, LANES), F32),
        ],
        compiler_params=_cparams("parallel"),
        name="in_proj",
    )(x2, g_mix, w_qkv, wf)


BIAS_TERMS = 3


def _bias_lane(head):
    return (head // 2) * LANES + (HEAD_DIM if head % 2 == 0 else 0)


def _forget_scan_kernel(f_ref, bf_ref, kb_ref, *, seq):
    f = f_ref[...] + bf_ref[...]
    c = jnp.minimum(f, 0.0) - jnp.log1p(jnp.exp(-jnp.abs(f)))
    row = lax.broadcasted_iota(jnp.int32, c.shape, 0)
    k = 1
    while k < seq:
        c = c + jnp.where(row >= k, pltpu.roll(c, k, axis=0), 0.0)
        k *= 2
    rest = c * (-math.log2(math.e))
    r_idx = lax.broadcasted_iota(jnp.int32, (LANES, D_SEC), 0)
    c_idx = lax.broadcasted_iota(jnp.int32, (LANES, D_SEC), 1)
    base = (r_idx // 2) * LANES + jnp.where(r_idx % 2 == 0, HEAD_DIM, 0)
    out = jnp.zeros((seq, D_SEC), F32)
    for t in range(BIAS_TERMS):
        term = rest.astype(BF16)
        rest = rest - term.astype(F32)
        place = ((r_idx < N_HEADS) & (c_idx == base + t)).astype(BF16)
        out = out + jnp.dot(term, place, preferred_element_type=F32)
    kb_ref[0] = out.astype(BF16)


def _forget_scan(f, bf, batch, seq):
    return pl.pallas_call(
        functools.partial(_forget_scan_kernel, seq=seq),
        grid=(batch,),
        in_specs=[
            pl.BlockSpec((seq, LANES), lambda b: (b, 0)),
            pl.BlockSpec((1, LANES), lambda b: (0, 0)),
        ],
        out_specs=pl.BlockSpec((1, seq, D_SEC), lambda b: (b, 0, 0)),
        out_shape=jax.ShapeDtypeStruct((batch, seq, D_SEC), BF16),
        compiler_params=_cparams("parallel"),
        name="forget_scan",
    )(f, bf)


def _head_sumsq_matrix():
    r = lax.broadcasted_iota(jnp.int32, (LANES, LANES), 0) // HEAD_DIM
    c = lax.broadcasted_iota(jnp.int32, (LANES, LANES), 1) // HEAD_DIM
    return (r == c).astype(BF16)


def _qk_norm(x, gain, gmat):
    ss = jnp.dot((x * x).astype(BF16), gmat, preferred_element_type=F32)
    return x * lax.rsqrt(ss * (1.0 / HEAD_DIM) + EPS) * gain


def _stack_heads(qb, head0):
    zero = jnp.zeros_like(qb)
    return jnp.concatenate([jnp.where(head0, qb, zero), jnp.where(head0, zero, qb)], axis=0)


def _unstack(a, head0):
    return jnp.where(head0, a[:TQ], a[TQ:])


def _dilated_kernel(q_ref, k_ref, v_ref, cos_ref, sin_ref, gq_ref, gk_ref, o_ref,
                    qn_s, kn_s, v_s, acc_s, m_s, l_s, bias_s, *, seq):
    gmat = _head_sumsq_matrix()
    lane = lax.broadcasted_iota(jnp.int32, (TQ, LANES), 1)
    head0 = lane < HEAD_DIM
    chunk = 512
    lane_c = lax.broadcasted_iota(jnp.int32, (chunk, LANES), 1) % HEAD_DIM
    first_half = lane_c < ROT_DIM // 2

    def rope(t, cs, sn):
        partner = jnp.where(first_half, pltpu.roll(t, LANES - ROT_DIM // 2, axis=1),
                            pltpu.roll(t, ROT_DIM // 2, axis=1))
        return t * cs + partner * sn

    for c0 in range(0, seq, chunk):
        rows = pl.ds(c0, chunk)
        cs = cos_ref[0, rows, :]
        sn = sin_ref[0, rows, :]
        qn = _qk_norm(q_ref[0, rows, :].astype(F32), gq_ref[...], gmat)
        qn_s[rows, :] = rope(qn, cs, sn) * Q_SCALE_LOG2
        kn = _qk_norm(k_ref[0, rows, :].astype(F32), gk_ref[...], gmat)
        kn_s[rows, :] = rope(kn, cs, sn)
        v_s[rows, :] = v_ref[0, rows, :].astype(F32)

    u = lax.broadcasted_iota(jnp.int32, (2 * TQ, 2 * TQ), 0) % TQ
    c = lax.broadcasted_iota(jnp.int32, (2 * TQ, 2 * TQ), 1)
    for slot, dist0 in enumerate((0, TQ)):
        dist = dist0 + u - c
        bias_s[slot] = jnp.where((dist >= 0) & (dist <= WINDOW), 0.0, NEG)

    def scores(blk):
        p, q_rows, k_rows, nk, dist0 = blk
        qst = _stack_heads(qn_s[q_rows, :].astype(BF16), head0)
        s = lax.dot_general(qst, kn_s[k_rows, :].astype(BF16), NT_DIMS, preferred_element_type=F32)
        return s + bias_s[dist0 // TQ, :, 0:nk]

    def finish(blk, s):
        p, q_rows, k_rows, nk, dist0 = blk
        m = jnp.max(s, axis=-1, keepdims=True)
        pr = jnp.exp2(s - m)
        l = jnp.sum(pr, axis=-1, keepdims=True)
        acc = jnp.dot(pr.astype(BF16), v_s[k_rows, :].astype(BF16), preferred_element_type=F32)
        acc_s[p, q_rows, :] = _unstack(acc, head0)
        m_s[p, q_rows, :] = _unstack(jnp.broadcast_to(m, (2 * TQ, LANES)), head0)
        l_s[p, q_rows, :] = _unstack(jnp.broadcast_to(l, (2 * TQ, LANES)), head0)

    blocks = []
    for i in range(seq // TQ):
        kb0 = max(i - 1, 0)
        blocks.append((0, pl.ds(i * TQ, TQ), pl.ds(kb0 * TQ, 2 * TQ), 2 * TQ, (i - kb0) * TQ))
    for r in range(4):
        for n in range(seq // (4 * TQ)):
            kb0 = max(n - 1, 0)
            blocks.append((1, pl.ds(4 * TQ * n + r, TQ, stride=4), pl.ds(4 * TQ * kb0 + r, 2 * TQ, stride=4),
                           2 * TQ, (n - kb0) * TQ))
    for r in range(16):
        rows = pl.ds(r, TQ, stride=16)
        blocks.append((2, rows, rows, TQ, 0))

    ahead = 2
    pending = [scores(b) for b in blocks[:ahead]]
    for idx, b in enumerate(blocks):
        s = pending.pop(0)
        if idx + ahead < len(blocks):
            pending.append(scores(blocks[idx + ahead]))
        finish(b, s)

    for c0 in range(0, seq, chunk):
        rows = pl.ds(c0, chunk)
        m0, m1, m2 = m_s[0, rows, :], m_s[1, rows, :], m_s[2, rows, :]
        mm = jnp.maximum(jnp.maximum(m0, m1), m2)
        e0, e1, e2 = jnp.exp2(m0 - mm), jnp.exp2(m1 - mm), jnp.exp2(m2 - mm)
        num = acc_s[0, rows, :] * e0 + acc_s[1, rows, :] * e1 + acc_s[2, rows, :] * e2
        den = l_s[0, rows, :] * e0 + l_s[1, rows, :] * e1 + l_s[2, rows, :] * e2
        o_ref[0, rows, :] = (num / den).astype(BF16)


def _dilated_attn(qkv3, cos_t, sin_t, gq, gk):
    batch, seq, _ = qkv3.shape
    sec = D_SEC // LANES
    blk = lambda s: pl.BlockSpec((1, seq, LANES), lambda b, h, s=s: (b, 0, s * sec + h))
    tab = pl.BlockSpec((1, seq, LANES), lambda b, h: (b, 0, 0))
    vec = pl.BlockSpec((1, LANES), lambda b, h: (0, 0))
    return pl.pallas_call(
        functools.partial(_dilated_kernel, seq=seq),
        grid=(batch, N_PAIRS),
        in_specs=[blk(0), blk(1), blk(2), tab, tab, vec, vec],
        out_specs=pl.BlockSpec((1, seq, LANES), lambda b, h: (b, 0, h)),
        out_shape=jax.ShapeDtypeStruct((batch, seq, D_SEC), BF16),
        scratch_shapes=[
            pltpu.VMEM((seq, LANES), F32), pltpu.VMEM((seq, LANES), F32), pltpu.VMEM((seq, LANES), F32),
            pltpu.VMEM((3, seq, LANES), F32), pltpu.VMEM((3, seq, LANES), F32),
            pltpu.VMEM((3, seq, LANES), F32), pltpu.VMEM((2, 2 * TQ, 2 * TQ), F32),
        ],
        compiler_params=_cparams("parallel", "parallel"),
        name="dilated_attn",
    )(qkv3, qkv3, qkv3, cos_t, sin_t, gq, gk)


def _fox_kernel(q_ref, k_ref, v_ref, kb_ref, gq_ref, gk_ref, o_ref, qat_s, ka_s, vt_s, acc_s, tri_s, *, seq):
    gmat = _head_sumsq_matrix()
    chunk = 512
    lane = lax.broadcasted_iota(jnp.int32, (chunk, LANES), 1)
    low = lane < HEAD_DIM
    sel = jnp.where((lane >= HEAD_DIM) & (lane < HEAD_DIM + BIAS_TERMS), 1.0, 0.0)

    for c0 in range(0, seq, chunk):
        rows = pl.ds(c0, chunk)
        for g in range(N_PAIRS):
            lanes = slice(g * LANES, (g + 1) * LANES)
            qn = _qk_norm(q_ref[0, rows, lanes].astype(F32), gq_ref[...], gmat) * Q_SCALE_LOG2
            for h, qa in ((2 * g, jnp.where(low, qn, sel)),
                          (2 * g + 1, jnp.where(low, pltpu.roll(qn, HEAD_DIM, axis=1), sel))):
                qt = qa.T
                for cc in range(chunk // TK):
                    qat_s[c0 // TK + cc, h * LANES:(h + 1) * LANES, :] = qt[:, cc * TK:(cc + 1) * TK].astype(BF16)
            kn =_qk_norm(k_ref[0, rows, lanes].astype(F32), gk_ref[...], gmat)
            kb = kb_ref[0, rows, lanes].astype(F32)
            ka_s[2 * g, rows, :] = jnp.where(low, kn, kb).astype(BF16)
            ka_s[2 * g + 1, rows, :] = pltpu.roll(jnp.where(low, kb, kn), HEAD_DIM, axis=1).astype(BF16)
            vt = v_ref[0, rows, lanes].astype(F32).T
            for cc in range(chunk // TK):
                vt_s[c0 // TK + cc, lanes, :] = vt[:, cc * TK:(cc + 1) * TK].astype(BF16)

    r = lax.broadcasted_iota(jnp.int32, (TK, TK), 0)
    c = lax.broadcasted_iota(jnp.int32, (TK, TK), 1)
    tri_s[...] = jnp.where(r <= c, 0.0, NEG)

    def q_block(qi, carry):
        q_rows = pl.ds(pl.multiple_of(qi * TK, TK), TK)
        for h in range(N_HEADS):
            acc_s[h] = jnp.zeros((HEAD_DIM, TK), F32)

        def step(j, st, diagonal):
            k_rows = pl.ds(pl.multiple_of(j * TK, TK), TK)

            def scores(h):
                s = jnp.dot(ka_s[h, k_rows, :], qat_s[qi, h * LANES:(h + 1) * LANES, :], preferred_element_type=F32)
                return s + tri_s[...] if diagonal else s

            ahead = 2
            pending = [scores(h) for h in range(ahead)]
            new = []
            for h in range(N_HEADS):
                m, l = st[h]
                s = pending.pop(0)
                if h + ahead < N_HEADS:
                    pending.append(scores(h + ahead))
                m_new = jnp.maximum(m, jnp.max(s, axis=0, keepdims=True))
                alpha = jnp.exp2(m - m_new)
                pr = jnp.exp2(s - m_new)
                l_new = alpha * l + jnp.sum(pr, axis=0, keepdims=True)
                pv = jnp.dot(vt_s[j, h * HEAD_DIM:(h + 1) * HEAD_DIM, :], pr.astype(BF16),
                             preferred_element_type=F32)
                acc_s[h] = alpha * acc_s[h] + pv
                new.append((m_new, l_new))
            return tuple(new)

        init = tuple((jnp.full((1, TK), NEG, F32), jnp.zeros((1, TK), F32)) for _ in range(N_HEADS))
        st = lax.fori_loop(0, qi, lambda j, st: step(j, st, False), init)
        st = step(qi, st, True)
        for g in range(N_PAIRS):
            o2 = jnp.concatenate([acc_s[2 * g] / st[2 * g][1], acc_s[2 * g + 1] / st[2 * g + 1][1]], axis=0)
            o_ref[0, q_rows, g * LANES:(g + 1) * LANES] = o2.T.astype(BF16)
        return carry

    lax.fori_loop(0, seq // TK, q_block, 0)


def _fox_attn(qkv3, kbias, gq, gk):
    batch, seq, _ = qkv3.shape
    blk = lambda s: pl.BlockSpec((1, seq, D_SEC), lambda b, s=s: (b, 0, s))
    vec = pl.BlockSpec((1, LANES), lambda b: (0, 0))
    return pl.pallas_call(
        functools.partial(_fox_kernel, seq=seq),
        grid=(batch,),
        in_specs=[blk(3), blk(4), blk(5), pl.BlockSpec((1, seq, D_SEC), lambda b: (b, 0, 0)), vec, vec],
        out_specs=pl.BlockSpec((1, seq, D_SEC), lambda b: (b, 0, 0)),
        out_shape=jax.ShapeDtypeStruct((batch, seq, D_SEC), BF16),
        scratch_shapes=[
            pltpu.VMEM((seq // TK, N_HEADS * LANES, TK), BF16),
            pltpu.VMEM((N_HEADS, seq, LANES), BF16),
            pltpu.VMEM((seq // TK, D_SEC, TK), BF16),
            pltpu.VMEM((N_HEADS, HEAD_DIM, TK), F32),
            pltpu.VMEM((TK, TK), F32),
        ],
        compiler_params=_cparams("parallel"),
        name="fox_attn",
    )(qkv3, qkv3, qkv3, kbias, gq, gk)


def _out_router_kernel(oa_ref, ob_ref, x_ref, wo_ref, g_ref, wr_ref, br_ref,
                       h1_ref, msort_ref, route_ref, cnt_ref):
    tm = TM_TOK
    h1 = (x_ref[...]
          + jnp.dot(oa_ref[...], wo_ref[0:D_SEC, :], preferred_element_type=F32)
          + jnp.dot(ob_ref[...], wo_ref[D_SEC:2 * D_SEC, :], preferred_element_type=F32))
    h1_ref[...] = h1
    ms = jnp.mean(h1 * h1, axis=-1, keepdims=True)
    mn = h1 * lax.rsqrt(ms + EPS) * g_ref[...]

    logits = jnp.dot(mn.astype(BF16), wr_ref[...], preferred_element_type=F32) + br_ref[...]
    lane = lax.broadcasted_iota(jnp.int32, (tm, LANES), 1).astype(F32)
    big = float(LANES)

    def first_argmax(vals):
        vmax = jnp.max(vals, axis=-1, keepdims=True)
        idx = jnp.min(jnp.where(vals == vmax, lane, big), axis=-1, keepdims=True)
        return vmax, idx

    lg = jnp.where(lane < N_GROUPS, logits, -jnp.inf)
    gmax, gidx = first_argmax(lg)
    gw = 1.0 / jnp.sum(jnp.exp(lg - gmax), axis=-1, keepdims=True)
    lo = ROUTER_LANE0 + EXPERTS_PER_GROUP * gidx
    le = jnp.where((lane >= lo) & (lane < lo + EXPERTS_PER_GROUP), logits, -jnp.inf)
    v0, i0 = first_argmax(le)
    v1, i1 = first_argmax(jnp.where(lane == i0, -jnp.inf, le))
    ex = jnp.exp(v1 - v0)
    w0 = gw / (1.0 + ex)
    w1 = gw * ex / (1.0 + ex)

    sel0 = lane == i0
    sel1 = lane == i1
    onehot = jnp.where(sel0 | sel1, 1.0, 0.0)
    r = lax.broadcasted_iota(jnp.int32, (tm, tm), 0)
    c = lax.broadcasted_iota(jnp.int32, (tm, tm), 1)
    ltri = (c <= r).astype(BF16)
    incl = jnp.dot(ltri, onehot.astype(BF16), preferred_element_type=F32)
    excl = incl - onehot
    counts = incl[tm - 1:tm, :]
    cnt_ref[0] = jnp.broadcast_to(counts, (SUBLANES, LANES))

    seg_rows = jnp.floor((counts + (SEG_ALIGN - 1.0)) * (1.0 / SEG_ALIGN)) * SEG_ALIGN
    seg_rows = jnp.broadcast_to(seg_rows, (tm, LANES))
    lp0 = jnp.sum(jnp.where(lane < i0, seg_rows, 0.0) + jnp.where(sel0, excl, 0.0), axis=-1, keepdims=True)
    lp1 = jnp.sum(jnp.where(lane < i1, seg_rows, 0.0) + jnp.where(sel1, excl, 0.0), axis=-1, keepdims=True)

    rec = jnp.zeros((tm, LANES), F32)
    for pos, val in enumerate((i0 - ROUTER_LANE0, i1 - ROUTER_LANE0, lp0, lp1, w0, w1)):
        rec = jnp.where(lane == float(pos), val, rec)
    route_ref[...] = rec[:, 0:SUBLANES]

    rec_t = rec.T
    out_row = lax.broadcasted_iota(jnp.int32, (SORT_ROWS, tm), 0).astype(F32)
    pick = jnp.where((out_row == rec_t[2:3, :]) | (out_row == rec_t[3:4, :]), 1.0, 0.0).astype(BF16)
    msort_ref[...] = jnp.dot(pick, mn.astype(BF16), preferred_element_type=F32).astype(BF16)


def _out_router(oa, ob, x2, w_o, g_ffn, w_r, b_r):
    T = x2.shape[0]
    tm = TM_TOK
    row = lambda w: pl.BlockSpec((tm, w), lambda i: (i, 0))
    const = lambda a, b: pl.BlockSpec((a, b), lambda i: (0, 0))
    return pl.pallas_call(
        _out_router_kernel,
        grid=(T // tm,),
        in_specs=[row(D_SEC), row(D_SEC), row(D_MODEL), const(D_MODEL, D_MODEL), const(1, D_MODEL),
                  const(D_MODEL, LANES), const(1, LANES)],
        out_specs=[row(D_MODEL), pl.BlockSpec((SORT_ROWS, D_MODEL), lambda i: (i, 0)), row(SUBLANES),
                   pl.BlockSpec((1, SUBLANES, LANES), lambda i: (i, 0, 0))],
        out_shape=[
            jax.ShapeDtypeStruct((T, D_MODEL), F32),
            jax.ShapeDtypeStruct((T // tm * SORT_ROWS, D_MODEL), BF16),
            jax.ShapeDtypeStruct((T, SUBLANES), F32),
            jax.ShapeDtypeStruct((T // tm, SUBLANES, LANES), F32),
        ],
        compiler_params=_cparams("parallel"),
        name="out_router",
    )(oa, ob, x2, w_o, g_ffn, w_r, b_r)


def _chunk_row(c):
    return pl.multiple_of(c * SEG_ALIGN, SEG_ALIGN)


MOE_CHUNKS = TM_MOE // SEG_ALIGN


def _moe_kernel(tile_e_ref, nused_ref, src_ref, next_e_ref, slot_e_ref, msort_ref, wup_ref, wdn_ref, ys_ref,
                x_s, wup_f, wdn_f, wup_s, wdn_s, xsem, wsem):
    i = pl.program_id(0)
    nused = nused_ref[0]

    def gather(k, act):
        def chunk(c, carry):
            src = pl.multiple_of(src_ref[k * MOE_CHUNKS + c], SEG_ALIGN)
            act(pltpu.make_async_copy(msort_ref.at[pl.ds(src, SEG_ALIGN)],
                                      x_s.at[k % 2, pl.ds(_chunk_row(c), SEG_ALIGN)], xsem.at[k % 2]))
            return carry
        lax.fori_loop(0, MOE_CHUNKS, chunk, 0)

    def weights(e, s, act):
        act(pltpu.make_async_copy(wup_ref.at[e], wup_f.at[s], wsem.at[0, s]))
        act(pltpu.make_async_copy(wdn_ref.at[e], wdn_f.at[s], wsem.at[1, s]))

    @pl.when(i == 0)
    def _():
        gather(0, lambda c: c.start())
        weights(tile_e_ref[0], 0, lambda c: c.start())

    @pl.when(i + 1 < nused)
    def _():
        gather(i + 1, lambda c: c.start())

    @pl.when(i < nused)
    def _():
        e = tile_e_ref[i]

        @pl.when((i == 0) | (tile_e_ref[jnp.maximum(i - 1, 0)] != e))
        def _():
            s = slot_e_ref[e]
            weights(e, s, lambda c: c.wait())

            @pl.when(next_e_ref[e] >= 0)
            def _():
                weights(next_e_ref[e], 1 - s, lambda c: c.start())
            wup_s[...] = wup_f[s].astype(BF16)
            wdn_s[...] = wdn_f[s].astype(BF16)

        gather(i, lambda c: c.wait())
        hu = jnp.dot(x_s[i % 2], wup_s[...], preferred_element_type=F32)
        gate = hu[:, :D_EXPERT]
        hid = gate * (1.0 / (1.0 + jnp.exp(-gate))) * hu[:, D_EXPERT:]
        ys_ref[...] = jnp.dot(hid.astype(BF16), wdn_s[...], preferred_element_type=F32).astype(BF16)

    @pl.when(i >= nused)
    def _():
        ys_ref[...] = jnp.zeros_like(ys_ref)


def _moe_experts(tile_e, nused, src_chunk, next_e, slot_e, msort, w_up, w_down):
    n_tiles = tile_e.shape[0]
    grid_spec = pltpu.PrefetchScalarGridSpec(
        num_scalar_prefetch=5,
        grid=(n_tiles,),
        in_specs=[pl.BlockSpec(memory_space=pl.ANY)] * 3,
        out_specs=pl.BlockSpec((TM_MOE, D_MODEL), lambda i, *_: (i, 0)),
        scratch_shapes=[
            pltpu.VMEM((2, TM_MOE, D_MODEL), BF16),
            pltpu.VMEM((2, D_MODEL, 2 * D_EXPERT), F32), pltpu.VMEM((2, D_EXPERT, D_MODEL), F32),
            pltpu.VMEM((D_MODEL, 2 * D_EXPERT), BF16), pltpu.VMEM((D_EXPERT, D_MODEL), BF16),
            pltpu.SemaphoreType.DMA((2,)), pltpu.SemaphoreType.DMA((2, 2)),
        ],
    )
    return pl.pallas_call(
        _moe_kernel,
        grid_spec=grid_spec,
        out_shape=jax.ShapeDtypeStruct((n_tiles * TM_MOE, D_MODEL), BF16),
        compiler_params=_cparams("arbitrary"),
        name="moe_experts",
    )(tile_e, nused, src_chunk, next_e, slot_e, msort, w_up, w_down)


def _combine_kernel(nchunk_ref, src_ref, ys_ref, h1_ref, route_ref, p_ref,
                    g_ref, wg_ref, wp_ref, o_ref, gath_s, sems):
    i = pl.program_id(0)
    n_steps = pl.num_programs(0)
    tm = TM_TOK
    slot = i % 2

    def gather(t, s, act):
        def chunk(c, carry):
            src = pl.multiple_of(src_ref[t * SORT_CHUNKS + c], SEG_ALIGN)
            act(pltpu.make_async_copy(ys_ref.at[pl.ds(src, SEG_ALIGN)],
                                      gath_s.at[s, pl.ds(_chunk_row(c), SEG_ALIGN)], sems.at[s]))
            return carry
        lax.fori_loop(0, nchunk_ref[t], chunk, 0)

    @pl.when(i == 0)
    def _():
        gath_s[...] = jnp.zeros_like(gath_s)
        gather(0, 0, lambda c: c.start())

    @pl.when(i + 1 < n_steps)
    def _():
        gather(i + 1, 1 - slot, lambda c: c.start())

    gather(i, slot, lambda c: c.wait())

    route = route_ref[...]
    col = lax.broadcasted_iota(jnp.int32, (tm, SORT_ROWS), 1).astype(F32)
    weights = (jnp.where(col == route[:, 2:3], route[:, 4:5], 0.0)
               + jnp.where(col == route[:, 3:4], route[:, 5:6], 0.0)).astype(BF16)
    y = jnp.dot(weights, gath_s[slot], preferred_element_type=F32)
    h2 = h1_ref[...] + y
    ms = jnp.mean(h2 * h2, axis=-1, keepdims=True)
    n = (h2 * lax.rsqrt(ms + EPS) * g_ref[...]).astype(BF16)
    z = jnp.dot(n, wg_ref[...], preferred_element_type=F32)
    gate = 1.0 / (1.0 + jnp.exp(-z))
    ple = jnp.dot(p_ref[...].astype(BF16), wp_ref[...], preferred_element_type=F32)
    o_ref[...] = h2 + gate * ple


def _combine_ple(n_chunks, chunk_dst, ys, h1, route, p2, g_ple, w_gate, w_proj):
    T = h1.shape[0]
    tm = TM_TOK
    row = lambda w: pl.BlockSpec((tm, w), lambda i, *_: (i, 0))
    const = lambda a, b: pl.BlockSpec((a, b), lambda i, *_: (0, 0))
    grid_spec = pltpu.PrefetchScalarGridSpec(
        num_scalar_prefetch=2,
        grid=(T // tm,),
        in_specs=[pl.BlockSpec(memory_space=pl.ANY), row(D_MODEL), row(SUBLANES), row(PLE_DIM),
                  const(1, D_MODEL), const(D_MODEL, D_MODEL), const(PLE_DIM, D_MODEL)],
        out_specs=row(D_MODEL),
        scratch_shapes=[pltpu.VMEM((2, SORT_ROWS, D_MODEL), BF16), pltpu.SemaphoreType.DMA((2,))],
    )
    return pl.pallas_call(
        _combine_kernel,
        grid_spec=grid_spec,
        out_shape=jax.ShapeDtypeStruct((T, D_MODEL), F32),
        compiler_params=_cparams("arbitrary"),
        name="combine_ple",
    )(n_chunks, chunk_dst, ys, h1, route, p2, g_ple, w_gate, w_proj)


def _rope_tables(positions):
    half = ROT_DIM // 2
    inv = ROPE_THETA ** (-jnp.arange(0, ROT_DIM, 2, dtype=F32) / ROT_DIM)
    ang = positions.astype(F32)[..., None] * inv
    cos, sin = jnp.cos(ang), jnp.sin(ang)
    ones = jnp.ones(ang.shape[:-1] + (HEAD_DIM - ROT_DIM,), F32)
    cos_h = jnp.concatenate([cos, cos, ones], axis=-1)
    sin_h = jnp.concatenate([-sin, sin, 0.0 * ones], axis=-1)
    return jnp.tile(cos_h, (1, 1, 2)), jnp.tile(sin_h, (1, 1, 2))


def _layer(i, h, p, cos_t, sin_t, g_mix, w_in, b_f, qn_a, kn_a, qn_b, kn_b, w_o, g_ffn, w_rg, b_rg,
           w_re, b_re, w_up, w_down, g_ple, w_ple_gate, w_ple_proj):
    B, S, _ = h.shape
    T = B * S
    x2 = h.reshape(T, D_MODEL)

    w_qkv = w_in[i][:, :N_QKV].astype(BF16)
    wf = jnp.zeros((D_MODEL, LANES), F32).at[:, :N_HEADS].set(w_in[i][:, N_QKV:]).astype(BF16)
    bf = jnp.zeros((1, LANES), F32).at[0, :N_HEADS].set(b_f[i])
    pair = lambda g: jnp.tile(g, 2).reshape(1, LANES)
    w_r = jnp.zeros((D_MODEL, LANES), F32)
    w_r = w_r.at[:, :N_GROUPS].set(w_rg[i])
    w_r = w_r.at[:, ROUTER_LANE0:ROUTER_LANE0 + N_EXPERTS].set(
        jnp.transpose(w_re[i], (1, 0, 2)).reshape(D_MODEL, N_EXPERTS))
    b_r = jnp.zeros((1, LANES), F32).at[0, :N_GROUPS].set(b_rg[i])
    b_r = b_r.at[0, ROUTER_LANE0:ROUTER_LANE0 + N_EXPERTS].set(b_re[i].reshape(-1))

    qkv, f = _in_proj(x2, g_mix[i].reshape(1, -1), w_qkv, wf)
    kbias = _forget_scan(f, bf, B, S)
    qkv3 = qkv.reshape(B, S, N_QKV)
    oa = _dilated_attn(qkv3, cos_t, sin_t, pair(qn_a[i]), pair(kn_a[i]))
    ob = _fox_attn(qkv3, kbias, pair(qn_b[i]), pair(kn_b[i]))

    h1, msort, route, cnt = _out_router(oa.reshape(T, D_SEC), ob.reshape(T, D_SEC), x2, w_o[i].astype(BF16),
                                        g_ffn[i].reshape(1, -1), w_r.astype(BF16), b_r)

    n_tok_tiles = T // TM_TOK
    counts = cnt[:, 0, ROUTER_LANE0:ROUTER_LANE0 + N_EXPERTS].astype(jnp.int32)
    seg_rows = (counts + SEG_ALIGN - 1) // SEG_ALIGN * SEG_ALIGN
    seg_local = jnp.cumsum(seg_rows, axis=1) - seg_rows
    rows_e = jnp.sum(seg_rows, axis=0)
    tile_end = jnp.cumsum((rows_e + TM_MOE - 1) // TM_MOE)
    off = jnp.concatenate([jnp.zeros((1,), jnp.int32), tile_end * TM_MOE]).astype(jnp.int32)
    seg_global = off[None, :N_EXPERTS] + jnp.cumsum(seg_rows, axis=0) - seg_rows
    n_tiles = (2 * T + n_tok_tiles * N_EXPERTS * (SEG_ALIGN - 1)) // TM_MOE + N_EXPERTS
    nused = tile_end[-1:].astype(jnp.int32)
    tile_ids = jnp.minimum(jnp.arange(n_tiles, dtype=jnp.int32), nused[0] - 1)
    tile_e = jnp.sum((tile_ids[:, None] >= tile_end[None, :]).astype(jnp.int32), axis=1).astype(jnp.int32)
    chunk_row = jnp.arange(SORT_CHUNKS, dtype=jnp.int32)[None, :, None] * SEG_ALIGN
    lo, hi = seg_local[:, None, :], (seg_local + seg_rows)[:, None, :]
    chunk_dst = jnp.sum(jnp.where((chunk_row >= lo) & (chunk_row < hi), seg_global[:, None, :] + chunk_row - lo, 0),
                        axis=2).reshape(-1).astype(jnp.int32)
    n_chunks = (jnp.sum(seg_rows, axis=1) // SEG_ALIGN).astype(jnp.int32)
    chunk_src = (jnp.arange(n_tok_tiles, dtype=jnp.int32)[:, None] * SORT_ROWS
                 + jnp.arange(SORT_CHUNKS, dtype=jnp.int32)[None, :] * SEG_ALIGN).reshape(-1)
    chunk_used = (jnp.arange(SORT_CHUNKS, dtype=jnp.int32)[None, :] < n_chunks[:, None]).reshape(-1)
    n_sorted_chunks = n_tiles * MOE_CHUNKS
    src_chunk = jnp.full((n_sorted_chunks,), SORT_ROWS - SEG_ALIGN, jnp.int32).at[
        jnp.where(chunk_used, chunk_dst // SEG_ALIGN, n_sorted_chunks)].set(chunk_src, mode="drop")
    e_ids = jnp.arange(N_EXPERTS, dtype=jnp.int32)
    active = rows_e > 0
    later_active = jnp.where(active[None, :] & (e_ids[None, :] > e_ids[:, None]), e_ids[None, :], N_EXPERTS)
    next_e = jnp.min(later_active, axis=1)
    next_e = jnp.where(next_e == N_EXPERTS, -1, next_e).astype(jnp.int32)
    slot_e = ((jnp.cumsum(active.astype(jnp.int32)) - 1) % 2).astype(jnp.int32)

    ys = _moe_experts(tile_e, nused, src_chunk, next_e, slot_e, msort, w_up[i], w_down[i])
    out = _combine_ple(n_chunks, chunk_dst, ys, h1, route, p[i].reshape(T, PLE_DIM),
                       g_ple[i].reshape(1, -1), w_ple_gate[i].astype(BF16), w_ple_proj[i].astype(BF16))
    return out.reshape(B, S, D_MODEL)


def kernel(x, p, positions, g_mix, w_in, b_f, qn_a, kn_a, qn_b, kn_b, w_o, g_ffn, w_rg, b_rg, w_re, b_re,
           w_up, w_down, g_ple, w_ple_gate, w_ple_proj):
    cos_t, sin_t = _rope_tables(positions)
    h = x
    for i in range(p.shape[0]):
        h = _layer(i, h, p, cos_t, sin_t, g_mix, w_in, b_f, qn_a, kn_a, qn_b, kn_b, w_o, g_ffn, w_rg, b_rg,
                   w_re, b_re, w_up, w_down, g_ple, w_ple_gate, w_ple_proj)
    return h
```

```python
import functools
import math

import jax
import jax.numpy as jnp
from jax import lax
from jax.experimental import pallas as pl
from jax.experimental.pallas import tpu as pltpu

D_MODEL = 1024
HEAD_DIM = 64
N_HEADS = 8
D_SEC = N_HEADS * HEAD_DIM
N_QKV = 6 * D_SEC
ROT_DIM = HEAD_DIM // 4
ROPE_THETA = 500000.0
N_GROUPS = 4
EXPERTS_PER_GROUP = 8
N_EXPERTS = N_GROUPS * EXPERTS_PER_GROUP
D_EXPERT = 512
PLE_DIM = 256
EPS = 1e-6
NEG = -1e30
WINDOW = 128

LANES = 128
SUBLANES = 8
VMEM_LIMIT = 48 * 1024 * 1024

TM_PROJ = 512
TQ = 128
TK = 256
N_PAIRS = N_HEADS // 2
TM_MOE = 512
TM_TOK = 512
SEG_ALIGN = 2 * SUBLANES
SORT_ROWS = 2 * TM_TOK + N_EXPERTS * SEG_ALIGN
SORT_CHUNKS = SORT_ROWS // SEG_ALIGN
ROUTER_LANE0 = N_GROUPS

Q_SCALE_LOG2 = math.log2(math.e) / math.sqrt(HEAD_DIM)

F32 = jnp.float32
BF16 = jnp.bfloat16
NT_DIMS = (((1,), (1,)), ((), ()))


def _cparams(*sem):
    return pltpu.CompilerParams(dimension_semantics=sem, vmem_limit_bytes=VMEM_LIMIT)


def _resident(shape):
    return pl.BlockSpec(shape, lambda i, *_: (0,) * len(shape), pipeline_mode=pl.Buffered(1))


def _in_proj_kernel(x_ref, g_ref, w_ref, qkv_ref, f_ref, w_s, wf_s):
    @pl.when(pl.program_id(0) == 0)
    def _():
        w_s[...] = w_ref[:, 0:N_QKV].astype(BF16)
        wf_s[...] = jnp.zeros_like(wf_s)
        wf_s[:, 0:N_HEADS] = w_ref[:, N_QKV:N_QKV + N_HEADS].astype(BF16)

    x = x_ref[...]
    ms = jnp.mean(x * x, axis=-1, keepdims=True)
    a = (x * lax.rsqrt(ms + EPS) * g_ref[...]).astype(BF16)
    qkv_ref[...] = jnp.dot(a, w_s[...], preferred_element_type=F32).astype(BF16)
    f_ref[...] = jnp.dot(a, wf_s[...], preferred_element_type=F32)


def _in_proj(x2, g_mix, w_in):
    T = x2.shape[0]
    return pl.pallas_call(
        _in_proj_kernel,
        grid=(T // TM_PROJ,),
        in_specs=[
            pl.BlockSpec((TM_PROJ, D_MODEL), lambda i: (i, 0)),
            pl.BlockSpec((1, D_MODEL), lambda i: (0, 0)),
            _resident(w_in.shape),
        ],
        out_specs=[
            pl.BlockSpec((TM_PROJ, N_QKV), lambda i: (i, 0)),
            pl.BlockSpec((TM_PROJ, LANES), lambda i: (i, 0)),
        ],
        out_shape=[
            jax.ShapeDtypeStruct((T, N_QKV), BF16),
            jax.ShapeDtypeStruct((T, LANES), F32),
        ],
        scratch_shapes=[pltpu.VMEM((D_MODEL, N_QKV), BF16), pltpu.VMEM((D_MODEL, LANES), BF16)],
        compiler_params=_cparams("arbitrary"),
        name="in_proj",
    )(x2, g_mix, w_in)


BIAS_TERMS = 3


def _bias_lane(head):
    return (head // 2) * LANES + (HEAD_DIM if head % 2 == 0 else 0)


def _forget_scan_kernel(f_ref, bf_ref, kb_ref, *, seq):
    f = f_ref[...] + bf_ref[...]
    c = jnp.minimum(f, 0.0) - jnp.log1p(jnp.exp(-jnp.abs(f)))
    row = lax.broadcasted_iota(jnp.int32, c.shape, 0)
    k = 1
    while k < seq:
        c = c + jnp.where(row >= k, pltpu.roll(c, k, axis=0), 0.0)
        k *= 2
    rest = c * (-math.log2(math.e))
    r_idx = lax.broadcasted_iota(jnp.int32, (LANES, D_SEC), 0)
    c_idx = lax.broadcasted_iota(jnp.int32, (LANES, D_SEC), 1)
    base = (r_idx // 2) * LANES + jnp.where(r_idx % 2 == 0, HEAD_DIM, 0)
    out = jnp.zeros((seq, D_SEC), F32)
    for t in range(BIAS_TERMS):
        term = rest.astype(BF16)
        rest = rest - term.astype(F32)
        place = ((r_idx < N_HEADS) & (c_idx == base + t)).astype(BF16)
        out = out + jnp.dot(term, place, preferred_element_type=F32)
    kb_ref[0] = out.astype(BF16)


def _forget_scan(f, bf, batch, seq):
    return pl.pallas_call(
        functools.partial(_forget_scan_kernel, seq=seq),
        grid=(batch,),
        in_specs=[
            pl.BlockSpec((seq, LANES), lambda b: (b, 0)),
            pl.BlockSpec((1, LANES), lambda b: (0, 0)),
        ],
        out_specs=pl.BlockSpec((1, seq, D_SEC), lambda b: (b, 0, 0)),
        out_shape=jax.ShapeDtypeStruct((batch, seq, D_SEC), BF16),
        compiler_params=_cparams("parallel"),
        name="forget_scan",
    )(f, bf)


def _head_sumsq_matrix():
    r = lax.broadcasted_iota(jnp.int32, (LANES, LANES), 0) // HEAD_DIM
    c = lax.broadcasted_iota(jnp.int32, (LANES, LANES), 1) // HEAD_DIM
    return (r == c).astype(BF16)


def _qk_norm(x, gain, gmat):
    ss = jnp.dot((x * x).astype(BF16), gmat, preferred_element_type=F32)
    return x * lax.rsqrt(ss * (1.0 / HEAD_DIM) + EPS) * gain


def _stack_heads(qb, head0):
    zero = jnp.zeros_like(qb)
    return jnp.concatenate([jnp.where(head0, qb, zero), jnp.where(head0, zero, qb)], axis=0)


def _unstack(a, head0):
    return jnp.where(head0, a[:TQ], a[TQ:])


def _dilated_kernel(q_ref, k_ref, v_ref, cos_ref, sin_ref, gq_ref, gk_ref, o_ref,
                    qn_s, kn_s, v_s, acc_s, m_s, l_s, bias_s, *, seq):
    gmat = _head_sumsq_matrix()
    lane = lax.broadcasted_iota(jnp.int32, (TQ, LANES), 1)
    head0 = lane < HEAD_DIM
    chunk = 512
    lane_c = lax.broadcasted_iota(jnp.int32, (chunk, LANES), 1) % HEAD_DIM
    first_half = lane_c < ROT_DIM // 2

    def rope(t, cs, sn):
        partner = jnp.where(first_half, pltpu.roll(t, LANES - ROT_DIM // 2, axis=1),
                            pltpu.roll(t, ROT_DIM // 2, axis=1))
        return t * cs + partner * sn

    for c0 in range(0, seq, chunk):
        rows = pl.ds(c0, chunk)
        cs = cos_ref[0, rows, :]
        sn = sin_ref[0, rows, :]
        qn = _qk_norm(q_ref[0, rows, :].astype(F32), gq_ref[...], gmat)
        qn_s[rows, :] = rope(qn, cs, sn) * Q_SCALE_LOG2
        kn = _qk_norm(k_ref[0, rows, :].astype(F32), gk_ref[...], gmat)
        kn_s[rows, :] = rope(kn, cs, sn)
        v_s[rows, :] = v_ref[0, rows, :].astype(F32)

    u = lax.broadcasted_iota(jnp.int32, (2 * TQ, 2 * TQ), 0) % TQ
    c = lax.broadcasted_iota(jnp.int32, (2 * TQ, 2 * TQ), 1)
    for slot, dist0 in enumerate((0, TQ)):
        dist = dist0 + u - c
        bias_s[slot] = jnp.where((dist >= 0) & (dist <= WINDOW), 0.0, NEG)

    def scores(blk):
        p, q_rows, k_rows, nk, dist0 = blk
        qst = _stack_heads(qn_s[q_rows, :].astype(BF16), head0)
        s = lax.dot_general(qst, kn_s[k_rows, :].astype(BF16), NT_DIMS, preferred_element_type=F32)
        return s + bias_s[dist0 // TQ, :, 0:nk]

    def finish(blk, s):
        p, q_rows, k_rows, nk, dist0 = blk
        m = jnp.max(s, axis=-1, keepdims=True)
        pr = jnp.exp2(s - m)
        l = jnp.sum(pr, axis=-1, keepdims=True)
        acc = jnp.dot(pr.astype(BF16), v_s[k_rows, :].astype(BF16), preferred_element_type=F32)
        acc_s[p, q_rows, :] = _unstack(acc, head0)
        m_s[p, q_rows, :] = _unstack(jnp.broadcast_to(m, (2 * TQ, LANES)), head0)
        l_s[p, q_rows, :] = _unstack(jnp.broadcast_to(l, (2 * TQ, LANES)), head0)

    blocks = []
    for i in range(seq // TQ):
        kb0 = max(i - 1, 0)
        blocks.append((0, pl.ds(i * TQ, TQ), pl.ds(kb0 * TQ, 2 * TQ), 2 * TQ, (i - kb0) * TQ))
    for r in range(4):
        for n in range(seq // (4 * TQ)):
            kb0 = max(n - 1, 0)
            blocks.append((1, pl.ds(4 * TQ * n + r, TQ, stride=4), pl.ds(4 * TQ * kb0 + r, 2 * TQ, stride=4),
                           2 * TQ, (n - kb0) * TQ))
    for r in range(16):
        rows = pl.ds(r, TQ, stride=16)
        blocks.append((2, rows, rows, TQ, 0))

    ahead = 2
    pending = [scores(b) for b in blocks[:ahead]]
    for idx, b in enumerate(blocks):
        s = pending.pop(0)
        if idx + ahead < len(blocks):
            pending.append(scores(blocks[idx + ahead]))
        finish(b, s)

    for c0 in range(0, seq, chunk):
        rows = pl.ds(c0, chunk)
        m0, m1, m2 = m_s[0, rows, :], m_s[1, rows, :], m_s[2, rows, :]
        mm = jnp.maximum(jnp.maximum(m0, m1), m2)
        e0, e1, e2 = jnp.exp2(m0 - mm), jnp.exp2(m1 - mm), jnp.exp2(m2 - mm)
        num = acc_s[0, rows, :] * e0 + acc_s[1, rows, :] * e1 + acc_s[2, rows, :] * e2
        den = l_s[0, rows, :] * e0 + l_s[1, rows, :] * e1 + l_s[2, rows, :] * e2
        o_ref[0, rows, :] = (num / den).astype(BF16)


def _dilated_attn(qkv3, cos_t, sin_t, gq, gk):
    batch, seq, _ = qkv3.shape
    sec = D_SEC // LANES
    blk = lambda s: pl.BlockSpec((1, seq, LANES), lambda b, h, s=s: (b, 0, s * sec + h))
    tab = pl.BlockSpec((1, seq, LANES), lambda b, h: (b, 0, 0))
    vec = pl.BlockSpec((1, LANES), lambda b, h: (0, 0))
    return pl.pallas_call(
        functools.partial(_dilated_kernel, seq=seq),
        grid=(batch, N_PAIRS),
        in_specs=[blk(0), blk(1), blk(2), tab, tab, vec, vec],
        out_specs=pl.BlockSpec((1, seq, LANES), lambda b, h: (b, 0, h)),
        out_shape=jax.ShapeDtypeStruct((batch, seq, D_SEC), BF16),
        scratch_shapes=[
            pltpu.VMEM((seq, LANES), F32), pltpu.VMEM((seq, LANES), F32), pltpu.VMEM((seq, LANES), F32),
            pltpu.VMEM((3, seq, LANES), F32), pltpu.VMEM((3, seq, LANES), F32),
            pltpu.VMEM((3, seq, LANES), F32), pltpu.VMEM((2, 2 * TQ, 2 * TQ), F32),
        ],
        compiler_params=_cparams("parallel", "parallel"),
        name="dilated_attn",
    )(qkv3, qkv3, qkv3, cos_t, sin_t, gq, gk)


def _fox_kernel(q_ref, k_ref, v_ref, kb_ref, gq_ref, gk_ref, o_ref, qat_s, ka_s, vt_s, acc_s, tri_s, *, seq):
    gmat = _head_sumsq_matrix()
    chunk = 512
    lane = lax.broadcasted_iota(jnp.int32, (chunk, LANES), 1)
    low = lane < HEAD_DIM
    sel = jnp.where((lane >= HEAD_DIM) & (lane < HEAD_DIM + BIAS_TERMS), 1.0, 0.0)

    for c0 in range(0, seq, chunk):
        rows = pl.ds(c0, chunk)
        for g in range(N_PAIRS):
            lanes = slice(g * LANES, (g + 1) * LANES)
            qn = _qk_norm(q_ref[0, rows, lanes].astype(F32), gq_ref[...], gmat) * Q_SCALE_LOG2
            for h, qa in ((2 * g, jnp.where(low, qn, sel)),
                          (2 * g + 1, jnp.where(low, pltpu.roll(qn, HEAD_DIM, axis=1), sel))):
                qt = qa.T
                for cc in range(chunk // TK):
                    qat_s[c0 // TK + cc, h * LANES:(h + 1) * LANES, :] = qt[:, cc * TK:(cc + 1) * TK].astype(BF16)
            kn =_qk_norm(k_ref[0, rows, lanes].astype(F32), gk_ref[...], gmat)
            kb = kb_ref[0, rows, lanes].astype(F32)
            ka_s[2 * g, rows, :] = jnp.where(low, kn, kb).astype(BF16)
            ka_s[2 * g + 1, rows, :] = pltpu.roll(jnp.where(low, kb, kn), HEAD_DIM, axis=1).astype(BF16)
            vt = v_ref[0, rows, lanes].astype(F32).T
            for cc in range(chunk // TK):
                vt_s[c0 // TK + cc, lanes, :] = vt[:, cc * TK:(cc + 1) * TK].astype(BF16)

    r = lax.broadcasted_iota(jnp.int32, (TK, TK), 0)
    c = lax.broadcasted_iota(jnp.int32, (TK, TK), 1)
    tri_s[...] = jnp.where(r <= c, 0.0, NEG)

    def q_block(qi, carry):
        q_rows = pl.ds(pl.multiple_of(qi * TK, TK), TK)
        for h in range(N_HEADS):
            acc_s[h] = jnp.zeros((HEAD_DIM, TK), F32)

        def step(j, st, diagonal):
            k_rows = pl.ds(pl.multiple_of(j * TK, TK), TK)

            def scores(h):
                s = jnp.dot(ka_s[h, k_rows, :], qat_s[qi, h * LANES:(h + 1) * LANES, :], preferred_element_type=F32)
                return s + tri_s[...] if diagonal else s

            ahead = 2
            pending = [scores(h) for h in range(ahead)]
            new = []
            for h in range(N_HEADS):
                m, l = st[h]
                s = pending.pop(0)
                if h + ahead < N_HEADS:
                    pending.append(scores(h + ahead))
                m_new = jnp.maximum(m, jnp.max(s, axis=0, keepdims=True))
                alpha = jnp.exp2(m - m_new)
                pr = jnp.exp2(s - m_new)
                l_new = alpha * l + jnp.sum(pr, axis=0, keepdims=True)
                pv = jnp.dot(vt_s[j, h * HEAD_DIM:(h + 1) * HEAD_DIM, :], pr.astype(BF16),
                             preferred_element_type=F32)
                acc_s[h] = alpha * acc_s[h] + pv
                new.append((m_new, l_new))
            return tuple(new)

        init = tuple((jnp.full((1, TK), NEG, F32), jnp.zeros((1, TK), F32)) for _ in range(N_HEADS))
        st = lax.fori_loop(0, qi // 2, lambda jj, st: step(2 * jj + 1, step(2 * jj, st, False), False), init)
        st = lax.cond(qi % 2 == 1, lambda st: step(qi - 1, st, False), lambda st: st, st)
        st = step(qi, st, True)
        for g in range(N_PAIRS):
            o2 = jnp.concatenate([acc_s[2 * g] / st[2 * g][1], acc_s[2 * g + 1] / st[2 * g + 1][1]], axis=0)
            o_ref[0, q_rows, g * LANES:(g + 1) * LANES] = o2.T.astype(BF16)
        return carry

    lax.fori_loop(0, seq // TK, q_block, 0)


def _fox_attn(qkv3, kbias, gq, gk):
    batch, seq, _ = qkv3.shape
    blk = lambda s: pl.BlockSpec((1, seq, D_SEC), lambda b, s=s: (b, 0, s))
    vec = pl.BlockSpec((1, LANES), lambda b: (0, 0))
    return pl.pallas_call(
        functools.partial(_fox_kernel, seq=seq),
        grid=(batch,),
        in_specs=[blk(3), blk(4), blk(5), pl.BlockSpec((1, seq, D_SEC), lambda b: (b, 0, 0)), vec, vec],
        out_specs=pl.BlockSpec((1, seq, D_SEC), lambda b: (b, 0, 0)),
        out_shape=jax.ShapeDtypeStruct((batch, seq, D_SEC), BF16),
        scratch_shapes=[
            pltpu.VMEM((seq // TK, N_HEADS * LANES, TK), BF16),
            pltpu.VMEM((N_HEADS, seq, LANES), BF16),
            pltpu.VMEM((seq // TK, D_SEC, TK), BF16),
            pltpu.VMEM((N_HEADS, HEAD_DIM, TK), F32),
            pltpu.VMEM((TK, TK), F32),
        ],
        compiler_params=_cparams("parallel"),
        name="fox_attn",
    )(qkv3, qkv3, qkv3, kbias, gq, gk)


def _out_router_kernel(oa_ref, ob_ref, x_ref, wo_ref, g_ref, wr_ref, br_ref,
                       h1_ref, msort_ref, route_ref, cnt_ref, wo_s):
    tm = TM_TOK

    @pl.when(pl.program_id(0) == 0)
    def _():
        wo_s[...] = wo_ref[...].astype(BF16)

    h1 = (x_ref[...]
          + jnp.dot(oa_ref[...], wo_s[0:D_SEC, :], preferred_element_type=F32)
          + jnp.dot(ob_ref[...], wo_s[D_SEC:2 * D_SEC, :], preferred_element_type=F32))
    h1_ref[...] = h1
    ms = jnp.mean(h1 * h1, axis=-1, keepdims=True)
    mn = h1 * lax.rsqrt(ms + EPS) * g_ref[...]

    logits = jnp.dot(mn.astype(BF16), wr_ref[...], preferred_element_type=F32) + br_ref[...]
    lane = lax.broadcasted_iota(jnp.int32, (tm, LANES), 1).astype(F32)
    big = float(LANES)

    def first_argmax(vals):
        vmax = jnp.max(vals, axis=-1, keepdims=True)
        idx = jnp.min(jnp.where(vals == vmax, lane, big), axis=-1, keepdims=True)
        return vmax, idx

    lg = jnp.where(lane < N_GROUPS, logits, -jnp.inf)
    gmax, gidx = first_argmax(lg)
    gw = 1.0 / jnp.sum(jnp.exp(lg - gmax), axis=-1, keepdims=True)
    lo = ROUTER_LANE0 + EXPERTS_PER_GROUP * gidx
    le = jnp.where((lane >= lo) & (lane < lo + EXPERTS_PER_GROUP), logits, -jnp.inf)
    v0, i0 = first_argmax(le)
    v1, i1 = first_argmax(jnp.where(lane == i0, -jnp.inf, le))
    ex = jnp.exp(v1 - v0)
    w0 = gw / (1.0 + ex)
    w1 = gw * ex / (1.0 + ex)

    sel0 = lane == i0
    sel1 = lane == i1
    onehot = jnp.where(sel0 | sel1, 1.0, 0.0)
    r = lax.broadcasted_iota(jnp.int32, (tm, tm), 0)
    c = lax.broadcasted_iota(jnp.int32, (tm, tm), 1)
    ltri = (c <= r).astype(BF16)
    incl = jnp.dot(ltri, onehot.astype(BF16), preferred_element_type=F32)
    excl = incl - onehot
    counts = incl[tm - 1:tm, :]
    cnt_ref[0] = jnp.broadcast_to(counts, (SUBLANES, LANES))

    seg_rows = jnp.floor((counts + (SEG_ALIGN - 1.0)) * (1.0 / SEG_ALIGN)) * SEG_ALIGN
    seg_rows = jnp.broadcast_to(seg_rows, (tm, LANES))
    lp0 = jnp.sum(jnp.where(lane < i0, seg_rows, 0.0) + jnp.where(sel0, excl, 0.0), axis=-1, keepdims=True)
    lp1 = jnp.sum(jnp.where(lane < i1, seg_rows, 0.0) + jnp.where(sel1, excl, 0.0), axis=-1, keepdims=True)

    rec = jnp.zeros((tm, LANES), F32)
    for pos, val in enumerate((i0 - ROUTER_LANE0, i1 - ROUTER_LANE0, lp0, lp1, w0, w1)):
        rec = jnp.where(lane == float(pos), val, rec)
    route_ref[...] = rec[:, 0:SUBLANES]

    rec_t = rec.T
    out_row = lax.broadcasted_iota(jnp.int32, (SORT_ROWS, tm), 0).astype(F32)
    pick = jnp.where((out_row == rec_t[2:3, :]) | (out_row == rec_t[3:4, :]), 1.0, 0.0).astype(BF16)
    msort_ref[...] = jnp.dot(pick, mn.astype(BF16), preferred_element_type=F32).astype(BF16)


def _out_router(oa, ob, x2, w_o, g_ffn, w_r, b_r):
    T = x2.shape[0]
    tm = TM_TOK
    row = lambda w: pl.BlockSpec((tm, w), lambda i: (i, 0))
    const = lambda a, b: pl.BlockSpec((a, b), lambda i: (0, 0))
    return pl.pallas_call(
        _out_router_kernel,
        grid=(T // tm,),
        in_specs=[row(D_SEC), row(D_SEC), row(D_MODEL), _resident((D_MODEL, D_MODEL)), const(1, D_MODEL),
                  const(D_MODEL, LANES), const(1, LANES)],
        out_specs=[row(D_MODEL), pl.BlockSpec((SORT_ROWS, D_MODEL), lambda i: (i, 0)), row(SUBLANES),
                   pl.BlockSpec((1, SUBLANES, LANES), lambda i: (i, 0, 0))],
        out_shape=[
            jax.ShapeDtypeStruct((T, D_MODEL), F32),
            jax.ShapeDtypeStruct((T // tm * SORT_ROWS, D_MODEL), BF16),
            jax.ShapeDtypeStruct((T, SUBLANES), F32),
            jax.ShapeDtypeStruct((T // tm, SUBLANES, LANES), F32),
        ],
        scratch_shapes=[pltpu.VMEM((D_MODEL, D_MODEL), BF16)],
        compiler_params=_cparams("arbitrary"),
        name="out_router",
    )(oa, ob, x2, w_o, g_ffn, w_r, b_r)


def _chunk_row(c):
    return pl.multiple_of(c * SEG_ALIGN, SEG_ALIGN)


MOE_CHUNKS = TM_MOE // SEG_ALIGN


def _moe_kernel(tile_e_ref, nused_ref, src_ref, next_e_ref, slot_e_ref, msort_ref, wup_ref, wdn_ref, ys_ref,
                x_s, wup_f, wdn_f, wup_s, wdn_s, xsem, wsem):
    i = pl.program_id(0)
    nused = nused_ref[0]

    def gather(k, act):
        def chunk(c, carry):
            src = pl.multiple_of(src_ref[k * MOE_CHUNKS + c], SEG_ALIGN)
            act(pltpu.make_async_copy(msort_ref.at[pl.ds(src, SEG_ALIGN)],
                                      x_s.at[k % 2, pl.ds(_chunk_row(c), SEG_ALIGN)], xsem.at[k % 2]))
            return carry
        lax.fori_loop(0, MOE_CHUNKS, chunk, 0)

    def weights(e, s, act):
        act(pltpu.make_async_copy(wup_ref.at[e], wup_f.at[s], wsem.at[0, s]))
        act(pltpu.make_async_copy(wdn_ref.at[e], wdn_f.at[s], wsem.at[1, s]))

    @pl.when(i == 0)
    def _():
        gather(0, lambda c: c.start())
        weights(tile_e_ref[0], 0, lambda c: c.start())

    @pl.when(i + 1 < nused)
    def _():
        gather(i + 1, lambda c: c.start())

    @pl.when(i < nused)
    def _():
        e = tile_e_ref[i]

        @pl.when((i == 0) | (tile_e_ref[jnp.maximum(i - 1, 0)] != e))
        def _():
            s = slot_e_ref[e]
            weights(e, s, lambda c: c.wait())

            @pl.when(next_e_ref[e] >= 0)
            def _():
                weights(next_e_ref[e], 1 - s, lambda c: c.start())
            wup_s[...] = wup_f[s].astype(BF16)
            wdn_s[...] = wdn_f[s].astype(BF16)

        gather(i, lambda c: c.wait())
        hu = jnp.dot(x_s[i % 2], wup_s[...], preferred_element_type=F32)
        gate = hu[:, :D_EXPERT]
        hid = gate * (1.0 / (1.0 + jnp.exp(-gate))) * hu[:, D_EXPERT:]
        ys_ref[...] = jnp.dot(hid.astype(BF16), wdn_s[...], preferred_element_type=F32).astype(BF16)

    @pl.when(i >= nused)
    def _():
        ys_ref[...] = jnp.zeros_like(ys_ref)


def _moe_experts(tile_e, nused, src_chunk, next_e, slot_e, msort, w_up, w_down):
    n_tiles = tile_e.shape[0]
    grid_spec = pltpu.PrefetchScalarGridSpec(
        num_scalar_prefetch=5,
        grid=(n_tiles,),
        in_specs=[pl.BlockSpec(memory_space=pl.ANY)] * 3,
        out_specs=pl.BlockSpec((TM_MOE, D_MODEL), lambda i, *_: (i, 0)),
        scratch_shapes=[
            pltpu.VMEM((2, TM_MOE, D_MODEL), BF16),
            pltpu.VMEM((2, D_MODEL, 2 * D_EXPERT), F32), pltpu.VMEM((2, D_EXPERT, D_MODEL), F32),
            pltpu.VMEM((D_MODEL, 2 * D_EXPERT), BF16), pltpu.VMEM((D_EXPERT, D_MODEL), BF16),
            pltpu.SemaphoreType.DMA((2,)), pltpu.SemaphoreType.DMA((2, 2)),
        ],
    )
    return pl.pallas_call(
        _moe_kernel,
        grid_spec=grid_spec,
        out_shape=jax.ShapeDtypeStruct((n_tiles * TM_MOE, D_MODEL), BF16),
        compiler_params=_cparams("arbitrary"),
        name="moe_experts",
    )(tile_e, nused, src_chunk, next_e, slot_e, msort, w_up, w_down)


def _combine_kernel(nchunk_ref, src_ref, ys_ref, h1_ref, route_ref, p_ref,
                    g_ref, wg_ref, wp_ref, o_ref, gath_s, sems, wg_s, wp_s):
    i = pl.program_id(0)
    n_steps = pl.num_programs(0)

    @pl.when(i == 0)
    def _():
        wg_s[...] = wg_ref[...].astype(BF16)
        wp_s[...] = wp_ref[...].astype(BF16)
    tm = TM_TOK
    slot = i % 2

    def gather(t, s, act):
        def chunk(c, carry):
            src = pl.multiple_of(src_ref[t * SORT_CHUNKS + c], SEG_ALIGN)
            act(pltpu.make_async_copy(ys_ref.at[pl.ds(src, SEG_ALIGN)],
                                      gath_s.at[s, pl.ds(_chunk_row(c), SEG_ALIGN)], sems.at[s]))
            return carry
        lax.fori_loop(0, nchunk_ref[t], chunk, 0)

    @pl.when(i == 0)
    def _():
        gath_s[...] = jnp.zeros_like(gath_s)
        gather(0, 0, lambda c: c.start())

    @pl.when(i + 1 < n_steps)
    def _():
        gather(i + 1, 1 - slot, lambda c: c.start())

    gather(i, slot, lambda c: c.wait())

    route = route_ref[...]
    col = lax.broadcasted_iota(jnp.int32, (tm, SORT_ROWS), 1).astype(F32)
    weights = (jnp.where(col == route[:, 2:3], route[:, 4:5], 0.0)
               + jnp.where(col == route[:, 3:4], route[:, 5:6], 0.0)).astype(BF16)
    y = jnp.dot(weights, gath_s[slot], preferred_element_type=F32)
    h2 = h1_ref[...] + y
    ms = jnp.mean(h2 * h2, axis=-1, keepdims=True)
    n = (h2 * lax.rsqrt(ms + EPS) * g_ref[...]).astype(BF16)
    z = jnp.dot(n, wg_s[...], preferred_element_type=F32)
    gate = 1.0 / (1.0 + jnp.exp(-z))
    ple = jnp.dot(p_ref[...].astype(BF16), wp_s[...], preferred_element_type=F32)
    o_ref[...] = h2 + gate * ple


def _combine_ple(n_chunks, chunk_dst, ys, h1, route, p2, g_ple, w_gate, w_proj):
    T = h1.shape[0]
    tm = TM_TOK
    row = lambda w: pl.BlockSpec((tm, w), lambda i, *_: (i, 0))
    const = lambda a, b: pl.BlockSpec((a, b), lambda i, *_: (0, 0))
    grid_spec = pltpu.PrefetchScalarGridSpec(
        num_scalar_prefetch=2,
        grid=(T // tm,),
        in_specs=[pl.BlockSpec(memory_space=pl.ANY), row(D_MODEL), row(SUBLANES), row(PLE_DIM),
                  const(1, D_MODEL), _resident((D_MODEL, D_MODEL)), _resident((PLE_DIM, D_MODEL))],
        out_specs=row(D_MODEL),
        scratch_shapes=[pltpu.VMEM((2, SORT_ROWS, D_MODEL), BF16), pltpu.SemaphoreType.DMA((2,)),
                        pltpu.VMEM((D_MODEL, D_MODEL), BF16), pltpu.VMEM((PLE_DIM, D_MODEL), BF16)],
    )
    return pl.pallas_call(
        _combine_kernel,
        grid_spec=grid_spec,
        out_shape=jax.ShapeDtypeStruct((T, D_MODEL), F32),
        compiler_params=_cparams("arbitrary"),
        name="combine_ple",
    )(n_chunks, chunk_dst, ys, h1, route, p2, g_ple, w_gate, w_proj)


def _rope_tables(positions):
    half = ROT_DIM // 2
    inv = ROPE_THETA ** (-jnp.arange(0, ROT_DIM, 2, dtype=F32) / ROT_DIM)
    ang = positions.astype(F32)[..., None] * inv
    cos, sin = jnp.cos(ang), jnp.sin(ang)
    ones = jnp.ones(ang.shape[:-1] + (HEAD_DIM - ROT_DIM,), F32)
    cos_h = jnp.concatenate([cos, cos, ones], axis=-1)
    sin_h = jnp.concatenate([-sin, sin, 0.0 * ones], axis=-1)
    return jnp.tile(cos_h, (1, 1, 2)), jnp.tile(sin_h, (1, 1, 2))


def _layer(i, h, p, cos_t, sin_t, g_mix, w_in, b_f, qn_a, kn_a, qn_b, kn_b, w_o, g_ffn, w_rg, b_rg,
           w_re, b_re, w_up, w_down, g_ple, w_ple_gate, w_ple_proj):
    B, S, _ = h.shape
    T = B * S
    x2 = h.reshape(T, D_MODEL)

    bf = jnp.zeros((1, LANES), F32).at[0, :N_HEADS].set(b_f[i])
    pair = lambda g: jnp.tile(g, 2).reshape(1, LANES)
    w_r = jnp.zeros((D_MODEL, LANES), F32)
    w_r = w_r.at[:, :N_GROUPS].set(w_rg[i])
    w_r = w_r.at[:, ROUTER_LANE0:ROUTER_LANE0 + N_EXPERTS].set(
        jnp.transpose(w_re[i], (1, 0, 2)).reshape(D_MODEL, N_EXPERTS))
    b_r = jnp.zeros((1, LANES), F32).at[0, :N_GROUPS].set(b_rg[i])
    b_r = b_r.at[0, ROUTER_LANE0:ROUTER_LANE0 + N_EXPERTS].set(b_re[i].reshape(-1))

    qkv, f = _in_proj(x2, g_mix[i].reshape(1, -1), w_in[i])
    kbias = _forget_scan(f, bf, B, S)
    qkv3 = qkv.reshape(B, S, N_QKV)
    oa = _dilated_attn(qkv3, cos_t, sin_t, pair(qn_a[i]), pair(kn_a[i]))
    ob = _fox_attn(qkv3, kbias, pair(qn_b[i]), pair(kn_b[i]))

    h1, msort, route, cnt = _out_router(oa.reshape(T, D_SEC), ob.reshape(T, D_SEC), x2, w_o[i],
                                        g_ffn[i].reshape(1, -1), w_r.astype(BF16), b_r)

    n_tok_tiles = T // TM_TOK
    counts = cnt[:, 0, ROUTER_LANE0:ROUTER_LANE0 + N_EXPERTS].astype(jnp.int32)
    seg_rows = (counts + SEG_ALIGN - 1) // SEG_ALIGN * SEG_ALIGN
    seg_local = jnp.cumsum(seg_rows, axis=1) - seg_rows
    rows_e = jnp.sum(seg_rows, axis=0)
    tile_end = jnp.cumsum((rows_e + TM_MOE - 1) // TM_MOE)
    off = jnp.concatenate([jnp.zeros((1,), jnp.int32), tile_end * TM_MOE]).astype(jnp.int32)
    seg_global = off[None, :N_EXPERTS] + jnp.cumsum(seg_rows, axis=0) - seg_rows
    n_tiles = (2 * T + n_tok_tiles * N_EXPERTS * (SEG_ALIGN - 1)) // TM_MOE + N_EXPERTS
    nused = tile_end[-1:].astype(jnp.int32)
    tile_ids = jnp.minimum(jnp.arange(n_tiles, dtype=jnp.int32), nused[0] - 1)
    tile_e = jnp.sum((tile_ids[:, None] >= tile_end[None, :]).astype(jnp.int32), axis=1).astype(jnp.int32)
    chunk_row = jnp.arange(SORT_CHUNKS, dtype=jnp.int32)[None, :, None] * SEG_ALIGN
    lo, hi = seg_local[:, None, :], (seg_local + seg_rows)[:, None, :]
    chunk_dst = jnp.sum(jnp.where((chunk_row >= lo) & (chunk_row < hi), seg_global[:, None, :] + chunk_row - lo, 0),
                        axis=2).reshape(-1).astype(jnp.int32)
    n_chunks = (jnp.sum(seg_rows, axis=1) // SEG_ALIGN).astype(jnp.int32)
    sorted_row = jnp.arange(n_tiles * MOE_CHUNKS, dtype=jnp.int32)[:, None] * SEG_ALIGN
    seg_lo = seg_global.reshape(1, -1)
    seg_src = (jnp.arange(n_tok_tiles, dtype=jnp.int32)[:, None] * SORT_ROWS + seg_local).reshape(1, -1)
    in_seg = (sorted_row >= seg_lo) & (sorted_row < seg_lo + seg_rows.reshape(1, -1))
    src_chunk = jnp.where(jnp.any(in_seg, axis=1),
                          jnp.sum(jnp.where(in_seg, seg_src + sorted_row - seg_lo, 0), axis=1),
                          SORT_ROWS - SEG_ALIGN).astype(jnp.int32)
    e_ids = jnp.arange(N_EXPERTS, dtype=jnp.int32)
    active = rows_e > 0
    later_active = jnp.where(active[None, :] & (e_ids[None, :] > e_ids[:, None]), e_ids[None, :], N_EXPERTS)
    next_e = jnp.min(later_active, axis=1)
    next_e = jnp.where(next_e == N_EXPERTS, -1, next_e).astype(jnp.int32)
    slot_e = ((jnp.cumsum(active.astype(jnp.int32)) - 1) % 2).astype(jnp.int32)

    ys = _moe_experts(tile_e, nused, src_chunk, next_e, slot_e, msort, w_up[i], w_down[i])
    out = _combine_ple(n_chunks, chunk_dst, ys, h1, route, p[i].reshape(T, PLE_DIM),
                       g_ple[i].reshape(1, -1), w_ple_gate[i], w_ple_proj[i])
    return out.reshape(B, S, D_MODEL)


def kernel(x, p, positions, g_mix, w_in, b_f, qn_a, kn_a, qn_b, kn_b, w_o, g_ffn, w_rg, b_rg, w_re, b_re,
           w_up, w_down, g_ple, w_ple_gate, w_ple_proj):
    cos_t, sin_t = _rope_tables(positions)
    h = x
    for i in range(p.shape[0]):
        h = _layer(i, h, p, cos_t, sin_t, g_mix, w_in, b_f, qn_a, kn_a, qn_b, kn_b, w_o, g_ffn, w_rg, b_rg,
                   w_re, b_re, w_up, w_down, g_ple, w_ple_gate, w_ple_proj)
    return h
```

```python
import functools
import math

import jax
import jax.numpy as jnp
from jax import lax
from jax.experimental import pallas as pl
from jax.experimental.pallas import tpu as pltpu

D_MODEL = 1024
HEAD_DIM = 64
N_HEADS = 8
D_SEC = N_HEADS * HEAD_DIM
N_QKV = 6 * D_SEC
ROT_DIM = HEAD_DIM // 4
ROPE_THETA = 500000.0
N_GROUPS = 4
EXPERTS_PER_GROUP = 8
N_EXPERTS = N_GROUPS * EXPERTS_PER_GROUP
D_EXPERT = 512
PLE_DIM = 256
EPS = 1e-6
NEG = -1e30
WINDOW = 128

LANES = 128
SUBLANES = 8
VMEM_LIMIT = 48 * 1024 * 1024

TM_PROJ = 512
TQ = 128
TK = 256
N_PAIRS = N_HEADS // 2
TM_MOE = 512
TM_TOK = 512
SEG_ALIGN = 2 * SUBLANES
SORT_ROWS = 2 * TM_TOK + N_EXPERTS * SEG_ALIGN
SORT_CHUNKS = SORT_ROWS // SEG_ALIGN
ROUTER_LANE0 = N_GROUPS

Q_SCALE_LOG2 = math.log2(math.e) / math.sqrt(HEAD_DIM)

F32 = jnp.float32
BF16 = jnp.bfloat16
NT_DIMS = (((1,), (1,)), ((), ()))


def _cparams(*sem):
    return pltpu.CompilerParams(dimension_semantics=sem, vmem_limit_bytes=VMEM_LIMIT)


def _resident(shape):
    return pl.BlockSpec(shape, lambda i, *_: (0,) * len(shape), pipeline_mode=pl.Buffered(1))


def _in_proj_kernel(x_ref, g_ref, w_ref, qkv_ref, f_ref, w_s, wf_s):
    @pl.when(pl.program_id(0) == 0)
    def _():
        w_s[...] = w_ref[:, 0:N_QKV].astype(BF16)
        wf_s[...] = jnp.zeros_like(wf_s)
        wf_s[:, 0:N_HEADS] = w_ref[:, N_QKV:N_QKV + N_HEADS].astype(BF16)

    x = x_ref[...]
    ms = jnp.mean(x * x, axis=-1, keepdims=True)
    a = (x * lax.rsqrt(ms + EPS) * g_ref[...]).astype(BF16)
    qkv_ref[...] = jnp.dot(a, w_s[...], preferred_element_type=F32).astype(BF16)
    f_ref[...] = jnp.dot(a, wf_s[...], preferred_element_type=F32)


def _in_proj(x2, g_mix, w_in):
    T = x2.shape[0]
    return pl.pallas_call(
        _in_proj_kernel,
        grid=(T // TM_PROJ,),
        in_specs=[
            pl.BlockSpec((TM_PROJ, D_MODEL), lambda i: (i, 0)),
            pl.BlockSpec((1, D_MODEL), lambda i: (0, 0)),
            _resident(w_in.shape),
        ],
        out_specs=[
            pl.BlockSpec((TM_PROJ, N_QKV), lambda i: (i, 0)),
            pl.BlockSpec((TM_PROJ, LANES), lambda i: (i, 0)),
        ],
        out_shape=[
            jax.ShapeDtypeStruct((T, N_QKV), BF16),
            jax.ShapeDtypeStruct((T, LANES), F32),
        ],
        scratch_shapes=[pltpu.VMEM((D_MODEL, N_QKV), BF16), pltpu.VMEM((D_MODEL, LANES), BF16)],
        compiler_params=_cparams("arbitrary"),
        name="in_proj",
    )(x2, g_mix, w_in)


BIAS_TERMS = 3


def _bias_lane(head):
    return (head // 2) * LANES + (HEAD_DIM if head % 2 == 0 else 0)


def _forget_scan_kernel(f_ref, bf_ref, kb_ref, *, seq):
    f = f_ref[...] + bf_ref[...]
    c = jnp.minimum(f, 0.0) - jnp.log1p(jnp.exp(-jnp.abs(f)))
    row = lax.broadcasted_iota(jnp.int32, c.shape, 0)
    k = 1
    while k < seq:
        c = c + jnp.where(row >= k, pltpu.roll(c, k, axis=0), 0.0)
        k *= 2
    rest = c * (-math.log2(math.e))
    r_idx = lax.broadcasted_iota(jnp.int32, (LANES, D_SEC), 0)
    c_idx = lax.broadcasted_iota(jnp.int32, (LANES, D_SEC), 1)
    base = (r_idx // 2) * LANES + jnp.where(r_idx % 2 == 0, HEAD_DIM, 0)
    out = jnp.zeros((seq, D_SEC), F32)
    for t in range(BIAS_TERMS):
        term = rest.astype(BF16)
        rest = rest - term.astype(F32)
        place = ((r_idx < N_HEADS) & (c_idx == base + t)).astype(BF16)
        out = out + jnp.dot(term, place, preferred_element_type=F32)
    kb_ref[0] = out.astype(BF16)


def _forget_scan(f, bf, batch, seq):
    return pl.pallas_call(
        functools.partial(_forget_scan_kernel, seq=seq),
        grid=(batch,),
        in_specs=[
            pl.BlockSpec((seq, LANES), lambda b: (b, 0)),
            pl.BlockSpec((1, LANES), lambda b: (0, 0)),
        ],
        out_specs=pl.BlockSpec((1, seq, D_SEC), lambda b: (b, 0, 0)),
        out_shape=jax.ShapeDtypeStruct((batch, seq, D_SEC), BF16),
        compiler_params=_cparams("parallel"),
        name="forget_scan",
    )(f, bf)


def _head_sumsq_matrix():
    r = lax.broadcasted_iota(jnp.int32, (LANES, LANES), 0) // HEAD_DIM
    c = lax.broadcasted_iota(jnp.int32, (LANES, LANES), 1) // HEAD_DIM
    return (r == c).astype(BF16)


def _qk_norm(x, gain, gmat):
    ss = jnp.dot((x * x).astype(BF16), gmat, preferred_element_type=F32)
    return x * lax.rsqrt(ss * (1.0 / HEAD_DIM) + EPS) * gain


def _stack_heads(qb, head0):
    zero = jnp.zeros_like(qb)
    return jnp.concatenate([jnp.where(head0, qb, zero), jnp.where(head0, zero, qb)], axis=0)


def _unstack(a, head0):
    return jnp.where(head0, a[:TQ], a[TQ:])


def _dilated_kernel(q_ref, k_ref, v_ref, cos_ref, sin_ref, gq_ref, gk_ref, o_ref,
                    qn_s, kn_s, v_s, acc_s, m_s, l_s, bias_s, *, seq):
    gmat = _head_sumsq_matrix()
    lane = lax.broadcasted_iota(jnp.int32, (TQ, LANES), 1)
    head0 = lane < HEAD_DIM
    chunk = 512
    lane_c = lax.broadcasted_iota(jnp.int32, (chunk, LANES), 1) % HEAD_DIM
    first_half = lane_c < ROT_DIM // 2

    def rope(t, cs, sn):
        partner = jnp.where(first_half, pltpu.roll(t, LANES - ROT_DIM // 2, axis=1),
                            pltpu.roll(t, ROT_DIM // 2, axis=1))
        return t * cs + partner * sn

    for c0 in range(0, seq, chunk):
        rows = pl.ds(c0, chunk)
        cs = cos_ref[0, rows, :]
        sn = sin_ref[0, rows, :]
        qn = _qk_norm(q_ref[0, rows, :].astype(F32), gq_ref[...], gmat)
        qn_s[rows, :] = rope(qn, cs, sn) * Q_SCALE_LOG2
        kn = _qk_norm(k_ref[0, rows, :].astype(F32), gk_ref[...], gmat)
        kn_s[rows, :] = rope(kn, cs, sn)
        v_s[rows, :] = v_ref[0, rows, :].astype(F32)

    u = lax.broadcasted_iota(jnp.int32, (2 * TQ, 2 * TQ), 0) % TQ
    c = lax.broadcasted_iota(jnp.int32, (2 * TQ, 2 * TQ), 1)
    for slot, dist0 in enumerate((0, TQ)):
        dist = dist0 + u - c
        bias_s[slot] = jnp.where((dist >= 0) & (dist <= WINDOW), 0.0, NEG)

    def scores(blk):
        p, q_rows, k_rows, nk, dist0 = blk
        qst = _stack_heads(qn_s[q_rows, :].astype(BF16), head0)
        s = lax.dot_general(qst, kn_s[k_rows, :].astype(BF16), NT_DIMS, preferred_element_type=F32)
        return s + bias_s[dist0 // TQ, :, 0:nk]

    def finish(blk, s):
        p, q_rows, k_rows, nk, dist0 = blk
        m = jnp.max(s, axis=-1, keepdims=True)
        pr = jnp.exp2(s - m)
        l = jnp.sum(pr, axis=-1, keepdims=True)
        acc = jnp.dot(pr.astype(BF16), v_s[k_rows, :].astype(BF16), preferred_element_type=F32)
        acc_s[p, q_rows, :] = _unstack(acc, head0)
        m_s[p, q_rows, :] = _unstack(jnp.broadcast_to(m, (2 * TQ, LANES)), head0)
        l_s[p, q_rows, :] = _unstack(jnp.broadcast_to(l, (2 * TQ, LANES)), head0)

    blocks = []
    for i in range(seq // TQ):
        kb0 = max(i - 1, 0)
        blocks.append((0, pl.ds(i * TQ, TQ), pl.ds(kb0 * TQ, 2 * TQ), 2 * TQ, (i - kb0) * TQ))
    for r in range(4):
        for n in range(seq // (4 * TQ)):
            kb0 = max(n - 1, 0)
            blocks.append((1, pl.ds(4 * TQ * n + r, TQ, stride=4), pl.ds(4 * TQ * kb0 + r, 2 * TQ, stride=4),
                           2 * TQ, (n - kb0) * TQ))
    for r in range(16):
        rows = pl.ds(r, TQ, stride=16)
        blocks.append((2, rows, rows, TQ, 0))

    ahead = 4
    pending = [scores(b) for b in blocks[:ahead]]
    for idx, b in enumerate(blocks):
        s = pending.pop(0)
        if idx + ahead < len(blocks):
            pending.append(scores(blocks[idx + ahead]))
        finish(b, s)

    for c0 in range(0, seq, chunk):
        rows = pl.ds(c0, chunk)
        m0, m1, m2 = m_s[0, rows, :], m_s[1, rows, :], m_s[2, rows, :]
        mm = jnp.maximum(jnp.maximum(m0, m1), m2)
        e0, e1, e2 = jnp.exp2(m0 - mm), jnp.exp2(m1 - mm), jnp.exp2(m2 - mm)
        num = acc_s[0, rows, :] * e0 + acc_s[1, rows, :] * e1 + acc_s[2, rows, :] * e2
        den = l_s[0, rows, :] * e0 + l_s[1, rows, :] * e1 + l_s[2, rows, :] * e2
        o_ref[0, rows, :] = (num / den).astype(BF16)


def _dilated_attn(qkv3, cos_t, sin_t, gq, gk):
    batch, seq, _ = qkv3.shape
    sec = D_SEC // LANES
    blk = lambda s: pl.BlockSpec((1, seq, LANES), lambda b, h, s=s: (b, 0, s * sec + h))
    tab = pl.BlockSpec((1, seq, LANES), lambda b, h: (b, 0, 0))
    vec = pl.BlockSpec((1, LANES), lambda b, h: (0, 0))
    return pl.pallas_call(
        functools.partial(_dilated_kernel, seq=seq),
        grid=(batch, N_PAIRS),
        in_specs=[blk(0), blk(1), blk(2), tab, tab, vec, vec],
        out_specs=pl.BlockSpec((1, seq, LANES), lambda b, h: (b, 0, h)),
        out_shape=jax.ShapeDtypeStruct((batch, seq, D_SEC), BF16),
        scratch_shapes=[
            pltpu.VMEM((seq, LANES), F32), pltpu.VMEM((seq, LANES), F32), pltpu.VMEM((seq, LANES), F32),
            pltpu.VMEM((3, seq, LANES), F32), pltpu.VMEM((3, seq, LANES), F32),
            pltpu.VMEM((3, seq, LANES), F32), pltpu.VMEM((2, 2 * TQ, 2 * TQ), F32),
        ],
        compiler_params=_cparams("parallel", "parallel"),
        name="dilated_attn",
    )(qkv3, qkv3, qkv3, cos_t, sin_t, gq, gk)


def _fox_kernel(q_ref, k_ref, v_ref, kb_ref, gq_ref, gk_ref, o_ref, qat_s, ka_s, vt_s, acc_s, tri_s, *, seq):
    gmat = _head_sumsq_matrix()
    chunk = 512
    lane = lax.broadcasted_iota(jnp.int32, (chunk, LANES), 1)
    low = lane < HEAD_DIM
    sel = jnp.where((lane >= HEAD_DIM) & (lane < HEAD_DIM + BIAS_TERMS), 1.0, 0.0)

    for c0 in range(0, seq, chunk):
        rows = pl.ds(c0, chunk)
        for g in range(N_PAIRS):
            lanes = slice(g * LANES, (g + 1) * LANES)
            qn = _qk_norm(q_ref[0, rows, lanes].astype(F32), gq_ref[...], gmat) * Q_SCALE_LOG2
            for h, qa in ((2 * g, jnp.where(low, qn, sel)),
                          (2 * g + 1, jnp.where(low, pltpu.roll(qn, HEAD_DIM, axis=1), sel))):
                qt = qa.T
                for cc in range(chunk // TK):
                    qat_s[c0 // TK + cc, h * LANES:(h + 1) * LANES, :] = qt[:, cc * TK:(cc + 1) * TK].astype(BF16)
            kn =_qk_norm(k_ref[0, rows, lanes].astype(F32), gk_ref[...], gmat)
            kb = kb_ref[0, rows, lanes].astype(F32)
            ka_s[2 * g, rows, :] = jnp.where(low, kn, kb).astype(BF16)
            ka_s[2 * g + 1, rows, :] = pltpu.roll(jnp.where(low, kb, kn), HEAD_DIM, axis=1).astype(BF16)
            vt = v_ref[0, rows, lanes].astype(F32).T
            for cc in range(chunk // TK):
                vt_s[c0 // TK + cc, lanes, :] = vt[:, cc * TK:(cc + 1) * TK].astype(BF16)

    r = lax.broadcasted_iota(jnp.int32, (TK, TK), 0)
    c = lax.broadcasted_iota(jnp.int32, (TK, TK), 1)
    tri_s[...] = jnp.where(r <= c, 0.0, NEG)

    def q_block(qi, carry):
        q_rows = pl.ds(pl.multiple_of(qi * TK, TK), TK)
        for h in range(N_HEADS):
            acc_s[h] = jnp.zeros((HEAD_DIM, TK), F32)

        def step(j, st, diagonal):
            k_rows = pl.ds(pl.multiple_of(j * TK, TK), TK)

            def scores(h):
                s = jnp.dot(ka_s[h, k_rows, :], qat_s[qi, h * LANES:(h + 1) * LANES, :], preferred_element_type=F32)
                return s + tri_s[...] if diagonal else s

            ahead = 6
            pending = [scores(h) for h in range(ahead)]
            new = []
            for h in range(N_HEADS):
                m, l = st[h]
                s = pending.pop(0)
                if h + ahead < N_HEADS:
                    pending.append(scores(h + ahead))
                m_new = jnp.maximum(m, jnp.max(s, axis=0, keepdims=True))
                alpha = jnp.exp2(m - m_new)
                pr = jnp.exp2(s - m_new)
                l_new = alpha * l + jnp.sum(pr, axis=0, keepdims=True)
                pv = jnp.dot(vt_s[j, h * HEAD_DIM:(h + 1) * HEAD_DIM, :], pr.astype(BF16),
                             preferred_element_type=F32)
                acc_s[h] = alpha * acc_s[h] + pv
                new.append((m_new, l_new))
            return tuple(new)

        init = tuple((jnp.full((1, TK), NEG, F32), jnp.zeros((1, TK), F32)) for _ in range(N_HEADS))
        st = lax.fori_loop(0, qi // 2, lambda jj, st: step(2 * jj + 1, step(2 * jj, st, False), False), init)
        st = lax.cond(qi % 2 == 1, lambda st: step(qi - 1, st, False), lambda st: st, st)
        st = step(qi, st, True)
        for g in range(N_PAIRS):
            o2 = jnp.concatenate([acc_s[2 * g] / st[2 * g][1], acc_s[2 * g + 1] / st[2 * g + 1][1]], axis=0)
            o_ref[0, q_rows, g * LANES:(g + 1) * LANES] = o2.T.astype(BF16)
        return carry

    lax.fori_loop(0, seq // TK, q_block, 0)


def _fox_attn(qkv3, kbias, gq, gk):
    batch, seq, _ = qkv3.shape
    blk = lambda s: pl.BlockSpec((1, seq, D_SEC), lambda b, s=s: (b, 0, s))
    vec = pl.BlockSpec((1, LANES), lambda b: (0, 0))
    return pl.pallas_call(
        functools.partial(_fox_kernel, seq=seq),
        grid=(batch,),
        in_specs=[blk(3), blk(4), blk(5), pl.BlockSpec((1, seq, D_SEC), lambda b: (b, 0, 0)), vec, vec],
        out_specs=pl.BlockSpec((1, seq, D_SEC), lambda b: (b, 0, 0)),
        out_shape=jax.ShapeDtypeStruct((batch, seq, D_SEC), BF16),
        scratch_shapes=[
            pltpu.VMEM((seq // TK, N_HEADS * LANES, TK), BF16),
            pltpu.VMEM((N_HEADS, seq, LANES), BF16),
            pltpu.VMEM((seq // TK, D_SEC, TK), BF16),
            pltpu.VMEM((N_HEADS, HEAD_DIM, TK), F32),
            pltpu.VMEM((TK, TK), F32),
        ],
        compiler_params=_cparams("parallel"),
        name="fox_attn",
    )(qkv3, qkv3, qkv3, kbias, gq, gk)


def _out_router_kernel(oa_ref, ob_ref, x_ref, wo_ref, g_ref, wr_ref, br_ref,
                       h1_ref, msort_ref, route_ref, cnt_ref, wo_s):
    tm = TM_TOK

    @pl.when(pl.program_id(0) == 0)
    def _():
        wo_s[...] = wo_ref[...].astype(BF16)

    h1 = (x_ref[...]
          + jnp.dot(oa_ref[...], wo_s[0:D_SEC, :], preferred_element_type=F32)
          + jnp.dot(ob_ref[...], wo_s[D_SEC:2 * D_SEC, :], preferred_element_type=F32))
    h1_ref[...] = h1
    ms = jnp.mean(h1 * h1, axis=-1, keepdims=True)
    mn = h1 * lax.rsqrt(ms + EPS) * g_ref[...]

    logits = jnp.dot(mn.astype(BF16), wr_ref[...], preferred_element_type=F32) + br_ref[...]
    lane = lax.broadcasted_iota(jnp.int32, (tm, LANES), 1).astype(F32)
    big = float(LANES)

    def first_argmax(vals):
        vmax = jnp.max(vals, axis=-1, keepdims=True)
        idx = jnp.min(jnp.where(vals == vmax, lane, big), axis=-1, keepdims=True)
        return vmax, idx

    lg = jnp.where(lane < N_GROUPS, logits, -jnp.inf)
    gmax, gidx = first_argmax(lg)
    gw = 1.0 / jnp.sum(jnp.exp(lg - gmax), axis=-1, keepdims=True)
    lo = ROUTER_LANE0 + EXPERTS_PER_GROUP * gidx
    le = jnp.where((lane >= lo) & (lane < lo + EXPERTS_PER_GROUP), logits, -jnp.inf)
    v0, i0 = first_argmax(le)
    v1, i1 = first_argmax(jnp.where(lane == i0, -jnp.inf, le))
    ex = jnp.exp(v1 - v0)
    w0 = gw / (1.0 + ex)
    w1 = gw * ex / (1.0 + ex)

    sel0 = lane == i0
    sel1 = lane == i1
    onehot = jnp.where(sel0 | sel1, 1.0, 0.0)
    r = lax.broadcasted_iota(jnp.int32, (tm, tm), 0)
    c = lax.broadcasted_iota(jnp.int32, (tm, tm), 1)
    ltri = (c <= r).astype(BF16)
    incl = jnp.dot(ltri, onehot.astype(BF16), preferred_element_type=F32)
    excl = incl - onehot
    counts = incl[tm - 1:tm, :]
    cnt_ref[0] = jnp.broadcast_to(counts, (SUBLANES, LANES))

    seg_rows = jnp.floor((counts + (SEG_ALIGN - 1.0)) * (1.0 / SEG_ALIGN)) * SEG_ALIGN
    seg_rows = jnp.broadcast_to(seg_rows, (tm, LANES))
    lp0 = jnp.sum(jnp.where(lane < i0, seg_rows, 0.0) + jnp.where(sel0, excl, 0.0), axis=-1, keepdims=True)
    lp1 = jnp.sum(jnp.where(lane < i1, seg_rows, 0.0) + jnp.where(sel1, excl, 0.0), axis=-1, keepdims=True)

    rec = jnp.zeros((tm, LANES), F32)
    for pos, val in enumerate((i0 - ROUTER_LANE0, i1 - ROUTER_LANE0, lp0, lp1, w0, w1)):
        rec = jnp.where(lane == float(pos), val, rec)
    route_ref[...] = rec[:, 0:SUBLANES]

    rec_t = rec.T
    out_row = lax.broadcasted_iota(jnp.int32, (SORT_ROWS, tm), 0).astype(F32)
    pick = jnp.where((out_row == rec_t[2:3, :]) | (out_row == rec_t[3:4, :]), 1.0, 0.0).astype(BF16)
    msort_ref[...] = jnp.dot(pick, mn.astype(BF16), preferred_element_type=F32).astype(BF16)


def _out_router(oa, ob, x2, w_o, g_ffn, w_r, b_r):
    T = x2.shape[0]
    tm = TM_TOK
    row = lambda w: pl.BlockSpec((tm, w), lambda i: (i, 0))
    const = lambda a, b: pl.BlockSpec((a, b), lambda i: (0, 0))
    return pl.pallas_call(
        _out_router_kernel,
        grid=(T // tm,),
        in_specs=[row(D_SEC), row(D_SEC), row(D_MODEL), _resident((D_MODEL, D_MODEL)), const(1, D_MODEL),
                  const(D_MODEL, LANES), const(1, LANES)],
        out_specs=[row(D_MODEL), pl.BlockSpec((SORT_ROWS, D_MODEL), lambda i: (i, 0)), row(SUBLANES),
                   pl.BlockSpec((1, SUBLANES, LANES), lambda i: (i, 0, 0))],
        out_shape=[
            jax.ShapeDtypeStruct((T, D_MODEL), F32),
            jax.ShapeDtypeStruct((T // tm * SORT_ROWS, D_MODEL), BF16),
            jax.ShapeDtypeStruct((T, SUBLANES), F32),
            jax.ShapeDtypeStruct((T // tm, SUBLANES, LANES), F32),
        ],
        scratch_shapes=[pltpu.VMEM((D_MODEL, D_MODEL), BF16)],
        compiler_params=_cparams("arbitrary"),
        name="out_router",
    )(oa, ob, x2, w_o, g_ffn, w_r, b_r)


def _chunk_row(c):
    return pl.multiple_of(c * SEG_ALIGN, SEG_ALIGN)


MOE_CHUNKS = TM_MOE // SEG_ALIGN


def _moe_kernel(tile_e_ref, nused_ref, src_ref, next_e_ref, slot_e_ref, msort_ref, wup_ref, wdn_ref, ys_ref,
                x_s, wup_f, wdn_f, wup_s, wdn_s, xsem, wsem):
    i = pl.program_id(0)
    nused = nused_ref[0]

    def gather(k, act):
        def chunk(c, carry):
            src = pl.multiple_of(src_ref[k * MOE_CHUNKS + c], SEG_ALIGN)
            act(pltpu.make_async_copy(msort_ref.at[pl.ds(src, SEG_ALIGN)],
                                      x_s.at[k % 2, pl.ds(_chunk_row(c), SEG_ALIGN)], xsem.at[k % 2]))
            return carry
        lax.fori_loop(0, MOE_CHUNKS, chunk, 0)

    def weights(e, s, act):
        act(pltpu.make_async_copy(wup_ref.at[e], wup_f.at[s], wsem.at[0, s]))
        act(pltpu.make_async_copy(wdn_ref.at[e], wdn_f.at[s], wsem.at[1, s]))

    @pl.when(i == 0)
    def _():
        gather(0, lambda c: c.start())
        weights(tile_e_ref[0], 0, lambda c: c.start())

    @pl.when(i + 1 < nused)
    def _():
        gather(i + 1, lambda c: c.start())

    @pl.when(i < nused)
    def _():
        e = tile_e_ref[i]

        @pl.when((i == 0) | (tile_e_ref[jnp.maximum(i - 1, 0)] != e))
        def _():
            s = slot_e_ref[e]
            weights(e, s, lambda c: c.wait())

            @pl.when(next_e_ref[e] >= 0)
            def _():
                weights(next_e_ref[e], 1 - s, lambda c: c.start())
            wup_s[...] = wup_f[s].astype(BF16)
            wdn_s[...] = wdn_f[s].astype(BF16)

        gather(i, lambda c: c.wait())
        hu = jnp.dot(x_s[i % 2], wup_s[...], preferred_element_type=F32)
        gate = hu[:, :D_EXPERT]
        hid = gate * (1.0 / (1.0 + jnp.exp(-gate))) * hu[:, D_EXPERT:]
        ys_ref[...] = jnp.dot(hid.astype(BF16), wdn_s[...], preferred_element_type=F32).astype(BF16)

    @pl.when(i >= nused)
    def _():
        ys_ref[...] = jnp.zeros_like(ys_ref)


def _moe_experts(tile_e, nused, src_chunk, next_e, slot_e, msort, w_up, w_down):
    n_tiles = tile_e.shape[0]
    grid_spec = pltpu.PrefetchScalarGridSpec(
        num_scalar_prefetch=5,
        grid=(n_tiles,),
        in_specs=[pl.BlockSpec(memory_space=pl.ANY)] * 3,
        out_specs=pl.BlockSpec((TM_MOE, D_MODEL), lambda i, *_: (i, 0)),
        scratch_shapes=[
            pltpu.VMEM((2, TM_MOE, D_MODEL), BF16),
            pltpu.VMEM((2, D_MODEL, 2 * D_EXPERT), F32), pltpu.VMEM((2, D_EXPERT, D_MODEL), F32),
            pltpu.VMEM((D_MODEL, 2 * D_EXPERT), BF16), pltpu.VMEM((D_EXPERT, D_MODEL), BF16),
            pltpu.SemaphoreType.DMA((2,)), pltpu.SemaphoreType.DMA((2, 2)),
        ],
    )
    return pl.pallas_call(
        _moe_kernel,
        grid_spec=grid_spec,
        out_shape=jax.ShapeDtypeStruct((n_tiles * TM_MOE, D_MODEL), BF16),
        compiler_params=_cparams("arbitrary"),
        name="moe_experts",
    )(tile_e, nused, src_chunk, next_e, slot_e, msort, w_up, w_down)


def _combine_kernel(nchunk_ref, src_ref, ys_ref, h1_ref, route_ref, p_ref,
                    g_ref, wg_ref, wp_ref, o_ref, gath_s, sems, wg_s, wp_s):
    i = pl.program_id(0)
    n_steps = pl.num_programs(0)

    @pl.when(i == 0)
    def _():
        wg_s[...] = wg_ref[...].astype(BF16)
        wp_s[...] = wp_ref[...].astype(BF16)
    tm = TM_TOK
    slot = i % 2

    def gather(t, s, act):
        def chunk(c, carry):
            src = pl.multiple_of(src_ref[t * SORT_CHUNKS + c], SEG_ALIGN)
            act(pltpu.make_async_copy(ys_ref.at[pl.ds(src, SEG_ALIGN)],
                                      gath_s.at[s, pl.ds(_chunk_row(c), SEG_ALIGN)], sems.at[s]))
            return carry
        lax.fori_loop(0, nchunk_ref[t], chunk, 0)

    @pl.when(i == 0)
    def _():
        gath_s[...] = jnp.zeros_like(gath_s)
        gather(0, 0, lambda c: c.start())

    @pl.when(i + 1 < n_steps)
    def _():
        gather(i + 1, 1 - slot, lambda c: c.start())

    gather(i, slot, lambda c: c.wait())

    route = route_ref[...]
    col = lax.broadcasted_iota(jnp.int32, (tm, SORT_ROWS), 1).astype(F32)
    weights = (jnp.where(col == route[:, 2:3], route[:, 4:5], 0.0)
               + jnp.where(col == route[:, 3:4], route[:, 5:6], 0.0)).astype(BF16)
    y = jnp.dot(weights, gath_s[slot], preferred_element_type=F32)
    h2 = h1_ref[...] + y
    ms = jnp.mean(h2 * h2, axis=-1, keepdims=True)
    n = (h2 * lax.rsqrt(ms + EPS) * g_ref[...]).astype(BF16)
    z = jnp.dot(n, wg_s[...], preferred_element_type=F32)
    gate = 1.0 / (1.0 + jnp.exp(-z))
    ple = jnp.dot(p_ref[...].astype(BF16), wp_s[...], preferred_element_type=F32)
    o_ref[...] = h2 + gate * ple


def _combine_ple(n_chunks, chunk_dst, ys, h1, route, p2, g_ple, w_gate, w_proj):
    T = h1.shape[0]
    tm = TM_TOK
    row = lambda w: pl.BlockSpec((tm, w), lambda i, *_: (i, 0))
    const = lambda a, b: pl.BlockSpec((a, b), lambda i, *_: (0, 0))
    grid_spec = pltpu.PrefetchScalarGridSpec(
        num_scalar_prefetch=2,
        grid=(T // tm,),
        in_specs=[pl.BlockSpec(memory_space=pl.ANY), row(D_MODEL), row(SUBLANES), row(PLE_DIM),
                  const(1, D_MODEL), _resident((D_MODEL, D_MODEL)), _resident((PLE_DIM, D_MODEL))],
        out_specs=row(D_MODEL),
        scratch_shapes=[pltpu.VMEM((2, SORT_ROWS, D_MODEL), BF16), pltpu.SemaphoreType.DMA((2,)),
                        pltpu.VMEM((D_MODEL, D_MODEL), BF16), pltpu.VMEM((PLE_DIM, D_MODEL), BF16)],
    )
    return pl.pallas_call(
        _combine_kernel,
        grid_spec=grid_spec,
        out_shape=jax.ShapeDtypeStruct((T, D_MODEL), F32),
        compiler_params=_cparams("arbitrary"),
        name="combine_ple",
    )(n_chunks, chunk_dst, ys, h1, route, p2, g_ple, w_gate, w_proj)


def _rope_tables(positions):
    half = ROT_DIM // 2
    inv = ROPE_THETA ** (-jnp.arange(0, ROT_DIM, 2, dtype=F32) / ROT_DIM)
    ang = positions.astype(F32)[..., None] * inv
    cos, sin = jnp.cos(ang), jnp.sin(ang)
    ones = jnp.ones(ang.shape[:-1] + (HEAD_DIM - ROT_DIM,), F32)
    cos_h = [cos, cos, ones]
    sin_h = [-sin, sin, 0.0 * ones]
    return jnp.concatenate(cos_h + cos_h, axis=-1), jnp.concatenate(sin_h + sin_h, axis=-1)


def _layer(i, h, p, cos_t, sin_t, g_mix, w_in, b_f, qn_a, kn_a, qn_b, kn_b, w_o, g_ffn, w_rg, b_rg,
           w_re, b_re, w_up, w_down, g_ple, w_ple_gate, w_ple_proj):
    B, S, _ = h.shape
    T = B * S
    x2 = h.reshape(T, D_MODEL)

    bf = jnp.zeros((1, LANES), F32).at[0, :N_HEADS].set(b_f[i])
    pair = lambda g: jnp.tile(g, 2).reshape(1, LANES)
    w_r = jnp.zeros((D_MODEL, LANES), F32)
    w_r = w_r.at[:, :N_GROUPS].set(w_rg[i])
    w_r = w_r.at[:, ROUTER_LANE0:ROUTER_LANE0 + N_EXPERTS].set(
        jnp.transpose(w_re[i], (1, 0, 2)).reshape(D_MODEL, N_EXPERTS))
    b_r = jnp.zeros((1, LANES), F32).at[0, :N_GROUPS].set(b_rg[i])
    b_r = b_r.at[0, ROUTER_LANE0:ROUTER_LANE0 + N_EXPERTS].set(b_re[i].reshape(-1))

    qkv, f = _in_proj(x2, g_mix[i].reshape(1, -1), w_in[i])
    kbias = _forget_scan(f, bf, B, S)
    qkv3 = qkv.reshape(B, S, N_QKV)
    oa = _dilated_attn(qkv3, cos_t, sin_t, pair(qn_a[i]), pair(kn_a[i]))
    ob = _fox_attn(qkv3, kbias, pair(qn_b[i]), pair(kn_b[i]))

    h1, msort, route, cnt = _out_router(oa.reshape(T, D_SEC), ob.reshape(T, D_SEC), x2, w_o[i],
                                        g_ffn[i].reshape(1, -1), w_r.astype(BF16), b_r)

    n_tok_tiles = T // TM_TOK
    counts = cnt[:, 0, ROUTER_LANE0:ROUTER_LANE0 + N_EXPERTS].astype(jnp.int32)
    seg_rows = (counts + SEG_ALIGN - 1) // SEG_ALIGN * SEG_ALIGN
    seg_local = jnp.cumsum(seg_rows, axis=1) - seg_rows
    rows_e = jnp.sum(seg_rows, axis=0)
    tile_end = jnp.cumsum((rows_e + TM_MOE - 1) // TM_MOE)
    off = jnp.concatenate([jnp.zeros((1,), jnp.int32), tile_end * TM_MOE]).astype(jnp.int32)
    seg_global = off[None, :N_EXPERTS] + jnp.cumsum(seg_rows, axis=0) - seg_rows
    n_tiles = (2 * T + n_tok_tiles * N_EXPERTS * (SEG_ALIGN - 1)) // TM_MOE + N_EXPERTS
    nused = tile_end[-1:].astype(jnp.int32)
    tile_ids = jnp.minimum(jnp.arange(n_tiles, dtype=jnp.int32), nused[0] - 1)
    tile_e = jnp.sum((tile_ids[:, None] >= tile_end[None, :]).astype(jnp.int32), axis=1).astype(jnp.int32)
    chunk_row = jnp.arange(SORT_CHUNKS, dtype=jnp.int32)[None, :, None] * SEG_ALIGN
    lo, hi = seg_local[:, None, :], (seg_local + seg_rows)[:, None, :]
    chunk_dst = jnp.sum(jnp.where((chunk_row >= lo) & (chunk_row < hi), seg_global[:, None, :] + chunk_row - lo, 0),
                        axis=2).reshape(-1).astype(jnp.int32)
    n_chunks = (jnp.sum(seg_rows, axis=1) // SEG_ALIGN).astype(jnp.int32)
    sorted_row = jnp.arange(n_tiles * MOE_CHUNKS, dtype=jnp.int32)[:, None] * SEG_ALIGN
    seg_lo = seg_global.reshape(1, -1)
    seg_src = (jnp.arange(n_tok_tiles, dtype=jnp.int32)[:, None] * SORT_ROWS + seg_local).reshape(1, -1)
    in_seg = (sorted_row >= seg_lo) & (sorted_row < seg_lo + seg_rows.reshape(1, -1))
    src_chunk = jnp.where(jnp.any(in_seg, axis=1),
                          jnp.sum(jnp.where(in_seg, seg_src + sorted_row - seg_lo, 0), axis=1),
                          SORT_ROWS - SEG_ALIGN).astype(jnp.int32)
    e_ids = jnp.arange(N_EXPERTS, dtype=jnp.int32)
    active = rows_e > 0
    later_active = jnp.where(active[None, :] & (e_ids[None, :] > e_ids[:, None]), e_ids[None, :], N_EXPERTS)
    next_e = jnp.min(later_active, axis=1)
    next_e = jnp.where(next_e == N_EXPERTS, -1, next_e).astype(jnp.int32)
    slot_e = ((jnp.cumsum(active.astype(jnp.int32)) - 1) % 2).astype(jnp.int32)

    ys = _moe_experts(tile_e, nused, src_chunk, next_e, slot_e, msort, w_up[i], w_down[i])
    out = _combine_ple(n_chunks, chunk_dst, ys, h1, route, p[i].reshape(T, PLE_DIM),
                       g_ple[i].reshape(1, -1), w_ple_gate[i], w_ple_proj[i])
    return out.reshape(B, S, D_MODEL)


def kernel(x, p, positions, g_mix, w_in, b_f, qn_a, kn_a, qn_b, kn_b, w_o, g_ffn, w_rg, b_rg, w_re, b_re,
           w_up, w_down, g_ple, w_ple_gate, w_ple_proj):
    cos_t, sin_t = _rope_tables(positions)
    h = x
    for i in range(p.shape[0]):
        h = _layer(i, h, p, cos_t, sin_t, g_mix, w_in, b_f, qn_a, kn_a, qn_b, kn_b, w_o, g_ffn, w_rg, b_rg,
                   w_re, b_re, w_up, w_down, g_ple, w_ple_gate, w_ple_proj)
    return h
```

```python
import functools
import math

import jax
import jax.numpy as jnp
from jax import lax
from jax.experimental import pallas as pl
from jax.experimental.pallas import tpu as pltpu

D_MODEL = 1024
HEAD_DIM = 64
N_HEADS = 8
D_SEC = N_HEADS * HEAD_DIM
N_QKV = 6 * D_SEC
ROT_DIM = HEAD_DIM // 4
ROPE_THETA = 500000.0
N_GROUPS = 4
EXPERTS_PER_GROUP = 8
N_EXPERTS = N_GROUPS * EXPERTS_PER_GROUP
D_EXPERT = 512
PLE_DIM = 256
EPS = 1e-6
NEG = -1e30
WINDOW = 128

LANES = 128
SUBLANES = 8
VMEM_LIMIT = 48 * 1024 * 1024

TM_PROJ = 512
TQ = 128
TK = 256
N_PAIRS = N_HEADS // 2
TM_MOE = 512
TM_TOK = 512
SEG_ALIGN = 2 * SUBLANES
SORT_ROWS = 2 * TM_TOK + N_EXPERTS * SEG_ALIGN
SORT_CHUNKS = SORT_ROWS // SEG_ALIGN
ROUTER_LANE0 = N_GROUPS

Q_SCALE_LOG2 = math.log2(math.e) / math.sqrt(HEAD_DIM)

F32 = jnp.float32
BF16 = jnp.bfloat16
NT_DIMS = (((1,), (1,)), ((), ()))


def _cparams(*sem):
    return pltpu.CompilerParams(dimension_semantics=sem, vmem_limit_bytes=VMEM_LIMIT)


def _resident(shape):
    return pl.BlockSpec(shape, lambda i, *_: (0,) * len(shape), pipeline_mode=pl.Buffered(1))


def _in_proj_kernel(x_ref, g_ref, w_ref, qkv_ref, f_ref, w_s, wf_s):
    @pl.when(pl.program_id(0) == 0)
    def _():
        w_s[...] = w_ref[0, :, 0:N_QKV].astype(BF16)
        wf_s[...] = jnp.zeros_like(wf_s)
        wf_s[:, 0:N_HEADS] = w_ref[0, :, N_QKV:N_QKV + N_HEADS].astype(BF16)

    x = x_ref[...]
    ms = jnp.mean(x * x, axis=-1, keepdims=True)
    a = (x * lax.rsqrt(ms + EPS) * g_ref[...]).astype(BF16)
    qkv_ref[...] = jnp.dot(a, w_s[...], preferred_element_type=F32).astype(BF16)
    f_ref[...] = jnp.dot(a, wf_s[...], preferred_element_type=F32)


def _in_proj(x2, g_mix, w_in, layer):
    T = x2.shape[0]
    w_block = (1,) + w_in.shape[1:]
    return pl.pallas_call(
        _in_proj_kernel,
        grid=(T // TM_PROJ,),
        in_specs=[
            pl.BlockSpec((TM_PROJ, D_MODEL), lambda i: (i, 0)),
            pl.BlockSpec((1, D_MODEL), lambda i: (0, 0)),
            pl.BlockSpec(w_block, lambda i: (layer, 0, 0), pipeline_mode=pl.Buffered(1)),
        ],
        out_specs=[
            pl.BlockSpec((TM_PROJ, N_QKV), lambda i: (i, 0)),
            pl.BlockSpec((TM_PROJ, LANES), lambda i: (i, 0)),
        ],
        out_shape=[
            jax.ShapeDtypeStruct((T, N_QKV), BF16),
            jax.ShapeDtypeStruct((T, LANES), F32),
        ],
        scratch_shapes=[pltpu.VMEM((D_MODEL, N_QKV), BF16), pltpu.VMEM((D_MODEL, LANES), BF16)],
        compiler_params=_cparams("arbitrary"),
        name="in_proj",
    )(x2, g_mix, w_in)


BIAS_TERMS = 3


def _bias_lane(head):
    return (head // 2) * LANES + (HEAD_DIM if head % 2 == 0 else 0)


def _forget_scan_kernel(f_ref, bf_ref, kb_ref, *, seq):
    f = f_ref[...] + bf_ref[...]
    logf = jnp.minimum(f, 0.0) - jnp.log1p(jnp.exp(-jnp.abs(f)))

    def split_bf16(v):
        terms = []
        for _ in range(BIAS_TERMS):
            t = v.astype(BF16)
            terms.append(t)
            v = v - t.astype(F32)
        return terms

    tri = (lax.broadcasted_iota(jnp.int32, (TK, TK), 1) <= lax.broadcasted_iota(jnp.int32, (TK, TK), 0)).astype(BF16)
    carry = jnp.zeros((1, LANES), F32)
    chunks = []
    for j in range(seq // TK):
        terms = jnp.concatenate(split_bf16(logf[j * TK:(j + 1) * TK, :]), axis=1)
        pre = jnp.dot(tri, terms, preferred_element_type=F32)
        cj = carry + sum(pre[:, t * LANES:(t + 1) * LANES] for t in range(BIAS_TERMS))
        carry = cj[TK - 1:TK, :]
        chunks.append(cj)
    c = jnp.concatenate(chunks, axis=0)
    rest = c * (-math.log2(math.e))
    r_idx = lax.broadcasted_iota(jnp.int32, (LANES, D_SEC), 0)
    c_idx = lax.broadcasted_iota(jnp.int32, (LANES, D_SEC), 1)
    base = (r_idx // 2) * LANES + jnp.where(r_idx % 2 == 0, HEAD_DIM, 0)
    out = jnp.zeros((seq, D_SEC), F32)
    for t, term in enumerate(split_bf16(rest)):
        place = ((r_idx < N_HEADS) & (c_idx == base + t)).astype(BF16)
        out = out + jnp.dot(term, place, preferred_element_type=F32)
    kb_ref[0] = out.astype(BF16)


def _forget_scan(f, bf, batch, seq):
    return pl.pallas_call(
        functools.partial(_forget_scan_kernel, seq=seq),
        grid=(batch,),
        in_specs=[
            pl.BlockSpec((seq, LANES), lambda b: (b, 0)),
            pl.BlockSpec((1, LANES), lambda b: (0, 0)),
        ],
        out_specs=pl.BlockSpec((1, seq, D_SEC), lambda b: (b, 0, 0)),
        out_shape=jax.ShapeDtypeStruct((batch, seq, D_SEC), BF16),
        compiler_params=_cparams("parallel"),
        name="forget_scan",
    )(f, bf)


def _head_sumsq_matrix():
    r = lax.broadcasted_iota(jnp.int32, (LANES, LANES), 0) // HEAD_DIM
    c = lax.broadcasted_iota(jnp.int32, (LANES, LANES), 1) // HEAD_DIM
    return (r == c).astype(BF16)


def _qk_norm(x, gain, gmat):
    ss = jnp.dot((x * x).astype(BF16), gmat, preferred_element_type=F32)
    return x * lax.rsqrt(ss * (1.0 / HEAD_DIM) + EPS) * gain


def _stack_heads(qb, head0):
    zero = jnp.zeros_like(qb)
    return jnp.concatenate([jnp.where(head0, qb, zero), jnp.where(head0, zero, qb)], axis=0)


def _unstack(a, head0):
    return jnp.where(head0, a[:TQ], a[TQ:])


def _dilated_kernel(q_ref, k_ref, v_ref, cos_ref, sin_ref, gq_ref, gk_ref, o_ref,
                    qn_s, kn_s, v_s, acc_s, m_s, l_s, bias_s, *, seq):
    gmat = _head_sumsq_matrix()
    lane = lax.broadcasted_iota(jnp.int32, (TQ, LANES), 1)
    head0 = lane < HEAD_DIM
    chunk = 512
    lane_c = lax.broadcasted_iota(jnp.int32, (chunk, LANES), 1) % HEAD_DIM
    first_half = lane_c < ROT_DIM // 2

    def rope(t, cs, sn):
        partner = jnp.where(first_half, pltpu.roll(t, LANES - ROT_DIM // 2, axis=1),
                            pltpu.roll(t, ROT_DIM // 2, axis=1))
        return t * cs + partner * sn

    for c0 in range(0, seq, chunk):
        rows = pl.ds(c0, chunk)
        cs = cos_ref[0, rows, :]
        sn = sin_ref[0, rows, :]
        qn = _qk_norm(q_ref[0, rows, :].astype(F32), gq_ref[...], gmat)
        qn_s[rows, :] = rope(qn, cs, sn) * Q_SCALE_LOG2
        kn = _qk_norm(k_ref[0, rows, :].astype(F32), gk_ref[...], gmat)
        kn_s[rows, :] = rope(kn, cs, sn)
        v_s[rows, :] = v_ref[0, rows, :].astype(F32)

    u = lax.broadcasted_iota(jnp.int32, (2 * TQ, 2 * TQ), 0) % TQ
    c = lax.broadcasted_iota(jnp.int32, (2 * TQ, 2 * TQ), 1)
    for slot, dist0 in enumerate((0, TQ)):
        dist = dist0 + u - c
        bias_s[slot] = jnp.where((dist >= 0) & (dist <= WINDOW), 0.0, NEG)

    def scores(blk):
        p, q_rows, k_rows, nk, dist0 = blk
        qst = _stack_heads(qn_s[q_rows, :].astype(BF16), head0)
        s = lax.dot_general(qst, kn_s[k_rows, :].astype(BF16), NT_DIMS, preferred_element_type=F32)
        return s + bias_s[dist0 // TQ, :, 0:nk]

    def finish(blk, s):
        p, q_rows, k_rows, nk, dist0 = blk
        m = jnp.max(s, axis=-1, keepdims=True)
        pr = jnp.exp2(s - m)
        l = jnp.sum(pr, axis=-1, keepdims=True)
        acc = jnp.dot(pr.astype(BF16), v_s[k_rows, :].astype(BF16), preferred_element_type=F32)
        acc_s[p, q_rows, :] = _unstack(acc, head0)
        m_s[p, q_rows, :] = _unstack(jnp.broadcast_to(m, (2 * TQ, LANES)), head0)
        l_s[p, q_rows, :] = _unstack(jnp.broadcast_to(l, (2 * TQ, LANES)), head0)

    blocks = []
    for i in range(seq // TQ):
        kb0 = max(i - 1, 0)
        blocks.append((0, pl.ds(i * TQ, TQ), pl.ds(kb0 * TQ, 2 * TQ), 2 * TQ, (i - kb0) * TQ))
    for r in range(4):
        for n in range(seq // (4 * TQ)):
            kb0 = max(n - 1, 0)
            blocks.append((1, pl.ds(4 * TQ * n + r, TQ, stride=4), pl.ds(4 * TQ * kb0 + r, 2 * TQ, stride=4),
                           2 * TQ, (n - kb0) * TQ))
    for r in range(16):
        rows = pl.ds(r, TQ, stride=16)
        blocks.append((2, rows, rows, TQ, 0))

    ahead = 4
    pending = [scores(b) for b in blocks[:ahead]]
    for idx, b in enumerate(blocks):
        s = pending.pop(0)
        if idx + ahead < len(blocks):
            pending.append(scores(blocks[idx + ahead]))
        finish(b, s)

    for c0 in range(0, seq, chunk):
        rows = pl.ds(c0, chunk)
        m0, m1, m2 = m_s[0, rows, :], m_s[1, rows, :], m_s[2, rows, :]
        mm = jnp.maximum(jnp.maximum(m0, m1), m2)
        e0, e1, e2 = jnp.exp2(m0 - mm), jnp.exp2(m1 - mm), jnp.exp2(m2 - mm)
        num = acc_s[0, rows, :] * e0 + acc_s[1, rows, :] * e1 + acc_s[2, rows, :] * e2
        den = l_s[0, rows, :] * e0 + l_s[1, rows, :] * e1 + l_s[2, rows, :] * e2
        o_ref[0, rows, :] = (num / den).astype(BF16)


def _dilated_attn(qkv3, cos_t, sin_t, gq, gk):
    batch, seq, _ = qkv3.shape
    sec = D_SEC // LANES
    blk = lambda s: pl.BlockSpec((1, seq, LANES), lambda b, h, s=s: (b, 0, s * sec + h))
    tab = pl.BlockSpec((1, seq, LANES), lambda b, h: (b, 0, 0))
    vec = pl.BlockSpec((1, LANES), lambda b, h: (0, 0))
    return pl.pallas_call(
        functools.partial(_dilated_kernel, seq=seq),
        grid=(batch, N_PAIRS),
        in_specs=[blk(0), blk(1), blk(2), tab, tab, vec, vec],
        out_specs=pl.BlockSpec((1, seq, LANES), lambda b, h: (b, 0, h)),
        out_shape=jax.ShapeDtypeStruct((batch, seq, D_SEC), BF16),
        scratch_shapes=[
            pltpu.VMEM((seq, LANES), F32), pltpu.VMEM((seq, LANES), F32), pltpu.VMEM((seq, LANES), F32),
            pltpu.VMEM((3, seq, LANES), F32), pltpu.VMEM((3, seq, LANES), F32),
            pltpu.VMEM((3, seq, LANES), F32), pltpu.VMEM((2, 2 * TQ, 2 * TQ), F32),
        ],
        compiler_params=_cparams("parallel", "parallel"),
        name="dilated_attn",
    )(qkv3, qkv3, qkv3, cos_t, sin_t, gq, gk)


def _fox_kernel(q_ref, k_ref, v_ref, kb_ref, gq_ref, gk_ref, o_ref, qat_s, ka_s, vt_s, acc_s, tri_s, *, seq):
    gmat = _head_sumsq_matrix()
    chunk = 512
    lane = lax.broadcasted_iota(jnp.int32, (chunk, LANES), 1)
    low = lane < HEAD_DIM
    sel = jnp.where((lane >= HEAD_DIM) & (lane < HEAD_DIM + BIAS_TERMS), 1.0, 0.0)

    for c0 in range(0, seq, chunk):
        rows = pl.ds(c0, chunk)
        for g in range(N_PAIRS):
            lanes = slice(g * LANES, (g + 1) * LANES)
            qn = _qk_norm(q_ref[0, rows, lanes].astype(F32), gq_ref[...], gmat) * Q_SCALE_LOG2
            for h, qa in ((2 * g, jnp.where(low, qn, sel)),
                          (2 * g + 1, jnp.where(low, pltpu.roll(qn, HEAD_DIM, axis=1), sel))):
                qt = qa.T
                for cc in range(chunk // TK):
                    qat_s[c0 // TK + cc, h * LANES:(h + 1) * LANES, :] = qt[:, cc * TK:(cc + 1) * TK].astype(BF16)
            kn =_qk_norm(k_ref[0, rows, lanes].astype(F32), gk_ref[...], gmat)
            kb = kb_ref[0, rows, lanes].astype(F32)
            ka_s[2 * g, rows, :] = jnp.where(low, kn, kb).astype(BF16)
            ka_s[2 * g + 1, rows, :] = pltpu.roll(jnp.where(low, kb, kn), HEAD_DIM, axis=1).astype(BF16)
            vt = v_ref[0, rows, lanes].astype(F32).T
            for cc in range(chunk // TK):
                vt_s[c0 // TK + cc, lanes, :] = vt[:, cc * TK:(cc + 1) * TK].astype(BF16)

    r = lax.broadcasted_iota(jnp.int32, (TK, TK), 0)
    c = lax.broadcasted_iota(jnp.int32, (TK, TK), 1)
    tri_s[...] = jnp.where(r <= c, 0.0, NEG)

    def q_block(qi, carry):
        q_rows = pl.ds(pl.multiple_of(qi * TK, TK), TK)
        for h in range(N_HEADS):
            acc_s[h] = jnp.zeros((HEAD_DIM, TK), F32)

        def step(j, st, diagonal):
            k_rows = pl.ds(pl.multiple_of(j * TK, TK), TK)

            def scores(h):
                s = jnp.dot(ka_s[h, k_rows, :], qat_s[qi, h * LANES:(h + 1) * LANES, :], preferred_element_type=F32)
                return s + tri_s[...] if diagonal else s

            ahead = 6
            pending = [scores(h) for h in range(ahead)]
            new = []
            for h in range(N_HEADS):
                m, l = st[h]
                s = pending.pop(0)
                if h + ahead < N_HEADS:
                    pending.append(scores(h + ahead))
                m_new = jnp.maximum(m, jnp.max(s, axis=0, keepdims=True))
                alpha = jnp.exp2(m - m_new)
                pr = jnp.exp2(s - m_new)
                l_new = alpha * l + jnp.sum(pr, axis=0, keepdims=True)
                pv = jnp.dot(vt_s[j, h * HEAD_DIM:(h + 1) * HEAD_DIM, :], pr.astype(BF16),
                             preferred_element_type=F32)
                acc_s[h] = alpha * acc_s[h] + pv
                new.append((m_new, l_new))
            return tuple(new)

        init = tuple((jnp.full((1, TK), NEG, F32), jnp.zeros((1, TK), F32)) for _ in range(N_HEADS))
        st = lax.fori_loop(0, qi // 2, lambda jj, st: step(2 * jj + 1, step(2 * jj, st, False), False), init)
        st = lax.cond(qi % 2 == 1, lambda st: step(qi - 1, st, False), lambda st: st, st)
        st = step(qi, st, True)
        for g in range(N_PAIRS):
            o2 = jnp.concatenate([acc_s[2 * g] / st[2 * g][1], acc_s[2 * g + 1] / st[2 * g + 1][1]], axis=0)
            o_ref[0, q_rows, g * LANES:(g + 1) * LANES] = o2.T.astype(BF16)
        return carry

    lax.fori_loop(0, seq // TK, q_block, 0)


def _fox_attn(qkv3, kbias, gq, gk):
    batch, seq, _ = qkv3.shape
    blk = lambda s: pl.BlockSpec((1, seq, D_SEC), lambda b, s=s: (b, 0, s))
    vec = pl.BlockSpec((1, LANES), lambda b: (0, 0))
    return pl.pallas_call(
        functools.partial(_fox_kernel, seq=seq),
        grid=(batch,),
        in_specs=[blk(3), blk(4), blk(5), pl.BlockSpec((1, seq, D_SEC), lambda b: (b, 0, 0)), vec, vec],
        out_specs=pl.BlockSpec((1, seq, D_SEC), lambda b: (b, 0, 0)),
        out_shape=jax.ShapeDtypeStruct((batch, seq, D_SEC), BF16),
        scratch_shapes=[
            pltpu.VMEM((seq // TK, N_HEADS * LANES, TK), BF16),
            pltpu.VMEM((N_HEADS, seq, LANES), BF16),
            pltpu.VMEM((seq // TK, D_SEC, TK), BF16),
            pltpu.VMEM((N_HEADS, HEAD_DIM, TK), F32),
            pltpu.VMEM((TK, TK), F32),
        ],
        compiler_params=_cparams("parallel"),
        name="fox_attn",
    )(qkv3, qkv3, qkv3, kbias, gq, gk)


def _out_router_kernel(oa_ref, ob_ref, x_ref, wo_ref, g_ref, wr_ref, br_ref,
                       h1_ref, msort_ref, route_ref, cnt_ref, wo_s):
    tm = TM_TOK

    @pl.when(pl.program_id(0) == 0)
    def _():
        wo_s[...] = wo_ref[...].astype(BF16)

    h1 = (x_ref[...]
          + jnp.dot(oa_ref[...], wo_s[0:D_SEC, :], preferred_element_type=F32)
          + jnp.dot(ob_ref[...], wo_s[D_SEC:2 * D_SEC, :], preferred_element_type=F32))
    h1_ref[...] = h1
    ms = jnp.mean(h1 * h1, axis=-1, keepdims=True)
    mn = h1 * lax.rsqrt(ms + EPS) * g_ref[...]

    logits = jnp.dot(mn.astype(BF16), wr_ref[...], preferred_element_type=F32) + br_ref[...]
    lane = lax.broadcasted_iota(jnp.int32, (tm, LANES), 1).astype(F32)
    big = float(LANES)

    def first_argmax(vals):
        vmax = jnp.max(vals, axis=-1, keepdims=True)
        idx = jnp.min(jnp.where(vals == vmax, lane, big), axis=-1, keepdims=True)
        return vmax, idx

    lg = jnp.where(lane < N_GROUPS, logits, -jnp.inf)
    gmax, gidx = first_argmax(lg)
    gw = 1.0 / jnp.sum(jnp.exp(lg - gmax), axis=-1, keepdims=True)
    lo = ROUTER_LANE0 + EXPERTS_PER_GROUP * gidx
    le = jnp.where((lane >= lo) & (lane < lo + EXPERTS_PER_GROUP), logits, -jnp.inf)
    v0, i0 = first_argmax(le)
    v1, i1 = first_argmax(jnp.where(lane == i0, -jnp.inf, le))
    ex = jnp.exp(v1 - v0)
    w0 = gw / (1.0 + ex)
    w1 = gw * ex / (1.0 + ex)

    sel0 = lane == i0
    sel1 = lane == i1
    onehot = jnp.where(sel0 | sel1, 1.0, 0.0)
    r = lax.broadcasted_iota(jnp.int32, (tm, tm), 0)
    c = lax.broadcasted_iota(jnp.int32, (tm, tm), 1)
    ltri = (c <= r).astype(BF16)
    incl = jnp.dot(ltri, onehot.astype(BF16), preferred_element_type=F32)
    excl = incl - onehot
    counts = incl[tm - 1:tm, :]
    cnt_ref[0] = jnp.broadcast_to(counts, (SUBLANES, LANES))

    seg_rows = jnp.floor((counts + (SEG_ALIGN - 1.0)) * (1.0 / SEG_ALIGN)) * SEG_ALIGN
    seg_rows = jnp.broadcast_to(seg_rows, (tm, LANES))
    lp0 = jnp.sum(jnp.where(lane < i0, seg_rows, 0.0) + jnp.where(sel0, excl, 0.0), axis=-1, keepdims=True)
    lp1 = jnp.sum(jnp.where(lane < i1, seg_rows, 0.0) + jnp.where(sel1, excl, 0.0), axis=-1, keepdims=True)

    rec = jnp.zeros((tm, LANES), F32)
    for pos, val in enumerate((i0 - ROUTER_LANE0, i1 - ROUTER_LANE0, lp0, lp1, w0, w1)):
        rec = jnp.where(lane == float(pos), val, rec)
    route_ref[...] = rec[:, 0:SUBLANES]

    rec_t = rec.T
    out_row = lax.broadcasted_iota(jnp.int32, (SORT_ROWS, tm), 0).astype(F32)
    pick = jnp.where((out_row == rec_t[2:3, :]) | (out_row == rec_t[3:4, :]), 1.0, 0.0).astype(BF16)
    msort_ref[...] = jnp.dot(pick, mn.astype(BF16), preferred_element_type=F32).astype(BF16)


def _out_router(oa, ob, x2, w_o, g_ffn, w_r, b_r):
    T = x2.shape[0]
    tm = TM_TOK
    row = lambda w: pl.BlockSpec((tm, w), lambda i: (i, 0))
    const = lambda a, b: pl.BlockSpec((a, b), lambda i: (0, 0))
    return pl.pallas_call(
        _out_router_kernel,
        grid=(T // tm,),
        in_specs=[row(D_SEC), row(D_SEC), row(D_MODEL), _resident((D_MODEL, D_MODEL)), const(1, D_MODEL),
                  const(D_MODEL, LANES), const(1, LANES)],
        out_specs=[row(D_MODEL), pl.BlockSpec((SORT_ROWS, D_MODEL), lambda i: (i, 0)), row(SUBLANES),
                   pl.BlockSpec((1, SUBLANES, LANES), lambda i: (i, 0, 0))],
        out_shape=[
            jax.ShapeDtypeStruct((T, D_MODEL), F32),
            jax.ShapeDtypeStruct((T // tm * SORT_ROWS, D_MODEL), BF16),
            jax.ShapeDtypeStruct((T, SUBLANES), F32),
            jax.ShapeDtypeStruct((T // tm, SUBLANES, LANES), F32),
        ],
        scratch_shapes=[pltpu.VMEM((D_MODEL, D_MODEL), BF16)],
        compiler_params=_cparams("arbitrary"),
        name="out_router",
    )(oa, ob, x2, w_o, g_ffn, w_r, b_r)


def _chunk_row(c):
    return pl.multiple_of(c * SEG_ALIGN, SEG_ALIGN)


MOE_CHUNKS = TM_MOE // SEG_ALIGN


def _moe_kernel(tile_e_ref, nused_ref, src_ref, next_e_ref, slot_e_ref, msort_ref, wup_ref, wdn_ref, ys_ref,
                x_s, wup_f, wdn_f, wup_s, wdn_s, xsem, wsem):
    i = pl.program_id(0)
    nused = nused_ref[0]

    def gather(k, act):
        def chunk(c, carry):
            src = pl.multiple_of(src_ref[k * MOE_CHUNKS + c], SEG_ALIGN)
            act(pltpu.make_async_copy(msort_ref.at[pl.ds(src, SEG_ALIGN)],
                                      x_s.at[k % 2, pl.ds(_chunk_row(c), SEG_ALIGN)], xsem.at[k % 2]))
            return carry
        lax.fori_loop(0, MOE_CHUNKS, chunk, 0)

    def weights(e, s, act):
        act(pltpu.make_async_copy(wup_ref.at[e], wup_f.at[s], wsem.at[0, s]))
        act(pltpu.make_async_copy(wdn_ref.at[e], wdn_f.at[s], wsem.at[1, s]))

    @pl.when(i == 0)
    def _():
        gather(0, lambda c: c.start())
        weights(tile_e_ref[0], 0, lambda c: c.start())

    @pl.when(i + 1 < nused)
    def _():
        gather(i + 1, lambda c: c.start())

    @pl.when(i < nused)
    def _():
        e = tile_e_ref[i]

        @pl.when((i == 0) | (tile_e_ref[jnp.maximum(i - 1, 0)] != e))
        def _():
            s = slot_e_ref[e]
            weights(e, s, lambda c: c.wait())

            @pl.when(next_e_ref[e] >= 0)
            def _():
                weights(next_e_ref[e], 1 - s, lambda c: c.start())
            wup_s[...] = wup_f[s].astype(BF16)
            wdn_s[...] = wdn_f[s].astype(BF16)

        gather(i, lambda c: c.wait())
        hu = jnp.dot(x_s[i % 2], wup_s[...], preferred_element_type=F32)
        gate = hu[:, :D_EXPERT]
        hid = gate * (1.0 / (1.0 + jnp.exp(-gate))) * hu[:, D_EXPERT:]
        ys_ref[...] = jnp.dot(hid.astype(BF16), wdn_s[...], preferred_element_type=F32).astype(BF16)

    @pl.when(i >= nused)
    def _():
        ys_ref[...] = jnp.zeros_like(ys_ref)


def _moe_experts(tile_e, nused, src_chunk, next_e, slot_e, msort, w_up, w_down):
    n_tiles = tile_e.shape[0]
    grid_spec = pltpu.PrefetchScalarGridSpec(
        num_scalar_prefetch=5,
        grid=(n_tiles,),
        in_specs=[pl.BlockSpec(memory_space=pl.ANY)] * 3,
        out_specs=pl.BlockSpec((TM_MOE, D_MODEL), lambda i, *_: (i, 0)),
        scratch_shapes=[
            pltpu.VMEM((2, TM_MOE, D_MODEL), BF16),
            pltpu.VMEM((2, D_MODEL, 2 * D_EXPERT), F32), pltpu.VMEM((2, D_EXPERT, D_MODEL), F32),
            pltpu.VMEM((D_MODEL, 2 * D_EXPERT), BF16), pltpu.VMEM((D_EXPERT, D_MODEL), BF16),
            pltpu.SemaphoreType.DMA((2,)), pltpu.SemaphoreType.DMA((2, 2)),
        ],
    )
    return pl.pallas_call(
        _moe_kernel,
        grid_spec=grid_spec,
        out_shape=jax.ShapeDtypeStruct((n_tiles * TM_MOE, D_MODEL), BF16),
        compiler_params=_cparams("arbitrary"),
        name="moe_experts",
    )(tile_e, nused, src_chunk, next_e, slot_e, msort, w_up, w_down)


def _combine_kernel(nchunk_ref, src_ref, ys_ref, h1_ref, route_ref, p_ref,
                    g_ref, wg_ref, wp_ref, o_ref, gath_s, sems, wg_s, wp_s):
    i = pl.program_id(0)
    n_steps = pl.num_programs(0)

    @pl.when(i == 0)
    def _():
        wg_s[...] = wg_ref[...].astype(BF16)
        wp_s[...] = wp_ref[...].astype(BF16)
    tm = TM_TOK
    slot = i % 2

    def gather(t, s, act):
        def chunk(c, carry):
            src = pl.multiple_of(src_ref[t * SORT_CHUNKS + c], SEG_ALIGN)
            act(pltpu.make_async_copy(ys_ref.at[pl.ds(src, SEG_ALIGN)],
                                      gath_s.at[s, pl.ds(_chunk_row(c), SEG_ALIGN)], sems.at[s]))
            return carry
        lax.fori_loop(0, nchunk_ref[t], chunk, 0)

    @pl.when(i == 0)
    def _():
        gath_s[...] = jnp.zeros_like(gath_s)
        gather(0, 0, lambda c: c.start())

    @pl.when(i + 1 < n_steps)
    def _():
        gather(i + 1, 1 - slot, lambda c: c.start())

    gather(i, slot, lambda c: c.wait())

    route = route_ref[...]
    col = lax.broadcasted_iota(jnp.int32, (tm, SORT_ROWS), 1).astype(F32)
    weights = (jnp.where(col == route[:, 2:3], route[:, 4:5], 0.0)
               + jnp.where(col == route[:, 3:4], route[:, 5:6], 0.0)).astype(BF16)
    y = jnp.dot(weights, gath_s[slot], preferred_element_type=F32)
    h2 = h1_ref[...] + y
    ms = jnp.mean(h2 * h2, axis=-1, keepdims=True)
    n = (h2 * lax.rsqrt(ms + EPS) * g_ref[...]).astype(BF16)
    z = jnp.dot(n, wg_s[...], preferred_element_type=F32)
    gate = 1.0 / (1.0 + jnp.exp(-z))
    ple = jnp.dot(p_ref[...].astype(BF16), wp_s[...], preferred_element_type=F32)
    o_ref[...] = h2 + gate * ple


def _combine_ple(n_chunks, chunk_dst, ys, h1, route, p2, g_ple, w_gate, w_proj):
    T = h1.shape[0]
    tm = TM_TOK
    row = lambda w: pl.BlockSpec((tm, w), lambda i, *_: (i, 0))
    const = lambda a, b: pl.BlockSpec((a, b), lambda i, *_: (0, 0))
    grid_spec = pltpu.PrefetchScalarGridSpec(
        num_scalar_prefetch=2,
        grid=(T // tm,),
        in_specs=[pl.BlockSpec(memory_space=pl.ANY), row(D_MODEL), row(SUBLANES), row(PLE_DIM),
                  const(1, D_MODEL), _resident((D_MODEL, D_MODEL)), _resident((PLE_DIM, D_MODEL))],
        out_specs=row(D_MODEL),
        scratch_shapes=[pltpu.VMEM((2, SORT_ROWS, D_MODEL), BF16), pltpu.SemaphoreType.DMA((2,)),
                        pltpu.VMEM((D_MODEL, D_MODEL), BF16), pltpu.VMEM((PLE_DIM, D_MODEL), BF16)],
    )
    return pl.pallas_call(
        _combine_kernel,
        grid_spec=grid_spec,
        out_shape=jax.ShapeDtypeStruct((T, D_MODEL), F32),
        compiler_params=_cparams("arbitrary"),
        name="combine_ple",
    )(n_chunks, chunk_dst, ys, h1, route, p2, g_ple, w_gate, w_proj)


def _rope_tables(positions):
    half = ROT_DIM // 2
    inv = ROPE_THETA ** (-jnp.arange(0, ROT_DIM, 2, dtype=F32) / ROT_DIM)
    ang = positions.astype(F32)[:, None, :] * inv[None, :, None]
    lane = jnp.arange(LANES) % HEAD_DIM
    freq = jnp.arange(half)[:, None]
    rot = (lane[None, :] < ROT_DIM) & (lane[None, :] % half == freq)
    e_cos = rot.astype(F32)
    e_sin = jnp.where(rot, jnp.where(lane[None, :] < half, -1.0, 1.0), 0.0).astype(F32)
    expand = functools.partial(jnp.einsum, "bfs,fl->bsl", precision=lax.Precision.HIGHEST)
    return expand(jnp.cos(ang), e_cos) + (lane >= ROT_DIM).astype(F32), expand(jnp.sin(ang), e_sin)


def _layer(i, h, p, cos_t, sin_t, g_mix, w_in, b_f, qn_a, kn_a, qn_b, kn_b, w_o, g_ffn, w_rg, b_rg,
           w_re, b_re, w_up, w_down, g_ple, w_ple_gate, w_ple_proj):
    B, S, _ = h.shape
    T = B * S
    x2 = h.reshape(T, D_MODEL)

    bf = jnp.zeros((1, LANES), F32).at[0, :N_HEADS].set(b_f[i])
    pair = lambda g: jnp.tile(g, 2).reshape(1, LANES)
    w_r = jnp.zeros((D_MODEL, LANES), F32)
    w_r = w_r.at[:, :N_GROUPS].set(w_rg[i])
    w_r = w_r.at[:, ROUTER_LANE0:ROUTER_LANE0 + N_EXPERTS].set(
        jnp.transpose(w_re[i], (1, 0, 2)).reshape(D_MODEL, N_EXPERTS))
    b_r = jnp.zeros((1, LANES), F32).at[0, :N_GROUPS].set(b_rg[i])
    b_r = b_r.at[0, ROUTER_LANE0:ROUTER_LANE0 + N_EXPERTS].set(b_re[i].reshape(-1))

    qkv, f = _in_proj(x2, g_mix[i].reshape(1, -1), w_in, i)
    kbias = _forget_scan(f, bf, B, S)
    qkv3 = qkv.reshape(B, S, N_QKV)
    oa = _dilated_attn(qkv3, cos_t, sin_t, pair(qn_a[i]), pair(kn_a[i]))
    ob = _fox_attn(qkv3, kbias, pair(qn_b[i]), pair(kn_b[i]))

    h1, msort, route, cnt = _out_router(oa.reshape(T, D_SEC), ob.reshape(T, D_SEC), x2, w_o[i],
                                        g_ffn[i].reshape(1, -1), w_r.astype(BF16), b_r)

    n_tok_tiles = T // TM_TOK
    counts = cnt[:, 0, ROUTER_LANE0:ROUTER_LANE0 + N_EXPERTS].astype(jnp.int32)
    seg_rows = (counts + SEG_ALIGN - 1) // SEG_ALIGN * SEG_ALIGN
    seg_local = jnp.cumsum(seg_rows, axis=1) - seg_rows
    rows_e = jnp.sum(seg_rows, axis=0)
    tile_end = jnp.cumsum((rows_e + TM_MOE - 1) // TM_MOE)
    off = jnp.concatenate([jnp.zeros((1,), jnp.int32), tile_end * TM_MOE]).astype(jnp.int32)
    seg_global = off[None, :N_EXPERTS] + jnp.cumsum(seg_rows, axis=0) - seg_rows
    n_tiles = (2 * T + n_tok_tiles * N_EXPERTS * (SEG_ALIGN - 1)) // TM_MOE + N_EXPERTS
    nused = tile_end[-1:].astype(jnp.int32)
    tile_ids = jnp.minimum(jnp.arange(n_tiles, dtype=jnp.int32), nused[0] - 1)
    tile_e = jnp.sum((tile_ids[:, None] >= tile_end[None, :]).astype(jnp.int32), axis=1).astype(jnp.int32)
    chunk_row = jnp.arange(SORT_CHUNKS, dtype=jnp.int32)[None, :, None] * SEG_ALIGN
    lo, hi = seg_local[:, None, :], (seg_local + seg_rows)[:, None, :]
    chunk_dst = jnp.sum(jnp.where((chunk_row >= lo) & (chunk_row < hi), seg_global[:, None, :] + chunk_row - lo, 0),
                        axis=2).reshape(-1).astype(jnp.int32)
    n_chunks = (jnp.sum(seg_rows, axis=1) // SEG_ALIGN).astype(jnp.int32)
    sorted_row = jnp.arange(n_tiles * MOE_CHUNKS, dtype=jnp.int32)[:, None] * SEG_ALIGN
    seg_lo = seg_global.reshape(1, -1)
    seg_src = (jnp.arange(n_tok_tiles, dtype=jnp.int32)[:, None] * SORT_ROWS + seg_local).reshape(1, -1)
    in_seg = (sorted_row >= seg_lo) & (sorted_row < seg_lo + seg_rows.reshape(1, -1))
    src_chunk = jnp.where(jnp.any(in_seg, axis=1),
                          jnp.sum(jnp.where(in_seg, seg_src + sorted_row - seg_lo, 0), axis=1),
                          SORT_ROWS - SEG_ALIGN).astype(jnp.int32)
    e_ids = jnp.arange(N_EXPERTS, dtype=jnp.int32)
    active = rows_e > 0
    later_active = jnp.where(active[None, :] & (e_ids[None, :] > e_ids[:, None]), e_ids[None, :], N_EXPERTS)
    next_e = jnp.min(later_active, axis=1)
    next_e = jnp.where(next_e == N_EXPERTS, -1, next_e).astype(jnp.int32)
    slot_e = ((jnp.cumsum(active.astype(jnp.int32)) - 1) % 2).astype(jnp.int32)

    ys = _moe_experts(tile_e, nused, src_chunk, next_e, slot_e, msort, w_up[i], w_down[i])
    out = _combine_ple(n_chunks, chunk_dst, ys, h1, route, p[i].reshape(T, PLE_DIM),
                       g_ple[i].reshape(1, -1), w_ple_gate[i], w_ple_proj[i])
    return out.reshape(B, S, D_MODEL)


def kernel(x, p, positions, g_mix, w_in, b_f, qn_a, kn_a, qn_b, kn_b, w_o, g_ffn, w_rg, b_rg, w_re, b_re,
           w_up, w_down, g_ple, w_ple_gate, w_ple_proj):
    cos_t, sin_t = _rope_tables(positions)
    h = x
    for i in range(p.shape[0]):
        h = _layer(i, h, p, cos_t, sin_t, g_mix, w_in, b_f, qn_a, kn_a, qn_b, kn_b, w_o, g_ffn, w_rg, b_rg,
                   w_re, b_re, w_up, w_down, g_ple, w_ple_gate, w_ple_proj)
    return h
```

```python
import functools
import math

import jax
import jax.numpy as jnp
from jax import lax
from jax.experimental import pallas as pl
from jax.experimental.pallas import tpu as pltpu

D_MODEL = 1024
HEAD_DIM = 64
N_HEADS = 8
D_SEC = N_HEADS * HEAD_DIM
N_QKV = 6 * D_SEC
ROT_DIM = HEAD_DIM // 4
ROPE_THETA = 500000.0
N_GROUPS = 4
EXPERTS_PER_GROUP = 8
N_EXPERTS = N_GROUPS * EXPERTS_PER_GROUP
D_EXPERT = 512
PLE_DIM = 256
EPS = 1e-6
NEG = -1e30
WINDOW = 128

LANES = 128
SUBLANES = 8
VMEM_LIMIT = 48 * 1024 * 1024

TM_PROJ = 512
TQ = 128
TK = 256
N_PAIRS = N_HEADS // 2
TM_MOE = 512
TM_TOK = 512
SEG_ALIGN = 2 * SUBLANES
SORT_ROWS = 2 * TM_TOK + N_EXPERTS * SEG_ALIGN
SORT_CHUNKS = SORT_ROWS // SEG_ALIGN
SORT_TAIL = 256
ROUTER_LANE0 = N_GROUPS

Q_SCALE_LOG2 = math.log2(math.e) / math.sqrt(HEAD_DIM)

F32 = jnp.float32
BF16 = jnp.bfloat16
NT_DIMS = (((1,), (1,)), ((), ()))


def _cparams(*sem):
    return pltpu.CompilerParams(dimension_semantics=sem, vmem_limit_bytes=VMEM_LIMIT)


def _resident(shape):
    return pl.BlockSpec(shape, lambda i, *_: (0,) * len(shape), pipeline_mode=pl.Buffered(1))


def _in_proj_kernel(x_ref, g_ref, w_ref, qkv_ref, f_ref, w_s, wf_s):
    @pl.when(pl.program_id(0) == 0)
    def _():
        w_s[...] = w_ref[0, :, 0:N_QKV].astype(BF16)
        wf_s[...] = jnp.zeros_like(wf_s)
        wf_s[:, 0:N_HEADS] = w_ref[0, :, N_QKV:N_QKV + N_HEADS].astype(BF16)

    x = x_ref[...]
    ms = jnp.mean(x * x, axis=-1, keepdims=True)
    a = (x * lax.rsqrt(ms + EPS) * g_ref[...]).astype(BF16)
    qkv_ref[...] = jnp.dot(a, w_s[...], preferred_element_type=F32).astype(BF16)
    f_ref[...] = jnp.dot(a, wf_s[...], preferred_element_type=F32)


def _in_proj(x2, g_mix, w_in, layer):
    T = x2.shape[0]
    w_block = (1,) + w_in.shape[1:]
    return pl.pallas_call(
        _in_proj_kernel,
        grid=(T // TM_PROJ,),
        in_specs=[
            pl.BlockSpec((TM_PROJ, D_MODEL), lambda i: (i, 0)),
            pl.BlockSpec((1, D_MODEL), lambda i: (0, 0)),
            pl.BlockSpec(w_block, lambda i: (layer, 0, 0), pipeline_mode=pl.Buffered(1)),
        ],
        out_specs=[
            pl.BlockSpec((TM_PROJ, N_QKV), lambda i: (i, 0)),
            pl.BlockSpec((TM_PROJ, LANES), lambda i: (i, 0)),
        ],
        out_shape=[
            jax.ShapeDtypeStruct((T, N_QKV), BF16),
            jax.ShapeDtypeStruct((T, LANES), F32),
        ],
        scratch_shapes=[pltpu.VMEM((D_MODEL, N_QKV), BF16), pltpu.VMEM((D_MODEL, LANES), BF16)],
        compiler_params=_cparams("arbitrary"),
        name="in_proj",
    )(x2, g_mix, w_in)


BIAS_TERMS = 3


def _bias_lane(head):
    return (head // 2) * LANES + (HEAD_DIM if head % 2 == 0 else 0)


def _forget_scan_kernel(f_ref, bf_ref, kb_ref, *, seq):
    f = f_ref[...] + bf_ref[...]
    logf = jnp.minimum(f, 0.0) - jnp.log1p(jnp.exp(-jnp.abs(f)))

    def split_bf16(v):
        terms = []
        for _ in range(BIAS_TERMS):
            t = v.astype(BF16)
            terms.append(t)
            v = v - t.astype(F32)
        return terms

    tri = (lax.broadcasted_iota(jnp.int32, (TK, TK), 1) <= lax.broadcasted_iota(jnp.int32, (TK, TK), 0)).astype(BF16)
    carry = jnp.zeros((1, LANES), F32)
    chunks = []
    for j in range(seq // TK):
        terms = jnp.concatenate(split_bf16(logf[j * TK:(j + 1) * TK, :]), axis=1)
        pre = jnp.dot(tri, terms, preferred_element_type=F32)
        cj = carry + sum(pre[:, t * LANES:(t + 1) * LANES] for t in range(BIAS_TERMS))
        carry = cj[TK - 1:TK, :]
        chunks.append(cj)
    c = jnp.concatenate(chunks, axis=0)
    rest = c * (-math.log2(math.e))
    r_idx = lax.broadcasted_iota(jnp.int32, (LANES, D_SEC), 0)
    c_idx = lax.broadcasted_iota(jnp.int32, (LANES, D_SEC), 1)
    base = (r_idx // 2) * LANES + jnp.where(r_idx % 2 == 0, HEAD_DIM, 0)
    out = jnp.zeros((seq, D_SEC), F32)
    for t, term in enumerate(split_bf16(rest)):
        place = ((r_idx < N_HEADS) & (c_idx == base + t)).astype(BF16)
        out = out + jnp.dot(term, place, preferred_element_type=F32)
    kb_ref[0] = out.astype(BF16)


def _forget_scan(f, bf, batch, seq):
    return pl.pallas_call(
        functools.partial(_forget_scan_kernel, seq=seq),
        grid=(batch,),
        in_specs=[
            pl.BlockSpec((seq, LANES), lambda b: (b, 0)),
            pl.BlockSpec((1, LANES), lambda b: (0, 0)),
        ],
        out_specs=pl.BlockSpec((1, seq, D_SEC), lambda b: (b, 0, 0)),
        out_shape=jax.ShapeDtypeStruct((batch, seq, D_SEC), BF16),
        compiler_params=_cparams("parallel"),
        name="forget_scan",
    )(f, bf)


def _head_sumsq_matrix():
    r = lax.broadcasted_iota(jnp.int32, (LANES, LANES), 0) // HEAD_DIM
    c = lax.broadcasted_iota(jnp.int32, (LANES, LANES), 1) // HEAD_DIM
    return (r == c).astype(BF16)


def _qk_norm(x, gain, gmat):
    ss = jnp.dot((x * x).astype(BF16), gmat, preferred_element_type=F32)
    return x * lax.rsqrt(ss * (1.0 / HEAD_DIM) + EPS) * gain


def _stack_heads(qb, head0):
    zero = jnp.zeros_like(qb)
    return jnp.concatenate([jnp.where(head0, qb, zero), jnp.where(head0, zero, qb)], axis=0)


def _unstack(a, head0):
    return jnp.where(head0, a[:TQ], a[TQ:])


def _dilated_kernel(q_ref, k_ref, v_ref, cos_ref, sin_ref, gq_ref, gk_ref, o_ref,
                    qn_s, kn_s, v_s, acc_s, m_s, l_s, bias_s, *, seq):
    gmat = _head_sumsq_matrix()
    lane = lax.broadcasted_iota(jnp.int32, (TQ, LANES), 1)
    head0 = lane < HEAD_DIM
    chunk = 512
    lane_c = lax.broadcasted_iota(jnp.int32, (chunk, LANES), 1) % HEAD_DIM
    first_half = lane_c < ROT_DIM // 2

    def rope(t, cs, sn):
        partner = jnp.where(first_half, pltpu.roll(t, LANES - ROT_DIM // 2, axis=1),
                            pltpu.roll(t, ROT_DIM // 2, axis=1))
        return t * cs + partner * sn

    for c0 in range(0, seq, chunk):
        rows = pl.ds(c0, chunk)
        cs = cos_ref[0, rows, :]
        sn = sin_ref[0, rows, :]
        qn = _qk_norm(q_ref[0, rows, :].astype(F32), gq_ref[...], gmat)
        qn_s[rows, :] = rope(qn, cs, sn) * Q_SCALE_LOG2
        kn = _qk_norm(k_ref[0, rows, :].astype(F32), gk_ref[...], gmat)
        kn_s[rows, :] = rope(kn, cs, sn)
        v_s[rows, :] = v_ref[0, rows, :].astype(F32)

    u = lax.broadcasted_iota(jnp.int32, (2 * TQ, 2 * TQ), 0) % TQ
    c = lax.broadcasted_iota(jnp.int32, (2 * TQ, 2 * TQ), 1)
    for slot, dist0 in enumerate((0, TQ)):
        dist = dist0 + u - c
        bias_s[slot] = jnp.where((dist >= 0) & (dist <= WINDOW), 0.0, NEG)

    def scores(blk):
        p, q_rows, k_rows, nk, dist0 = blk
        qst = _stack_heads(qn_s[q_rows, :].astype(BF16), head0)
        s = lax.dot_general(qst, kn_s[k_rows, :].astype(BF16), NT_DIMS, preferred_element_type=F32)
        return s + bias_s[dist0 // TQ, :, 0:nk]

    def finish(blk, s):
        p, q_rows, k_rows, nk, dist0 = blk
        m = jnp.max(s, axis=-1, keepdims=True)
        pr = jnp.exp2(s - m)
        l = jnp.sum(pr, axis=-1, keepdims=True)
        acc = jnp.dot(pr.astype(BF16), v_s[k_rows, :].astype(BF16), preferred_element_type=F32)
        acc_s[p, q_rows, :] = _unstack(acc, head0)
        m_s[p, q_rows, :] = _unstack(jnp.broadcast_to(m, (2 * TQ, LANES)), head0)
        l_s[p, q_rows, :] = _unstack(jnp.broadcast_to(l, (2 * TQ, LANES)), head0)

    blocks = []
    for i in range(seq // TQ):
        kb0 = max(i - 1, 0)
        blocks.append((0, pl.ds(i * TQ, TQ), pl.ds(kb0 * TQ, 2 * TQ), 2 * TQ, (i - kb0) * TQ))
    for r in range(4):
        for n in range(seq // (4 * TQ)):
            kb0 = max(n - 1, 0)
            blocks.append((1, pl.ds(4 * TQ * n + r, TQ, stride=4), pl.ds(4 * TQ * kb0 + r, 2 * TQ, stride=4),
                           2 * TQ, (n - kb0) * TQ))
    for r in range(16):
        rows = pl.ds(r, TQ, stride=16)
        blocks.append((2, rows, rows, TQ, 0))

    ahead = 4
    pending = [scores(b) for b in blocks[:ahead]]
    for idx, b in enumerate(blocks):
        s = pending.pop(0)
        if idx + ahead < len(blocks):
            pending.append(scores(blocks[idx + ahead]))
        finish(b, s)

    for c0 in range(0, seq, chunk):
        rows = pl.ds(c0, chunk)
        m0, m1, m2 = m_s[0, rows, :], m_s[1, rows, :], m_s[2, rows, :]
        mm = jnp.maximum(jnp.maximum(m0, m1), m2)
        e0, e1, e2 = jnp.exp2(m0 - mm), jnp.exp2(m1 - mm), jnp.exp2(m2 - mm)
        num = acc_s[0, rows, :] * e0 + acc_s[1, rows, :] * e1 + acc_s[2, rows, :] * e2
        den = l_s[0, rows, :] * e0 + l_s[1, rows, :] * e1 + l_s[2, rows, :] * e2
        o_ref[0, rows, :] = (num / den).astype(BF16)


def _dilated_attn(qkv3, cos_t, sin_t, gq, gk):
    batch, seq, _ = qkv3.shape
    sec = D_SEC // LANES
    blk = lambda s: pl.BlockSpec((1, seq, LANES), lambda b, h, s=s: (b, 0, s * sec + h))
    tab = pl.BlockSpec((1, seq, LANES), lambda b, h: (b, 0, 0))
    vec = pl.BlockSpec((1, LANES), lambda b, h: (0, 0))
    return pl.pallas_call(
        functools.partial(_dilated_kernel, seq=seq),
        grid=(batch, N_PAIRS),
        in_specs=[blk(0), blk(1), blk(2), tab, tab, vec, vec],
        out_specs=pl.BlockSpec((1, seq, LANES), lambda b, h: (b, 0, h)),
        out_shape=jax.ShapeDtypeStruct((batch, seq, D_SEC), BF16),
        scratch_shapes=[
            pltpu.VMEM((seq, LANES), F32), pltpu.VMEM((seq, LANES), F32), pltpu.VMEM((seq, LANES), F32),
            pltpu.VMEM((3, seq, LANES), F32), pltpu.VMEM((3, seq, LANES), F32),
            pltpu.VMEM((3, seq, LANES), F32), pltpu.VMEM((2, 2 * TQ, 2 * TQ), F32),
        ],
        compiler_params=_cparams("parallel", "parallel"),
        name="dilated_attn",
    )(qkv3, qkv3, qkv3, cos_t, sin_t, gq, gk)


def _fox_kernel(q_ref, k_ref, v_ref, kb_ref, gq_ref, gk_ref, o_ref, qat_s, ka_s, vt_s, acc_s, tri_s, *, seq):
    gmat = _head_sumsq_matrix()
    chunk = 512
    lane = lax.broadcasted_iota(jnp.int32, (chunk, LANES), 1)
    low = lane < HEAD_DIM
    sel = jnp.where((lane >= HEAD_DIM) & (lane < HEAD_DIM + BIAS_TERMS), 1.0, 0.0)

    for c0 in range(0, seq, chunk):
        rows = pl.ds(c0, chunk)
        for g in range(N_PAIRS):
            lanes = slice(g * LANES, (g + 1) * LANES)
            qn = _qk_norm(q_ref[0, rows, lanes].astype(F32), gq_ref[...], gmat) * Q_SCALE_LOG2
            for h, qa in ((2 * g, jnp.where(low, qn, sel)),
                          (2 * g + 1, jnp.where(low, pltpu.roll(qn, HEAD_DIM, axis=1), sel))):
                qt = qa.T
                for cc in range(chunk // TK):
                    qat_s[c0 // TK + cc, h * LANES:(h + 1) * LANES, :] = qt[:, cc * TK:(cc + 1) * TK].astype(BF16)
            kn =_qk_norm(k_ref[0, rows, lanes].astype(F32), gk_ref[...], gmat)
            kb = kb_ref[0, rows, lanes].astype(F32)
            ka_s[2 * g, rows, :] = jnp.where(low, kn, kb).astype(BF16)
            ka_s[2 * g + 1, rows, :] = pltpu.roll(jnp.where(low, kb, kn), HEAD_DIM, axis=1).astype(BF16)
            vt = v_ref[0, rows, lanes].astype(F32).T
            for cc in range(chunk // TK):
                vt_s[c0 // TK + cc, lanes, :] = vt[:, cc * TK:(cc + 1) * TK].astype(BF16)

    r = lax.broadcasted_iota(jnp.int32, (TK, TK), 0)
    c = lax.broadcasted_iota(jnp.int32, (TK, TK), 1)
    tri_s[...] = jnp.where(r <= c, 0.0, NEG)

    def q_block(qi, carry):
        q_rows = pl.ds(pl.multiple_of(qi * TK, TK), TK)
        for h in range(N_HEADS):
            acc_s[h] = jnp.zeros((HEAD_DIM, TK), F32)

        def step(j, st, diagonal):
            k_rows = pl.ds(pl.multiple_of(j * TK, TK), TK)

            def scores(h):
                s = jnp.dot(ka_s[h, k_rows, :], qat_s[qi, h * LANES:(h + 1) * LANES, :], preferred_element_type=F32)
                return s + tri_s[...] if diagonal else s

            ahead = 6
            pending = [scores(h) for h in range(ahead)]
            new = []
            for h in range(N_HEADS):
                m, l = st[h]
                s = pending.pop(0)
                if h + ahead < N_HEADS:
                    pending.append(scores(h + ahead))
                m_new = jnp.maximum(m, jnp.max(s, axis=0, keepdims=True))
                alpha = jnp.exp2(m - m_new)
                pr = jnp.exp2(s - m_new)
                l_new = alpha * l + jnp.sum(pr, axis=0, keepdims=True)
                pv = jnp.dot(vt_s[j, h * HEAD_DIM:(h + 1) * HEAD_DIM, :], pr.astype(BF16),
                             preferred_element_type=F32)
                acc_s[h] = alpha * acc_s[h] + pv
                new.append((m_new, l_new))
            return tuple(new)

        init = tuple((jnp.full((1, TK), NEG, F32), jnp.zeros((1, TK), F32)) for _ in range(N_HEADS))
        st = lax.fori_loop(0, qi // 2, lambda jj, st: step(2 * jj + 1, step(2 * jj, st, False), False), init)
        st = lax.cond(qi % 2 == 1, lambda st: step(qi - 1, st, False), lambda st: st, st)
        st = step(qi, st, True)
        for g in range(N_PAIRS):
            o2 = jnp.concatenate([acc_s[2 * g] / st[2 * g][1], acc_s[2 * g + 1] / st[2 * g + 1][1]], axis=0)
            o_ref[0, q_rows, g * LANES:(g + 1) * LANES] = o2.T.astype(BF16)
        return carry

    lax.fori_loop(0, seq // TK, q_block, 0)


def _fox_attn(qkv3, kbias, gq, gk):
    batch, seq, _ = qkv3.shape
    blk = lambda s: pl.BlockSpec((1, seq, D_SEC), lambda b, s=s: (b, 0, s))
    vec = pl.BlockSpec((1, LANES), lambda b: (0, 0))
    return pl.pallas_call(
        functools.partial(_fox_kernel, seq=seq),
        grid=(batch,),
        in_specs=[blk(3), blk(4), blk(5), pl.BlockSpec((1, seq, D_SEC), lambda b: (b, 0, 0)), vec, vec],
        out_specs=pl.BlockSpec((1, seq, D_SEC), lambda b: (b, 0, 0)),
        out_shape=jax.ShapeDtypeStruct((batch, seq, D_SEC), BF16),
        scratch_shapes=[
            pltpu.VMEM((seq // TK, N_HEADS * LANES, TK), BF16),
            pltpu.VMEM((N_HEADS, seq, LANES), BF16),
            pltpu.VMEM((seq // TK, D_SEC, TK), BF16),
            pltpu.VMEM((N_HEADS, HEAD_DIM, TK), F32),
            pltpu.VMEM((TK, TK), F32),
        ],
        compiler_params=_cparams("parallel"),
        name="fox_attn",
    )(qkv3, qkv3, qkv3, kbias, gq, gk)


def _out_router_kernel(oa_ref, ob_ref, x_ref, wo_ref, g_ref, wr_ref, br_ref,
                       h1_ref, msort_ref, route_ref, cnt_ref, wo_s):
    tm = TM_TOK

    @pl.when(pl.program_id(0) == 0)
    def _():
        wo_s[...] = wo_ref[...].astype(BF16)

    h1 = (x_ref[...]
          + jnp.dot(oa_ref[...], wo_s[0:D_SEC, :], preferred_element_type=F32)
          + jnp.dot(ob_ref[...], wo_s[D_SEC:2 * D_SEC, :], preferred_element_type=F32))
    h1_ref[...] = h1
    ms = jnp.mean(h1 * h1, axis=-1, keepdims=True)
    mn = h1 * lax.rsqrt(ms + EPS) * g_ref[...]

    logits = jnp.dot(mn.astype(BF16), wr_ref[...], preferred_element_type=F32) + br_ref[...]
    lane = lax.broadcasted_iota(jnp.int32, (tm, LANES), 1).astype(F32)
    big = float(LANES)

    def first_argmax(vals):
        vmax = jnp.max(vals, axis=-1, keepdims=True)
        idx = jnp.min(jnp.where(vals == vmax, lane, big), axis=-1, keepdims=True)
        return vmax, idx

    lg = jnp.where(lane < N_GROUPS, logits, -jnp.inf)
    gmax, gidx = first_argmax(lg)
    gw = 1.0 / jnp.sum(jnp.exp(lg - gmax), axis=-1, keepdims=True)
    lo = ROUTER_LANE0 + EXPERTS_PER_GROUP * gidx
    le = jnp.where((lane >= lo) & (lane < lo + EXPERTS_PER_GROUP), logits, -jnp.inf)
    v0, i0 = first_argmax(le)
    v1, i1 = first_argmax(jnp.where(lane == i0, -jnp.inf, le))
    ex = jnp.exp(v1 - v0)
    w0 = gw / (1.0 + ex)
    w1 = gw * ex / (1.0 + ex)

    sel0 = lane == i0
    sel1 = lane == i1
    onehot = jnp.where(sel0 | sel1, 1.0, 0.0)
    r = lax.broadcasted_iota(jnp.int32, (tm, tm), 0)
    c = lax.broadcasted_iota(jnp.int32, (tm, tm), 1)
    ltri = (c <= r).astype(BF16)
    incl = jnp.dot(ltri, onehot.astype(BF16), preferred_element_type=F32)
    excl = incl - onehot
    counts = incl[tm - 1:tm, :]
    cnt_ref[0] = jnp.broadcast_to(counts, (SUBLANES, LANES))

    seg_rows = jnp.floor((counts + (SEG_ALIGN - 1.0)) * (1.0 / SEG_ALIGN)) * SEG_ALIGN
    used_rows = jnp.sum(seg_rows)
    seg_rows = jnp.broadcast_to(seg_rows, (tm, LANES))
    lp0 = jnp.sum(jnp.where(lane < i0, seg_rows, 0.0) + jnp.where(sel0, excl, 0.0), axis=-1, keepdims=True)
    lp1 = jnp.sum(jnp.where(lane < i1, seg_rows, 0.0) + jnp.where(sel1, excl, 0.0), axis=-1, keepdims=True)

    rec = jnp.zeros((tm, LANES), F32)
    for pos, val in enumerate((i0 - ROUTER_LANE0, i1 - ROUTER_LANE0, lp0, lp1, w0, w1)):
        rec = jnp.where(lane == float(pos), val, rec)
    route_ref[...] = rec[:, 0:SUBLANES]

    rec_t = rec.T
    out_row = lax.broadcasted_iota(jnp.int32, (SORT_ROWS, tm), 0).astype(F32)
    pick = jnp.where((out_row == rec_t[2:3, :]) | (out_row == rec_t[3:4, :]), 1.0, 0.0).astype(BF16)
    mn_bf = mn.astype(BF16)
    head_rows = SORT_ROWS - SORT_TAIL
    msort_ref[0:head_rows, :] = jnp.dot(pick[:head_rows], mn_bf, preferred_element_type=F32).astype(BF16)

    @pl.when(used_rows > head_rows)
    def _():
        msort_ref[head_rows:, :] = jnp.dot(pick[head_rows:], mn_bf, preferred_element_type=F32).astype(BF16)

    @pl.when(used_rows <= head_rows)
    def _():
        msort_ref[head_rows:, :] = jnp.zeros((SORT_TAIL, D_MODEL), BF16)


def _out_router(oa, ob, x2, w_o, g_ffn, w_r, b_r):
    T = x2.shape[0]
    tm = TM_TOK
    row = lambda w: pl.BlockSpec((tm, w), lambda i: (i, 0))
    const = lambda a, b: pl.BlockSpec((a, b), lambda i: (0, 0))
    return pl.pallas_call(
        _out_router_kernel,
        grid=(T // tm,),
        in_specs=[row(D_SEC), row(D_SEC), row(D_MODEL), _resident((D_MODEL, D_MODEL)), const(1, D_MODEL),
                  const(D_MODEL, LANES), const(1, LANES)],
        out_specs=[row(D_MODEL), pl.BlockSpec((SORT_ROWS, D_MODEL), lambda i: (i, 0)), row(SUBLANES),
                   pl.BlockSpec((1, SUBLANES, LANES), lambda i: (i, 0, 0))],
        out_shape=[
            jax.ShapeDtypeStruct((T, D_MODEL), F32),
            jax.ShapeDtypeStruct((T // tm * SORT_ROWS, D_MODEL), BF16),
            jax.ShapeDtypeStruct((T, SUBLANES), F32),
            jax.ShapeDtypeStruct((T // tm, SUBLANES, LANES), F32),
        ],
        scratch_shapes=[pltpu.VMEM((D_MODEL, D_MODEL), BF16)],
        compiler_params=_cparams("arbitrary"),
        name="out_router",
    )(oa, ob, x2, w_o, g_ffn, w_r, b_r)


def _chunk_row(c):
    return pl.multiple_of(c * SEG_ALIGN, SEG_ALIGN)


MOE_CHUNKS = TM_MOE // SEG_ALIGN


def _moe_kernel(tile_e_ref, nused_ref, src_ref, next_e_ref, slot_e_ref, msort_ref, wup_ref, wdn_ref, ys_ref,
                x_s, wup_f, wdn_f, wup_s, wdn_s, xsem, wsem):
    i = pl.program_id(0)
    nused = nused_ref[0]

    def gather(k, act):
        def chunk(c, carry):
            src = pl.multiple_of(src_ref[k * MOE_CHUNKS + c], SEG_ALIGN)
            act(pltpu.make_async_copy(msort_ref.at[pl.ds(src, SEG_ALIGN)],
                                      x_s.at[k % 2, pl.ds(_chunk_row(c), SEG_ALIGN)], xsem.at[k % 2]))
            return carry
        lax.fori_loop(0, MOE_CHUNKS, chunk, 0, unroll=4)

    def weights(e, s, act):
        act(pltpu.make_async_copy(wup_ref.at[e], wup_f.at[s], wsem.at[0, s]))
        act(pltpu.make_async_copy(wdn_ref.at[e], wdn_f.at[s], wsem.at[1, s]))

    @pl.when(i == 0)
    def _():
        gather(0, lambda c: c.start())
        weights(tile_e_ref[0], 0, lambda c: c.start())

    @pl.when(i + 1 < nused)
    def _():
        gather(i + 1, lambda c: c.start())

    @pl.when(i < nused)
    def _():
        e = tile_e_ref[i]

        @pl.when((i == 0) | (tile_e_ref[jnp.maximum(i - 1, 0)] != e))
        def _():
            s = slot_e_ref[e]
            weights(e, s, lambda c: c.wait())

            @pl.when(next_e_ref[e] >= 0)
            def _():
                weights(next_e_ref[e], 1 - s, lambda c: c.start())
            wup_s[...] = wup_f[s].astype(BF16)
            wdn_s[...] = wdn_f[s].astype(BF16)

        gather(i, lambda c: c.wait())
        hu = jnp.dot(x_s[i % 2], wup_s[...], preferred_element_type=F32)
        gate = hu[:, :D_EXPERT]
        hid = gate * (1.0 / (1.0 + jnp.exp(-gate))) * hu[:, D_EXPERT:]
        ys_ref[...] = jnp.dot(hid.astype(BF16), wdn_s[...], preferred_element_type=F32).astype(BF16)

    @pl.when(i >= nused)
    def _():
        ys_ref[...] = jnp.zeros_like(ys_ref)


def _moe_experts(tile_e, nused, src_chunk, next_e, slot_e, msort, w_up, w_down):
    n_tiles = tile_e.shape[0]
    grid_spec = pltpu.PrefetchScalarGridSpec(
        num_scalar_prefetch=5,
        grid=(n_tiles,),
        in_specs=[pl.BlockSpec(memory_space=pl.ANY)] * 3,
        out_specs=pl.BlockSpec((TM_MOE, D_MODEL), lambda i, *_: (i, 0)),
        scratch_shapes=[
            pltpu.VMEM((2, TM_MOE, D_MODEL), BF16),
            pltpu.VMEM((2, D_MODEL, 2 * D_EXPERT), F32), pltpu.VMEM((2, D_EXPERT, D_MODEL), F32),
            pltpu.VMEM((D_MODEL, 2 * D_EXPERT), BF16), pltpu.VMEM((D_EXPERT, D_MODEL), BF16),
            pltpu.SemaphoreType.DMA((2,)), pltpu.SemaphoreType.DMA((2, 2)),
        ],
    )
    return pl.pallas_call(
        _moe_kernel,
        grid_spec=grid_spec,
        out_shape=jax.ShapeDtypeStruct((n_tiles * TM_MOE, D_MODEL), BF16),
        compiler_params=_cparams("arbitrary"),
        name="moe_experts",
    )(tile_e, nused, src_chunk, next_e, slot_e, msort, w_up, w_down)


def _combine_kernel(nchunk_ref, src_ref, ys_ref, h1_ref, route_ref, p_ref,
                    g_ref, wg_ref, wp_ref, o_ref, gath_s, sems, wg_s, wp_s):
    i = pl.program_id(0)
    n_steps = pl.num_programs(0)

    @pl.when(i == 0)
    def _():
        wg_s[...] = wg_ref[...].astype(BF16)
        wp_s[...] = wp_ref[...].astype(BF16)
    tm = TM_TOK
    slot = i % 2

    def gather(t, s, act):
        def chunk(c, carry):
            src = pl.multiple_of(src_ref[t * SORT_CHUNKS + c], SEG_ALIGN)
            act(pltpu.make_async_copy(ys_ref.at[pl.ds(src, SEG_ALIGN)],
                                      gath_s.at[s, pl.ds(_chunk_row(c), SEG_ALIGN)], sems.at[s]))
            return carry
        lax.fori_loop(0, nchunk_ref[t], chunk, 0)

    @pl.when(i == 0)
    def _():
        gath_s[...] = jnp.zeros_like(gath_s)
        gather(0, 0, lambda c: c.start())

    @pl.when(i + 1 < n_steps)
    def _():
        gather(i + 1, 1 - slot, lambda c: c.start())

    gather(i, slot, lambda c: c.wait())

    route = route_ref[...]
    col = lax.broadcasted_iota(jnp.int32, (tm, SORT_ROWS), 1).astype(F32)
    weights = (jnp.where(col == route[:, 2:3], route[:, 4:5], 0.0)
               + jnp.where(col == route[:, 3:4], route[:, 5:6], 0.0)).astype(BF16)
    y = lax.cond(
        nchunk_ref[i] * SEG_ALIGN <= SORT_ROWS - SORT_TAIL,
        lambda: jnp.dot(weights[:, :SORT_ROWS - SORT_TAIL], gath_s[slot, 0:SORT_ROWS - SORT_TAIL, :],
                        preferred_element_type=F32),
        lambda: jnp.dot(weights, gath_s[slot], preferred_element_type=F32))
    h2 = h1_ref[...] + y
    ms = jnp.mean(h2 * h2, axis=-1, keepdims=True)
    n = (h2 * lax.rsqrt(ms + EPS) * g_ref[...]).astype(BF16)
    z = jnp.dot(n, wg_s[...], preferred_element_type=F32)
    gate = 1.0 / (1.0 + jnp.exp(-z))
    ple = jnp.dot(p_ref[...].astype(BF16), wp_s[...], preferred_element_type=F32)
    o_ref[...] = h2 + gate * ple


def _combine_ple(n_chunks, chunk_dst, ys, h1, route, p2, g_ple, w_gate, w_proj):
    T = h1.shape[0]
    tm = TM_TOK
    row = lambda w: pl.BlockSpec((tm, w), lambda i, *_: (i, 0))
    const = lambda a, b: pl.BlockSpec((a, b), lambda i, *_: (0, 0))
    grid_spec = pltpu.PrefetchScalarGridSpec(
        num_scalar_prefetch=2,
        grid=(T // tm,),
        in_specs=[pl.BlockSpec(memory_space=pl.ANY), row(D_MODEL), row(SUBLANES), row(PLE_DIM),
                  const(1, D_MODEL), _resident((D_MODEL, D_MODEL)), _resident((PLE_DIM, D_MODEL))],
        out_specs=row(D_MODEL),
        scratch_shapes=[pltpu.VMEM((2, SORT_ROWS, D_MODEL), BF16), pltpu.SemaphoreType.DMA((2,)),
                        pltpu.VMEM((D_MODEL, D_MODEL), BF16), pltpu.VMEM((PLE_DIM, D_MODEL), BF16)],
    )
    return pl.pallas_call(
        _combine_kernel,
        grid_spec=grid_spec,
        out_shape=jax.ShapeDtypeStruct((T, D_MODEL), F32),
        compiler_params=_cparams("arbitrary"),
        name="combine_ple",
    )(n_chunks, chunk_dst, ys, h1, route, p2, g_ple, w_gate, w_proj)


def _rope_tables(positions):
    half = ROT_DIM // 2
    inv = ROPE_THETA ** (-jnp.arange(0, ROT_DIM, 2, dtype=F32) / ROT_DIM)
    ang = positions.astype(F32)[:, None, :] * inv[None, :, None]
    lane = jnp.arange(LANES) % HEAD_DIM
    freq = jnp.arange(half)[:, None]
    rot = (lane[None, :] < ROT_DIM) & (lane[None, :] % half == freq)
    e_cos = rot.astype(F32)
    e_sin = jnp.where(rot, jnp.where(lane[None, :] < half, -1.0, 1.0), 0.0).astype(F32)
    expand = functools.partial(jnp.einsum, "bfs,fl->bsl", precision=lax.Precision.HIGHEST)
    return expand(jnp.cos(ang), e_cos) + (lane >= ROT_DIM).astype(F32), expand(jnp.sin(ang), e_sin)


def _layer(i, h, p, cos_t, sin_t, g_mix, w_in, b_f, qn_a, kn_a, qn_b, kn_b, w_o, g_ffn, w_rg, b_rg,
           w_re, b_re, w_up, w_down, g_ple, w_ple_gate, w_ple_proj):
    B, S, _ = h.shape
    T = B * S
    x2 = h.reshape(T, D_MODEL)

    bf = jnp.zeros((1, LANES), F32).at[0, :N_HEADS].set(b_f[i])
    pair = lambda g: jnp.tile(g, 2).reshape(1, LANES)
    w_r = jnp.zeros((D_MODEL, LANES), F32)
    w_r = w_r.at[:, :N_GROUPS].set(w_rg[i])
    w_r = w_r.at[:, ROUTER_LANE0:ROUTER_LANE0 + N_EXPERTS].set(
        jnp.transpose(w_re[i], (1, 0, 2)).reshape(D_MODEL, N_EXPERTS))
    b_r = jnp.zeros((1, LANES), F32).at[0, :N_GROUPS].set(b_rg[i])
    b_r = b_r.at[0, ROUTER_LANE0:ROUTER_LANE0 + N_EXPERTS].set(b_re[i].reshape(-1))

    qkv, f = _in_proj(x2, g_mix[i].reshape(1, -1), w_in, i)
    kbias = _forget_scan(f, bf, B, S)
    qkv3 = qkv.reshape(B, S, N_QKV)
    oa = _dilated_attn(qkv3, cos_t, sin_t, pair(qn_a[i]), pair(kn_a[i]))
    ob = _fox_attn(qkv3, kbias, pair(qn_b[i]), pair(kn_b[i]))

    h1, msort, route, cnt = _out_router(oa.reshape(T, D_SEC), ob.reshape(T, D_SEC), x2, w_o[i],
                                        g_ffn[i].reshape(1, -1), w_r.astype(BF16), b_r)

    n_tok_tiles = T // TM_TOK
    counts = cnt[:, 0, ROUTER_LANE0:ROUTER_LANE0 + N_EXPERTS].astype(jnp.int32)
    seg_rows = (counts + SEG_ALIGN - 1) // SEG_ALIGN * SEG_ALIGN
    seg_local = jnp.cumsum(seg_rows, axis=1) - seg_rows
    rows_e = jnp.sum(seg_rows, axis=0)
    tile_end = jnp.cumsum((rows_e + TM_MOE - 1) // TM_MOE)
    off = jnp.concatenate([jnp.zeros((1,), jnp.int32), tile_end * TM_MOE]).astype(jnp.int32)
    seg_global = off[None, :N_EXPERTS] + jnp.cumsum(seg_rows, axis=0) - seg_rows
    n_tiles = (2 * T + n_tok_tiles * N_EXPERTS * (SEG_ALIGN - 1)) // TM_MOE + N_EXPERTS
    nused = tile_end[-1:].astype(jnp.int32)
    tile_ids = jnp.minimum(jnp.arange(n_tiles, dtype=jnp.int32), nused[0] - 1)
    tile_e = jnp.sum((tile_ids[:, None] >= tile_end[None, :]).astype(jnp.int32), axis=1).astype(jnp.int32)
    chunk_row = jnp.arange(SORT_CHUNKS, dtype=jnp.int32)[None, :, None] * SEG_ALIGN
    lo, hi = seg_local[:, None, :], (seg_local + seg_rows)[:, None, :]
    chunk_dst = jnp.sum(jnp.where((chunk_row >= lo) & (chunk_row < hi), seg_global[:, None, :] + chunk_row - lo, 0),
                        axis=2).reshape(-1).astype(jnp.int32)
    n_chunks = (jnp.sum(seg_rows, axis=1) // SEG_ALIGN).astype(jnp.int32)
    sorted_row = jnp.arange(n_tiles * MOE_CHUNKS, dtype=jnp.int32)[:, None] * SEG_ALIGN
    seg_lo = seg_global.reshape(1, -1)
    seg_src = (jnp.arange(n_tok_tiles, dtype=jnp.int32)[:, None] * SORT_ROWS + seg_local).reshape(1, -1)
    in_seg = (sorted_row >= seg_lo) & (sorted_row < seg_lo + seg_rows.reshape(1, -1))
    src_chunk = jnp.where(jnp.any(in_seg, axis=1),
                          jnp.sum(jnp.where(in_seg, seg_src + sorted_row - seg_lo, 0), axis=1),
                          SORT_ROWS - SEG_ALIGN).astype(jnp.int32)
    e_ids = jnp.arange(N_EXPERTS, dtype=jnp.int32)
    active = rows_e > 0
    later_active = jnp.where(active[None, :] & (e_ids[None, :] > e_ids[:, None]), e_ids[None, :], N_EXPERTS)
    next_e = jnp.min(later_active, axis=1)
    next_e = jnp.where(next_e == N_EXPERTS, -1, next_e).astype(jnp.int32)
    slot_e = ((jnp.cumsum(active.astype(jnp.int32)) - 1) % 2).astype(jnp.int32)

    ys = _moe_experts(tile_e, nused, src_chunk, next_e, slot_e, msort, w_up[i], w_down[i])
    out = _combine_ple(n_chunks, chunk_dst, ys, h1, route, p[i].reshape(T, PLE_DIM),
                       g_ple[i].reshape(1, -1), w_ple_gate[i], w_ple_proj[i])
    return out.reshape(B, S, D_MODEL)


def kernel(x, p, positions, g_mix, w_in, b_f, qn_a, kn_a, qn_b, kn_b, w_o, g_ffn, w_rg, b_rg, w_re, b_re,
           w_up, w_down, g_ple, w_ple_gate, w_ple_proj):
    cos_t, sin_t = _rope_tables(positions)
    h = x
    for i in range(p.shape[0]):
        h = _layer(i, h, p, cos_t, sin_t, g_mix, w_in, b_f, qn_a, kn_a, qn_b, kn_b, w_o, g_ffn, w_rg, b_rg,
                   w_re, b_re, w_up, w_down, g_ple, w_ple_gate, w_ple_proj)
    return h
```

```python
import functools
import math

import jax
import jax.numpy as jnp
from jax import lax
from jax.experimental import pallas as pl
from jax.experimental.pallas import tpu as pltpu

D_MODEL = 1024
HEAD_DIM = 64
N_HEADS = 8
D_SEC = N_HEADS * HEAD_DIM
N_QKV = 6 * D_SEC
ROT_DIM = HEAD_DIM // 4
ROPE_THETA = 500000.0
N_GROUPS = 4
EXPERTS_PER_GROUP = 8
N_EXPERTS = N_GROUPS * EXPERTS_PER_GROUP
D_EXPERT = 512
PLE_DIM = 256
EPS = 1e-6
NEG = -1e30
WINDOW = 128

LANES = 128
SUBLANES = 8
VMEM_LIMIT = 48 * 1024 * 1024

TM_PROJ = 512
TQ = 128
TK = 256
N_PAIRS = N_HEADS // 2
TM_MOE = 512
TM_TOK = 512
SEG_ALIGN = 2 * SUBLANES
SORT_ROWS = 2 * TM_TOK + N_EXPERTS * SEG_ALIGN
SORT_CHUNKS = SORT_ROWS // SEG_ALIGN
SORT_TAIL = 256
GATHER_UNROLL = 4
ROUTER_LANE0 = N_GROUPS

Q_SCALE_LOG2 = math.log2(math.e) / math.sqrt(HEAD_DIM)

F32 = jnp.float32
BF16 = jnp.bfloat16
NT_DIMS = (((1,), (1,)), ((), ()))


def _cparams(*sem):
    return pltpu.CompilerParams(dimension_semantics=sem, vmem_limit_bytes=VMEM_LIMIT)


def _resident(shape):
    return pl.BlockSpec(shape, lambda i, *_: (0,) * len(shape), pipeline_mode=pl.Buffered(1))


def _in_proj_kernel(x_ref, g_ref, w_ref, qkv_ref, f_ref, w_s, wf_s):
    @pl.when(pl.program_id(0) == 0)
    def _():
        w_s[...] = w_ref[0, :, 0:N_QKV].astype(BF16)
        wf_s[...] = jnp.zeros_like(wf_s)
        wf_s[:, 0:N_HEADS] = w_ref[0, :, N_QKV:N_QKV + N_HEADS].astype(BF16)

    x = x_ref[...]
    ms = jnp.mean(x * x, axis=-1, keepdims=True)
    a = (x * lax.rsqrt(ms + EPS) * g_ref[...]).astype(BF16)
    qkv_ref[...] = jnp.dot(a, w_s[...], preferred_element_type=F32).astype(BF16)
    f_ref[...] = jnp.dot(a, wf_s[...], preferred_element_type=F32)


def _in_proj(x2, g_mix, w_in, layer):
    T = x2.shape[0]
    w_block = (1,) + w_in.shape[1:]
    return pl.pallas_call(
        _in_proj_kernel,
        grid=(T // TM_PROJ,),
        in_specs=[
            pl.BlockSpec((TM_PROJ, D_MODEL), lambda i: (i, 0)),
            pl.BlockSpec((1, D_MODEL), lambda i: (0, 0)),
            pl.BlockSpec(w_block, lambda i: (layer, 0, 0), pipeline_mode=pl.Buffered(1)),
        ],
        out_specs=[
            pl.BlockSpec((TM_PROJ, N_QKV), lambda i: (i, 0)),
            pl.BlockSpec((TM_PROJ, LANES), lambda i: (i, 0)),
        ],
        out_shape=[
            jax.ShapeDtypeStruct((T, N_QKV), BF16),
            jax.ShapeDtypeStruct((T, LANES), F32),
        ],
        scratch_shapes=[pltpu.VMEM((D_MODEL, N_QKV), BF16), pltpu.VMEM((D_MODEL, LANES), BF16)],
        compiler_params=_cparams("arbitrary"),
        name="in_proj",
    )(x2, g_mix, w_in)


BIAS_TERMS = 3


def _bias_lane(head):
    return (head // 2) * LANES + (HEAD_DIM if head % 2 == 0 else 0)


def _forget_scan_kernel(f_ref, bf_ref, kb_ref, *, seq):
    f = f_ref[...] + bf_ref[...]
    logf = jnp.minimum(f, 0.0) - jnp.log1p(jnp.exp(-jnp.abs(f)))

    def split_bf16(v):
        terms = []
        for _ in range(BIAS_TERMS):
            t = v.astype(BF16)
            terms.append(t)
            v = v - t.astype(F32)
        return terms

    tri = (lax.broadcasted_iota(jnp.int32, (TK, TK), 1) <= lax.broadcasted_iota(jnp.int32, (TK, TK), 0)).astype(BF16)
    carry = jnp.zeros((1, LANES), F32)
    chunks = []
    for j in range(seq // TK):
        terms = jnp.concatenate(split_bf16(logf[j * TK:(j + 1) * TK, :]), axis=1)
        pre = jnp.dot(tri, terms, preferred_element_type=F32)
        cj = carry + sum(pre[:, t * LANES:(t + 1) * LANES] for t in range(BIAS_TERMS))
        carry = cj[TK - 1:TK, :]
        chunks.append(cj)
    c = jnp.concatenate(chunks, axis=0)
    rest = c * (-math.log2(math.e))
    r_idx = lax.broadcasted_iota(jnp.int32, (LANES, D_SEC), 0)
    c_idx = lax.broadcasted_iota(jnp.int32, (LANES, D_SEC), 1)
    base = (r_idx // 2) * LANES + jnp.where(r_idx % 2 == 0, HEAD_DIM, 0)
    out = jnp.zeros((seq, D_SEC), F32)
    for t, term in enumerate(split_bf16(rest)):
        place = ((r_idx < N_HEADS) & (c_idx == base + t)).astype(BF16)
        out = out + jnp.dot(term, place, preferred_element_type=F32)
    kb_ref[0] = out.astype(BF16)


def _forget_scan(f, bf, batch, seq):
    return pl.pallas_call(
        functools.partial(_forget_scan_kernel, seq=seq),
        grid=(batch,),
        in_specs=[
            pl.BlockSpec((seq, LANES), lambda b: (b, 0)),
            pl.BlockSpec((1, LANES), lambda b: (0, 0)),
        ],
        out_specs=pl.BlockSpec((1, seq, D_SEC), lambda b: (b, 0, 0)),
        out_shape=jax.ShapeDtypeStruct((batch, seq, D_SEC), BF16),
        compiler_params=_cparams("parallel"),
        name="forget_scan",
    )(f, bf)


def _head_sumsq_matrix():
    r = lax.broadcasted_iota(jnp.int32, (LANES, LANES), 0) // HEAD_DIM
    c = lax.broadcasted_iota(jnp.int32, (LANES, LANES), 1) // HEAD_DIM
    return (r == c).astype(BF16)


def _qk_norm(x, gain, gmat):
    ss = jnp.dot((x * x).astype(BF16), gmat, preferred_element_type=F32)
    return x * lax.rsqrt(ss * (1.0 / HEAD_DIM) + EPS) * gain


def _stack_heads(qb, head0):
    zero = jnp.zeros_like(qb)
    return jnp.concatenate([jnp.where(head0, qb, zero), jnp.where(head0, zero, qb)], axis=0)


def _unstack(a, head0):
    return jnp.where(head0, a[:TQ], a[TQ:])


def _dilated_kernel(q_ref, k_ref, v_ref, cos_ref, sin_ref, gq_ref, gk_ref, o_ref,
                    qn_s, kn_s, v_s, acc_s, m_s, l_s, bias_s, *, seq):
    gmat = _head_sumsq_matrix()
    lane = lax.broadcasted_iota(jnp.int32, (TQ, LANES), 1)
    head0 = lane < HEAD_DIM
    chunk = 512
    lane_c = lax.broadcasted_iota(jnp.int32, (chunk, LANES), 1) % HEAD_DIM
    first_half = lane_c < ROT_DIM // 2

    def rope(t, cs, sn):
        partner = jnp.where(first_half, pltpu.roll(t, LANES - ROT_DIM // 2, axis=1),
                            pltpu.roll(t, ROT_DIM // 2, axis=1))
        return t * cs + partner * sn

    for c0 in range(0, seq, chunk):
        rows = pl.ds(c0, chunk)
        cs = cos_ref[0, rows, :]
        sn = sin_ref[0, rows, :]
        qn = _qk_norm(q_ref[0, rows, :].astype(F32), gq_ref[...], gmat)
        qn_s[rows, :] = rope(qn, cs, sn) * Q_SCALE_LOG2
        kn = _qk_norm(k_ref[0, rows, :].astype(F32), gk_ref[...], gmat)
        kn_s[rows, :] = rope(kn, cs, sn)
        v_s[rows, :] = v_ref[0, rows, :].astype(F32)

    u = lax.broadcasted_iota(jnp.int32, (2 * TQ, 2 * TQ), 0) % TQ
    c = lax.broadcasted_iota(jnp.int32, (2 * TQ, 2 * TQ), 1)
    for slot, dist0 in enumerate((0, TQ)):
        dist = dist0 + u - c
        bias_s[slot] = jnp.where((dist >= 0) & (dist <= WINDOW), 0.0, NEG)

    def scores(blk):
        p, q_rows, k_rows, nk, dist0 = blk
        qst = _stack_heads(qn_s[q_rows, :].astype(BF16), head0)
        s = lax.dot_general(qst, kn_s[k_rows, :].astype(BF16), NT_DIMS, preferred_element_type=F32)
        return s + bias_s[dist0 // TQ, :, 0:nk]

    def finish(blk, s):
        p, q_rows, k_rows, nk, dist0 = blk
        m = jnp.max(s, axis=-1, keepdims=True)
        pr = jnp.exp2(s - m)
        l = jnp.sum(pr, axis=-1, keepdims=True)
        acc = jnp.dot(pr.astype(BF16), v_s[k_rows, :].astype(BF16), preferred_element_type=F32)
        acc_s[p, q_rows, :] = _unstack(acc, head0)
        m_s[p, q_rows, :] = _unstack(jnp.broadcast_to(m, (2 * TQ, LANES)), head0)
        l_s[p, q_rows, :] = _unstack(jnp.broadcast_to(l, (2 * TQ, LANES)), head0)

    blocks = []
    for i in range(seq // TQ):
        kb0 = max(i - 1, 0)
        blocks.append((0, pl.ds(i * TQ, TQ), pl.ds(kb0 * TQ, 2 * TQ), 2 * TQ, (i - kb0) * TQ))
    for r in range(4):
        for n in range(seq // (4 * TQ)):
            kb0 = max(n - 1, 0)
            blocks.append((1, pl.ds(4 * TQ * n + r, TQ, stride=4), pl.ds(4 * TQ * kb0 + r, 2 * TQ, stride=4),
                           2 * TQ, (n - kb0) * TQ))
    for r in range(16):
        rows = pl.ds(r, TQ, stride=16)
        blocks.append((2, rows, rows, TQ, 0))

    ahead = 4
    pending = [scores(b) for b in blocks[:ahead]]
    for idx, b in enumerate(blocks):
        s = pending.pop(0)
        if idx + ahead < len(blocks):
            pending.append(scores(blocks[idx + ahead]))
        finish(b, s)

    for c0 in range(0, seq, chunk):
        rows = pl.ds(c0, chunk)
        m0, m1, m2 = m_s[0, rows, :], m_s[1, rows, :], m_s[2, rows, :]
        mm = jnp.maximum(jnp.maximum(m0, m1), m2)
        e0, e1, e2 = jnp.exp2(m0 - mm), jnp.exp2(m1 - mm), jnp.exp2(m2 - mm)
        num = acc_s[0, rows, :] * e0 + acc_s[1, rows, :] * e1 + acc_s[2, rows, :] * e2
        den = l_s[0, rows, :] * e0 + l_s[1, rows, :] * e1 + l_s[2, rows, :] * e2
        o_ref[0, rows, :] = (num / den).astype(BF16)


def _dilated_attn(qkv3, cos_t, sin_t, gq, gk):
    batch, seq, _ = qkv3.shape
    sec = D_SEC // LANES
    blk = lambda s: pl.BlockSpec((1, seq, LANES), lambda b, h, s=s: (b, 0, s * sec + h))
    tab = pl.BlockSpec((1, seq, LANES), lambda b, h: (b, 0, 0))
    vec = pl.BlockSpec((1, LANES), lambda b, h: (0, 0))
    return pl.pallas_call(
        functools.partial(_dilated_kernel, seq=seq),
        grid=(batch, N_PAIRS),
        in_specs=[blk(0), blk(1), blk(2), tab, tab, vec, vec],
        out_specs=pl.BlockSpec((1, seq, LANES), lambda b, h: (b, 0, h)),
        out_shape=jax.ShapeDtypeStruct((batch, seq, D_SEC), BF16),
        scratch_shapes=[
            pltpu.VMEM((seq, LANES), F32), pltpu.VMEM((seq, LANES), F32), pltpu.VMEM((seq, LANES), F32),
            pltpu.VMEM((3, seq, LANES), F32), pltpu.VMEM((3, seq, LANES), F32),
            pltpu.VMEM((3, seq, LANES), F32), pltpu.VMEM((2, 2 * TQ, 2 * TQ), F32),
        ],
        compiler_params=_cparams("parallel", "parallel"),
        name="dilated_attn",
    )(qkv3, qkv3, qkv3, cos_t, sin_t, gq, gk)


def _fox_kernel(q_ref, k_ref, v_ref, kb_ref, gq_ref, gk_ref, o_ref, qat_s, ka_s, vt_s, acc_s, tri_s, *, seq):
    gmat = _head_sumsq_matrix()
    chunk = 512
    lane = lax.broadcasted_iota(jnp.int32, (chunk, LANES), 1)
    low = lane < HEAD_DIM
    sel = jnp.where((lane >= HEAD_DIM) & (lane < HEAD_DIM + BIAS_TERMS), 1.0, 0.0)

    for c0 in range(0, seq, chunk):
        rows = pl.ds(c0, chunk)
        for g in range(N_PAIRS):
            lanes = slice(g * LANES, (g + 1) * LANES)
            qn = _qk_norm(q_ref[0, rows, lanes].astype(F32), gq_ref[...], gmat) * Q_SCALE_LOG2
            for h, qa in ((2 * g, jnp.where(low, qn, sel)),
                          (2 * g + 1, jnp.where(low, pltpu.roll(qn, HEAD_DIM, axis=1), sel))):
                qt = qa.T
                for cc in range(chunk // TK):
                    qat_s[c0 // TK + cc, h * LANES:(h + 1) * LANES, :] = qt[:, cc * TK:(cc + 1) * TK].astype(BF16)
            kn =_qk_norm(k_ref[0, rows, lanes].astype(F32), gk_ref[...], gmat)
            kb = kb_ref[0, rows, lanes].astype(F32)
            ka_s[2 * g, rows, :] = jnp.where(low, kn, kb).astype(BF16)
            ka_s[2 * g + 1, rows, :] = pltpu.roll(jnp.where(low, kb, kn), HEAD_DIM, axis=1).astype(BF16)
            vt = v_ref[0, rows, lanes].astype(F32).T
            for cc in range(chunk // TK):
                vt_s[c0 // TK + cc, lanes, :] = vt[:, cc * TK:(cc + 1) * TK].astype(BF16)

    r = lax.broadcasted_iota(jnp.int32, (TK, TK), 0)
    c = lax.broadcasted_iota(jnp.int32, (TK, TK), 1)
    tri_s[...] = jnp.where(r <= c, 0.0, NEG)

    def q_block(qi, carry):
        q_rows = pl.ds(pl.multiple_of(qi * TK, TK), TK)
        for h in range(N_HEADS):
            acc_s[h] = jnp.zeros((HEAD_DIM, TK), F32)

        def step(j, st, diagonal):
            k_rows = pl.ds(pl.multiple_of(j * TK, TK), TK)

            def scores(h):
                s = jnp.dot(ka_s[h, k_rows, :], qat_s[qi, h * LANES:(h + 1) * LANES, :], preferred_element_type=F32)
                return s + tri_s[...] if diagonal else s

            ahead = 6
            pending = [scores(h) for h in range(ahead)]
            new = []
            for h in range(N_HEADS):
                m, l = st[h]
                s = pending.pop(0)
                if h + ahead < N_HEADS:
                    pending.append(scores(h + ahead))
                m_new = jnp.maximum(m, jnp.max(s, axis=0, keepdims=True))
                alpha = jnp.exp2(m - m_new)
                pr = jnp.exp2(s - m_new)
                l_new = alpha * l + jnp.sum(pr, axis=0, keepdims=True)
                pv = jnp.dot(vt_s[j, h * HEAD_DIM:(h + 1) * HEAD_DIM, :], pr.astype(BF16),
                             preferred_element_type=F32)
                acc_s[h] = alpha * acc_s[h] + pv
                new.append((m_new, l_new))
            return tuple(new)

        init = tuple((jnp.full((1, TK), NEG, F32), jnp.zeros((1, TK), F32)) for _ in range(N_HEADS))
        st = lax.fori_loop(0, qi // 2, lambda jj, st: step(2 * jj + 1, step(2 * jj, st, False), False), init)
        st = lax.cond(qi % 2 == 1, lambda st: step(qi - 1, st, False), lambda st: st, st)
        st = step(qi, st, True)
        for g in range(N_PAIRS):
            o2 = jnp.concatenate([acc_s[2 * g] / st[2 * g][1], acc_s[2 * g + 1] / st[2 * g + 1][1]], axis=0)
            o_ref[0, q_rows, g * LANES:(g + 1) * LANES] = o2.T.astype(BF16)
        return carry

    lax.fori_loop(0, seq // TK, q_block, 0)


def _fox_attn(qkv3, kbias, gq, gk):
    batch, seq, _ = qkv3.shape
    blk = lambda s: pl.BlockSpec((1, seq, D_SEC), lambda b, s=s: (b, 0, s))
    vec = pl.BlockSpec((1, LANES), lambda b: (0, 0))
    return pl.pallas_call(
        functools.partial(_fox_kernel, seq=seq),
        grid=(batch,),
        in_specs=[blk(3), blk(4), blk(5), pl.BlockSpec((1, seq, D_SEC), lambda b: (b, 0, 0)), vec, vec],
        out_specs=pl.BlockSpec((1, seq, D_SEC), lambda b: (b, 0, 0)),
        out_shape=jax.ShapeDtypeStruct((batch, seq, D_SEC), BF16),
        scratch_shapes=[
            pltpu.VMEM((seq // TK, N_HEADS * LANES, TK), BF16),
            pltpu.VMEM((N_HEADS, seq, LANES), BF16),
            pltpu.VMEM((seq // TK, D_SEC, TK), BF16),
            pltpu.VMEM((N_HEADS, HEAD_DIM, TK), F32),
            pltpu.VMEM((TK, TK), F32),
        ],
        compiler_params=_cparams("parallel"),
        name="fox_attn",
    )(qkv3, qkv3, qkv3, kbias, gq, gk)


def _out_router_kernel(oa_ref, ob_ref, x_ref, wo_ref, g_ref, wr_ref, br_ref,
                       h1_ref, msort_ref, route_ref, cnt_ref, wo_s):
    tm = TM_TOK

    @pl.when(pl.program_id(0) == 0)
    def _():
        wo_s[...] = wo_ref[...].astype(BF16)

    h1 = (x_ref[...]
          + jnp.dot(oa_ref[...], wo_s[0:D_SEC, :], preferred_element_type=F32)
          + jnp.dot(ob_ref[...], wo_s[D_SEC:2 * D_SEC, :], preferred_element_type=F32))
    h1_ref[...] = h1
    ms = jnp.mean(h1 * h1, axis=-1, keepdims=True)
    mn = h1 * lax.rsqrt(ms + EPS) * g_ref[...]

    logits = jnp.dot(mn.astype(BF16), wr_ref[...], preferred_element_type=F32) + br_ref[...]
    lane = lax.broadcasted_iota(jnp.int32, (tm, LANES), 1).astype(F32)
    big = float(LANES)

    def first_argmax(vals):
        vmax = jnp.max(vals, axis=-1, keepdims=True)
        idx = jnp.min(jnp.where(vals == vmax, lane, big), axis=-1, keepdims=True)
        return vmax, idx

    lg = jnp.where(lane < N_GROUPS, logits, -jnp.inf)
    gmax, gidx = first_argmax(lg)
    gw = 1.0 / jnp.sum(jnp.exp(lg - gmax), axis=-1, keepdims=True)
    lo = ROUTER_LANE0 + EXPERTS_PER_GROUP * gidx
    le = jnp.where((lane >= lo) & (lane < lo + EXPERTS_PER_GROUP), logits, -jnp.inf)
    v0, i0 = first_argmax(le)
    v1, i1 = first_argmax(jnp.where(lane == i0, -jnp.inf, le))
    ex = jnp.exp(v1 - v0)
    w0 = gw / (1.0 + ex)
    w1 = gw * ex / (1.0 + ex)

    sel0 = lane == i0
    sel1 = lane == i1
    onehot = jnp.where(sel0 | sel1, 1.0, 0.0)
    r = lax.broadcasted_iota(jnp.int32, (tm, tm), 0)
    c = lax.broadcasted_iota(jnp.int32, (tm, tm), 1)
    ltri = (c <= r).astype(BF16)
    incl = jnp.dot(ltri, onehot.astype(BF16), preferred_element_type=F32)
    excl = incl - onehot
    counts = incl[tm - 1:tm, :]
    cnt_ref[0] = jnp.broadcast_to(counts, (SUBLANES, LANES))

    seg_rows = jnp.floor((counts + (SEG_ALIGN - 1.0)) * (1.0 / SEG_ALIGN)) * SEG_ALIGN
    used_rows = jnp.sum(seg_rows)
    seg_rows = jnp.broadcast_to(seg_rows, (tm, LANES))
    lp0 = jnp.sum(jnp.where(lane < i0, seg_rows, 0.0) + jnp.where(sel0, excl, 0.0), axis=-1, keepdims=True)
    lp1 = jnp.sum(jnp.where(lane < i1, seg_rows, 0.0) + jnp.where(sel1, excl, 0.0), axis=-1, keepdims=True)

    rec = jnp.zeros((tm, LANES), F32)
    for pos, val in enumerate((i0 - ROUTER_LANE0, i1 - ROUTER_LANE0, lp0, lp1, w0, w1)):
        rec = jnp.where(lane == float(pos), val, rec)
    route_ref[...] = rec[:, 0:SUBLANES]

    rec_t = rec.T
    out_row = lax.broadcasted_iota(jnp.int32, (SORT_ROWS, tm), 0).astype(F32)
    pick = jnp.where((out_row == rec_t[2:3, :]) | (out_row == rec_t[3:4, :]), 1.0, 0.0).astype(BF16)
    mn_bf = mn.astype(BF16)
    head_rows = SORT_ROWS - SORT_TAIL
    msort_ref[0:head_rows, :] = jnp.dot(pick[:head_rows], mn_bf, preferred_element_type=F32).astype(BF16)

    @pl.when(used_rows > head_rows)
    def _():
        msort_ref[head_rows:, :] = jnp.dot(pick[head_rows:], mn_bf, preferred_element_type=F32).astype(BF16)

    @pl.when(used_rows <= head_rows)
    def _():
        msort_ref[head_rows:, :] = jnp.zeros((SORT_TAIL, D_MODEL), BF16)


def _out_router(oa, ob, x2, w_o, g_ffn, w_r, b_r):
    T = x2.shape[0]
    tm = TM_TOK
    row = lambda w: pl.BlockSpec((tm, w), lambda i: (i, 0))
    const = lambda a, b: pl.BlockSpec((a, b), lambda i: (0, 0))
    return pl.pallas_call(
        _out_router_kernel,
        grid=(T // tm,),
        in_specs=[row(D_SEC), row(D_SEC), row(D_MODEL), _resident((D_MODEL, D_MODEL)), const(1, D_MODEL),
                  const(D_MODEL, LANES), const(1, LANES)],
        out_specs=[row(D_MODEL), pl.BlockSpec((SORT_ROWS, D_MODEL), lambda i: (i, 0)), row(SUBLANES),
                   pl.BlockSpec((1, SUBLANES, LANES), lambda i: (i, 0, 0))],
        out_shape=[
            jax.ShapeDtypeStruct((T, D_MODEL), F32),
            jax.ShapeDtypeStruct((T // tm * SORT_ROWS, D_MODEL), BF16),
            jax.ShapeDtypeStruct((T, SUBLANES), F32),
            jax.ShapeDtypeStruct((T // tm, SUBLANES, LANES), F32),
        ],
        scratch_shapes=[pltpu.VMEM((D_MODEL, D_MODEL), BF16)],
        compiler_params=_cparams("arbitrary"),
        name="out_router",
    )(oa, ob, x2, w_o, g_ffn, w_r, b_r)


def _chunk_row(c):
    return pl.multiple_of(c * SEG_ALIGN, SEG_ALIGN)


MOE_CHUNKS = TM_MOE // SEG_ALIGN


def _moe_kernel(tile_e_ref, nused_ref, src_ref, next_e_ref, slot_e_ref, msort_ref, wup_ref, wdn_ref, ys_ref,
                x_s, wup_f, wdn_f, wup_s, wdn_s, xsem, wsem):
    i = pl.program_id(0)
    nused = nused_ref[0]

    def gather(k, act):
        def chunk(c, carry):
            src = pl.multiple_of(src_ref[k * MOE_CHUNKS + c], SEG_ALIGN)
            act(pltpu.make_async_copy(msort_ref.at[pl.ds(src, SEG_ALIGN)],
                                      x_s.at[k % 2, pl.ds(_chunk_row(c), SEG_ALIGN)], xsem.at[k % 2]))
            return carry
        lax.fori_loop(0, MOE_CHUNKS, chunk, 0, unroll=4)

    def weights(e, s, act):
        act(pltpu.make_async_copy(wup_ref.at[e], wup_f.at[s], wsem.at[0, s]))
        act(pltpu.make_async_copy(wdn_ref.at[e], wdn_f.at[s], wsem.at[1, s]))

    @pl.when(i == 0)
    def _():
        gather(0, lambda c: c.start())
        weights(tile_e_ref[0], 0, lambda c: c.start())

    @pl.when(i + 1 < nused)
    def _():
        gather(i + 1, lambda c: c.start())

    @pl.when(i < nused)
    def _():
        e = tile_e_ref[i]

        @pl.when((i == 0) | (tile_e_ref[jnp.maximum(i - 1, 0)] != e))
        def _():
            s = slot_e_ref[e]
            weights(e, s, lambda c: c.wait())

            @pl.when(next_e_ref[e] >= 0)
            def _():
                weights(next_e_ref[e], 1 - s, lambda c: c.start())
            wup_s[...] = wup_f[s].astype(BF16)
            wdn_s[...] = wdn_f[s].astype(BF16)

        gather(i, lambda c: c.wait())
        hu = jnp.dot(x_s[i % 2], wup_s[...], preferred_element_type=F32)
        gate = hu[:, :D_EXPERT]
        hid = gate * (1.0 / (1.0 + jnp.exp(-gate))) * hu[:, D_EXPERT:]
        ys_ref[...] = jnp.dot(hid.astype(BF16), wdn_s[...], preferred_element_type=F32).astype(BF16)

    @pl.when(i >= nused)
    def _():
        ys_ref[...] = jnp.zeros_like(ys_ref)


def _moe_experts(tile_e, nused, src_chunk, next_e, slot_e, msort, w_up, w_down):
    n_tiles = tile_e.shape[0]
    grid_spec = pltpu.PrefetchScalarGridSpec(
        num_scalar_prefetch=5,
        grid=(n_tiles,),
        in_specs=[pl.BlockSpec(memory_space=pl.ANY)] * 3,
        out_specs=pl.BlockSpec((TM_MOE, D_MODEL), lambda i, *_: (i, 0)),
        scratch_shapes=[
            pltpu.VMEM((2, TM_MOE, D_MODEL), BF16),
            pltpu.VMEM((2, D_MODEL, 2 * D_EXPERT), F32), pltpu.VMEM((2, D_EXPERT, D_MODEL), F32),
            pltpu.VMEM((D_MODEL, 2 * D_EXPERT), BF16), pltpu.VMEM((D_EXPERT, D_MODEL), BF16),
            pltpu.SemaphoreType.DMA((2,)), pltpu.SemaphoreType.DMA((2, 2)),
        ],
    )
    return pl.pallas_call(
        _moe_kernel,
        grid_spec=grid_spec,
        out_shape=jax.ShapeDtypeStruct((n_tiles * TM_MOE, D_MODEL), BF16),
        compiler_params=_cparams("arbitrary"),
        name="moe_experts",
    )(tile_e, nused, src_chunk, next_e, slot_e, msort, w_up, w_down)


def _combine_kernel(nchunk_ref, src_ref, ys_ref, h1_ref, route_ref, p_ref,
                    g_ref, wg_ref, wp_ref, o_ref, gath_s, sems, wg_s, wp_s):
    i = pl.program_id(0)
    n_steps = pl.num_programs(0)

    @pl.when(i == 0)
    def _():
        wg_s[...] = wg_ref[...].astype(BF16)
        wp_s[...] = wp_ref[...].astype(BF16)
    tm = TM_TOK
    slot = i % 2

    def gather(t, s, act):
        def group(g, carry):
            for k in range(GATHER_UNROLL):
                c = g * GATHER_UNROLL + k
                src = pl.multiple_of(src_ref[t * SORT_CHUNKS + c], SEG_ALIGN)
                act(pltpu.make_async_copy(ys_ref.at[pl.ds(src, SEG_ALIGN)],
                                          gath_s.at[s, pl.ds(_chunk_row(c), SEG_ALIGN)], sems.at[s]))
            return carry
        lax.fori_loop(0, (nchunk_ref[t] + GATHER_UNROLL - 1) // GATHER_UNROLL, group, 0)

    @pl.when(i == 0)
    def _():
        gath_s[...] = jnp.zeros_like(gath_s)
        gather(0, 0, lambda c: c.start())

    @pl.when(i + 1 < n_steps)
    def _():
        gather(i + 1, 1 - slot, lambda c: c.start())

    gather(i, slot, lambda c: c.wait())

    route = route_ref[...]
    col = lax.broadcasted_iota(jnp.int32, (tm, SORT_ROWS), 1).astype(F32)
    weights = (jnp.where(col == route[:, 2:3], route[:, 4:5], 0.0)
               + jnp.where(col == route[:, 3:4], route[:, 5:6], 0.0)).astype(BF16)
    y = jnp.dot(weights, gath_s[slot], preferred_element_type=F32)
    h2 = h1_ref[...] + y
    ms = jnp.mean(h2 * h2, axis=-1, keepdims=True)
    n = (h2 * lax.rsqrt(ms + EPS) * g_ref[...]).astype(BF16)
    z = jnp.dot(n, wg_s[...], preferred_element_type=F32)
    gate = 1.0 / (1.0 + jnp.exp(-z))
    ple = jnp.dot(p_ref[...].astype(BF16), wp_s[...], preferred_element_type=F32)
    o_ref[...] = h2 + gate * ple


def _combine_ple(n_chunks, chunk_dst, ys, h1, route, p2, g_ple, w_gate, w_proj):
    T = h1.shape[0]
    tm = TM_TOK
    row = lambda w: pl.BlockSpec((tm, w), lambda i, *_: (i, 0))
    const = lambda a, b: pl.BlockSpec((a, b), lambda i, *_: (0, 0))
    grid_spec = pltpu.PrefetchScalarGridSpec(
        num_scalar_prefetch=2,
        grid=(T // tm,),
        in_specs=[pl.BlockSpec(memory_space=pl.ANY), row(D_MODEL), row(SUBLANES), row(PLE_DIM),
                  const(1, D_MODEL), _resident((D_MODEL, D_MODEL)), _resident((PLE_DIM, D_MODEL))],
        out_specs=row(D_MODEL),
        scratch_shapes=[pltpu.VMEM((2, SORT_ROWS, D_MODEL), BF16), pltpu.SemaphoreType.DMA((2,)),
                        pltpu.VMEM((D_MODEL, D_MODEL), BF16), pltpu.VMEM((PLE_DIM, D_MODEL), BF16)],
    )
    return pl.pallas_call(
        _combine_kernel,
        grid_spec=grid_spec,
        out_shape=jax.ShapeDtypeStruct((T, D_MODEL), F32),
        compiler_params=_cparams("arbitrary"),
        name="combine_ple",
    )(n_chunks, chunk_dst, ys, h1, route, p2, g_ple, w_gate, w_proj)


def _rope_tables(positions):
    half = ROT_DIM // 2
    inv = ROPE_THETA ** (-jnp.arange(0, ROT_DIM, 2, dtype=F32) / ROT_DIM)
    ang = positions.astype(F32)[:, None, :] * inv[None, :, None]
    lane = jnp.arange(LANES) % HEAD_DIM
    freq = jnp.arange(half)[:, None]
    rot = (lane[None, :] < ROT_DIM) & (lane[None, :] % half == freq)
    e_cos = rot.astype(F32)
    e_sin = jnp.where(rot, jnp.where(lane[None, :] < half, -1.0, 1.0), 0.0).astype(F32)
    expand = functools.partial(jnp.einsum, "bfs,fl->bsl", precision=lax.Precision.HIGHEST)
    return expand(jnp.cos(ang), e_cos) + (lane >= ROT_DIM).astype(F32), expand(jnp.sin(ang), e_sin)


def _layer(i, h, p, cos_t, sin_t, g_mix, w_in, b_f, qn_a, kn_a, qn_b, kn_b, w_o, g_ffn, w_rg, b_rg,
           w_re, b_re, w_up, w_down, g_ple, w_ple_gate, w_ple_proj):
    B, S, _ = h.shape
    T = B * S
    x2 = h.reshape(T, D_MODEL)

    bf = jnp.zeros((1, LANES), F32).at[0, :N_HEADS].set(b_f[i])
    pair = lambda g: jnp.tile(g, 2).reshape(1, LANES)
    w_r = jnp.zeros((D_MODEL, LANES), F32)
    w_r = w_r.at[:, :N_GROUPS].set(w_rg[i])
    w_r = w_r.at[:, ROUTER_LANE0:ROUTER_LANE0 + N_EXPERTS].set(
        jnp.transpose(w_re[i], (1, 0, 2)).reshape(D_MODEL, N_EXPERTS))
    b_r = jnp.zeros((1, LANES), F32).at[0, :N_GROUPS].set(b_rg[i])
    b_r = b_r.at[0, ROUTER_LANE0:ROUTER_LANE0 + N_EXPERTS].set(b_re[i].reshape(-1))

    qkv, f = _in_proj(x2, g_mix[i].reshape(1, -1), w_in, i)
    kbias = _forget_scan(f, bf, B, S)
    qkv3 = qkv.reshape(B, S, N_QKV)
    oa = _dilated_attn(qkv3, cos_t, sin_t, pair(qn_a[i]), pair(kn_a[i]))
    ob = _fox_attn(qkv3, kbias, pair(qn_b[i]), pair(kn_b[i]))

    h1, msort, route, cnt = _out_router(oa.reshape(T, D_SEC), ob.reshape(T, D_SEC), x2, w_o[i],
                                        g_ffn[i].reshape(1, -1), w_r.astype(BF16), b_r)

    n_tok_tiles = T // TM_TOK
    counts = cnt[:, 0, ROUTER_LANE0:ROUTER_LANE0 + N_EXPERTS].astype(jnp.int32)
    seg_rows = (counts + SEG_ALIGN - 1) // SEG_ALIGN * SEG_ALIGN
    seg_local = jnp.cumsum(seg_rows, axis=1) - seg_rows
    rows_e = jnp.sum(seg_rows, axis=0)
    tile_end = jnp.cumsum((rows_e + TM_MOE - 1) // TM_MOE)
    off = jnp.concatenate([jnp.zeros((1,), jnp.int32), tile_end * TM_MOE]).astype(jnp.int32)
    seg_global = off[None, :N_EXPERTS] + jnp.cumsum(seg_rows, axis=0) - seg_rows
    n_tiles = (2 * T + n_tok_tiles * N_EXPERTS * (SEG_ALIGN - 1)) // TM_MOE + N_EXPERTS
    nused = tile_end[-1:].astype(jnp.int32)
    tile_ids = jnp.minimum(jnp.arange(n_tiles, dtype=jnp.int32), nused[0] - 1)
    tile_e = jnp.sum((tile_ids[:, None] >= tile_end[None, :]).astype(jnp.int32), axis=1).astype(jnp.int32)
    chunk_row = jnp.arange(SORT_CHUNKS, dtype=jnp.int32)[None, :, None] * SEG_ALIGN
    lo, hi = seg_local[:, None, :], (seg_local + seg_rows)[:, None, :]
    chunk_dst = jnp.sum(jnp.where((chunk_row >= lo) & (chunk_row < hi), seg_global[:, None, :] + chunk_row - lo, 0),
                        axis=2).reshape(-1).astype(jnp.int32)
    n_chunks = (jnp.sum(seg_rows, axis=1) // SEG_ALIGN).astype(jnp.int32)
    sorted_row = jnp.arange(n_tiles * MOE_CHUNKS, dtype=jnp.int32)[:, None] * SEG_ALIGN
    seg_lo = seg_global.reshape(1, -1)
    seg_src = (jnp.arange(n_tok_tiles, dtype=jnp.int32)[:, None] * SORT_ROWS + seg_local).reshape(1, -1)
    in_seg = (sorted_row >= seg_lo) & (sorted_row < seg_lo + seg_rows.reshape(1, -1))
    src_chunk = jnp.where(jnp.any(in_seg, axis=1),
                          jnp.sum(jnp.where(in_seg, seg_src + sorted_row - seg_lo, 0), axis=1),
                          SORT_ROWS - SEG_ALIGN).astype(jnp.int32)
    e_ids = jnp.arange(N_EXPERTS, dtype=jnp.int32)
    active = rows_e > 0
    later_active = jnp.where(active[None, :] & (e_ids[None, :] > e_ids[:, None]), e_ids[None, :], N_EXPERTS)
    next_e = jnp.min(later_active, axis=1)
    next_e = jnp.where(next_e == N_EXPERTS, -1, next_e).astype(jnp.int32)
    slot_e = ((jnp.cumsum(active.astype(jnp.int32)) - 1) % 2).astype(jnp.int32)

    ys = _moe_experts(tile_e, nused, src_chunk, next_e, slot_e, msort, w_up[i], w_down[i])
    out = _combine_ple(n_chunks, chunk_dst, ys, h1, route, p[i].reshape(T, PLE_DIM),
                       g_ple[i].reshape(1, -1), w_ple_gate[i], w_ple_proj[i])
    return out.reshape(B, S, D_MODEL)


def kernel(x, p, positions, g_mix, w_in, b_f, qn_a, kn_a, qn_b, kn_b, w_o, g_ffn, w_rg, b_rg, w_re, b_re,
           w_up, w_down, g_ple, w_ple_gate, w_ple_proj):
    cos_t, sin_t = _rope_tables(positions)
    h = x
    for i in range(p.shape[0]):
        h = _layer(i, h, p, cos_t, sin_t, g_mix, w_in, b_f, qn_a, kn_a, qn_b, kn_b, w_o, g_ffn, w_rg, b_rg,
                   w_re, b_re, w_up, w_down, g_ple, w_ple_gate, w_ple_proj)
    return h
```

```python
import functools
import math

import jax
import jax.numpy as jnp
from jax import lax
from jax.experimental import pallas as pl
from jax.experimental.pallas import tpu as pltpu

D_MODEL = 1024
HEAD_DIM = 64
N_HEADS = 8
D_SEC = N_HEADS * HEAD_DIM
N_QKV = 6 * D_SEC
ROT_DIM = HEAD_DIM // 4
ROPE_THETA = 500000.0
N_GROUPS = 4
EXPERTS_PER_GROUP = 8
N_EXPERTS = N_GROUPS * EXPERTS_PER_GROUP
D_EXPERT = 512
PLE_DIM = 256
EPS = 1e-6
NEG = -1e30
WINDOW = 128

LANES = 128
SUBLANES = 8
VMEM_LIMIT = 48 * 1024 * 1024

TM_PROJ = 512
TQ = 128
TK = 256
N_PAIRS = N_HEADS // 2
TM_MOE = 512
TM_TOK = 512
SEG_ALIGN = 2 * SUBLANES
SORT_ROWS = 2 * TM_TOK + N_EXPERTS * SEG_ALIGN
SORT_CHUNKS = SORT_ROWS // SEG_ALIGN
SORT_TAIL = 256
ROUTER_LANE0 = N_GROUPS

Q_SCALE_LOG2 = math.log2(math.e) / math.sqrt(HEAD_DIM)

F32 = jnp.float32
BF16 = jnp.bfloat16
NT_DIMS = (((1,), (1,)), ((), ()))


def _cparams(*sem):
    return pltpu.CompilerParams(dimension_semantics=sem, vmem_limit_bytes=VMEM_LIMIT)


def _resident(shape):
    return pl.BlockSpec(shape, lambda i, *_: (0,) * len(shape), pipeline_mode=pl.Buffered(1))


def _in_proj_kernel(x_ref, g_ref, w_ref, qkv_ref, f_ref, w_s, wf_s):
    @pl.when(pl.program_id(0) == 0)
    def _():
        w_s[...] = w_ref[0, :, 0:N_QKV].astype(BF16)
        wf_s[...] = jnp.zeros_like(wf_s)
        wf_s[:, 0:N_HEADS] = w_ref[0, :, N_QKV:N_QKV + N_HEADS].astype(BF16)

    x = x_ref[...]
    ms = jnp.mean(x * x, axis=-1, keepdims=True)
    a = (x * lax.rsqrt(ms + EPS) * g_ref[...]).astype(BF16)
    qkv_ref[...] = jnp.dot(a, w_s[...], preferred_element_type=F32).astype(BF16)
    f_ref[...] = jnp.dot(a, wf_s[...], preferred_element_type=F32)


def _in_proj(x2, g_mix, w_in, layer):
    T = x2.shape[0]
    w_block = (1,) + w_in.shape[1:]
    return pl.pallas_call(
        _in_proj_kernel,
        grid=(T // TM_PROJ,),
        in_specs=[
            pl.BlockSpec((TM_PROJ, D_MODEL), lambda i: (i, 0)),
            pl.BlockSpec((1, D_MODEL), lambda i: (0, 0)),
            pl.BlockSpec(w_block, lambda i: (layer, 0, 0), pipeline_mode=pl.Buffered(1)),
        ],
        out_specs=[
            pl.BlockSpec((TM_PROJ, N_QKV), lambda i: (i, 0)),
            pl.BlockSpec((TM_PROJ, LANES), lambda i: (i, 0)),
        ],
        out_shape=[
            jax.ShapeDtypeStruct((T, N_QKV), BF16),
            jax.ShapeDtypeStruct((T, LANES), F32),
        ],
        scratch_shapes=[pltpu.VMEM((D_MODEL, N_QKV), BF16), pltpu.VMEM((D_MODEL, LANES), BF16)],
        compiler_params=_cparams("arbitrary"),
        name="in_proj",
    )(x2, g_mix, w_in)


BIAS_TERMS = 3


def _bias_lane(head):
    return (head // 2) * LANES + (HEAD_DIM if head % 2 == 0 else 0)


def _forget_scan_kernel(f_ref, bf_ref, kb_ref, *, seq):
    f = f_ref[...] + bf_ref[...]
    logf = jnp.minimum(f, 0.0) - jnp.log1p(jnp.exp(-jnp.abs(f)))

    def split_bf16(v):
        terms = []
        for _ in range(BIAS_TERMS):
            t = v.astype(BF16)
            terms.append(t)
            v = v - t.astype(F32)
        return terms

    tri = (lax.broadcasted_iota(jnp.int32, (TK, TK), 1) <= lax.broadcasted_iota(jnp.int32, (TK, TK), 0)).astype(BF16)
    carry = jnp.zeros((1, LANES), F32)
    chunks = []
    for j in range(seq // TK):
        terms = jnp.concatenate(split_bf16(logf[j * TK:(j + 1) * TK, :]), axis=1)
        pre = jnp.dot(tri, terms, preferred_element_type=F32)
        cj = carry + sum(pre[:, t * LANES:(t + 1) * LANES] for t in range(BIAS_TERMS))
        carry = cj[TK - 1:TK, :]
        chunks.append(cj)
    c = jnp.concatenate(chunks, axis=0)
    rest = c * (-math.log2(math.e))
    r_idx = lax.broadcasted_iota(jnp.int32, (LANES, D_SEC), 0)
    c_idx = lax.broadcasted_iota(jnp.int32, (LANES, D_SEC), 1)
    base = (r_idx // 2) * LANES + jnp.where(r_idx % 2 == 0, HEAD_DIM, 0)
    out = jnp.zeros((seq, D_SEC), F32)
    for t, term in enumerate(split_bf16(rest)):
        place = ((r_idx < N_HEADS) & (c_idx == base + t)).astype(BF16)
        out = out + jnp.dot(term, place, preferred_element_type=F32)
    kb_ref[0] = out.astype(BF16)


def _forget_scan(f, bf, batch, seq):
    return pl.pallas_call(
        functools.partial(_forget_scan_kernel, seq=seq),
        grid=(batch,),
        in_specs=[
            pl.BlockSpec((seq, LANES), lambda b: (b, 0)),
            pl.BlockSpec((1, LANES), lambda b: (0, 0)),
        ],
        out_specs=pl.BlockSpec((1, seq, D_SEC), lambda b: (b, 0, 0)),
        out_shape=jax.ShapeDtypeStruct((batch, seq, D_SEC), BF16),
        compiler_params=_cparams("parallel"),
        name="forget_scan",
    )(f, bf)


def _head_sumsq_matrix():
    r = lax.broadcasted_iota(jnp.int32, (LANES, LANES), 0) // HEAD_DIM
    c = lax.broadcasted_iota(jnp.int32, (LANES, LANES), 1) // HEAD_DIM
    return (r == c).astype(BF16)


def _qk_norm(x, gain, gmat):
    ss = jnp.dot((x * x).astype(BF16), gmat, preferred_element_type=F32)
    return x * lax.rsqrt(ss * (1.0 / HEAD_DIM) + EPS) * gain


def _stack_heads(qb, head0):
    zero = jnp.zeros_like(qb)
    return jnp.concatenate([jnp.where(head0, qb, zero), jnp.where(head0, zero, qb)], axis=0)


def _unstack(a, head0):
    return jnp.where(head0, a[:TQ], a[TQ:])


def _dilated_kernel(q_ref, k_ref, v_ref, cos_ref, sin_ref, gq_ref, gk_ref, o_ref,
                    qn_s, kn_s, v_s, acc_s, m_s, l_s, bias_s, *, seq):
    gmat = _head_sumsq_matrix()
    lane = lax.broadcasted_iota(jnp.int32, (TQ, LANES), 1)
    head0 = lane < HEAD_DIM
    chunk = 512
    lane_c = lax.broadcasted_iota(jnp.int32, (chunk, LANES), 1) % HEAD_DIM
    first_half = lane_c < ROT_DIM // 2

    def rope(t, cs, sn):
        partner = jnp.where(first_half, pltpu.roll(t, LANES - ROT_DIM // 2, axis=1),
                            pltpu.roll(t, ROT_DIM // 2, axis=1))
        return t * cs + partner * sn

    for c0 in range(0, seq, chunk):
        rows = pl.ds(c0, chunk)
        cs = cos_ref[0, rows, :]
        sn = sin_ref[0, rows, :]
        qn = _qk_norm(q_ref[0, rows, :].astype(F32), gq_ref[...], gmat)
        qn_s[rows, :] = rope(qn, cs, sn) * Q_SCALE_LOG2
        kn = _qk_norm(k_ref[0, rows, :].astype(F32), gk_ref[...], gmat)
        kn_s[rows, :] = rope(kn, cs, sn)
        v_s[rows, :] = v_ref[0, rows, :].astype(F32)

    u = lax.broadcasted_iota(jnp.int32, (2 * TQ, 2 * TQ), 0) % TQ
    c = lax.broadcasted_iota(jnp.int32, (2 * TQ, 2 * TQ), 1)
    for slot, dist0 in enumerate((0, TQ)):
        dist = dist0 + u - c
        bias_s[slot] = jnp.where((dist >= 0) & (dist <= WINDOW), 0.0, NEG)

    def scores(blk):
        p, q_rows, k_rows, nk, dist0 = blk
        qst = _stack_heads(qn_s[q_rows, :].astype(BF16), head0)
        s = lax.dot_general(qst, kn_s[k_rows, :].astype(BF16), NT_DIMS, preferred_element_type=F32)
        return s + bias_s[dist0 // TQ, :, 0:nk]

    def finish(blk, s):
        p, q_rows, k_rows, nk, dist0 = blk
        m = jnp.max(s, axis=-1, keepdims=True)
        pr = jnp.exp2(s - m)
        l = jnp.sum(pr, axis=-1, keepdims=True)
        acc = jnp.dot(pr.astype(BF16), v_s[k_rows, :].astype(BF16), preferred_element_type=F32)
        acc_s[p, q_rows, :] = _unstack(acc, head0)
        m_s[p, q_rows, :] = _unstack(jnp.broadcast_to(m, (2 * TQ, LANES)), head0)
        l_s[p, q_rows, :] = _unstack(jnp.broadcast_to(l, (2 * TQ, LANES)), head0)

    blocks = []
    for i in range(seq // TQ):
        kb0 = max(i - 1, 0)
        blocks.append((0, pl.ds(i * TQ, TQ), pl.ds(kb0 * TQ, 2 * TQ), 2 * TQ, (i - kb0) * TQ))
    for r in range(4):
        for n in range(seq // (4 * TQ)):
            kb0 = max(n - 1, 0)
            blocks.append((1, pl.ds(4 * TQ * n + r, TQ, stride=4), pl.ds(4 * TQ * kb0 + r, 2 * TQ, stride=4),
                           2 * TQ, (n - kb0) * TQ))
    for r in range(16):
        rows = pl.ds(r, TQ, stride=16)
        blocks.append((2, rows, rows, TQ, 0))

    ahead = 4
    pending = [scores(b) for b in blocks[:ahead]]
    for idx, b in enumerate(blocks):
        s = pending.pop(0)
        if idx + ahead < len(blocks):
            pending.append(scores(blocks[idx + ahead]))
        finish(b, s)

    for c0 in range(0, seq, chunk):
        rows = pl.ds(c0, chunk)
        m0, m1, m2 = m_s[0, rows, :], m_s[1, rows, :], m_s[2, rows, :]
        mm = jnp.maximum(jnp.maximum(m0, m1), m2)
        e0, e1, e2 = jnp.exp2(m0 - mm), jnp.exp2(m1 - mm), jnp.exp2(m2 - mm)
        num = acc_s[0, rows, :] * e0 + acc_s[1, rows, :] * e1 + acc_s[2, rows, :] * e2
        den = l_s[0, rows, :] * e0 + l_s[1, rows, :] * e1 + l_s[2, rows, :] * e2
        o_ref[0, rows, :] = (num / den).astype(BF16)


def _dilated_attn(qkv3, cos_t, sin_t, gq, gk):
    batch, seq, _ = qkv3.shape
    sec = D_SEC // LANES
    blk = lambda s: pl.BlockSpec((1, seq, LANES), lambda b, h, s=s: (b, 0, s * sec + h))
    tab = pl.BlockSpec((1, seq, LANES), lambda b, h: (b, 0, 0))
    vec = pl.BlockSpec((1, LANES), lambda b, h: (0, 0))
    return pl.pallas_call(
        functools.partial(_dilated_kernel, seq=seq),
        grid=(batch, N_PAIRS),
        in_specs=[blk(0), blk(1), blk(2), tab, tab, vec, vec],
        out_specs=pl.BlockSpec((1, seq, LANES), lambda b, h: (b, 0, h)),
        out_shape=jax.ShapeDtypeStruct((batch, seq, D_SEC), BF16),
        scratch_shapes=[
            pltpu.VMEM((seq, LANES), F32), pltpu.VMEM((seq, LANES), F32), pltpu.VMEM((seq, LANES), F32),
            pltpu.VMEM((3, seq, LANES), F32), pltpu.VMEM((3, seq, LANES), F32),
            pltpu.VMEM((3, seq, LANES), F32), pltpu.VMEM((2, 2 * TQ, 2 * TQ), F32),
        ],
        compiler_params=_cparams("parallel", "parallel"),
        name="dilated_attn",
    )(qkv3, qkv3, qkv3, cos_t, sin_t, gq, gk)


def _fox_kernel(q_ref, k_ref, v_ref, kb_ref, gq_ref, gk_ref, o_ref, qat_s, ka_s, vt_s, acc_s, tri_s, *, seq):
    gmat = _head_sumsq_matrix()
    chunk = 512
    lane = lax.broadcasted_iota(jnp.int32, (chunk, LANES), 1)
    low = lane < HEAD_DIM
    sel = jnp.where((lane >= HEAD_DIM) & (lane < HEAD_DIM + BIAS_TERMS), 1.0, 0.0)

    for c0 in range(0, seq, chunk):
        rows = pl.ds(c0, chunk)
        for g in range(N_PAIRS):
            lanes = slice(g * LANES, (g + 1) * LANES)
            qn = _qk_norm(q_ref[0, rows, lanes].astype(F32), gq_ref[...], gmat) * Q_SCALE_LOG2
            for h, qa in ((2 * g, jnp.where(low, qn, sel)),
                          (2 * g + 1, jnp.where(low, pltpu.roll(qn, HEAD_DIM, axis=1), sel))):
                qt = qa.T
                for cc in range(chunk // TK):
                    qat_s[c0 // TK + cc, h * LANES:(h + 1) * LANES, :] = qt[:, cc * TK:(cc + 1) * TK].astype(BF16)
            kn =_qk_norm(k_ref[0, rows, lanes].astype(F32), gk_ref[...], gmat)
            kb = kb_ref[0, rows, lanes].astype(F32)
            ka_s[2 * g, rows, :] = jnp.where(low, kn, kb).astype(BF16)
            ka_s[2 * g + 1, rows, :] = pltpu.roll(jnp.where(low, kb, kn), HEAD_DIM, axis=1).astype(BF16)
            vt = v_ref[0, rows, lanes].astype(F32).T
            for cc in range(chunk // TK):
                vt_s[c0 // TK + cc, lanes, :] = vt[:, cc * TK:(cc + 1) * TK].astype(BF16)

    r = lax.broadcasted_iota(jnp.int32, (TK, TK), 0)
    c = lax.broadcasted_iota(jnp.int32, (TK, TK), 1)
    tri_s[...] = jnp.where(r <= c, 0.0, NEG)

    def q_block(qi, carry):
        q_rows = pl.ds(pl.multiple_of(qi * TK, TK), TK)
        for h in range(N_HEADS):
            acc_s[h] = jnp.zeros((HEAD_DIM, TK), F32)

        def step(j, st, diagonal):
            k_rows = pl.ds(pl.multiple_of(j * TK, TK), TK)

            def scores(h):
                s = jnp.dot(ka_s[h, k_rows, :], qat_s[qi, h * LANES:(h + 1) * LANES, :], preferred_element_type=F32)
                return s + tri_s[...] if diagonal else s

            ahead = 6
            pending = [scores(h) for h in range(ahead)]
            new = []
            for h in range(N_HEADS):
                m, l = st[h]
                s = pending.pop(0)
                if h + ahead < N_HEADS:
                    pending.append(scores(h + ahead))
                m_new = jnp.maximum(m, jnp.max(s, axis=0, keepdims=True))
                alpha = jnp.exp2(m - m_new)
                pr = jnp.exp2(s - m_new)
                l_new = alpha * l + jnp.sum(pr, axis=0, keepdims=True)
                pv = jnp.dot(vt_s[j, h * HEAD_DIM:(h + 1) * HEAD_DIM, :], pr.astype(BF16),
                             preferred_element_type=F32)
                acc_s[h] = alpha * acc_s[h] + pv
                new.append((m_new, l_new))
            return tuple(new)

        init = tuple((jnp.full((1, TK), NEG, F32), jnp.zeros((1, TK), F32)) for _ in range(N_HEADS))
        st = lax.fori_loop(0, qi // 2, lambda jj, st: step(2 * jj + 1, step(2 * jj, st, False), False), init)
        st = lax.cond(qi % 2 == 1, lambda st: step(qi - 1, st, False), lambda st: st, st)
        st = step(qi, st, True)
        for g in range(N_PAIRS):
            o2 = jnp.concatenate([acc_s[2 * g] / st[2 * g][1], acc_s[2 * g + 1] / st[2 * g + 1][1]], axis=0)
            o_ref[0, q_rows, g * LANES:(g + 1) * LANES] = o2.T.astype(BF16)
        return carry

    lax.fori_loop(0, seq // TK, q_block, 0)


def _fox_attn(qkv3, kbias, gq, gk):
    batch, seq, _ = qkv3.shape
    blk = lambda s: pl.BlockSpec((1, seq, D_SEC), lambda b, s=s: (b, 0, s))
    vec = pl.BlockSpec((1, LANES), lambda b: (0, 0))
    return pl.pallas_call(
        functools.partial(_fox_kernel, seq=seq),
        grid=(batch,),
        in_specs=[blk(3), blk(4), blk(5), pl.BlockSpec((1, seq, D_SEC), lambda b: (b, 0, 0)), vec, vec],
        out_specs=pl.BlockSpec((1, seq, D_SEC), lambda b: (b, 0, 0)),
        out_shape=jax.ShapeDtypeStruct((batch, seq, D_SEC), BF16),
        scratch_shapes=[
            pltpu.VMEM((seq // TK, N_HEADS * LANES, TK), BF16),
            pltpu.VMEM((N_HEADS, seq, LANES), BF16),
            pltpu.VMEM((seq // TK, D_SEC, TK), BF16),
            pltpu.VMEM((N_HEADS, HEAD_DIM, TK), F32),
            pltpu.VMEM((TK, TK), F32),
        ],
        compiler_params=_cparams("parallel"),
        name="fox_attn",
    )(qkv3, qkv3, qkv3, kbias, gq, gk)


def _out_router_kernel(oa_ref, ob_ref, x_ref, wo_ref, g_ref, wr_ref, br_ref,
                       h1_ref, msort_ref, route_ref, cnt_ref, wo_s):
    tm = TM_TOK

    @pl.when(pl.program_id(0) == 0)
    def _():
        wo_s[...] = wo_ref[...].astype(BF16)

    h1 = (x_ref[...]
          + jnp.dot(oa_ref[...], wo_s[0:D_SEC, :], preferred_element_type=F32)
          + jnp.dot(ob_ref[...], wo_s[D_SEC:2 * D_SEC, :], preferred_element_type=F32))
    h1_ref[...] = h1
    ms = jnp.mean(h1 * h1, axis=-1, keepdims=True)
    mn = h1 * lax.rsqrt(ms + EPS) * g_ref[...]

    logits = jnp.dot(mn.astype(BF16), wr_ref[...], preferred_element_type=F32) + br_ref[...]
    lane = lax.broadcasted_iota(jnp.int32, (tm, LANES), 1).astype(F32)
    big = float(LANES)

    def first_argmax(vals):
        vmax = jnp.max(vals, axis=-1, keepdims=True)
        idx = jnp.min(jnp.where(vals == vmax, lane, big), axis=-1, keepdims=True)
        return vmax, idx

    lg = jnp.where(lane < N_GROUPS, logits, -jnp.inf)
    gmax, gidx = first_argmax(lg)
    gw = 1.0 / jnp.sum(jnp.exp(lg - gmax), axis=-1, keepdims=True)
    lo = ROUTER_LANE0 + EXPERTS_PER_GROUP * gidx
    le = jnp.where((lane >= lo) & (lane < lo + EXPERTS_PER_GROUP), logits, -jnp.inf)
    v0, i0 = first_argmax(le)
    v1, i1 = first_argmax(jnp.where(lane == i0, -jnp.inf, le))
    ex = jnp.exp(v1 - v0)
    w0 = gw / (1.0 + ex)
    w1 = gw * ex / (1.0 + ex)

    sel0 = lane == i0
    sel1 = lane == i1
    onehot = jnp.where(sel0 | sel1, 1.0, 0.0)
    r = lax.broadcasted_iota(jnp.int32, (tm, tm), 0)
    c = lax.broadcasted_iota(jnp.int32, (tm, tm), 1)
    ltri = (c <= r).astype(BF16)
    incl = jnp.dot(ltri, onehot.astype(BF16), preferred_element_type=F32)
    excl = incl - onehot
    counts = incl[tm - 1:tm, :]
    cnt_ref[0] = jnp.broadcast_to(counts, (SUBLANES, LANES))

    seg_rows = jnp.floor((counts + (SEG_ALIGN - 1.0)) * (1.0 / SEG_ALIGN)) * SEG_ALIGN
    used_rows = jnp.sum(seg_rows)
    seg_rows = jnp.broadcast_to(seg_rows, (tm, LANES))
    lp0 = jnp.sum(jnp.where(lane < i0, seg_rows, 0.0) + jnp.where(sel0, excl, 0.0), axis=-1, keepdims=True)
    lp1 = jnp.sum(jnp.where(lane < i1, seg_rows, 0.0) + jnp.where(sel1, excl, 0.0), axis=-1, keepdims=True)

    rec = jnp.zeros((tm, LANES), F32)
    for pos, val in enumerate((i0 - ROUTER_LANE0, i1 - ROUTER_LANE0, lp0, lp1, w0, w1)):
        rec = jnp.where(lane == float(pos), val, rec)
    route_ref[...] = rec[:, 0:SUBLANES]

    rec_t = rec.T
    out_row = lax.broadcasted_iota(jnp.int32, (SORT_ROWS, tm), 0).astype(F32)
    pick = jnp.where((out_row == rec_t[2:3, :]) | (out_row == rec_t[3:4, :]), 1.0, 0.0).astype(BF16)
    mn_bf = mn.astype(BF16)
    head_rows = SORT_ROWS - SORT_TAIL
    msort_ref[0:head_rows, :] = jnp.dot(pick[:head_rows], mn_bf, preferred_element_type=F32).astype(BF16)

    @pl.when(used_rows > head_rows)
    def _():
        msort_ref[head_rows:, :] = jnp.dot(pick[head_rows:], mn_bf, preferred_element_type=F32).astype(BF16)

    @pl.when(used_rows <= head_rows)
    def _():
        msort_ref[head_rows:, :] = jnp.zeros((SORT_TAIL, D_MODEL), BF16)


def _out_router(oa, ob, x2, w_o, g_ffn, w_r, b_r):
    T = x2.shape[0]
    tm = TM_TOK
    row = lambda w: pl.BlockSpec((tm, w), lambda i: (i, 0))
    const = lambda a, b: pl.BlockSpec((a, b), lambda i: (0, 0))
    return pl.pallas_call(
        _out_router_kernel,
        grid=(T // tm,),
        in_specs=[row(D_SEC), row(D_SEC), row(D_MODEL), _resident((D_MODEL, D_MODEL)), const(1, D_MODEL),
                  const(D_MODEL, LANES), const(1, LANES)],
        out_specs=[row(D_MODEL), pl.BlockSpec((SORT_ROWS, D_MODEL), lambda i: (i, 0)), row(SUBLANES),
                   pl.BlockSpec((1, SUBLANES, LANES), lambda i: (i, 0, 0))],
        out_shape=[
            jax.ShapeDtypeStruct((T, D_MODEL), F32),
            jax.ShapeDtypeStruct((T // tm * SORT_ROWS, D_MODEL), BF16),
            jax.ShapeDtypeStruct((T, SUBLANES), F32),
            jax.ShapeDtypeStruct((T // tm, SUBLANES, LANES), F32),
        ],
        scratch_shapes=[pltpu.VMEM((D_MODEL, D_MODEL), BF16)],
        compiler_params=_cparams("arbitrary"),
        name="out_router",
    )(oa, ob, x2, w_o, g_ffn, w_r, b_r)


MOE_CHUNKS = TM_MOE // SEG_ALIGN


def _chunk_row(c):
    return pl.multiple_of(c * SEG_ALIGN, SEG_ALIGN)


def _moe_kernel(tile_e_ref, nused_ref, src_ref, next_e_ref, slot_e_ref, msort_ref, wup_ref, wdn_ref, ys_ref,
                x_s, wup_f, wdn_f, wup_s, wdn_s, xsem, wsem):
    i = pl.program_id(0)
    nused = nused_ref[0]

    def gather(k, slot, act):
        for c in range(MOE_CHUNKS):
            src = pl.multiple_of(src_ref[k * MOE_CHUNKS + c], SEG_ALIGN)
            act(pltpu.make_async_copy(msort_ref.at[pl.ds(src, SEG_ALIGN)],
                                      x_s.at[slot, pl.ds(c * SEG_ALIGN, SEG_ALIGN)], xsem.at[slot]))

    def weights(e, s, act):
        act(pltpu.make_async_copy(wup_ref.at[e], wup_f.at[s], wsem.at[0, s]))
        act(pltpu.make_async_copy(wdn_ref.at[e], wdn_f.at[s], wsem.at[1, s]))

    @pl.when(i == 0)
    def _():
        gather(0, 0, lambda c: c.start())
        weights(tile_e_ref[0], 0, lambda c: c.start())

    @pl.when(i < nused)
    def _():
        e = tile_e_ref[i]

        @pl.when((i == 0) | (tile_e_ref[jnp.maximum(i - 1, 0)] != e))
        def _():
            s = slot_e_ref[e]
            weights(e, s, lambda c: c.wait())

            @pl.when(next_e_ref[e] >= 0)
            def _():
                weights(next_e_ref[e], 1 - s, lambda c: c.start())
            wup_s[...] = wup_f[s].astype(BF16)
            wdn_s[...] = wdn_f[s].astype(BF16)

        nxt = jnp.minimum(i + 1, nused - 1)
        gather(i, i % 2, lambda c: c.wait())
        gather(nxt, (i + 1) % 2, lambda c: c.start())
        hu = jnp.dot(x_s[i % 2], wup_s[...], preferred_element_type=F32)
        gate = hu[:, :D_EXPERT]
        hid = gate * (1.0 / (1.0 + jnp.exp(-gate))) * hu[:, D_EXPERT:]
        ys_ref[...] = jnp.dot(hid.astype(BF16), wdn_s[...], preferred_element_type=F32).astype(BF16)

        @pl.when(i == nused - 1)
        def _():
            gather(nxt, (i + 1) % 2, lambda c: c.wait())

    @pl.when(i >= nused)
    def _():
        ys_ref[...] = jnp.zeros_like(ys_ref)


def _moe_experts(tile_e, nused, src_chunk, next_e, slot_e, msort, w_up, w_down):
    n_tiles = tile_e.shape[0]
    grid_spec = pltpu.PrefetchScalarGridSpec(
        num_scalar_prefetch=5,
        grid=(n_tiles,),
        in_specs=[pl.BlockSpec(memory_space=pl.ANY)] * 3,
        out_specs=pl.BlockSpec((TM_MOE, D_MODEL), lambda i, *_: (i, 0)),
        scratch_shapes=[
            pltpu.VMEM((2, TM_MOE, D_MODEL), BF16),
            pltpu.VMEM((2, D_MODEL, 2 * D_EXPERT), F32), pltpu.VMEM((2, D_EXPERT, D_MODEL), F32),
            pltpu.VMEM((D_MODEL, 2 * D_EXPERT), BF16), pltpu.VMEM((D_EXPERT, D_MODEL), BF16),
            pltpu.SemaphoreType.DMA((2,)), pltpu.SemaphoreType.DMA((2, 2)),
        ],
    )
    return pl.pallas_call(
        _moe_kernel,
        grid_spec=grid_spec,
        out_shape=jax.ShapeDtypeStruct((n_tiles * TM_MOE, D_MODEL), BF16),
        compiler_params=_cparams("arbitrary"),
        name="moe_experts",
    )(tile_e, nused, src_chunk, next_e, slot_e, msort, w_up, w_down)


def _combine_kernel(src_ref, ys_ref, h1_ref, route_ref, p_ref,
                    g_ref, wg_ref, wp_ref, o_ref, gath_s, sems, wg_s, wp_s):
    i = pl.program_id(0)
    n_steps = pl.num_programs(0)

    @pl.when(i == 0)
    def _():
        wg_s[...] = wg_ref[...].astype(BF16)
        wp_s[...] = wp_ref[...].astype(BF16)
    tm = TM_TOK
    slot = i % 2

    def gather(t, s, act):
        for c in range(SORT_CHUNKS):
            src = pl.multiple_of(src_ref[t * SORT_CHUNKS + c], SEG_ALIGN)
            act(pltpu.make_async_copy(ys_ref.at[pl.ds(src, SEG_ALIGN)],
                                      gath_s.at[s, pl.ds(c * SEG_ALIGN, SEG_ALIGN)], sems.at[s]))

    @pl.when(i == 0)
    def _():
        gather(0, 0, lambda c: c.start())

    nxt = jnp.minimum(i + 1, n_steps - 1)
    gather(i, slot, lambda c: c.wait())
    gather(nxt, 1 - slot, lambda c: c.start())

    route = route_ref[...]
    col = lax.broadcasted_iota(jnp.int32, (tm, SORT_ROWS), 1).astype(F32)
    weights = (jnp.where(col == route[:, 2:3], route[:, 4:5], 0.0)
               + jnp.where(col == route[:, 3:4], route[:, 5:6], 0.0)).astype(BF16)
    y = jnp.dot(weights, gath_s[slot], preferred_element_type=F32)
    h2 = h1_ref[...] + y
    ms = jnp.mean(h2 * h2, axis=-1, keepdims=True)
    n = (h2 * lax.rsqrt(ms + EPS) * g_ref[...]).astype(BF16)
    z = jnp.dot(n, wg_s[...], preferred_element_type=F32)
    gate = 1.0 / (1.0 + jnp.exp(-z))
    ple = jnp.dot(p_ref[...].astype(BF16), wp_s[...], preferred_element_type=F32)
    o_ref[...] = h2 + gate * ple

    @pl.when(i == n_steps - 1)
    def _():
        gather(nxt, 1 - slot, lambda c: c.wait())


def _combine_ple(chunk_dst, ys, h1, route, p2, g_ple, w_gate, w_proj):
    T = h1.shape[0]
    tm = TM_TOK
    row = lambda w: pl.BlockSpec((tm, w), lambda i, *_: (i, 0))
    const = lambda a, b: pl.BlockSpec((a, b), lambda i, *_: (0, 0))
    grid_spec = pltpu.PrefetchScalarGridSpec(
        num_scalar_prefetch=1,
        grid=(T // tm,),
        in_specs=[pl.BlockSpec(memory_space=pl.ANY), row(D_MODEL), row(SUBLANES), row(PLE_DIM),
                  const(1, D_MODEL), _resident((D_MODEL, D_MODEL)), _resident((PLE_DIM, D_MODEL))],
        out_specs=row(D_MODEL),
        scratch_shapes=[pltpu.VMEM((2, SORT_ROWS, D_MODEL), BF16), pltpu.SemaphoreType.DMA((2,)),
                        pltpu.VMEM((D_MODEL, D_MODEL), BF16), pltpu.VMEM((PLE_DIM, D_MODEL), BF16)],
    )
    return pl.pallas_call(
        _combine_kernel,
        grid_spec=grid_spec,
        out_shape=jax.ShapeDtypeStruct((T, D_MODEL), F32),
        compiler_params=_cparams("arbitrary"),
        name="combine_ple",
    )(chunk_dst, ys, h1, route, p2, g_ple, w_gate, w_proj)


def _rope_tables(positions):
    half = ROT_DIM // 2
    inv = ROPE_THETA ** (-jnp.arange(0, ROT_DIM, 2, dtype=F32) / ROT_DIM)
    ang = positions.astype(F32)[:, None, :] * inv[None, :, None]
    lane = jnp.arange(LANES) % HEAD_DIM
    freq = jnp.arange(half)[:, None]
    rot = (lane[None, :] < ROT_DIM) & (lane[None, :] % half == freq)
    e_cos = rot.astype(F32)
    e_sin = jnp.where(rot, jnp.where(lane[None, :] < half, -1.0, 1.0), 0.0).astype(F32)
    expand = functools.partial(jnp.einsum, "bfs,fl->bsl", precision=lax.Precision.HIGHEST)
    return expand(jnp.cos(ang), e_cos) + (lane >= ROT_DIM).astype(F32), expand(jnp.sin(ang), e_sin)


def _layer(i, h, p, cos_t, sin_t, g_mix, w_in, b_f, qn_a, kn_a, qn_b, kn_b, w_o, g_ffn, w_rg, b_rg,
           w_re, b_re, w_up, w_down, g_ple, w_ple_gate, w_ple_proj):
    B, S, _ = h.shape
    T = B * S
    x2 = h.reshape(T, D_MODEL)

    bf = jnp.zeros((1, LANES), F32).at[0, :N_HEADS].set(b_f[i])
    pair = lambda g: jnp.tile(g, 2).reshape(1, LANES)
    w_r = jnp.zeros((D_MODEL, LANES), F32)
    w_r = w_r.at[:, :N_GROUPS].set(w_rg[i])
    w_r = w_r.at[:, ROUTER_LANE0:ROUTER_LANE0 + N_EXPERTS].set(
        jnp.transpose(w_re[i], (1, 0, 2)).reshape(D_MODEL, N_EXPERTS))
    b_r = jnp.zeros((1, LANES), F32).at[0, :N_GROUPS].set(b_rg[i])
    b_r = b_r.at[0, ROUTER_LANE0:ROUTER_LANE0 + N_EXPERTS].set(b_re[i].reshape(-1))

    qkv, f = _in_proj(x2, g_mix[i].reshape(1, -1), w_in, i)
    kbias = _forget_scan(f, bf, B, S)
    qkv3 = qkv.reshape(B, S, N_QKV)
    oa = _dilated_attn(qkv3, cos_t, sin_t, pair(qn_a[i]), pair(kn_a[i]))
    ob = _fox_attn(qkv3, kbias, pair(qn_b[i]), pair(kn_b[i]))

    h1, msort, route, cnt = _out_router(oa.reshape(T, D_SEC), ob.reshape(T, D_SEC), x2, w_o[i],
                                        g_ffn[i].reshape(1, -1), w_r.astype(BF16), b_r)

    n_tok_tiles = T // TM_TOK
    counts = cnt[:, 0, ROUTER_LANE0:ROUTER_LANE0 + N_EXPERTS].astype(jnp.int32)
    seg_rows = (counts + SEG_ALIGN - 1) // SEG_ALIGN * SEG_ALIGN
    seg_local = jnp.cumsum(seg_rows, axis=1) - seg_rows
    rows_e = jnp.sum(seg_rows, axis=0)
    tile_end = jnp.cumsum((rows_e + TM_MOE - 1) // TM_MOE)
    off = jnp.concatenate([jnp.zeros((1,), jnp.int32), tile_end * TM_MOE]).astype(jnp.int32)
    seg_global = off[None, :N_EXPERTS] + jnp.cumsum(seg_rows, axis=0) - seg_rows
    n_tiles = (2 * T + n_tok_tiles * N_EXPERTS * (SEG_ALIGN - 1)) // TM_MOE + N_EXPERTS
    nused = tile_end[-1:].astype(jnp.int32)
    tile_ids = jnp.minimum(jnp.arange(n_tiles, dtype=jnp.int32), nused[0] - 1)
    tile_e = jnp.sum((tile_ids[:, None] >= tile_end[None, :]).astype(jnp.int32), axis=1).astype(jnp.int32)
    chunk_row = jnp.arange(SORT_CHUNKS, dtype=jnp.int32)[None, :, None] * SEG_ALIGN
    lo, hi = seg_local[:, None, :], (seg_local + seg_rows)[:, None, :]
    chunk_dst = jnp.sum(jnp.where((chunk_row >= lo) & (chunk_row < hi), seg_global[:, None, :] + chunk_row - lo, 0),
                        axis=2).reshape(-1).astype(jnp.int32)
    sorted_row = jnp.arange(n_tiles * MOE_CHUNKS, dtype=jnp.int32)[:, None] * SEG_ALIGN
    seg_lo = seg_global.reshape(1, -1)
    seg_src = (jnp.arange(n_tok_tiles, dtype=jnp.int32)[:, None] * SORT_ROWS + seg_local).reshape(1, -1)
    in_seg = (sorted_row >= seg_lo) & (sorted_row < seg_lo + seg_rows.reshape(1, -1))
    src_chunk = jnp.where(jnp.any(in_seg, axis=1),
                          jnp.sum(jnp.where(in_seg, seg_src + sorted_row - seg_lo, 0), axis=1),
                          SORT_ROWS - SEG_ALIGN).astype(jnp.int32)
    e_ids = jnp.arange(N_EXPERTS, dtype=jnp.int32)
    active = rows_e > 0
    later_active = jnp.where(active[None, :] & (e_ids[None, :] > e_ids[:, None]), e_ids[None, :], N_EXPERTS)
    next_e = jnp.min(later_active, axis=1)
    next_e = jnp.where(next_e == N_EXPERTS, -1, next_e).astype(jnp.int32)
    slot_e = ((jnp.cumsum(active.astype(jnp.int32)) - 1) % 2).astype(jnp.int32)

    ys = _moe_experts(tile_e, nused, src_chunk, next_e, slot_e, msort, w_up[i], w_down[i])
    out = _combine_ple(chunk_dst, ys, h1, route, p[i].reshape(T, PLE_DIM),
                       g_ple[i].reshape(1, -1), w_ple_gate[i], w_ple_proj[i])
    return out.reshape(B, S, D_MODEL)


def kernel(x, p, positions, g_mix, w_in, b_f, qn_a, kn_a, qn_b, kn_b, w_o, g_ffn, w_rg, b_rg, w_re, b_re,
           w_up, w_down, g_ple, w_ple_gate, w_ple_proj):
    cos_t, sin_t = _rope_tables(positions)
    h = x
    for i in range(p.shape[0]):
        h = _layer(i, h, p, cos_t, sin_t, g_mix, w_in, b_f, qn_a, kn_a, qn_b, kn_b, w_o, g_ffn, w_rg, b_rg,
                   w_re, b_re, w_up, w_down, g_ple, w_ple_gate, w_ple_proj)
    return h
```

```python
import functools
import math

import jax
import jax.numpy as jnp
from jax import lax
from jax.experimental import pallas as pl
from jax.experimental.pallas import tpu as pltpu

D_MODEL = 1024
HEAD_DIM = 64
N_HEADS = 8
D_SEC = N_HEADS * HEAD_DIM
N_QKV = 6 * D_SEC
ROT_DIM = HEAD_DIM // 4
ROPE_THETA = 500000.0
N_GROUPS = 4
EXPERTS_PER_GROUP = 8
N_EXPERTS = N_GROUPS * EXPERTS_PER_GROUP
D_EXPERT = 512
PLE_DIM = 256
EPS = 1e-6
NEG = -1e30
WINDOW = 128

LANES = 128
SUBLANES = 8
VMEM_LIMIT = 48 * 1024 * 1024

TM_PROJ = 512
TQ = 128
TK = 256
N_PAIRS = N_HEADS // 2
TM_MOE = 512
TM_TOK = 512
SEG_ALIGN = 2 * SUBLANES
SORT_ROWS = 2 * TM_TOK + N_EXPERTS * SEG_ALIGN
SORT_CHUNKS = SORT_ROWS // SEG_ALIGN
SORT_TAIL = 256
ROUTER_LANE0 = N_GROUPS

Q_SCALE_LOG2 = math.log2(math.e) / math.sqrt(HEAD_DIM)

F32 = jnp.float32
BF16 = jnp.bfloat16
NT_DIMS = (((1,), (1,)), ((), ()))


def _cparams(*sem):
    return pltpu.CompilerParams(dimension_semantics=sem, vmem_limit_bytes=VMEM_LIMIT)


def _resident(shape):
    return pl.BlockSpec(shape, lambda i, *_: (0,) * len(shape), pipeline_mode=pl.Buffered(1))


def _in_proj_kernel(x_ref, g_ref, w_ref, qkv_ref, f_ref, w_s, wf_s):
    @pl.when(pl.program_id(0) == 0)
    def _():
        w_s[...] = w_ref[0, :, 0:N_QKV].astype(BF16)
        wf_s[...] = jnp.zeros_like(wf_s)
        wf_s[:, 0:N_HEADS] = w_ref[0, :, N_QKV:N_QKV + N_HEADS].astype(BF16)

    x = x_ref[...]
    ms = jnp.mean(x * x, axis=-1, keepdims=True)
    a = (x * lax.rsqrt(ms + EPS) * g_ref[...]).astype(BF16)
    qkv_ref[...] = jnp.dot(a, w_s[...], preferred_element_type=F32).astype(BF16)
    f_ref[...] = jnp.dot(a, wf_s[...], preferred_element_type=F32)


def _in_proj(x2, g_mix, w_in, layer):
    T = x2.shape[0]
    w_block = (1,) + w_in.shape[1:]
    return pl.pallas_call(
        _in_proj_kernel,
        grid=(T // TM_PROJ,),
        in_specs=[
            pl.BlockSpec((TM_PROJ, D_MODEL), lambda i: (i, 0)),
            pl.BlockSpec((1, D_MODEL), lambda i: (0, 0)),
            pl.BlockSpec(w_block, lambda i: (layer, 0, 0), pipeline_mode=pl.Buffered(1)),
        ],
        out_specs=[
            pl.BlockSpec((TM_PROJ, N_QKV), lambda i: (i, 0)),
            pl.BlockSpec((TM_PROJ, LANES), lambda i: (i, 0)),
        ],
        out_shape=[
            jax.ShapeDtypeStruct((T, N_QKV), BF16),
            jax.ShapeDtypeStruct((T, LANES), F32),
        ],
        scratch_shapes=[pltpu.VMEM((D_MODEL, N_QKV), BF16), pltpu.VMEM((D_MODEL, LANES), BF16)],
        compiler_params=_cparams("arbitrary"),
        name="in_proj",
    )(x2, g_mix, w_in)


BIAS_TERMS = 3


def _bias_lane(head):
    return (head // 2) * LANES + (HEAD_DIM if head % 2 == 0 else 0)


def _forget_scan_kernel(f_ref, bf_ref, kb_ref, *, seq):
    f = f_ref[...] + bf_ref[...]
    logf = jnp.minimum(f, 0.0) - jnp.log1p(jnp.exp(-jnp.abs(f)))

    def split_bf16(v):
        terms = []
        for _ in range(BIAS_TERMS):
            t = v.astype(BF16)
            terms.append(t)
            v = v - t.astype(F32)
        return terms

    tri = (lax.broadcasted_iota(jnp.int32, (TK, TK), 1) <= lax.broadcasted_iota(jnp.int32, (TK, TK), 0)).astype(BF16)
    carry = jnp.zeros((1, LANES), F32)
    chunks = []
    for j in range(seq // TK):
        terms = jnp.concatenate(split_bf16(logf[j * TK:(j + 1) * TK, :]), axis=1)
        pre = jnp.dot(tri, terms, preferred_element_type=F32)
        cj = carry + sum(pre[:, t * LANES:(t + 1) * LANES] for t in range(BIAS_TERMS))
        carry = cj[TK - 1:TK, :]
        chunks.append(cj)
    c = jnp.concatenate(chunks, axis=0)
    rest = c * (-math.log2(math.e))
    r_idx = lax.broadcasted_iota(jnp.int32, (LANES, D_SEC), 0)
    c_idx = lax.broadcasted_iota(jnp.int32, (LANES, D_SEC), 1)
    base = (r_idx // 2) * LANES + jnp.where(r_idx % 2 == 0, HEAD_DIM, 0)
    out = jnp.zeros((seq, D_SEC), F32)
    for t, term in enumerate(split_bf16(rest)):
        place = ((r_idx < N_HEADS) & (c_idx == base + t)).astype(BF16)
        out = out + jnp.dot(term, place, preferred_element_type=F32)
    kb_ref[0] = out.astype(BF16)


def _forget_scan(f, bf, batch, seq):
    return pl.pallas_call(
        functools.partial(_forget_scan_kernel, seq=seq),
        grid=(batch,),
        in_specs=[
            pl.BlockSpec((seq, LANES), lambda b: (b, 0)),
            pl.BlockSpec((1, LANES), lambda b: (0, 0)),
        ],
        out_specs=pl.BlockSpec((1, seq, D_SEC), lambda b: (b, 0, 0)),
        out_shape=jax.ShapeDtypeStruct((batch, seq, D_SEC), BF16),
        compiler_params=_cparams("parallel"),
        name="forget_scan",
    )(f, bf)


def _head_sumsq_matrix():
    r = lax.broadcasted_iota(jnp.int32, (LANES, LANES), 0) // HEAD_DIM
    c = lax.broadcasted_iota(jnp.int32, (LANES, LANES), 1) // HEAD_DIM
    return (r == c).astype(BF16)


def _qk_norm(x, gain, gmat):
    ss = jnp.dot((x * x).astype(BF16), gmat, preferred_element_type=F32)
    return x * lax.rsqrt(ss * (1.0 / HEAD_DIM) + EPS) * gain


def _stack_heads(qb, head0):
    zero = jnp.zeros_like(qb)
    return jnp.concatenate([jnp.where(head0, qb, zero), jnp.where(head0, zero, qb)], axis=0)


def _unstack(a, head0):
    return jnp.where(head0, a[:TQ], a[TQ:])


def _dilated_kernel(q_ref, k_ref, v_ref, cos_ref, sin_ref, gq_ref, gk_ref, o_ref,
                    qn_s, kn_s, v_s, acc_s, m_s, l_s, bias_s, *, seq):
    gmat = _head_sumsq_matrix()
    lane = lax.broadcasted_iota(jnp.int32, (TQ, LANES), 1)
    head0 = lane < HEAD_DIM
    chunk = 512
    lane_c = lax.broadcasted_iota(jnp.int32, (chunk, LANES), 1) % HEAD_DIM
    first_half = lane_c < ROT_DIM // 2

    def rope(t, cs, sn):
        partner = jnp.where(first_half, pltpu.roll(t, LANES - ROT_DIM // 2, axis=1),
                            pltpu.roll(t, ROT_DIM // 2, axis=1))
        return t * cs + partner * sn

    for c0 in range(0, seq, chunk):
        rows = pl.ds(c0, chunk)
        cs = cos_ref[0, rows, :]
        sn = sin_ref[0, rows, :]
        qn = _qk_norm(q_ref[0, rows, :].astype(F32), gq_ref[...], gmat)
        qn_s[rows, :] = rope(qn, cs, sn) * Q_SCALE_LOG2
        kn = _qk_norm(k_ref[0, rows, :].astype(F32), gk_ref[...], gmat)
        kn_s[rows, :] = rope(kn, cs, sn)
        v_s[rows, :] = v_ref[0, rows, :].astype(F32)

    u = lax.broadcasted_iota(jnp.int32, (2 * TQ, 2 * TQ), 0) % TQ
    c = lax.broadcasted_iota(jnp.int32, (2 * TQ, 2 * TQ), 1)
    for slot, dist0 in enumerate((0, TQ)):
        dist = dist0 + u - c
        bias_s[slot] = jnp.where((dist >= 0) & (dist <= WINDOW), 0.0, NEG)

    def scores(blk):
        p, q_rows, k_rows, nk, dist0 = blk
        qst = _stack_heads(qn_s[q_rows, :].astype(BF16), head0)
        s = lax.dot_general(qst, kn_s[k_rows, :].astype(BF16), NT_DIMS, preferred_element_type=F32)
        return s + bias_s[dist0 // TQ, :, 0:nk]

    def finish(blk, s):
        p, q_rows, k_rows, nk, dist0 = blk
        m = jnp.max(s, axis=-1, keepdims=True)
        pr = jnp.exp2(s - m)
        l = jnp.sum(pr, axis=-1, keepdims=True)
        acc = jnp.dot(pr.astype(BF16), v_s[k_rows, :].astype(BF16), preferred_element_type=F32)
        acc_s[p, q_rows, :] = _unstack(acc, head0)
        m_s[p, q_rows, :] = _unstack(jnp.broadcast_to(m, (2 * TQ, LANES)), head0)
        l_s[p, q_rows, :] = _unstack(jnp.broadcast_to(l, (2 * TQ, LANES)), head0)

    blocks = []
    for i in range(seq // TQ):
        kb0 = max(i - 1, 0)
        blocks.append((0, pl.ds(i * TQ, TQ), pl.ds(kb0 * TQ, 2 * TQ), 2 * TQ, (i - kb0) * TQ))
    for r in range(4):
        for n in range(seq // (4 * TQ)):
            kb0 = max(n - 1, 0)
            blocks.append((1, pl.ds(4 * TQ * n + r, TQ, stride=4), pl.ds(4 * TQ * kb0 + r, 2 * TQ, stride=4),
                           2 * TQ, (n - kb0) * TQ))
    for r in range(16):
        rows = pl.ds(r, TQ, stride=16)
        blocks.append((2, rows, rows, TQ, 0))

    ahead = 4
    pending = [scores(b) for b in blocks[:ahead]]
    for idx, b in enumerate(blocks):
        s = pending.pop(0)
        if idx + ahead < len(blocks):
            pending.append(scores(blocks[idx + ahead]))
        finish(b, s)

    for c0 in range(0, seq, chunk):
        rows = pl.ds(c0, chunk)
        m0, m1, m2 = m_s[0, rows, :], m_s[1, rows, :], m_s[2, rows, :]
        mm = jnp.maximum(jnp.maximum(m0, m1), m2)
        e0, e1, e2 = jnp.exp2(m0 - mm), jnp.exp2(m1 - mm), jnp.exp2(m2 - mm)
        num = acc_s[0, rows, :] * e0 + acc_s[1, rows, :] * e1 + acc_s[2, rows, :] * e2
        den = l_s[0, rows, :] * e0 + l_s[1, rows, :] * e1 + l_s[2, rows, :] * e2
        o_ref[0, rows, :] = (num / den).astype(BF16)


def _dilated_attn(qkv3, cos_t, sin_t, gq, gk):
    batch, seq, _ = qkv3.shape
    sec = D_SEC // LANES
    blk = lambda s: pl.BlockSpec((1, seq, LANES), lambda b, h, s=s: (b, 0, s * sec + h))
    tab = pl.BlockSpec((1, seq, LANES), lambda b, h: (b, 0, 0))
    vec = pl.BlockSpec((1, LANES), lambda b, h: (0, 0))
    return pl.pallas_call(
        functools.partial(_dilated_kernel, seq=seq),
        grid=(batch, N_PAIRS),
        in_specs=[blk(0), blk(1), blk(2), tab, tab, vec, vec],
        out_specs=pl.BlockSpec((1, seq, LANES), lambda b, h: (b, 0, h)),
        out_shape=jax.ShapeDtypeStruct((batch, seq, D_SEC), BF16),
        scratch_shapes=[
            pltpu.VMEM((seq, LANES), F32), pltpu.VMEM((seq, LANES), F32), pltpu.VMEM((seq, LANES), F32),
            pltpu.VMEM((3, seq, LANES), F32), pltpu.VMEM((3, seq, LANES), F32),
            pltpu.VMEM((3, seq, LANES), F32), pltpu.VMEM((2, 2 * TQ, 2 * TQ), F32),
        ],
        compiler_params=_cparams("parallel", "parallel"),
        name="dilated_attn",
    )(qkv3, qkv3, qkv3, cos_t, sin_t, gq, gk)


def _fox_kernel(q_ref, k_ref, v_ref, kb_ref, gq_ref, gk_ref, o_ref, qat_s, ka_s, vt_s, acc_s, tri_s, *, seq):
    gmat = _head_sumsq_matrix()
    chunk = 512
    lane = lax.broadcasted_iota(jnp.int32, (chunk, LANES), 1)
    low = lane < HEAD_DIM
    sel = jnp.where((lane >= HEAD_DIM) & (lane < HEAD_DIM + BIAS_TERMS), 1.0, 0.0)

    for c0 in range(0, seq, chunk):
        rows = pl.ds(c0, chunk)
        for g in range(N_PAIRS):
            lanes = slice(g * LANES, (g + 1) * LANES)
            qn = _qk_norm(q_ref[0, rows, lanes].astype(F32), gq_ref[...], gmat) * Q_SCALE_LOG2
            for h, qa in ((2 * g, jnp.where(low, qn, sel)),
                          (2 * g + 1, jnp.where(low, pltpu.roll(qn, HEAD_DIM, axis=1), sel))):
                qt = qa.T
                for cc in range(chunk // TK):
                    qat_s[c0 // TK + cc, h * LANES:(h + 1) * LANES, :] = qt[:, cc * TK:(cc + 1) * TK].astype(BF16)
            kn =_qk_norm(k_ref[0, rows, lanes].astype(F32), gk_ref[...], gmat)
            kb = kb_ref[0, rows, lanes].astype(F32)
            ka_s[2 * g, rows, :] = jnp.where(low, kn, kb).astype(BF16)
            ka_s[2 * g + 1, rows, :] = pltpu.roll(jnp.where(low, kb, kn), HEAD_DIM, axis=1).astype(BF16)
            vt = v_ref[0, rows, lanes].astype(F32).T
            for cc in range(chunk // TK):
                vt_s[c0 // TK + cc, lanes, :] = vt[:, cc * TK:(cc + 1) * TK].astype(BF16)

    r = lax.broadcasted_iota(jnp.int32, (TK, TK), 0)
    c = lax.broadcasted_iota(jnp.int32, (TK, TK), 1)
    tri_s[...] = jnp.where(r <= c, 0.0, NEG)

    def q_block(qi, carry):
        q_rows = pl.ds(pl.multiple_of(qi * TK, TK), TK)
        for h in range(N_HEADS):
            acc_s[h] = jnp.zeros((HEAD_DIM, TK), F32)

        def step(j, st, diagonal):
            k_rows = pl.ds(pl.multiple_of(j * TK, TK), TK)

            def scores(h):
                s = jnp.dot(ka_s[h, k_rows, :], qat_s[qi, h * LANES:(h + 1) * LANES, :], preferred_element_type=F32)
                return s + tri_s[...] if diagonal else s

            ahead = 6
            pending = [scores(h) for h in range(ahead)]
            new = []
            for h in range(N_HEADS):
                m, l = st[h]
                s = pending.pop(0)
                if h + ahead < N_HEADS:
                    pending.append(scores(h + ahead))
                m_new = jnp.maximum(m, jnp.max(s, axis=0, keepdims=True))
                alpha = jnp.exp2(m - m_new)
                pr = jnp.exp2(s - m_new)
                l_new = alpha * l + jnp.sum(pr, axis=0, keepdims=True)
                pv = jnp.dot(vt_s[j, h * HEAD_DIM:(h + 1) * HEAD_DIM, :], pr.astype(BF16),
                             preferred_element_type=F32)
                acc_s[h] = alpha * acc_s[h] + pv
                new.append((m_new, l_new))
            return tuple(new)

        init = tuple((jnp.full((1, TK), NEG, F32), jnp.zeros((1, TK), F32)) for _ in range(N_HEADS))
        st = lax.fori_loop(0, qi // 2, lambda jj, st: step(2 * jj + 1, step(2 * jj, st, False), False), init)
        st = lax.cond(qi % 2 == 1, lambda st: step(qi - 1, st, False), lambda st: st, st)
        st = step(qi, st, True)
        for g in range(N_PAIRS):
            o2 = jnp.concatenate([acc_s[2 * g] / st[2 * g][1], acc_s[2 * g + 1] / st[2 * g + 1][1]], axis=0)
            o_ref[0, q_rows, g * LANES:(g + 1) * LANES] = o2.T.astype(BF16)
        return carry

    lax.fori_loop(0, seq // TK, q_block, 0)


def _fox_attn(qkv3, kbias, gq, gk):
    batch, seq, _ = qkv3.shape
    blk = lambda s: pl.BlockSpec((1, seq, D_SEC), lambda b, s=s: (b, 0, s))
    vec = pl.BlockSpec((1, LANES), lambda b: (0, 0))
    return pl.pallas_call(
        functools.partial(_fox_kernel, seq=seq),
        grid=(batch,),
        in_specs=[blk(3), blk(4), blk(5), pl.BlockSpec((1, seq, D_SEC), lambda b: (b, 0, 0)), vec, vec],
        out_specs=pl.BlockSpec((1, seq, D_SEC), lambda b: (b, 0, 0)),
        out_shape=jax.ShapeDtypeStruct((batch, seq, D_SEC), BF16),
        scratch_shapes=[
            pltpu.VMEM((seq // TK, N_HEADS * LANES, TK), BF16),
            pltpu.VMEM((N_HEADS, seq, LANES), BF16),
            pltpu.VMEM((seq // TK, D_SEC, TK), BF16),
            pltpu.VMEM((N_HEADS, HEAD_DIM, TK), F32),
            pltpu.VMEM((TK, TK), F32),
        ],
        compiler_params=_cparams("parallel"),
        name="fox_attn",
    )(qkv3, qkv3, qkv3, kbias, gq, gk)


def _out_router_kernel(oa_ref, ob_ref, x_ref, wo_ref, g_ref, wr_ref, br_ref,
                       h1_ref, msort_ref, route_ref, cnt_ref, wo_s):
    tm = TM_TOK

    @pl.when(pl.program_id(0) == 0)
    def _():
        wo_s[...] = wo_ref[...].astype(BF16)

    h1 = (x_ref[...]
          + jnp.dot(oa_ref[...], wo_s[0:D_SEC, :], preferred_element_type=F32)
          + jnp.dot(ob_ref[...], wo_s[D_SEC:2 * D_SEC, :], preferred_element_type=F32))
    h1_ref[...] = h1
    ms = jnp.mean(h1 * h1, axis=-1, keepdims=True)
    mn = h1 * lax.rsqrt(ms + EPS) * g_ref[...]

    logits = jnp.dot(mn.astype(BF16), wr_ref[...], preferred_element_type=F32) + br_ref[...]
    lane = lax.broadcasted_iota(jnp.int32, (tm, LANES), 1).astype(F32)
    big = float(LANES)

    def first_argmax(vals):
        vmax = jnp.max(vals, axis=-1, keepdims=True)
        idx = jnp.min(jnp.where(vals == vmax, lane, big), axis=-1, keepdims=True)
        return vmax, idx

    lg = jnp.where(lane < N_GROUPS, logits, -jnp.inf)
    gmax, gidx = first_argmax(lg)
    gw = 1.0 / jnp.sum(jnp.exp(lg - gmax), axis=-1, keepdims=True)
    lo = ROUTER_LANE0 + EXPERTS_PER_GROUP * gidx
    le = jnp.where((lane >= lo) & (lane < lo + EXPERTS_PER_GROUP), logits, -jnp.inf)
    v0, i0 = first_argmax(le)
    v1, i1 = first_argmax(jnp.where(lane == i0, -jnp.inf, le))
    ex = jnp.exp(v1 - v0)
    w0 = gw / (1.0 + ex)
    w1 = gw * ex / (1.0 + ex)

    sel0 = lane == i0
    sel1 = lane == i1
    onehot = jnp.where(sel0 | sel1, 1.0, 0.0)
    r = lax.broadcasted_iota(jnp.int32, (tm, tm), 0)
    c = lax.broadcasted_iota(jnp.int32, (tm, tm), 1)
    ltri = (c <= r).astype(BF16)
    incl = jnp.dot(ltri, onehot.astype(BF16), preferred_element_type=F32)
    excl = incl - onehot
    counts = incl[tm - 1:tm, :]
    cnt_ref[0] = jnp.broadcast_to(counts, (SUBLANES, LANES))

    seg_rows = jnp.floor((counts + (SEG_ALIGN - 1.0)) * (1.0 / SEG_ALIGN)) * SEG_ALIGN
    used_rows = jnp.sum(seg_rows)
    seg_rows = jnp.broadcast_to(seg_rows, (tm, LANES))
    lp0 = jnp.sum(jnp.where(lane < i0, seg_rows, 0.0) + jnp.where(sel0, excl, 0.0), axis=-1, keepdims=True)
    lp1 = jnp.sum(jnp.where(lane < i1, seg_rows, 0.0) + jnp.where(sel1, excl, 0.0), axis=-1, keepdims=True)

    rec = jnp.zeros((tm, LANES), F32)
    for pos, val in enumerate((i0 - ROUTER_LANE0, i1 - ROUTER_LANE0, lp0, lp1, w0, w1)):
        rec = jnp.where(lane == float(pos), val, rec)
    route_ref[...] = rec[:, 0:SUBLANES]

    rec_t = rec.T
    out_row = lax.broadcasted_iota(jnp.int32, (SORT_ROWS, tm), 0).astype(F32)
    pick = jnp.where((out_row == rec_t[2:3, :]) | (out_row == rec_t[3:4, :]), 1.0, 0.0).astype(BF16)
    mn_bf = mn.astype(BF16)
    head_rows = SORT_ROWS - SORT_TAIL
    msort_ref[0:head_rows, :] = jnp.dot(pick[:head_rows], mn_bf, preferred_element_type=F32).astype(BF16)

    @pl.when(used_rows > head_rows)
    def _():
        msort_ref[head_rows:, :] = jnp.dot(pick[head_rows:], mn_bf, preferred_element_type=F32).astype(BF16)

    @pl.when(used_rows <= head_rows)
    def _():
        msort_ref[head_rows:, :] = jnp.zeros((SORT_TAIL, D_MODEL), BF16)


def _out_router(oa, ob, x2, w_o, g_ffn, w_r, b_r):
    T = x2.shape[0]
    tm = TM_TOK
    row = lambda w: pl.BlockSpec((tm, w), lambda i: (i, 0))
    const = lambda a, b: pl.BlockSpec((a, b), lambda i: (0, 0))
    return pl.pallas_call(
        _out_router_kernel,
        grid=(T // tm,),
        in_specs=[row(D_SEC), row(D_SEC), row(D_MODEL), _resident((D_MODEL, D_MODEL)), const(1, D_MODEL),
                  const(D_MODEL, LANES), const(1, LANES)],
        out_specs=[row(D_MODEL), pl.BlockSpec((SORT_ROWS, D_MODEL), lambda i: (i, 0)), row(SUBLANES),
                   pl.BlockSpec((1, SUBLANES, LANES), lambda i: (i, 0, 0))],
        out_shape=[
            jax.ShapeDtypeStruct((T, D_MODEL), F32),
            jax.ShapeDtypeStruct((T // tm * SORT_ROWS, D_MODEL), BF16),
            jax.ShapeDtypeStruct((T, SUBLANES), F32),
            jax.ShapeDtypeStruct((T // tm, SUBLANES, LANES), F32),
        ],
        scratch_shapes=[pltpu.VMEM((D_MODEL, D_MODEL), BF16)],
        compiler_params=_cparams("arbitrary"),
        name="out_router",
    )(oa, ob, x2, w_o, g_ffn, w_r, b_r)


MOE_CHUNKS = TM_MOE // SEG_ALIGN


def _chunk_row(c):
    return pl.multiple_of(c * SEG_ALIGN, SEG_ALIGN)


def _moe_kernel(tile_e_ref, nused_ref, src_ref, next_e_ref, slot_e_ref, msort_ref, wup_ref, wdn_ref, ys_ref,
                x_s, wup_f, wdn_f, wup_s, wdn_s, xsem, wsem):
    i = pl.program_id(0)
    nused = nused_ref[0]

    def gather(k, act):
        def chunk(c, carry):
            src = pl.multiple_of(src_ref[k * MOE_CHUNKS + c], SEG_ALIGN)
            act(pltpu.make_async_copy(msort_ref.at[pl.ds(src, SEG_ALIGN)],
                                      x_s.at[k % 2, pl.ds(_chunk_row(c), SEG_ALIGN)], xsem.at[k % 2]))
            return carry
        lax.fori_loop(0, MOE_CHUNKS, chunk, 0, unroll=8)

    def weights(e, s, act):
        act(pltpu.make_async_copy(wup_ref.at[e], wup_f.at[s], wsem.at[0, s]))
        act(pltpu.make_async_copy(wdn_ref.at[e], wdn_f.at[s], wsem.at[1, s]))

    @pl.when(i == 0)
    def _():
        gather(0, lambda c: c.start())
        weights(tile_e_ref[0], 0, lambda c: c.start())

    @pl.when(i + 1 < nused)
    def _():
        gather(i + 1, lambda c: c.start())

    @pl.when(i < nused)
    def _():
        e = tile_e_ref[i]

        @pl.when((i == 0) | (tile_e_ref[jnp.maximum(i - 1, 0)] != e))
        def _():
            s = slot_e_ref[e]
            weights(e, s, lambda c: c.wait())

            @pl.when(next_e_ref[e] >= 0)
            def _():
                weights(next_e_ref[e], 1 - s, lambda c: c.start())
            wup_s[...] = wup_f[s].astype(BF16)
            wdn_s[...] = wdn_f[s].astype(BF16)

        gather(i, lambda c: c.wait())
        hu = jnp.dot(x_s[i % 2], wup_s[...], preferred_element_type=F32)
        gate = hu[:, :D_EXPERT]
        hid = gate * (1.0 / (1.0 + jnp.exp(-gate))) * hu[:, D_EXPERT:]
        ys_ref[...] = jnp.dot(hid.astype(BF16), wdn_s[...], preferred_element_type=F32).astype(BF16)

    @pl.when(i >= nused)
    def _():
        ys_ref[...] = jnp.zeros_like(ys_ref)


def _moe_experts(tile_e, nused, src_chunk, next_e, slot_e, msort, w_up, w_down):
    n_tiles = tile_e.shape[0]
    grid_spec = pltpu.PrefetchScalarGridSpec(
        num_scalar_prefetch=5,
        grid=(n_tiles,),
        in_specs=[pl.BlockSpec(memory_space=pl.ANY)] * 3,
        out_specs=pl.BlockSpec((TM_MOE, D_MODEL), lambda i, *_: (i, 0)),
        scratch_shapes=[
            pltpu.VMEM((2, TM_MOE, D_MODEL), BF16),
            pltpu.VMEM((2, D_MODEL, 2 * D_EXPERT), F32), pltpu.VMEM((2, D_EXPERT, D_MODEL), F32),
            pltpu.VMEM((D_MODEL, 2 * D_EXPERT), BF16), pltpu.VMEM((D_EXPERT, D_MODEL), BF16),
            pltpu.SemaphoreType.DMA((2,)), pltpu.SemaphoreType.DMA((2, 2)),
        ],
    )
    return pl.pallas_call(
        _moe_kernel,
        grid_spec=grid_spec,
        out_shape=jax.ShapeDtypeStruct((n_tiles * TM_MOE, D_MODEL), BF16),
        compiler_params=_cparams("arbitrary"),
        name="moe_experts",
    )(tile_e, nused, src_chunk, next_e, slot_e, msort, w_up, w_down)


def _combine_kernel(src_ref, ys_ref, h1_ref, route_ref, p_ref,
                    g_ref, wg_ref, wp_ref, o_ref, gath_s, sems, wg_s, wp_s):
    i = pl.program_id(0)
    n_steps = pl.num_programs(0)

    @pl.when(i == 0)
    def _():
        wg_s[...] = wg_ref[...].astype(BF16)
        wp_s[...] = wp_ref[...].astype(BF16)
    tm = TM_TOK
    slot = i % 2

    def gather(t, s, act):
        for c in range(SORT_CHUNKS):
            src = pl.multiple_of(src_ref[t * SORT_CHUNKS + c], SEG_ALIGN)
            act(pltpu.make_async_copy(ys_ref.at[pl.ds(src, SEG_ALIGN)],
                                      gath_s.at[s, pl.ds(c * SEG_ALIGN, SEG_ALIGN)], sems.at[s]))

    @pl.when(i == 0)
    def _():
        gather(0, 0, lambda c: c.start())

    nxt = jnp.minimum(i + 1, n_steps - 1)
    gather(i, slot, lambda c: c.wait())
    gather(nxt, 1 - slot, lambda c: c.start())

    route = route_ref[...]
    col = lax.broadcasted_iota(jnp.int32, (tm, SORT_ROWS), 1).astype(F32)
    weights = (jnp.where(col == route[:, 2:3], route[:, 4:5], 0.0)
               + jnp.where(col == route[:, 3:4], route[:, 5:6], 0.0)).astype(BF16)
    y = jnp.dot(weights, gath_s[slot], preferred_element_type=F32)
    h2 = h1_ref[...] + y
    ms = jnp.mean(h2 * h2, axis=-1, keepdims=True)
    n = (h2 * lax.rsqrt(ms + EPS) * g_ref[...]).astype(BF16)
    z = jnp.dot(n, wg_s[...], preferred_element_type=F32)
    gate = 1.0 / (1.0 + jnp.exp(-z))
    ple = jnp.dot(p_ref[...].astype(BF16), wp_s[...], preferred_element_type=F32)
    o_ref[...] = h2 + gate * ple

    @pl.when(i == n_steps - 1)
    def _():
        gather(nxt, 1 - slot, lambda c: c.wait())


def _combine_ple(chunk_dst, ys, h1, route, p2, g_ple, w_gate, w_proj):
    T = h1.shape[0]
    tm = TM_TOK
    row = lambda w: pl.BlockSpec((tm, w), lambda i, *_: (i, 0))
    const = lambda a, b: pl.BlockSpec((a, b), lambda i, *_: (0, 0))
    grid_spec = pltpu.PrefetchScalarGridSpec(
        num_scalar_prefetch=1,
        grid=(T // tm,),
        in_specs=[pl.BlockSpec(memory_space=pl.ANY), row(D_MODEL), row(SUBLANES), row(PLE_DIM),
                  const(1, D_MODEL), _resident((D_MODEL, D_MODEL)), _resident((PLE_DIM, D_MODEL))],
        out_specs=row(D_MODEL),
        scratch_shapes=[pltpu.VMEM((2, SORT_ROWS, D_MODEL), BF16), pltpu.SemaphoreType.DMA((2,)),
                        pltpu.VMEM((D_MODEL, D_MODEL), BF16), pltpu.VMEM((PLE_DIM, D_MODEL), BF16)],
    )
    return pl.pallas_call(
        _combine_kernel,
        grid_spec=grid_spec,
        out_shape=jax.ShapeDtypeStruct((T, D_MODEL), F32),
        compiler_params=_cparams("arbitrary"),
        name="combine_ple",
    )(chunk_dst, ys, h1, route, p2, g_ple, w_gate, w_proj)


def _rope_tables(positions):
    half = ROT_DIM // 2
    inv = ROPE_THETA ** (-jnp.arange(0, ROT_DIM, 2, dtype=F32) / ROT_DIM)
    ang = positions.astype(F32)[:, None, :] * inv[None, :, None]
    lane = jnp.arange(LANES) % HEAD_DIM
    freq = jnp.arange(half)[:, None]
    rot = (lane[None, :] < ROT_DIM) & (lane[None, :] % half == freq)
    e_cos = rot.astype(F32)
    e_sin = jnp.where(rot, jnp.where(lane[None, :] < half, -1.0, 1.0), 0.0).astype(F32)
    expand = functools.partial(jnp.einsum, "bfs,fl->bsl", precision=lax.Precision.HIGHEST)
    return expand(jnp.cos(ang), e_cos) + (lane >= ROT_DIM).astype(F32), expand(jnp.sin(ang), e_sin)


def _layer(i, h, p, cos_t, sin_t, g_mix, w_in, b_f, qn_a, kn_a, qn_b, kn_b, w_o, g_ffn, w_rg, b_rg,
           w_re, b_re, w_up, w_down, g_ple, w_ple_gate, w_ple_proj):
    B, S, _ = h.shape
    T = B * S
    x2 = h.reshape(T, D_MODEL)

    bf = jnp.zeros((1, LANES), F32).at[0, :N_HEADS].set(b_f[i])
    pair = lambda g: jnp.tile(g, 2).reshape(1, LANES)
    w_r = jnp.zeros((D_MODEL, LANES), F32)
    w_r = w_r.at[:, :N_GROUPS].set(w_rg[i])
    w_r = w_r.at[:, ROUTER_LANE0:ROUTER_LANE0 + N_EXPERTS].set(
        jnp.transpose(w_re[i], (1, 0, 2)).reshape(D_MODEL, N_EXPERTS))
    b_r = jnp.zeros((1, LANES), F32).at[0, :N_GROUPS].set(b_rg[i])
    b_r = b_r.at[0, ROUTER_LANE0:ROUTER_LANE0 + N_EXPERTS].set(b_re[i].reshape(-1))

    qkv, f = _in_proj(x2, g_mix[i].reshape(1, -1), w_in, i)
    kbias = _forget_scan(f, bf, B, S)
    qkv3 = qkv.reshape(B, S, N_QKV)
    oa = _dilated_attn(qkv3, cos_t, sin_t, pair(qn_a[i]), pair(kn_a[i]))
    ob = _fox_attn(qkv3, kbias, pair(qn_b[i]), pair(kn_b[i]))

    h1, msort, route, cnt = _out_router(oa.reshape(T, D_SEC), ob.reshape(T, D_SEC), x2, w_o[i],
                                        g_ffn[i].reshape(1, -1), w_r.astype(BF16), b_r)

    n_tok_tiles = T // TM_TOK
    counts = cnt[:, 0, ROUTER_LANE0:ROUTER_LANE0 + N_EXPERTS].astype(jnp.int32)
    seg_rows = (counts + SEG_ALIGN - 1) // SEG_ALIGN * SEG_ALIGN
    seg_local = jnp.cumsum(seg_rows, axis=1) - seg_rows
    rows_e = jnp.sum(seg_rows, axis=0)
    tile_end = jnp.cumsum((rows_e + TM_MOE - 1) // TM_MOE)
    off = jnp.concatenate([jnp.zeros((1,), jnp.int32), tile_end * TM_MOE]).astype(jnp.int32)
    seg_global = off[None, :N_EXPERTS] + jnp.cumsum(seg_rows, axis=0) - seg_rows
    n_tiles = (2 * T + n_tok_tiles * N_EXPERTS * (SEG_ALIGN - 1)) // TM_MOE + N_EXPERTS
    nused = tile_end[-1:].astype(jnp.int32)
    tile_ids = jnp.minimum(jnp.arange(n_tiles, dtype=jnp.int32), nused[0] - 1)
    tile_e = jnp.sum((tile_ids[:, None] >= tile_end[None, :]).astype(jnp.int32), axis=1).astype(jnp.int32)
    chunk_row = jnp.arange(SORT_CHUNKS, dtype=jnp.int32)[None, :, None] * SEG_ALIGN
    lo, hi = seg_local[:, None, :], (seg_local + seg_rows)[:, None, :]
    chunk_dst = jnp.sum(jnp.where((chunk_row >= lo) & (chunk_row < hi), seg_global[:, None, :] + chunk_row - lo, 0),
                        axis=2).reshape(-1).astype(jnp.int32)
    sorted_row = jnp.arange(n_tiles * MOE_CHUNKS, dtype=jnp.int32)[:, None] * SEG_ALIGN
    seg_lo = seg_global.reshape(1, -1)
    seg_src = (jnp.arange(n_tok_tiles, dtype=jnp.int32)[:, None] * SORT_ROWS + seg_local).reshape(1, -1)
    in_seg = (sorted_row >= seg_lo) & (sorted_row < seg_lo + seg_rows.reshape(1, -1))
    src_chunk = jnp.where(jnp.any(in_seg, axis=1),
                          jnp.sum(jnp.where(in_seg, seg_src + sorted_row - seg_lo, 0), axis=1),
                          SORT_ROWS - SEG_ALIGN).astype(jnp.int32)
    e_ids = jnp.arange(N_EXPERTS, dtype=jnp.int32)
    active = rows_e > 0
    later_active = jnp.where(active[None, :] & (e_ids[None, :] > e_ids[:, None]), e_ids[None, :], N_EXPERTS)
    next_e = jnp.min(later_active, axis=1)
    next_e = jnp.where(next_e == N_EXPERTS, -1, next_e).astype(jnp.int32)
    slot_e = ((jnp.cumsum(active.astype(jnp.int32)) - 1) % 2).astype(jnp.int32)

    ys = _moe_experts(tile_e, nused, src_chunk, next_e, slot_e, msort, w_up[i], w_down[i])
    out = _combine_ple(chunk_dst, ys, h1, route, p[i].reshape(T, PLE_DIM),
                       g_ple[i].reshape(1, -1), w_ple_gate[i], w_ple_proj[i])
    return out.reshape(B, S, D_MODEL)


def kernel(x, p, positions, g_mix, w_in, b_f, qn_a, kn_a, qn_b, kn_b, w_o, g_ffn, w_rg, b_rg, w_re, b_re,
           w_up, w_down, g_ple, w_ple_gate, w_ple_proj):
    cos_t, sin_t = _rope_tables(positions)
    h = x
    for i in range(p.shape[0]):
        h = _layer(i, h, p, cos_t, sin_t, g_mix, w_in, b_f, qn_a, kn_a, qn_b, kn_b, w_o, g_ffn, w_rg, b_rg,
                   w_re, b_re, w_up, w_down, g_ple, w_ple_gate, w_ple_proj)
    return h
```

```python
import functools
import math

import jax
import jax.numpy as jnp
from jax import lax
from jax.experimental import pallas as pl
from jax.experimental.pallas import tpu as pltpu

D_MODEL = 1024
HEAD_DIM = 64
N_HEADS = 8
D_SEC = N_HEADS * HEAD_DIM
N_QKV = 6 * D_SEC
ROT_DIM = HEAD_DIM // 4
ROPE_THETA = 500000.0
N_GROUPS = 4
EXPERTS_PER_GROUP = 8
N_EXPERTS = N_GROUPS * EXPERTS_PER_GROUP
D_EXPERT = 512
PLE_DIM = 256
EPS = 1e-6
NEG = -1e30
WINDOW = 128

LANES = 128
SUBLANES = 8
VMEM_LIMIT = 48 * 1024 * 1024

TM_PROJ = 512
TQ = 128
TK = 256
N_PAIRS = N_HEADS // 2
TM_MOE = 512
TM_TOK = 512
SEG_ALIGN = 2 * SUBLANES
SORT_ROWS = 2 * TM_TOK + N_EXPERTS * SEG_ALIGN
SORT_CHUNKS = SORT_ROWS // SEG_ALIGN
ROUTER_LANE0 = N_GROUPS

Q_SCALE_LOG2 = math.log2(math.e) / math.sqrt(HEAD_DIM)

F32 = jnp.float32
BF16 = jnp.bfloat16
NT_DIMS = (((1,), (1,)), ((), ()))


def _cparams(*sem):
    return pltpu.CompilerParams(dimension_semantics=sem, vmem_limit_bytes=VMEM_LIMIT)


def _resident(shape):
    return pl.BlockSpec(shape, lambda i, *_: (0,) * len(shape), pipeline_mode=pl.Buffered(1))


def _in_proj_kernel(x_ref, g_ref, w_ref, qkv_ref, f_ref, w_s, wf_s):
    @pl.when(pl.program_id(0) == 0)
    def _():
        w_s[...] = w_ref[0, :, 0:N_QKV].astype(BF16)
        wf_s[...] = jnp.zeros_like(wf_s)
        wf_s[:, 0:N_HEADS] = w_ref[0, :, N_QKV:N_QKV + N_HEADS].astype(BF16)

    x = x_ref[...]
    ms = jnp.mean(x * x, axis=-1, keepdims=True)
    a = (x * lax.rsqrt(ms + EPS) * g_ref[...]).astype(BF16)
    qkv_ref[...] = jnp.dot(a, w_s[...], preferred_element_type=F32).astype(BF16)
    f_ref[...] = jnp.dot(a, wf_s[...], preferred_element_type=F32)


def _in_proj(x2, g_mix, w_in, layer):
    T = x2.shape[0]
    w_block = (1,) + w_in.shape[1:]
    return pl.pallas_call(
        _in_proj_kernel,
        grid=(T // TM_PROJ,),
        in_specs=[
            pl.BlockSpec((TM_PROJ, D_MODEL), lambda i: (i, 0)),
            pl.BlockSpec((1, D_MODEL), lambda i: (0, 0)),
            pl.BlockSpec(w_block, lambda i: (layer, 0, 0), pipeline_mode=pl.Buffered(1)),
        ],
        out_specs=[
            pl.BlockSpec((TM_PROJ, N_QKV), lambda i: (i, 0)),
            pl.BlockSpec((TM_PROJ, LANES), lambda i: (i, 0)),
        ],
        out_shape=[
            jax.ShapeDtypeStruct((T, N_QKV), BF16),
            jax.ShapeDtypeStruct((T, LANES), F32),
        ],
        scratch_shapes=[pltpu.VMEM((D_MODEL, N_QKV), BF16), pltpu.VMEM((D_MODEL, LANES), BF16)],
        compiler_params=_cparams("arbitrary"),
        name="in_proj",
    )(x2, g_mix, w_in)


BIAS_TERMS = 3


def _bias_lane(head):
    return (head // 2) * LANES + (HEAD_DIM if head % 2 == 0 else 0)


def _forget_scan_kernel(f_ref, bf_ref, kb_ref, *, seq):
    f = f_ref[...] + bf_ref[...]
    logf = jnp.minimum(f, 0.0) - jnp.log1p(jnp.exp(-jnp.abs(f)))

    def split_bf16(v):
        terms = []
        for _ in range(BIAS_TERMS):
            t = v.astype(BF16)
            terms.append(t)
            v = v - t.astype(F32)
        return terms

    tri = (lax.broadcasted_iota(jnp.int32, (TK, TK), 1) <= lax.broadcasted_iota(jnp.int32, (TK, TK), 0)).astype(BF16)
    carry = jnp.zeros((1, LANES), F32)
    chunks = []
    for j in range(seq // TK):
        terms = jnp.concatenate(split_bf16(logf[j * TK:(j + 1) * TK, :]), axis=1)
        pre = jnp.dot(tri, terms, preferred_element_type=F32)
        cj = carry + sum(pre[:, t * LANES:(t + 1) * LANES] for t in range(BIAS_TERMS))
        carry = cj[TK - 1:TK, :]
        chunks.append(cj)
    c = jnp.concatenate(chunks, axis=0)
    rest = c * (-math.log2(math.e))
    r_idx = lax.broadcasted_iota(jnp.int32, (LANES, D_SEC), 0)
    c_idx = lax.broadcasted_iota(jnp.int32, (LANES, D_SEC), 1)
    base = (r_idx // 2) * LANES + jnp.where(r_idx % 2 == 0, HEAD_DIM, 0)
    out = jnp.zeros((seq, D_SEC), F32)
    for t, term in enumerate(split_bf16(rest)):
        place = ((r_idx < N_HEADS) & (c_idx == base + t)).astype(BF16)
        out = out + jnp.dot(term, place, preferred_element_type=F32)
    kb_ref[0] = out.astype(BF16)


def _forget_scan(f, bf, batch, seq):
    return pl.pallas_call(
        functools.partial(_forget_scan_kernel, seq=seq),
        grid=(batch,),
        in_specs=[
            pl.BlockSpec((seq, LANES), lambda b: (b, 0)),
            pl.BlockSpec((1, LANES), lambda b: (0, 0)),
        ],
        out_specs=pl.BlockSpec((1, seq, D_SEC), lambda b: (b, 0, 0)),
        out_shape=jax.ShapeDtypeStruct((batch, seq, D_SEC), BF16),
        compiler_params=_cparams("parallel"),
        name="forget_scan",
    )(f, bf)


def _head_sumsq_matrix():
    r = lax.broadcasted_iota(jnp.int32, (LANES, LANES), 0) // HEAD_DIM
    c = lax.broadcasted_iota(jnp.int32, (LANES, LANES), 1) // HEAD_DIM
    return (r == c).astype(BF16)


def _qk_norm(x, gain, gmat):
    ss = jnp.dot((x * x).astype(BF16), gmat, preferred_element_type=F32)
    return x * lax.rsqrt(ss * (1.0 / HEAD_DIM) + EPS) * gain


def _stack_heads(qb, head0):
    zero = jnp.zeros_like(qb)
    return jnp.concatenate([jnp.where(head0, qb, zero), jnp.where(head0, zero, qb)], axis=0)


def _unstack(a, head0):
    return jnp.where(head0, a[:TQ], a[TQ:])


def _dilated_kernel(q_ref, k_ref, v_ref, cos_ref, sin_ref, gq_ref, gk_ref, o_ref,
                    qn_s, kn_s, v_s, acc_s, m_s, l_s, bias_s, *, seq):
    gmat = _head_sumsq_matrix()
    lane = lax.broadcasted_iota(jnp.int32, (TQ, LANES), 1)
    head0 = lane < HEAD_DIM
    chunk = 512
    lane_c = lax.broadcasted_iota(jnp.int32, (chunk, LANES), 1) % HEAD_DIM
    first_half = lane_c < ROT_DIM // 2

    def rope(t, cs, sn):
        partner = jnp.where(first_half, pltpu.roll(t, LANES - ROT_DIM // 2, axis=1),
                            pltpu.roll(t, ROT_DIM // 2, axis=1))
        return t * cs + partner * sn

    for c0 in range(0, seq, chunk):
        rows = pl.ds(c0, chunk)
        cs = cos_ref[0, rows, :]
        sn = sin_ref[0, rows, :]
        qn = _qk_norm(q_ref[0, rows, :].astype(F32), gq_ref[...], gmat)
        qn_s[rows, :] = rope(qn, cs, sn) * Q_SCALE_LOG2
        kn = _qk_norm(k_ref[0, rows, :].astype(F32), gk_ref[...], gmat)
        kn_s[rows, :] = rope(kn, cs, sn)
        v_s[rows, :] = v_ref[0, rows, :].astype(F32)

    u = lax.broadcasted_iota(jnp.int32, (2 * TQ, 2 * TQ), 0) % TQ
    c = lax.broadcasted_iota(jnp.int32, (2 * TQ, 2 * TQ), 1)
    for slot, dist0 in enumerate((0, TQ)):
        dist = dist0 + u - c
        bias_s[slot] = jnp.where((dist >= 0) & (dist <= WINDOW), 0.0, NEG)

    def scores(blk):
        p, q_rows, k_rows, nk, dist0 = blk
        qst = _stack_heads(qn_s[q_rows, :].astype(BF16), head0)
        s = lax.dot_general(qst, kn_s[k_rows, :].astype(BF16), NT_DIMS, preferred_element_type=F32)
        return s + bias_s[dist0 // TQ, :, 0:nk]

    def finish(blk, s):
        p, q_rows, k_rows, nk, dist0 = blk
        m = jnp.max(s, axis=-1, keepdims=True)
        pr = jnp.exp2(s - m)
        l = jnp.sum(pr, axis=-1, keepdims=True)
        acc = jnp.dot(pr.astype(BF16), v_s[k_rows, :].astype(BF16), preferred_element_type=F32)
        acc_s[p, q_rows, :] = _unstack(acc, head0)
        m_s[p, q_rows, :] = _unstack(jnp.broadcast_to(m, (2 * TQ, LANES)), head0)
        l_s[p, q_rows, :] = _unstack(jnp.broadcast_to(l, (2 * TQ, LANES)), head0)

    blocks = []
    for i in range(seq // TQ):
        kb0 = max(i - 1, 0)
        blocks.append((0, pl.ds(i * TQ, TQ), pl.ds(kb0 * TQ, 2 * TQ), 2 * TQ, (i - kb0) * TQ))
    for r in range(4):
        for n in range(seq // (4 * TQ)):
            kb0 = max(n - 1, 0)
            blocks.append((1, pl.ds(4 * TQ * n + r, TQ, stride=4), pl.ds(4 * TQ * kb0 + r, 2 * TQ, stride=4),
                           2 * TQ, (n - kb0) * TQ))
    for r in range(16):
        rows = pl.ds(r, TQ, stride=16)
        blocks.append((2, rows, rows, TQ, 0))

    ahead = 4
    pending = [scores(b) for b in blocks[:ahead]]
    for idx, b in enumerate(blocks):
        s = pending.pop(0)
        if idx + ahead < len(blocks):
            pending.append(scores(blocks[idx + ahead]))
        finish(b, s)

    for c0 in range(0, seq, chunk):
        rows = pl.ds(c0, chunk)
        m0, m1, m2 = m_s[0, rows, :], m_s[1, rows, :], m_s[2, rows, :]
        mm = jnp.maximum(jnp.maximum(m0, m1), m2)
        e0, e1, e2 = jnp.exp2(m0 - mm), jnp.exp2(m1 - mm), jnp.exp2(m2 - mm)
        num = acc_s[0, rows, :] * e0 + acc_s[1, rows, :] * e1 + acc_s[2, rows, :] * e2
        den = l_s[0, rows, :] * e0 + l_s[1, rows, :] * e1 + l_s[2, rows, :] * e2
        o_ref[0, rows, :] = (num / den).astype(BF16)


def _dilated_attn(qkv3, cos_t, sin_t, gq, gk):
    batch, seq, _ = qkv3.shape
    sec = D_SEC // LANES
    blk = lambda s: pl.BlockSpec((1, seq, LANES), lambda b, h, s=s: (b, 0, s * sec + h))
    tab = pl.BlockSpec((1, seq, LANES), lambda b, h: (b, 0, 0))
    vec = pl.BlockSpec((1, LANES), lambda b, h: (0, 0))
    return pl.pallas_call(
        functools.partial(_dilated_kernel, seq=seq),
        grid=(batch, N_PAIRS),
        in_specs=[blk(0), blk(1), blk(2), tab, tab, vec, vec],
        out_specs=pl.BlockSpec((1, seq, LANES), lambda b, h: (b, 0, h)),
        out_shape=jax.ShapeDtypeStruct((batch, seq, D_SEC), BF16),
        scratch_shapes=[
            pltpu.VMEM((seq, LANES), F32), pltpu.VMEM((seq, LANES), F32), pltpu.VMEM((seq, LANES), F32),
            pltpu.VMEM((3, seq, LANES), F32), pltpu.VMEM((3, seq, LANES), F32),
            pltpu.VMEM((3, seq, LANES), F32), pltpu.VMEM((2, 2 * TQ, 2 * TQ), F32),
        ],
        compiler_params=_cparams("parallel", "parallel"),
        name="dilated_attn",
    )(qkv3, qkv3, qkv3, cos_t, sin_t, gq, gk)


def _fox_kernel(q_ref, k_ref, v_ref, kb_ref, gq_ref, gk_ref, o_ref, qat_s, ka_s, vt_s, acc_s, tri_s, *, seq):
    gmat = _head_sumsq_matrix()
    chunk = 512
    lane = lax.broadcasted_iota(jnp.int32, (chunk, LANES), 1)
    low = lane < HEAD_DIM
    sel = jnp.where((lane >= HEAD_DIM) & (lane < HEAD_DIM + BIAS_TERMS), 1.0, 0.0)

    for c0 in range(0, seq, chunk):
        rows = pl.ds(c0, chunk)
        for g in range(N_PAIRS):
            lanes = slice(g * LANES, (g + 1) * LANES)
            qn = _qk_norm(q_ref[0, rows, lanes].astype(F32), gq_ref[...], gmat) * Q_SCALE_LOG2
            for h, qa in ((2 * g, jnp.where(low, qn, sel)),
                          (2 * g + 1, jnp.where(low, pltpu.roll(qn, HEAD_DIM, axis=1), sel))):
                qt = qa.T
                for cc in range(chunk // TK):
                    qat_s[c0 // TK + cc, h * LANES:(h + 1) * LANES, :] = qt[:, cc * TK:(cc + 1) * TK].astype(BF16)
            kn =_qk_norm(k_ref[0, rows, lanes].astype(F32), gk_ref[...], gmat)
            kb = kb_ref[0, rows, lanes].astype(F32)
            ka_s[2 * g, rows, :] = jnp.where(low, kn, kb).astype(BF16)
            ka_s[2 * g + 1, rows, :] = pltpu.roll(jnp.where(low, kb, kn), HEAD_DIM, axis=1).astype(BF16)
            vt = v_ref[0, rows, lanes].astype(F32).T
            for cc in range(chunk // TK):
                vt_s[c0 // TK + cc, lanes, :] = vt[:, cc * TK:(cc + 1) * TK].astype(BF16)

    r = lax.broadcasted_iota(jnp.int32, (TK, TK), 0)
    c = lax.broadcasted_iota(jnp.int32, (TK, TK), 1)
    tri_s[...] = jnp.where(r <= c, 0.0, NEG)

    def q_block(qi, carry):
        q_rows = pl.ds(pl.multiple_of(qi * TK, TK), TK)
        for h in range(N_HEADS):
            acc_s[h] = jnp.zeros((HEAD_DIM, TK), F32)

        def step(j, st, diagonal):
            k_rows = pl.ds(pl.multiple_of(j * TK, TK), TK)

            def scores(h):
                s = jnp.dot(ka_s[h, k_rows, :], qat_s[qi, h * LANES:(h + 1) * LANES, :], preferred_element_type=F32)
                return s + tri_s[...] if diagonal else s

            ahead = 6
            pending = [scores(h) for h in range(ahead)]
            new = []
            for h in range(N_HEADS):
                m, l = st[h]
                s = pending.pop(0)
                if h + ahead < N_HEADS:
                    pending.append(scores(h + ahead))
                m_new = jnp.maximum(m, jnp.max(s, axis=0, keepdims=True))
                alpha = jnp.exp2(m - m_new)
                pr = jnp.exp2(s - m_new)
                l_new = alpha * l + jnp.sum(pr, axis=0, keepdims=True)
                pv = jnp.dot(vt_s[j, h * HEAD_DIM:(h + 1) * HEAD_DIM, :], pr.astype(BF16),
                             preferred_element_type=F32)
                acc_s[h] = alpha * acc_s[h] + pv
                new.append((m_new, l_new))
            return tuple(new)

        init = tuple((jnp.full((1, TK), NEG, F32), jnp.zeros((1, TK), F32)) for _ in range(N_HEADS))
        st = lax.fori_loop(0, qi // 2, lambda jj, st: step(2 * jj + 1, step(2 * jj, st, False), False), init)
        st = lax.cond(qi % 2 == 1, lambda st: step(qi - 1, st, False), lambda st: st, st)
        st = step(qi, st, True)
        for g in range(N_PAIRS):
            o2 = jnp.concatenate([acc_s[2 * g] / st[2 * g][1], acc_s[2 * g + 1] / st[2 * g + 1][1]], axis=0)
            o_ref[0, q_rows, g * LANES:(g + 1) * LANES] = o2.T.astype(BF16)
        return carry

    lax.fori_loop(0, seq // TK, q_block, 0)


def _fox_attn(qkv3, kbias, gq, gk):
    batch, seq, _ = qkv3.shape
    blk = lambda s: pl.BlockSpec((1, seq, D_SEC), lambda b, s=s: (b, 0, s))
    vec = pl.BlockSpec((1, LANES), lambda b: (0, 0))
    return pl.pallas_call(
        functools.partial(_fox_kernel, seq=seq),
        grid=(batch,),
        in_specs=[blk(3), blk(4), blk(5), pl.BlockSpec((1, seq, D_SEC), lambda b: (b, 0, 0)), vec, vec],
        out_specs=pl.BlockSpec((1, seq, D_SEC), lambda b: (b, 0, 0)),
        out_shape=jax.ShapeDtypeStruct((batch, seq, D_SEC), BF16),
        scratch_shapes=[
            pltpu.VMEM((seq // TK, N_HEADS * LANES, TK), BF16),
            pltpu.VMEM((N_HEADS, seq, LANES), BF16),
            pltpu.VMEM((seq // TK, D_SEC, TK), BF16),
            pltpu.VMEM((N_HEADS, HEAD_DIM, TK), F32),
            pltpu.VMEM((TK, TK), F32),
        ],
        compiler_params=_cparams("parallel"),
        name="fox_attn",
    )(qkv3, qkv3, qkv3, kbias, gq, gk)


def _out_router_kernel(oa_ref, ob_ref, x_ref, wo_ref, g_ref, wr_ref, br_ref,
                       h1_ref, msort_ref, route_ref, cnt_ref, wo_s, pick_s, mn_s):
    tm = TM_TOK

    @pl.when(pl.program_id(0) == 0)
    def _():
        wo_s[...] = wo_ref[...].astype(BF16)
        pick_s[...] = jnp.zeros_like(pick_s)
        mn_s[...] = jnp.zeros_like(mn_s)

    h1 = (x_ref[...]
          + jnp.dot(oa_ref[...], wo_s[0:D_SEC, :], preferred_element_type=F32)
          + jnp.dot(ob_ref[...], wo_s[D_SEC:2 * D_SEC, :], preferred_element_type=F32))
    h1_ref[...] = h1
    ms = jnp.mean(h1 * h1, axis=-1, keepdims=True)
    mn = h1 * lax.rsqrt(ms + EPS) * g_ref[...]

    logits = jnp.dot(mn.astype(BF16), wr_ref[...], preferred_element_type=F32) + br_ref[...]

    msort_ref[...] = jnp.dot(pick_s[...], mn_s[...], preferred_element_type=F32).astype(BF16)

    lane = lax.broadcasted_iota(jnp.int32, (tm, LANES), 1).astype(F32)
    big = float(LANES)

    def first_argmax(vals):
        vmax = jnp.max(vals, axis=-1, keepdims=True)
        idx = jnp.min(jnp.where(vals == vmax, lane, big), axis=-1, keepdims=True)
        return vmax, idx

    lg = jnp.where(lane < N_GROUPS, logits, -jnp.inf)
    gmax, gidx = first_argmax(lg)
    gw = 1.0 / jnp.sum(jnp.exp(lg - gmax), axis=-1, keepdims=True)
    lo = ROUTER_LANE0 + EXPERTS_PER_GROUP * gidx
    le = jnp.where((lane >= lo) & (lane < lo + EXPERTS_PER_GROUP), logits, -jnp.inf)
    v0, i0 = first_argmax(le)
    v1, i1 = first_argmax(jnp.where(lane == i0, -jnp.inf, le))
    ex = jnp.exp(v1 - v0)
    w0 = gw / (1.0 + ex)
    w1 = gw * ex / (1.0 + ex)

    sel0 = lane == i0
    sel1 = lane == i1
    onehot = jnp.where(sel0 | sel1, 1.0, 0.0)
    r = lax.broadcasted_iota(jnp.int32, (tm, tm), 0)
    c = lax.broadcasted_iota(jnp.int32, (tm, tm), 1)
    ltri = (c <= r).astype(BF16)
    incl = jnp.dot(ltri, onehot.astype(BF16), preferred_element_type=F32)
    excl = incl - onehot
    counts = incl[tm - 1:tm, :]
    cnt_ref[0] = jnp.broadcast_to(counts, (SUBLANES, LANES))

    seg_rows = jnp.floor((counts + (SEG_ALIGN - 1.0)) * (1.0 / SEG_ALIGN)) * SEG_ALIGN
    seg_rows = jnp.broadcast_to(seg_rows, (tm, LANES))
    lp0 = jnp.sum(jnp.where(lane < i0, seg_rows, 0.0) + jnp.where(sel0, excl, 0.0), axis=-1, keepdims=True)
    lp1 = jnp.sum(jnp.where(lane < i1, seg_rows, 0.0) + jnp.where(sel1, excl, 0.0), axis=-1, keepdims=True)

    rec = jnp.zeros((tm, LANES), F32)
    for pos, val in enumerate((i0 - ROUTER_LANE0, i1 - ROUTER_LANE0, lp0, lp1, w0, w1)):
        rec = jnp.where(lane == float(pos), val, rec)
    route_ref[...] = rec[:, 0:SUBLANES]

    rec_t = rec.T
    out_row = lax.broadcasted_iota(jnp.int32, (SORT_ROWS, tm), 0).astype(F32)
    pick_s[...] = jnp.where((out_row == rec_t[2:3, :]) | (out_row == rec_t[3:4, :]), 1.0, 0.0).astype(BF16)
    mn_s[...] = mn.astype(BF16)


def _out_router(oa, ob, x2, w_o, g_ffn, w_r, b_r):
    T = x2.shape[0]
    tm = TM_TOK
    last = T // tm - 1
    row = lambda w: pl.BlockSpec((tm, w), lambda i: (jnp.minimum(i, last), 0))
    const = lambda a, b: pl.BlockSpec((a, b), lambda i: (0, 0))
    return pl.pallas_call(
        _out_router_kernel,
        grid=(T // tm + 1,),
        in_specs=[row(D_SEC), row(D_SEC), row(D_MODEL), _resident((D_MODEL, D_MODEL)), const(1, D_MODEL),
                  const(D_MODEL, LANES), const(1, LANES)],
        out_specs=[row(D_MODEL), pl.BlockSpec((SORT_ROWS, D_MODEL), lambda i: (jnp.maximum(i - 1, 0), 0)),
                   row(SUBLANES), pl.BlockSpec((1, SUBLANES, LANES), lambda i: (jnp.minimum(i, last), 0, 0))],
        out_shape=[
            jax.ShapeDtypeStruct((T, D_MODEL), F32),
            jax.ShapeDtypeStruct((T // tm * SORT_ROWS, D_MODEL), BF16),
            jax.ShapeDtypeStruct((T, SUBLANES), F32),
            jax.ShapeDtypeStruct((T // tm, SUBLANES, LANES), F32),
        ],
        scratch_shapes=[pltpu.VMEM((D_MODEL, D_MODEL), BF16), pltpu.VMEM((SORT_ROWS, tm), BF16),
                        pltpu.VMEM((tm, D_MODEL), BF16)],
        compiler_params=_cparams("arbitrary"),
        name="out_router",
    )(oa, ob, x2, w_o, g_ffn, w_r, b_r)


MOE_CHUNKS = TM_MOE // SEG_ALIGN


def _chunk_row(c):
    return pl.multiple_of(c * SEG_ALIGN, SEG_ALIGN)


def _moe_kernel(tile_e_ref, nused_ref, src_ref, next_e_ref, slot_e_ref, msort_ref, wup_ref, wdn_ref, ys_ref,
                x_s, wup_f, wdn_f, wup_s, wdn_s, xsem, wsem):
    i = pl.program_id(0)
    nused = nused_ref[0]

    def gather(k, act):
        def chunk(c, carry):
            src = pl.multiple_of(src_ref[k * MOE_CHUNKS + c], SEG_ALIGN)
            act(pltpu.make_async_copy(msort_ref.at[pl.ds(src, SEG_ALIGN)],
                                      x_s.at[k % 2, pl.ds(_chunk_row(c), SEG_ALIGN)], xsem.at[k % 2]))
            return carry
        lax.fori_loop(0, MOE_CHUNKS, chunk, 0, unroll=8)

    def weights(e, s, act):
        act(pltpu.make_async_copy(wup_ref.at[e], wup_f.at[s], wsem.at[0, s]))
        act(pltpu.make_async_copy(wdn_ref.at[e], wdn_f.at[s], wsem.at[1, s]))

    @pl.when(i == 0)
    def _():
        gather(0, lambda c: c.start())
        weights(tile_e_ref[0], 0, lambda c: c.start())

    @pl.when(i + 1 < nused)
    def _():
        gather(i + 1, lambda c: c.start())

    @pl.when(i < nused)
    def _():
        e = tile_e_ref[i]

        @pl.when((i == 0) | (tile_e_ref[jnp.maximum(i - 1, 0)] != e))
        def _():
            s = slot_e_ref[e]
            weights(e, s, lambda c: c.wait())

            @pl.when(next_e_ref[e] >= 0)
            def _():
                weights(next_e_ref[e], 1 - s, lambda c: c.start())
            wup_s[...] = wup_f[s].astype(BF16)
            wdn_s[...] = wdn_f[s].astype(BF16)

        gather(i, lambda c: c.wait())
        hu = jnp.dot(x_s[i % 2], wup_s[...], preferred_element_type=F32)
        gate = hu[:, :D_EXPERT]
        hid = gate * (1.0 / (1.0 + jnp.exp(-gate))) * hu[:, D_EXPERT:]
        ys_ref[...] = jnp.dot(hid.astype(BF16), wdn_s[...], preferred_element_type=F32).astype(BF16)

    @pl.when(i >= nused)
    def _():
        ys_ref[...] = jnp.zeros_like(ys_ref)


def _moe_experts(tile_e, nused, src_chunk, next_e, slot_e, msort, w_up, w_down):
    n_tiles = tile_e.shape[0]
    grid_spec = pltpu.PrefetchScalarGridSpec(
        num_scalar_prefetch=5,
        grid=(n_tiles,),
        in_specs=[pl.BlockSpec(memory_space=pl.ANY)] * 3,
        out_specs=pl.BlockSpec((TM_MOE, D_MODEL), lambda i, *_: (i, 0)),
        scratch_shapes=[
            pltpu.VMEM((2, TM_MOE, D_MODEL), BF16),
            pltpu.VMEM((2, D_MODEL, 2 * D_EXPERT), F32), pltpu.VMEM((2, D_EXPERT, D_MODEL), F32),
            pltpu.VMEM((D_MODEL, 2 * D_EXPERT), BF16), pltpu.VMEM((D_EXPERT, D_MODEL), BF16),
            pltpu.SemaphoreType.DMA((2,)), pltpu.SemaphoreType.DMA((2, 2)),
        ],
    )
    return pl.pallas_call(
        _moe_kernel,
        grid_spec=grid_spec,
        out_shape=jax.ShapeDtypeStruct((n_tiles * TM_MOE, D_MODEL), BF16),
        compiler_params=_cparams("arbitrary"),
        name="moe_experts",
    )(tile_e, nused, src_chunk, next_e, slot_e, msort, w_up, w_down)


def _combine_kernel(src_ref, ys_ref, h1_ref, route_ref, p_ref,
                    g_ref, wg_ref, wp_ref, o_ref, gath_s, sems, wg_s, wp_s):
    i = pl.program_id(0)
    n_steps = pl.num_programs(0)

    @pl.when(i == 0)
    def _():
        wg_s[...] = wg_ref[...].astype(BF16)
        wp_s[...] = wp_ref[...].astype(BF16)
    tm = TM_TOK
    slot = i % 2

    def gather(t, s, act):
        for c in range(SORT_CHUNKS):
            src = pl.multiple_of(src_ref[t * SORT_CHUNKS + c], SEG_ALIGN)
            act(pltpu.make_async_copy(ys_ref.at[pl.ds(src, SEG_ALIGN)],
                                      gath_s.at[s, pl.ds(c * SEG_ALIGN, SEG_ALIGN)], sems.at[s]))

    @pl.when(i == 0)
    def _():
        gather(0, 0, lambda c: c.start())

    nxt = jnp.minimum(i + 1, n_steps - 1)
    gather(i, slot, lambda c: c.wait())
    gather(nxt, 1 - slot, lambda c: c.start())

    route = route_ref[...]
    col = lax.broadcasted_iota(jnp.int32, (tm, SORT_ROWS), 1).astype(F32)
    weights = (jnp.where(col == route[:, 2:3], route[:, 4:5], 0.0)
               + jnp.where(col == route[:, 3:4], route[:, 5:6], 0.0)).astype(BF16)
    y = jnp.dot(weights, gath_s[slot], preferred_element_type=F32)
    h2 = h1_ref[...] + y
    ms = jnp.mean(h2 * h2, axis=-1, keepdims=True)
    n = (h2 * lax.rsqrt(ms + EPS) * g_ref[...]).astype(BF16)
    z = jnp.dot(n, wg_s[...], preferred_element_type=F32)
    gate = 1.0 / (1.0 + jnp.exp(-z))
    ple = jnp.dot(p_ref[...].astype(BF16), wp_s[...], preferred_element_type=F32)
    o_ref[...] = h2 + gate * ple

    @pl.when(i == n_steps - 1)
    def _():
        gather(nxt, 1 - slot, lambda c: c.wait())


def _combine_ple(chunk_dst, ys, h1, route, p2, g_ple, w_gate, w_proj):
    T = h1.shape[0]
    tm = TM_TOK
    row = lambda w: pl.BlockSpec((tm, w), lambda i, *_: (i, 0))
    const = lambda a, b: pl.BlockSpec((a, b), lambda i, *_: (0, 0))
    grid_spec = pltpu.PrefetchScalarGridSpec(
        num_scalar_prefetch=1,
        grid=(T // tm,),
        in_specs=[pl.BlockSpec(memory_space=pl.ANY), row(D_MODEL), row(SUBLANES), row(PLE_DIM),
                  const(1, D_MODEL), _resident((D_MODEL, D_MODEL)), _resident((PLE_DIM, D_MODEL))],
        out_specs=row(D_MODEL),
        scratch_shapes=[pltpu.VMEM((2, SORT_ROWS, D_MODEL), BF16), pltpu.SemaphoreType.DMA((2,)),
                        pltpu.VMEM((D_MODEL, D_MODEL), BF16), pltpu.VMEM((PLE_DIM, D_MODEL), BF16)],
    )
    return pl.pallas_call(
        _combine_kernel,
        grid_spec=grid_spec,
        out_shape=jax.ShapeDtypeStruct((T, D_MODEL), F32),
        compiler_params=_cparams("arbitrary"),
        name="combine_ple",
    )(chunk_dst, ys, h1, route, p2, g_ple, w_gate, w_proj)


def _rope_tables(positions):
    half = ROT_DIM // 2
    inv = ROPE_THETA ** (-jnp.arange(0, ROT_DIM, 2, dtype=F32) / ROT_DIM)
    ang = positions.astype(F32)[:, None, :] * inv[None, :, None]
    lane = jnp.arange(LANES) % HEAD_DIM
    freq = jnp.arange(half)[:, None]
    rot = (lane[None, :] < ROT_DIM) & (lane[None, :] % half == freq)
    e_cos = rot.astype(F32)
    e_sin = jnp.where(rot, jnp.where(lane[None, :] < half, -1.0, 1.0), 0.0).astype(F32)
    expand = functools.partial(jnp.einsum, "bfs,fl->bsl", precision=lax.Precision.HIGHEST)
    return expand(jnp.cos(ang), e_cos) + (lane >= ROT_DIM).astype(F32), expand(jnp.sin(ang), e_sin)


def _layer(i, h, p, cos_t, sin_t, g_mix, w_in, b_f, qn_a, kn_a, qn_b, kn_b, w_o, g_ffn, w_rg, b_rg,
           w_re, b_re, w_up, w_down, g_ple, w_ple_gate, w_ple_proj):
    B, S, _ = h.shape
    T = B * S
    x2 = h.reshape(T, D_MODEL)

    bf = jnp.zeros((1, LANES), F32).at[0, :N_HEADS].set(b_f[i])
    pair = lambda g: jnp.tile(g, 2).reshape(1, LANES)
    w_r = jnp.zeros((D_MODEL, LANES), F32)
    w_r = w_r.at[:, :N_GROUPS].set(w_rg[i])
    w_r = w_r.at[:, ROUTER_LANE0:ROUTER_LANE0 + N_EXPERTS].set(
        jnp.transpose(w_re[i], (1, 0, 2)).reshape(D_MODEL, N_EXPERTS))
    b_r = jnp.zeros((1, LANES), F32).at[0, :N_GROUPS].set(b_rg[i])
    b_r = b_r.at[0, ROUTER_LANE0:ROUTER_LANE0 + N_EXPERTS].set(b_re[i].reshape(-1))

    qkv, f = _in_proj(x2, g_mix[i].reshape(1, -1), w_in, i)
    kbias = _forget_scan(f, bf, B, S)
    qkv3 = qkv.reshape(B, S, N_QKV)
    oa = _dilated_attn(qkv3, cos_t, sin_t, pair(qn_a[i]), pair(kn_a[i]))
    ob = _fox_attn(qkv3, kbias, pair(qn_b[i]), pair(kn_b[i]))

    h1, msort, route, cnt = _out_router(oa.reshape(T, D_SEC), ob.reshape(T, D_SEC), x2, w_o[i],
                                        g_ffn[i].reshape(1, -1), w_r.astype(BF16), b_r)

    n_tok_tiles = T // TM_TOK
    counts = cnt[:, 0, ROUTER_LANE0:ROUTER_LANE0 + N_EXPERTS].astype(jnp.int32)
    seg_rows = (counts + SEG_ALIGN - 1) // SEG_ALIGN * SEG_ALIGN
    seg_local = jnp.cumsum(seg_rows, axis=1) - seg_rows
    rows_e = jnp.sum(seg_rows, axis=0)
    tile_end = jnp.cumsum((rows_e + TM_MOE - 1) // TM_MOE)
    off = jnp.concatenate([jnp.zeros((1,), jnp.int32), tile_end * TM_MOE]).astype(jnp.int32)
    seg_global = off[None, :N_EXPERTS] + jnp.cumsum(seg_rows, axis=0) - seg_rows
    n_tiles = (2 * T + n_tok_tiles * N_EXPERTS * (SEG_ALIGN - 1)) // TM_MOE + N_EXPERTS
    nused = tile_end[-1:].astype(jnp.int32)
    tile_ids = jnp.minimum(jnp.arange(n_tiles, dtype=jnp.int32), nused[0] - 1)
    tile_e = jnp.sum((tile_ids[:, None] >= tile_end[None, :]).astype(jnp.int32), axis=1).astype(jnp.int32)
    chunk_row = jnp.arange(SORT_CHUNKS, dtype=jnp.int32)[None, :, None] * SEG_ALIGN
    lo, hi = seg_local[:, None, :], (seg_local + seg_rows)[:, None, :]
    chunk_dst = jnp.sum(jnp.where((chunk_row >= lo) & (chunk_row < hi), seg_global[:, None, :] + chunk_row - lo, 0),
                        axis=2).reshape(-1).astype(jnp.int32)
    sorted_row = jnp.arange(n_tiles * MOE_CHUNKS, dtype=jnp.int32)[:, None] * SEG_ALIGN
    seg_lo = seg_global.reshape(1, -1)
    seg_src = (jnp.arange(n_tok_tiles, dtype=jnp.int32)[:, None] * SORT_ROWS + seg_local).reshape(1, -1)
    in_seg = (sorted_row >= seg_lo) & (sorted_row < seg_lo + seg_rows.reshape(1, -1))
    src_chunk = jnp.where(jnp.any(in_seg, axis=1),
                          jnp.sum(jnp.where(in_seg, seg_src + sorted_row - seg_lo, 0), axis=1),
                          SORT_ROWS - SEG_ALIGN).astype(jnp.int32)
    e_ids = jnp.arange(N_EXPERTS, dtype=jnp.int32)
    active = rows_e > 0
    later_active = jnp.where(active[None, :] & (e_ids[None, :] > e_ids[:, None]), e_ids[None, :], N_EXPERTS)
    next_e = jnp.min(later_active, axis=1)
    next_e = jnp.where(next_e == N_EXPERTS, -1, next_e).astype(jnp.int32)
    slot_e = ((jnp.cumsum(active.astype(jnp.int32)) - 1) % 2).astype(jnp.int32)

    ys = _moe_experts(tile_e, nused, src_chunk, next_e, slot_e, msort, w_up[i], w_down[i])
    out = _combine_ple(chunk_dst, ys, h1, route, p[i].reshape(T, PLE_DIM),
                       g_ple[i].reshape(1, -1), w_ple_gate[i], w_ple_proj[i])
    return out.reshape(B, S, D_MODEL)


def kernel(x, p, positions, g_mix, w_in, b_f, qn_a, kn_a, qn_b, kn_b, w_o, g_ffn, w_rg, b_rg, w_re, b_re,
           w_up, w_down, g_ple, w_ple_gate, w_ple_proj):
    cos_t, sin_t = _rope_tables(positions)
    h = x
    for i in range(p.shape[0]):
        h = _layer(i, h, p, cos_t, sin_t, g_mix, w_in, b_f, qn_a, kn_a, qn_b, kn_b, w_o, g_ffn, w_rg, b_rg,
                   w_re, b_re, w_up, w_down, g_ple, w_ple_gate, w_ple_proj)
    return h
```

```python
import functools
import math

import jax
import jax.numpy as jnp
from jax import lax
from jax.experimental import pallas as pl
from jax.experimental.pallas import tpu as pltpu

D_MODEL = 1024
HEAD_DIM = 64
N_HEADS = 8
D_SEC = N_HEADS * HEAD_DIM
N_QKV = 6 * D_SEC
ROT_DIM = HEAD_DIM // 4
ROPE_THETA = 500000.0
N_GROUPS = 4
EXPERTS_PER_GROUP = 8
N_EXPERTS = N_GROUPS * EXPERTS_PER_GROUP
D_EXPERT = 512
PLE_DIM = 256
EPS = 1e-6
NEG = -1e30
WINDOW = 128

LANES = 128
SUBLANES = 8
VMEM_LIMIT = 48 * 1024 * 1024

TM_PROJ = 512
TQ = 128
TK = 256
N_PAIRS = N_HEADS // 2
TM_MOE = 512
TM_TOK = 512
SEG_ALIGN = 2 * SUBLANES
SORT_ROWS = 2 * TM_TOK + N_EXPERTS * SEG_ALIGN
SORT_CHUNKS = SORT_ROWS // SEG_ALIGN
ROUTER_LANE0 = N_GROUPS

Q_SCALE_LOG2 = math.log2(math.e) / math.sqrt(HEAD_DIM)

F32 = jnp.float32
BF16 = jnp.bfloat16
NT_DIMS = (((1,), (1,)), ((), ()))


def _cparams(*sem):
    return pltpu.CompilerParams(dimension_semantics=sem, vmem_limit_bytes=VMEM_LIMIT)


def _resident(shape):
    return pl.BlockSpec(shape, lambda i, *_: (0,) * len(shape), pipeline_mode=pl.Buffered(1))


def _in_proj_kernel(x_ref, g_ref, w_ref, qkv_ref, f_ref, w_s, wf_s):
    @pl.when(pl.program_id(0) == 0)
    def _():
        w_s[...] = w_ref[0, :, 0:N_QKV].astype(BF16)
        wf_s[...] = jnp.zeros_like(wf_s)
        wf_s[:, 0:N_HEADS] = w_ref[0, :, N_QKV:N_QKV + N_HEADS].astype(BF16)

    x = x_ref[...]
    ms = jnp.mean(x * x, axis=-1, keepdims=True)
    a = (x * lax.rsqrt(ms + EPS) * g_ref[...]).astype(BF16)
    qkv_ref[...] = jnp.dot(a, w_s[...], preferred_element_type=F32).astype(BF16)
    f_ref[...] = jnp.dot(a, wf_s[...], preferred_element_type=F32)


def _in_proj(x2, g_mix, w_in, layer):
    T = x2.shape[0]
    w_block = (1,) + w_in.shape[1:]
    return pl.pallas_call(
        _in_proj_kernel,
        grid=(T // TM_PROJ,),
        in_specs=[
            pl.BlockSpec((TM_PROJ, D_MODEL), lambda i: (i, 0)),
            pl.BlockSpec((1, D_MODEL), lambda i: (0, 0)),
            pl.BlockSpec(w_block, lambda i: (layer, 0, 0), pipeline_mode=pl.Buffered(1)),
        ],
        out_specs=[
            pl.BlockSpec((TM_PROJ, N_QKV), lambda i: (i, 0)),
            pl.BlockSpec((TM_PROJ, LANES), lambda i: (i, 0)),
        ],
        out_shape=[
            jax.ShapeDtypeStruct((T, N_QKV), BF16),
            jax.ShapeDtypeStruct((T, LANES), F32),
        ],
        scratch_shapes=[pltpu.VMEM((D_MODEL, N_QKV), BF16), pltpu.VMEM((D_MODEL, LANES), BF16)],
        compiler_params=_cparams("arbitrary"),
        name="in_proj",
    )(x2, g_mix, w_in)


BIAS_TERMS = 3


def _bias_lane(head):
    return (head // 2) * LANES + (HEAD_DIM if head % 2 == 0 else 0)


def _forget_scan_kernel(f_ref, bf_ref, kb_ref, *, seq):
    f = f_ref[...] + bf_ref[...]
    logf = jnp.minimum(f, 0.0) - jnp.log1p(jnp.exp(-jnp.abs(f)))

    def split_bf16(v):
        terms = []
        for _ in range(BIAS_TERMS):
            t = v.astype(BF16)
            terms.append(t)
            v = v - t.astype(F32)
        return terms

    tri = (lax.broadcasted_iota(jnp.int32, (TK, TK), 1) <= lax.broadcasted_iota(jnp.int32, (TK, TK), 0)).astype(BF16)
    carry = jnp.zeros((1, LANES), F32)
    chunks = []
    for j in range(seq // TK):
        terms = jnp.concatenate(split_bf16(logf[j * TK:(j + 1) * TK, :]), axis=1)
        pre = jnp.dot(tri, terms, preferred_element_type=F32)
        cj = carry + sum(pre[:, t * LANES:(t + 1) * LANES] for t in range(BIAS_TERMS))
        carry = cj[TK - 1:TK, :]
        chunks.append(cj)
    c = jnp.concatenate(chunks, axis=0)
    rest = c * (-math.log2(math.e))
    r_idx = lax.broadcasted_iota(jnp.int32, (LANES, D_SEC), 0)
    c_idx = lax.broadcasted_iota(jnp.int32, (LANES, D_SEC), 1)
    base = (r_idx // 2) * LANES + jnp.where(r_idx % 2 == 0, HEAD_DIM, 0)
    out = jnp.zeros((seq, D_SEC), F32)
    for t, term in enumerate(split_bf16(rest)):
        place = ((r_idx < N_HEADS) & (c_idx == base + t)).astype(BF16)
        out = out + jnp.dot(term, place, preferred_element_type=F32)
    kb_ref[0] = out.astype(BF16)


def _forget_scan(f, bf, batch, seq):
    return pl.pallas_call(
        functools.partial(_forget_scan_kernel, seq=seq),
        grid=(batch,),
        in_specs=[
            pl.BlockSpec((seq, LANES), lambda b: (b, 0)),
            pl.BlockSpec((1, LANES), lambda b: (0, 0)),
        ],
        out_specs=pl.BlockSpec((1, seq, D_SEC), lambda b: (b, 0, 0)),
        out_shape=jax.ShapeDtypeStruct((batch, seq, D_SEC), BF16),
        compiler_params=_cparams("parallel"),
        name="forget_scan",
    )(f, bf)


def _head_sumsq_matrix():
    r = lax.broadcasted_iota(jnp.int32, (LANES, LANES), 0) // HEAD_DIM
    c = lax.broadcasted_iota(jnp.int32, (LANES, LANES), 1) // HEAD_DIM
    return (r == c).astype(BF16)


def _qk_norm(x, gain, gmat):
    ss = jnp.dot((x * x).astype(BF16), gmat, preferred_element_type=F32)
    return x * lax.rsqrt(ss * (1.0 / HEAD_DIM) + EPS) * gain


def _stack_heads(qb, head0):
    zero = jnp.zeros_like(qb)
    return jnp.concatenate([jnp.where(head0, qb, zero), jnp.where(head0, zero, qb)], axis=0)


def _unstack(a, head0):
    return jnp.where(head0, a[:TQ], a[TQ:])


def _dilated_kernel(q_ref, k_ref, v_ref, cos_ref, sin_ref, gq_ref, gk_ref, o_ref,
                    qn_s, kn_s, v_s, acc_s, m_s, l_s, bias_s, *, seq):
    gmat = _head_sumsq_matrix()
    lane = lax.broadcasted_iota(jnp.int32, (TQ, LANES), 1)
    head0 = lane < HEAD_DIM
    chunk = 512
    lane_c = lax.broadcasted_iota(jnp.int32, (chunk, LANES), 1) % HEAD_DIM
    first_half = lane_c < ROT_DIM // 2

    def rope(t, cs, sn):
        partner = jnp.where(first_half, pltpu.roll(t, LANES - ROT_DIM // 2, axis=1),
                            pltpu.roll(t, ROT_DIM // 2, axis=1))
        return t * cs + partner * sn

    for c0 in range(0, seq, chunk):
        rows = pl.ds(c0, chunk)
        cs = cos_ref[0, rows, :]
        sn = sin_ref[0, rows, :]
        qn = _qk_norm(q_ref[0, rows, :].astype(F32), gq_ref[...], gmat)
        qn_s[rows, :] = rope(qn, cs, sn) * Q_SCALE_LOG2
        kn = _qk_norm(k_ref[0, rows, :].astype(F32), gk_ref[...], gmat)
        kn_s[rows, :] = rope(kn, cs, sn)
        v_s[rows, :] = v_ref[0, rows, :].astype(F32)

    u = lax.broadcasted_iota(jnp.int32, (2 * TQ, 2 * TQ), 0) % TQ
    c = lax.broadcasted_iota(jnp.int32, (2 * TQ, 2 * TQ), 1)
    for slot, dist0 in enumerate((0, TQ)):
        dist = dist0 + u - c
        bias_s[slot] = jnp.where((dist >= 0) & (dist <= WINDOW), 0.0, NEG)

    def scores(blk):
        p, q_rows, k_rows, nk, dist0 = blk
        qst = _stack_heads(qn_s[q_rows, :].astype(BF16), head0)
        s = lax.dot_general(qst, kn_s[k_rows, :].astype(BF16), NT_DIMS, preferred_element_type=F32)
        return s + bias_s[dist0 // TQ, :, 0:nk]

    def finish(blk, s):
        p, q_rows, k_rows, nk, dist0 = blk
        m = jnp.max(s, axis=-1, keepdims=True)
        pr = jnp.exp2(s - m)
        l = jnp.sum(pr, axis=-1, keepdims=True)
        acc = jnp.dot(pr.astype(BF16), v_s[k_rows, :].astype(BF16), preferred_element_type=F32)
        acc_s[p, q_rows, :] = _unstack(acc, head0)
        m_s[p, q_rows, :] = _unstack(jnp.broadcast_to(m, (2 * TQ, LANES)), head0)
        l_s[p, q_rows, :] = _unstack(jnp.broadcast_to(l, (2 * TQ, LANES)), head0)

    blocks = []
    for i in range(seq // TQ):
        kb0 = max(i - 1, 0)
        blocks.append((0, pl.ds(i * TQ, TQ), pl.ds(kb0 * TQ, 2 * TQ), 2 * TQ, (i - kb0) * TQ))
    for r in range(4):
        for n in range(seq // (4 * TQ)):
            kb0 = max(n - 1, 0)
            blocks.append((1, pl.ds(4 * TQ * n + r, TQ, stride=4), pl.ds(4 * TQ * kb0 + r, 2 * TQ, stride=4),
                           2 * TQ, (n - kb0) * TQ))
    for r in range(16):
        rows = pl.ds(r, TQ, stride=16)
        blocks.append((2, rows, rows, TQ, 0))

    ahead = 4
    pending = [scores(b) for b in blocks[:ahead]]
    for idx, b in enumerate(blocks):
        s = pending.pop(0)
        if idx + ahead < len(blocks):
            pending.append(scores(blocks[idx + ahead]))
        finish(b, s)

    for c0 in range(0, seq, chunk):
        rows = pl.ds(c0, chunk)
        m0, m1, m2 = m_s[0, rows, :], m_s[1, rows, :], m_s[2, rows, :]
        mm = jnp.maximum(jnp.maximum(m0, m1), m2)
        e0, e1, e2 = jnp.exp2(m0 - mm), jnp.exp2(m1 - mm), jnp.exp2(m2 - mm)
        num = acc_s[0, rows, :] * e0 + acc_s[1, rows, :] * e1 + acc_s[2, rows, :] * e2
        den = l_s[0, rows, :] * e0 + l_s[1, rows, :] * e1 + l_s[2, rows, :] * e2
        o_ref[0, rows, :] = (num / den).astype(BF16)


def _dilated_attn(qkv3, cos_t, sin_t, gq, gk):
    batch, seq, _ = qkv3.shape
    sec = D_SEC // LANES
    blk = lambda s: pl.BlockSpec((1, seq, LANES), lambda b, h, s=s: (b, 0, s * sec + h))
    tab = pl.BlockSpec((1, seq, LANES), lambda b, h: (b, 0, 0))
    vec = pl.BlockSpec((1, LANES), lambda b, h: (0, 0))
    return pl.pallas_call(
        functools.partial(_dilated_kernel, seq=seq),
        grid=(batch, N_PAIRS),
        in_specs=[blk(0), blk(1), blk(2), tab, tab, vec, vec],
        out_specs=pl.BlockSpec((1, seq, LANES), lambda b, h: (b, 0, h)),
        out_shape=jax.ShapeDtypeStruct((batch, seq, D_SEC), BF16),
        scratch_shapes=[
            pltpu.VMEM((seq, LANES), F32), pltpu.VMEM((seq, LANES), F32), pltpu.VMEM((seq, LANES), F32),
            pltpu.VMEM((3, seq, LANES), F32), pltpu.VMEM((3, seq, LANES), F32),
            pltpu.VMEM((3, seq, LANES), F32), pltpu.VMEM((2, 2 * TQ, 2 * TQ), F32),
        ],
        compiler_params=_cparams("parallel", "parallel"),
        name="dilated_attn",
    )(qkv3, qkv3, qkv3, cos_t, sin_t, gq, gk)


def _fox_kernel(q_ref, k_ref, v_ref, kb_ref, gq_ref, gk_ref, o_ref, qat_s, ka_s, vt_s, acc_s, tri_s, *, seq):
    gmat = _head_sumsq_matrix()
    chunk = 512
    lane = lax.broadcasted_iota(jnp.int32, (chunk, LANES), 1)
    low = lane < HEAD_DIM
    sel = jnp.where((lane >= HEAD_DIM) & (lane < HEAD_DIM + BIAS_TERMS), 1.0, 0.0)

    for c0 in range(0, seq, chunk):
        rows = pl.ds(c0, chunk)
        for g in range(N_PAIRS):
            lanes = slice(g * LANES, (g + 1) * LANES)
            qn = _qk_norm(q_ref[0, rows, lanes].astype(F32), gq_ref[...], gmat) * Q_SCALE_LOG2
            for h, qa in ((2 * g, jnp.where(low, qn, sel)),
                          (2 * g + 1, jnp.where(low, pltpu.roll(qn, HEAD_DIM, axis=1), sel))):
                qt = qa.T
                for cc in range(chunk // TK):
                    qat_s[c0 // TK + cc, h * LANES:(h + 1) * LANES, :] = qt[:, cc * TK:(cc + 1) * TK].astype(BF16)
            kn =_qk_norm(k_ref[0, rows, lanes].astype(F32), gk_ref[...], gmat)
            kb = kb_ref[0, rows, lanes].astype(F32)
            ka_s[2 * g, rows, :] = jnp.where(low, kn, kb).astype(BF16)
            ka_s[2 * g + 1, rows, :] = pltpu.roll(jnp.where(low, kb, kn), HEAD_DIM, axis=1).astype(BF16)
            vt = v_ref[0, rows, lanes].astype(F32).T
            for cc in range(chunk // TK):
                vt_s[c0 // TK + cc, lanes, :] = vt[:, cc * TK:(cc + 1) * TK].astype(BF16)

    r = lax.broadcasted_iota(jnp.int32, (TK, TK), 0)
    c = lax.broadcasted_iota(jnp.int32, (TK, TK), 1)
    tri_s[...] = jnp.where(r <= c, 0.0, NEG)

    def q_block(qi, carry):
        q_rows = pl.ds(pl.multiple_of(qi * TK, TK), TK)
        for h in range(N_HEADS):
            acc_s[h] = jnp.zeros((HEAD_DIM, TK), F32)

        def step(j, st, diagonal):
            k_rows = pl.ds(pl.multiple_of(j * TK, TK), TK)

            def scores(h):
                s = jnp.dot(ka_s[h, k_rows, :], qat_s[qi, h * LANES:(h + 1) * LANES, :], preferred_element_type=F32)
                return s + tri_s[...] if diagonal else s

            ahead = 6
            pending = [scores(h) for h in range(ahead)]
            new = []
            for h in range(N_HEADS):
                m, l = st[h]
                s = pending.pop(0)
                if h + ahead < N_HEADS:
                    pending.append(scores(h + ahead))
                m_new = jnp.maximum(m, jnp.max(s, axis=0, keepdims=True))
                alpha = jnp.exp2(m - m_new)
                pr = jnp.exp2(s - m_new)
                l_new = alpha * l + jnp.sum(pr, axis=0, keepdims=True)
                pv = jnp.dot(vt_s[j, h * HEAD_DIM:(h + 1) * HEAD_DIM, :], pr.astype(BF16),
                             preferred_element_type=F32)
                acc_s[h] = alpha * acc_s[h] + pv
                new.append((m_new, l_new))
            return tuple(new)

        init = tuple((jnp.full((1, TK), NEG, F32), jnp.zeros((1, TK), F32)) for _ in range(N_HEADS))
        st = lax.fori_loop(0, qi // 2, lambda jj, st: step(2 * jj + 1, step(2 * jj, st, False), False), init)
        st = lax.cond(qi % 2 == 1, lambda st: step(qi - 1, st, False), lambda st: st, st)
        st = step(qi, st, True)
        for g in range(N_PAIRS):
            o2 = jnp.concatenate([acc_s[2 * g] / st[2 * g][1], acc_s[2 * g + 1] / st[2 * g + 1][1]], axis=0)
            o_ref[0, q_rows, g * LANES:(g + 1) * LANES] = o2.T.astype(BF16)
        return carry

    lax.fori_loop(0, seq // TK, q_block, 0)


def _fox_attn(qkv3, kbias, gq, gk):
    batch, seq, _ = qkv3.shape
    blk = lambda s: pl.BlockSpec((1, seq, D_SEC), lambda b, s=s: (b, 0, s))
    vec = pl.BlockSpec((1, LANES), lambda b: (0, 0))
    return pl.pallas_call(
        functools.partial(_fox_kernel, seq=seq),
        grid=(batch,),
        in_specs=[blk(3), blk(4), blk(5), pl.BlockSpec((1, seq, D_SEC), lambda b: (b, 0, 0)), vec, vec],
        out_specs=pl.BlockSpec((1, seq, D_SEC), lambda b: (b, 0, 0)),
        out_shape=jax.ShapeDtypeStruct((batch, seq, D_SEC), BF16),
        scratch_shapes=[
            pltpu.VMEM((seq // TK, N_HEADS * LANES, TK), BF16),
            pltpu.VMEM((N_HEADS, seq, LANES), BF16),
            pltpu.VMEM((seq // TK, D_SEC, TK), BF16),
            pltpu.VMEM((N_HEADS, HEAD_DIM, TK), F32),
            pltpu.VMEM((TK, TK), F32),
        ],
        compiler_params=_cparams("parallel"),
        name="fox_attn",
    )(qkv3, qkv3, qkv3, kbias, gq, gk)


def _out_router_kernel(oa_ref, ob_ref, x_ref, wo_ref, g_ref, wr_ref, br_ref,
                       h1_ref, msort_ref, route_ref, cnt_ref, wo_s, pick_s, mn_s):
    tm = TM_TOK

    @pl.when(pl.program_id(0) == 0)
    def _():
        wo_s[...] = wo_ref[...].astype(BF16)
        pick_s[...] = jnp.zeros_like(pick_s)
        mn_s[...] = jnp.zeros_like(mn_s)

    h1 = (x_ref[...]
          + jnp.dot(oa_ref[...], wo_s[0:D_SEC, :], preferred_element_type=F32)
          + jnp.dot(ob_ref[...], wo_s[D_SEC:2 * D_SEC, :], preferred_element_type=F32))
    h1_ref[...] = h1
    ms = jnp.mean(h1 * h1, axis=-1, keepdims=True)
    mn = h1 * lax.rsqrt(ms + EPS) * g_ref[...]

    logits = jnp.dot(mn.astype(BF16), wr_ref[...], preferred_element_type=F32) + br_ref[...]

    msort_ref[...] = jnp.dot(pick_s[...], mn_s[...], preferred_element_type=F32).astype(BF16)

    lane = lax.broadcasted_iota(jnp.int32, (tm, LANES), 1).astype(F32)
    big = float(LANES)

    def first_argmax(vals):
        vmax = jnp.max(vals, axis=-1, keepdims=True)
        idx = jnp.min(jnp.where(vals == vmax, lane, big), axis=-1, keepdims=True)
        return vmax, idx

    lg = jnp.where(lane < N_GROUPS, logits, -jnp.inf)
    gmax, gidx = first_argmax(lg)
    gw = 1.0 / jnp.sum(jnp.exp(lg - gmax), axis=-1, keepdims=True)
    lo = ROUTER_LANE0 + EXPERTS_PER_GROUP * gidx
    le = jnp.where((lane >= lo) & (lane < lo + EXPERTS_PER_GROUP), logits, -jnp.inf)
    v0, i0 = first_argmax(le)
    v1, i1 = first_argmax(jnp.where(lane == i0, -jnp.inf, le))
    ex = jnp.exp(v1 - v0)
    w0 = gw / (1.0 + ex)
    w1 = gw * ex / (1.0 + ex)

    sel0 = lane == i0
    sel1 = lane == i1
    onehot = jnp.where(sel0 | sel1, 1.0, 0.0)
    r = lax.broadcasted_iota(jnp.int32, (tm, tm), 0)
    c = lax.broadcasted_iota(jnp.int32, (tm, tm), 1)
    ltri = (c <= r).astype(BF16)
    incl = jnp.dot(ltri, onehot.astype(BF16), preferred_element_type=F32)
    excl = incl - onehot
    counts = incl[tm - 1:tm, :]
    cnt_ref[0] = jnp.broadcast_to(counts, (SUBLANES, LANES))

    seg_rows = jnp.floor((counts + (SEG_ALIGN - 1.0)) * (1.0 / SEG_ALIGN)) * SEG_ALIGN
    seg_rows = jnp.broadcast_to(seg_rows, (tm, LANES))
    lp0 = jnp.sum(jnp.where(lane < i0, seg_rows, 0.0) + jnp.where(sel0, excl, 0.0), axis=-1, keepdims=True)
    lp1 = jnp.sum(jnp.where(lane < i1, seg_rows, 0.0) + jnp.where(sel1, excl, 0.0), axis=-1, keepdims=True)

    rec = jnp.zeros((tm, LANES), F32)
    for pos, val in enumerate((i0 - ROUTER_LANE0, i1 - ROUTER_LANE0, lp0, lp1, w0, w1)):
        rec = jnp.where(lane == float(pos), val, rec)
    route_ref[...] = rec[:, 0:SUBLANES]

    rec_t = rec.T
    out_row = lax.broadcasted_iota(jnp.int32, (SORT_ROWS, tm), 0).astype(F32)
    pick_s[...] = jnp.where((out_row == rec_t[2:3, :]) | (out_row == rec_t[3:4, :]), 1.0, 0.0).astype(BF16)
    mn_s[...] = mn.astype(BF16)


def _out_router(oa, ob, x2, w_o, g_ffn, w_r, b_r):
    T = x2.shape[0]
    tm = TM_TOK
    last = T // tm - 1
    row = lambda w: pl.BlockSpec((tm, w), lambda i: (jnp.minimum(i, last), 0))
    const = lambda a, b: pl.BlockSpec((a, b), lambda i: (0, 0))
    return pl.pallas_call(
        _out_router_kernel,
        grid=(T // tm + 1,),
        in_specs=[row(D_SEC), row(D_SEC), row(D_MODEL), _resident((D_MODEL, D_MODEL)), const(1, D_MODEL),
                  const(D_MODEL, LANES), const(1, LANES)],
        out_specs=[row(D_MODEL), pl.BlockSpec((SORT_ROWS, D_MODEL), lambda i: (jnp.maximum(i - 1, 0), 0)),
                   row(SUBLANES), pl.BlockSpec((1, SUBLANES, LANES), lambda i: (jnp.minimum(i, last), 0, 0))],
        out_shape=[
            jax.ShapeDtypeStruct((T, D_MODEL), F32),
            jax.ShapeDtypeStruct((T // tm * SORT_ROWS, D_MODEL), BF16),
            jax.ShapeDtypeStruct((T, SUBLANES), F32),
            jax.ShapeDtypeStruct((T // tm, SUBLANES, LANES), F32),
        ],
        scratch_shapes=[pltpu.VMEM((D_MODEL, D_MODEL), BF16), pltpu.VMEM((SORT_ROWS, tm), BF16),
                        pltpu.VMEM((tm, D_MODEL), BF16)],
        compiler_params=_cparams("arbitrary"),
        name="out_router",
    )(oa, ob, x2, w_o, g_ffn, w_r, b_r)


MOE_CHUNKS = TM_MOE // SEG_ALIGN


def _chunk_row(c):
    return pl.multiple_of(c * SEG_ALIGN, SEG_ALIGN)


def _moe_kernel(tile_e_ref, nused_ref, src_ref, next_e_ref, slot_e_ref, msort_ref, wup_ref, wdn_ref, ys_ref,
                x_s, wup_f, wdn_f, wup_s, wdn_s, xsem, wsem):
    i = pl.program_id(0)
    nused = nused_ref[0]

    def gather(k, act):
        def chunk(c, carry):
            src = pl.multiple_of(src_ref[k * MOE_CHUNKS + c], SEG_ALIGN)
            act(pltpu.make_async_copy(msort_ref.at[pl.ds(src, SEG_ALIGN)],
                                      x_s.at[k % 2, pl.ds(_chunk_row(c), SEG_ALIGN)], xsem.at[k % 2]))
            return carry
        lax.fori_loop(0, MOE_CHUNKS, chunk, 0, unroll=8)

    def weights(e, s, act):
        act(pltpu.make_async_copy(wup_ref.at[e], wup_f.at[s], wsem.at[0, s]))
        act(pltpu.make_async_copy(wdn_ref.at[e], wdn_f.at[s], wsem.at[1, s]))

    @pl.when(i == 0)
    def _():
        gather(0, lambda c: c.start())
        weights(tile_e_ref[0], 0, lambda c: c.start())

    @pl.when(i + 1 < nused)
    def _():
        gather(i + 1, lambda c: c.start())

    @pl.when(i < nused)
    def _():
        e = tile_e_ref[i]

        @pl.when((i == 0) | (tile_e_ref[jnp.maximum(i - 1, 0)] != e))
        def _():
            s = slot_e_ref[e]
            weights(e, s, lambda c: c.wait())

            @pl.when(next_e_ref[e] >= 0)
            def _():
                weights(next_e_ref[e], 1 - s, lambda c: c.start())
            wup_s[...] = wup_f[s].astype(BF16)
            wdn_s[...] = wdn_f[s].astype(BF16)

        gather(i, lambda c: c.wait())
        hu = jnp.dot(x_s[i % 2], wup_s[...], preferred_element_type=F32)
        gate = hu[:, :D_EXPERT]
        hid = gate * (1.0 / (1.0 + jnp.exp(-gate))) * hu[:, D_EXPERT:]
        ys_ref[...] = jnp.dot(hid.astype(BF16), wdn_s[...], preferred_element_type=F32).astype(BF16)

    @pl.when(i >= nused)
    def _():
        ys_ref[...] = jnp.zeros_like(ys_ref)


def _moe_experts(tile_e, nused, src_chunk, next_e, slot_e, msort, w_up, w_down):
    n_tiles = tile_e.shape[0]
    grid_spec = pltpu.PrefetchScalarGridSpec(
        num_scalar_prefetch=5,
        grid=(n_tiles,),
        in_specs=[pl.BlockSpec(memory_space=pl.ANY)] * 3,
        out_specs=pl.BlockSpec((TM_MOE, D_MODEL), lambda i, *_: (i, 0)),
        scratch_shapes=[
            pltpu.VMEM((2, TM_MOE, D_MODEL), BF16),
            pltpu.VMEM((2, D_MODEL, 2 * D_EXPERT), F32), pltpu.VMEM((2, D_EXPERT, D_MODEL), F32),
            pltpu.VMEM((D_MODEL, 2 * D_EXPERT), BF16), pltpu.VMEM((D_EXPERT, D_MODEL), BF16),
            pltpu.SemaphoreType.DMA((2,)), pltpu.SemaphoreType.DMA((2, 2)),
        ],
    )
    return pl.pallas_call(
        _moe_kernel,
        grid_spec=grid_spec,
        out_shape=jax.ShapeDtypeStruct((n_tiles * TM_MOE, D_MODEL), BF16),
        compiler_params=_cparams("arbitrary"),
        name="moe_experts",
    )(tile_e, nused, src_chunk, next_e, slot_e, msort, w_up, w_down)


def _combine_kernel(src_ref, ys_ref, h1_ref, route_ref, p_ref,
                    g_ref, wg_ref, wp_ref, o_ref, gath_s, sems, wg_s, wp_s, h2_s, n_s):
    i = pl.program_id(0)
    n_steps = pl.num_programs(0)
    last_tile = n_steps - 2

    tm = TM_TOK
    slot = i % 2

    def gather(t, s, act):
        for c in range(SORT_CHUNKS):
            src = pl.multiple_of(src_ref[t * SORT_CHUNKS + c], SEG_ALIGN)
            act(pltpu.make_async_copy(ys_ref.at[pl.ds(src, SEG_ALIGN)],
                                      gath_s.at[s, pl.ds(c * SEG_ALIGN, SEG_ALIGN)], sems.at[s]))

    @pl.when(i == 0)
    def _():
        wg_s[...] = wg_ref[...].astype(BF16)
        wp_s[...] = wp_ref[...].astype(BF16)
        h2_s[...] = jnp.zeros_like(h2_s)
        n_s[...] = jnp.zeros_like(n_s)
        gather(0, 0, lambda c: c.start())

    cur = jnp.minimum(i, last_tile)
    nxt = jnp.minimum(i + 1, last_tile)
    gather(cur, slot, lambda c: c.wait())
    gather(nxt, 1 - slot, lambda c: c.start())

    ple = jnp.dot(p_ref[...].astype(BF16), wp_s[...], preferred_element_type=F32)
    z = jnp.dot(n_s[...], wg_s[...], preferred_element_type=F32)
    o_ref[...] = h2_s[...] + (1.0 / (1.0 + jnp.exp(-z))) * ple

    route = route_ref[...]
    col = lax.broadcasted_iota(jnp.int32, (tm, SORT_ROWS), 1).astype(F32)
    weights = (jnp.where(col == route[:, 2:3], route[:, 4:5], 0.0)
               + jnp.where(col == route[:, 3:4], route[:, 5:6], 0.0)).astype(BF16)
    y = jnp.dot(weights, gath_s[slot], preferred_element_type=F32)
    h2 = h1_ref[...] + y
    ms = jnp.mean(h2 * h2, axis=-1, keepdims=True)
    h2_s[...] = h2
    n_s[...] = (h2 * lax.rsqrt(ms + EPS) * g_ref[...]).astype(BF16)

    @pl.when(i == n_steps - 1)
    def _():
        gather(nxt, 1 - slot, lambda c: c.wait())


def _combine_ple(chunk_dst, ys, h1, route, p2, g_ple, w_gate, w_proj):
    T = h1.shape[0]
    tm = TM_TOK
    last = T // tm - 1
    cur = lambda w: pl.BlockSpec((tm, w), lambda i, *_: (jnp.minimum(i, last), 0))
    prev = lambda w: pl.BlockSpec((tm, w), lambda i, *_: (jnp.maximum(i - 1, 0), 0))
    const = lambda a, b: pl.BlockSpec((a, b), lambda i, *_: (0, 0))
    grid_spec = pltpu.PrefetchScalarGridSpec(
        num_scalar_prefetch=1,
        grid=(T // tm + 1,),
        in_specs=[pl.BlockSpec(memory_space=pl.ANY), cur(D_MODEL), cur(SUBLANES), prev(PLE_DIM),
                  const(1, D_MODEL), _resident((D_MODEL, D_MODEL)), _resident((PLE_DIM, D_MODEL))],
        out_specs=prev(D_MODEL),
        scratch_shapes=[pltpu.VMEM((2, SORT_ROWS, D_MODEL), BF16), pltpu.SemaphoreType.DMA((2,)),
                        pltpu.VMEM((D_MODEL, D_MODEL), BF16), pltpu.VMEM((PLE_DIM, D_MODEL), BF16),
                        pltpu.VMEM((tm, D_MODEL), F32), pltpu.VMEM((tm, D_MODEL), BF16)],
    )
    return pl.pallas_call(
        _combine_kernel,
        grid_spec=grid_spec,
        out_shape=jax.ShapeDtypeStruct((T, D_MODEL), F32),
        compiler_params=_cparams("arbitrary"),
        name="combine_ple",
    )(chunk_dst, ys, h1, route, p2, g_ple, w_gate, w_proj)


def _rope_tables(positions):
    half = ROT_DIM // 2
    inv = ROPE_THETA ** (-jnp.arange(0, ROT_DIM, 2, dtype=F32) / ROT_DIM)
    ang = positions.astype(F32)[:, None, :] * inv[None, :, None]
    lane = jnp.arange(LANES) % HEAD_DIM
    freq = jnp.arange(half)[:, None]
    rot = (lane[None, :] < ROT_DIM) & (lane[None, :] % half == freq)
    e_cos = rot.astype(F32)
    e_sin = jnp.where(rot, jnp.where(lane[None, :] < half, -1.0, 1.0), 0.0).astype(F32)
    expand = functools.partial(jnp.einsum, "bfs,fl->bsl", precision=lax.Precision.HIGHEST)
    return expand(jnp.cos(ang), e_cos) + (lane >= ROT_DIM).astype(F32), expand(jnp.sin(ang), e_sin)


def _layer(i, h, p, cos_t, sin_t, g_mix, w_in, b_f, qn_a, kn_a, qn_b, kn_b, w_o, g_ffn, w_rg, b_rg,
           w_re, b_re, w_up, w_down, g_ple, w_ple_gate, w_ple_proj):
    B, S, _ = h.shape
    T = B * S
    x2 = h.reshape(T, D_MODEL)

    bf = jnp.zeros((1, LANES), F32).at[0, :N_HEADS].set(b_f[i])
    pair = lambda g: jnp.tile(g, 2).reshape(1, LANES)
    w_r = jnp.zeros((D_MODEL, LANES), F32)
    w_r = w_r.at[:, :N_GROUPS].set(w_rg[i])
    w_r = w_r.at[:, ROUTER_LANE0:ROUTER_LANE0 + N_EXPERTS].set(
        jnp.transpose(w_re[i], (1, 0, 2)).reshape(D_MODEL, N_EXPERTS))
    b_r = jnp.zeros((1, LANES), F32).at[0, :N_GROUPS].set(b_rg[i])
    b_r = b_r.at[0, ROUTER_LANE0:ROUTER_LANE0 + N_EXPERTS].set(b_re[i].reshape(-1))

    qkv, f = _in_proj(x2, g_mix[i].reshape(1, -1), w_in, i)
    kbias = _forget_scan(f, bf, B, S)
    qkv3 = qkv.reshape(B, S, N_QKV)
    oa = _dilated_attn(qkv3, cos_t, sin_t, pair(qn_a[i]), pair(kn_a[i]))
    ob = _fox_attn(qkv3, kbias, pair(qn_b[i]), pair(kn_b[i]))

    h1, msort, route, cnt = _out_router(oa.reshape(T, D_SEC), ob.reshape(T, D_SEC), x2, w_o[i],
                                        g_ffn[i].reshape(1, -1), w_r.astype(BF16), b_r)

    n_tok_tiles = T // TM_TOK
    counts = cnt[:, 0, ROUTER_LANE0:ROUTER_LANE0 + N_EXPERTS].astype(jnp.int32)
    seg_rows = (counts + SEG_ALIGN - 1) // SEG_ALIGN * SEG_ALIGN
    seg_local = jnp.cumsum(seg_rows, axis=1) - seg_rows
    rows_e = jnp.sum(seg_rows, axis=0)
    tile_end = jnp.cumsum((rows_e + TM_MOE - 1) // TM_MOE)
    off = jnp.concatenate([jnp.zeros((1,), jnp.int32), tile_end * TM_MOE]).astype(jnp.int32)
    seg_global = off[None, :N_EXPERTS] + jnp.cumsum(seg_rows, axis=0) - seg_rows
    n_tiles = (2 * T + n_tok_tiles * N_EXPERTS * (SEG_ALIGN - 1)) // TM_MOE + N_EXPERTS
    nused = tile_end[-1:].astype(jnp.int32)
    tile_ids = jnp.minimum(jnp.arange(n_tiles, dtype=jnp.int32), nused[0] - 1)
    tile_e = jnp.sum((tile_ids[:, None] >= tile_end[None, :]).astype(jnp.int32), axis=1).astype(jnp.int32)
    chunk_row = jnp.arange(SORT_CHUNKS, dtype=jnp.int32)[None, :, None] * SEG_ALIGN
    lo, hi = seg_local[:, None, :], (seg_local + seg_rows)[:, None, :]
    chunk_dst = jnp.sum(jnp.where((chunk_row >= lo) & (chunk_row < hi), seg_global[:, None, :] + chunk_row - lo, 0),
                        axis=2).reshape(-1).astype(jnp.int32)
    sorted_row = jnp.arange(n_tiles * MOE_CHUNKS, dtype=jnp.int32)[:, None] * SEG_ALIGN
    seg_lo = seg_global.reshape(1, -1)
    seg_src = (jnp.arange(n_tok_tiles, dtype=jnp.int32)[:, None] * SORT_ROWS + seg_local).reshape(1, -1)
    in_seg = (sorted_row >= seg_lo) & (sorted_row < seg_lo + seg_rows.reshape(1, -1))
    src_chunk = jnp.where(jnp.any(in_seg, axis=1),
                          jnp.sum(jnp.where(in_seg, seg_src + sorted_row - seg_lo, 0), axis=1),
                          SORT_ROWS - SEG_ALIGN).astype(jnp.int32)
    e_ids = jnp.arange(N_EXPERTS, dtype=jnp.int32)
    active = rows_e > 0
    later_active = jnp.where(active[None, :] & (e_ids[None, :] > e_ids[:, None]), e_ids[None, :], N_EXPERTS)
    next_e = jnp.min(later_active, axis=1)
    next_e = jnp.where(next_e == N_EXPERTS, -1, next_e).astype(jnp.int32)
    slot_e = ((jnp.cumsum(active.astype(jnp.int32)) - 1) % 2).astype(jnp.int32)

    ys = _moe_experts(tile_e, nused, src_chunk, next_e, slot_e, msort, w_up[i], w_down[i])
    out = _combine_ple(chunk_dst, ys, h1, route, p[i].reshape(T, PLE_DIM),
                       g_ple[i].reshape(1, -1), w_ple_gate[i], w_ple_proj[i])
    return out.reshape(B, S, D_MODEL)


def kernel(x, p, positions, g_mix, w_in, b_f, qn_a, kn_a, qn_b, kn_b, w_o, g_ffn, w_rg, b_rg, w_re, b_re,
           w_up, w_down, g_ple, w_ple_gate, w_ple_proj):
    cos_t, sin_t = _rope_tables(positions)
    h = x
    for i in range(p.shape[0]):
        h = _layer(i, h, p, cos_t, sin_t, g_mix, w_in, b_f, qn_a, kn_a, qn_b, kn_b, w_o, g_ffn, w_rg, b_rg,
                   w_re, b_re, w_up, w_down, g_ple, w_ple_gate, w_ple_proj)
    return h
```

```python
import functools
import math

import jax
import jax.numpy as jnp
from jax import lax
from jax.experimental import pallas as pl
from jax.experimental.pallas import tpu as pltpu

D_MODEL = 1024
HEAD_DIM = 64
N_HEADS = 8
D_SEC = N_HEADS * HEAD_DIM
N_QKV = 6 * D_SEC
ROT_DIM = HEAD_DIM // 4
ROPE_THETA = 500000.0
N_GROUPS = 4
EXPERTS_PER_GROUP = 8
N_EXPERTS = N_GROUPS * EXPERTS_PER_GROUP
D_EXPERT = 512
PLE_DIM = 256
EPS = 1e-6
NEG = -1e30
WINDOW = 128

LANES = 128
SUBLANES = 8
VMEM_LIMIT = 48 * 1024 * 1024

TM_PROJ = 512
TQ = 128
TK = 256
N_PAIRS = N_HEADS // 2
TM_MOE = 256
TM_TOK = 512
SEG_ALIGN = 2 * SUBLANES
SORT_ROWS = 2 * TM_TOK + N_EXPERTS * SEG_ALIGN
SORT_CHUNKS = SORT_ROWS // SEG_ALIGN
ROUTER_LANE0 = N_GROUPS

Q_SCALE_LOG2 = math.log2(math.e) / math.sqrt(HEAD_DIM)

F32 = jnp.float32
BF16 = jnp.bfloat16
NT_DIMS = (((1,), (1,)), ((), ()))


def _cparams(*sem):
    return pltpu.CompilerParams(dimension_semantics=sem, vmem_limit_bytes=VMEM_LIMIT)


def _resident(shape):
    return pl.BlockSpec(shape, lambda i, *_: (0,) * len(shape), pipeline_mode=pl.Buffered(1))


def _in_proj_kernel(x_ref, g_ref, w_ref, qkv_ref, f_ref, w_s, wf_s):
    @pl.when(pl.program_id(0) == 0)
    def _():
        w_s[...] = w_ref[0, :, 0:N_QKV].astype(BF16)
        wf_s[...] = jnp.zeros_like(wf_s)
        wf_s[:, 0:N_HEADS] = w_ref[0, :, N_QKV:N_QKV + N_HEADS].astype(BF16)

    x = x_ref[...]
    ms = jnp.mean(x * x, axis=-1, keepdims=True)
    a = (x * lax.rsqrt(ms + EPS) * g_ref[...]).astype(BF16)
    qkv_ref[...] = jnp.dot(a, w_s[...], preferred_element_type=F32).astype(BF16)
    f_ref[...] = jnp.dot(a, wf_s[...], preferred_element_type=F32)


def _in_proj(x2, g_mix, w_in, layer):
    T = x2.shape[0]
    w_block = (1,) + w_in.shape[1:]
    return pl.pallas_call(
        _in_proj_kernel,
        grid=(T // TM_PROJ,),
        in_specs=[
            pl.BlockSpec((TM_PROJ, D_MODEL), lambda i: (i, 0)),
            pl.BlockSpec((1, D_MODEL), lambda i: (0, 0)),
            pl.BlockSpec(w_block, lambda i: (layer, 0, 0), pipeline_mode=pl.Buffered(1)),
        ],
        out_specs=[
            pl.BlockSpec((TM_PROJ, N_QKV), lambda i: (i, 0)),
            pl.BlockSpec((TM_PROJ, LANES), lambda i: (i, 0)),
        ],
        out_shape=[
            jax.ShapeDtypeStruct((T, N_QKV), BF16),
            jax.ShapeDtypeStruct((T, LANES), F32),
        ],
        scratch_shapes=[pltpu.VMEM((D_MODEL, N_QKV), BF16), pltpu.VMEM((D_MODEL, LANES), BF16)],
        compiler_params=_cparams("arbitrary"),
        name="in_proj",
    )(x2, g_mix, w_in)


BIAS_TERMS = 3


def _bias_lane(head):
    return (head // 2) * LANES + (HEAD_DIM if head % 2 == 0 else 0)


def _forget_scan_kernel(f_ref, bf_ref, kb_ref, *, seq):
    f = f_ref[...] + bf_ref[...]
    logf = jnp.minimum(f, 0.0) - jnp.log1p(jnp.exp(-jnp.abs(f)))

    def split_bf16(v):
        terms = []
        for _ in range(BIAS_TERMS):
            t = v.astype(BF16)
            terms.append(t)
            v = v - t.astype(F32)
        return terms

    tri = (lax.broadcasted_iota(jnp.int32, (TK, TK), 1) <= lax.broadcasted_iota(jnp.int32, (TK, TK), 0)).astype(BF16)
    carry = jnp.zeros((1, LANES), F32)
    chunks = []
    for j in range(seq // TK):
        terms = jnp.concatenate(split_bf16(logf[j * TK:(j + 1) * TK, :]), axis=1)
        pre = jnp.dot(tri, terms, preferred_element_type=F32)
        cj = carry + sum(pre[:, t * LANES:(t + 1) * LANES] for t in range(BIAS_TERMS))
        carry = cj[TK - 1:TK, :]
        chunks.append(cj)
    c = jnp.concatenate(chunks, axis=0)
    rest = c * (-math.log2(math.e))
    r_idx = lax.broadcasted_iota(jnp.int32, (LANES, D_SEC), 0)
    c_idx = lax.broadcasted_iota(jnp.int32, (LANES, D_SEC), 1)
    base = (r_idx // 2) * LANES + jnp.where(r_idx % 2 == 0, HEAD_DIM, 0)
    out = jnp.zeros((seq, D_SEC), F32)
    for t, term in enumerate(split_bf16(rest)):
        place = ((r_idx < N_HEADS) & (c_idx == base + t)).astype(BF16)
        out = out + jnp.dot(term, place, preferred_element_type=F32)
    kb_ref[0] = out.astype(BF16)


def _forget_scan(f, bf, batch, seq):
    return pl.pallas_call(
        functools.partial(_forget_scan_kernel, seq=seq),
        grid=(batch,),
        in_specs=[
            pl.BlockSpec((seq, LANES), lambda b: (b, 0)),
            pl.BlockSpec((1, LANES), lambda b: (0, 0)),
        ],
        out_specs=pl.BlockSpec((1, seq, D_SEC), lambda b: (b, 0, 0)),
        out_shape=jax.ShapeDtypeStruct((batch, seq, D_SEC), BF16),
        compiler_params=_cparams("parallel"),
        name="forget_scan",
    )(f, bf)


def _head_sumsq_matrix():
    r = lax.broadcasted_iota(jnp.int32, (LANES, LANES), 0) // HEAD_DIM
    c = lax.broadcasted_iota(jnp.int32, (LANES, LANES), 1) // HEAD_DIM
    return (r == c).astype(BF16)


def _qk_norm(x, gain, gmat):
    ss = jnp.dot((x * x).astype(BF16), gmat, preferred_element_type=F32)
    return x * lax.rsqrt(ss * (1.0 / HEAD_DIM) + EPS) * gain


def _stack_heads(qb, head0):
    zero = jnp.zeros_like(qb)
    return jnp.concatenate([jnp.where(head0, qb, zero), jnp.where(head0, zero, qb)], axis=0)


def _unstack(a, head0):
    return jnp.where(head0, a[:TQ], a[TQ:])


def _dilated_kernel(q_ref, k_ref, v_ref, cos_ref, sin_ref, gq_ref, gk_ref, o_ref,
                    qn_s, kn_s, v_s, acc_s, m_s, l_s, bias_s, *, seq):
    gmat = _head_sumsq_matrix()
    lane = lax.broadcasted_iota(jnp.int32, (TQ, LANES), 1)
    head0 = lane < HEAD_DIM
    chunk = 512
    lane_c = lax.broadcasted_iota(jnp.int32, (chunk, LANES), 1) % HEAD_DIM
    first_half = lane_c < ROT_DIM // 2

    def rope(t, cs, sn):
        partner = jnp.where(first_half, pltpu.roll(t, LANES - ROT_DIM // 2, axis=1),
                            pltpu.roll(t, ROT_DIM // 2, axis=1))
        return t * cs + partner * sn

    for c0 in range(0, seq, chunk):
        rows = pl.ds(c0, chunk)
        cs = cos_ref[0, rows, :]
        sn = sin_ref[0, rows, :]
        qn = _qk_norm(q_ref[0, rows, :].astype(F32), gq_ref[...], gmat)
        qn_s[rows, :] = rope(qn, cs, sn) * Q_SCALE_LOG2
        kn = _qk_norm(k_ref[0, rows, :].astype(F32), gk_ref[...], gmat)
        kn_s[rows, :] = rope(kn, cs, sn)
        v_s[rows, :] = v_ref[0, rows, :].astype(F32)

    u = lax.broadcasted_iota(jnp.int32, (2 * TQ, 2 * TQ), 0) % TQ
    c = lax.broadcasted_iota(jnp.int32, (2 * TQ, 2 * TQ), 1)
    for slot, dist0 in enumerate((0, TQ)):
        dist = dist0 + u - c
        bias_s[slot] = jnp.where((dist >= 0) & (dist <= WINDOW), 0.0, NEG)

    def scores(blk):
        p, q_rows, k_rows, nk, dist0 = blk
        qst = _stack_heads(qn_s[q_rows, :].astype(BF16), head0)
        s = lax.dot_general(qst, kn_s[k_rows, :].astype(BF16), NT_DIMS, preferred_element_type=F32)
        return s + bias_s[dist0 // TQ, :, 0:nk]

    def finish(blk, s):
        p, q_rows, k_rows, nk, dist0 = blk
        m = jnp.max(s, axis=-1, keepdims=True)
        pr = jnp.exp2(s - m)
        l = jnp.sum(pr, axis=-1, keepdims=True)
        acc = jnp.dot(pr.astype(BF16), v_s[k_rows, :].astype(BF16), preferred_element_type=F32)
        acc_s[p, q_rows, :] = _unstack(acc, head0)
        m_s[p, q_rows, :] = _unstack(jnp.broadcast_to(m, (2 * TQ, LANES)), head0)
        l_s[p, q_rows, :] = _unstack(jnp.broadcast_to(l, (2 * TQ, LANES)), head0)

    blocks = []
    for i in range(seq // TQ):
        kb0 = max(i - 1, 0)
        blocks.append((0, pl.ds(i * TQ, TQ), pl.ds(kb0 * TQ, 2 * TQ), 2 * TQ, (i - kb0) * TQ))
    for r in range(4):
        for n in range(seq // (4 * TQ)):
            kb0 = max(n - 1, 0)
            blocks.append((1, pl.ds(4 * TQ * n + r, TQ, stride=4), pl.ds(4 * TQ * kb0 + r, 2 * TQ, stride=4),
                           2 * TQ, (n - kb0) * TQ))
    for r in range(16):
        rows = pl.ds(r, TQ, stride=16)
        blocks.append((2, rows, rows, TQ, 0))

    ahead = 4
    pending = [scores(b) for b in blocks[:ahead]]
    for idx, b in enumerate(blocks):
        s = pending.pop(0)
        if idx + ahead < len(blocks):
            pending.append(scores(blocks[idx + ahead]))
        finish(b, s)

    for c0 in range(0, seq, chunk):
        rows = pl.ds(c0, chunk)
        m0, m1, m2 = m_s[0, rows, :], m_s[1, rows, :], m_s[2, rows, :]
        mm = jnp.maximum(jnp.maximum(m0, m1), m2)
        e0, e1, e2 = jnp.exp2(m0 - mm), jnp.exp2(m1 - mm), jnp.exp2(m2 - mm)
        num = acc_s[0, rows, :] * e0 + acc_s[1, rows, :] * e1 + acc_s[2, rows, :] * e2
        den = l_s[0, rows, :] * e0 + l_s[1, rows, :] * e1 + l_s[2, rows, :] * e2
        o_ref[0, rows, :] = (num / den).astype(BF16)


def _dilated_attn(qkv3, cos_t, sin_t, gq, gk):
    batch, seq, _ = qkv3.shape
    sec = D_SEC // LANES
    blk = lambda s: pl.BlockSpec((1, seq, LANES), lambda b, h, s=s: (b, 0, s * sec + h))
    tab = pl.BlockSpec((1, seq, LANES), lambda b, h: (b, 0, 0))
    vec = pl.BlockSpec((1, LANES), lambda b, h: (0, 0))
    return pl.pallas_call(
        functools.partial(_dilated_kernel, seq=seq),
        grid=(batch, N_PAIRS),
        in_specs=[blk(0), blk(1), blk(2), tab, tab, vec, vec],
        out_specs=pl.BlockSpec((1, seq, LANES), lambda b, h: (b, 0, h)),
        out_shape=jax.ShapeDtypeStruct((batch, seq, D_SEC), BF16),
        scratch_shapes=[
            pltpu.VMEM((seq, LANES), F32), pltpu.VMEM((seq, LANES), F32), pltpu.VMEM((seq, LANES), F32),
            pltpu.VMEM((3, seq, LANES), F32), pltpu.VMEM((3, seq, LANES), F32),
            pltpu.VMEM((3, seq, LANES), F32), pltpu.VMEM((2, 2 * TQ, 2 * TQ), F32),
        ],
        compiler_params=_cparams("parallel", "parallel"),
        name="dilated_attn",
    )(qkv3, qkv3, qkv3, cos_t, sin_t, gq, gk)


def _fox_kernel(q_ref, k_ref, v_ref, kb_ref, gq_ref, gk_ref, o_ref, qat_s, ka_s, vt_s, acc_s, tri_s, *, seq):
    gmat = _head_sumsq_matrix()
    chunk = 512
    lane = lax.broadcasted_iota(jnp.int32, (chunk, LANES), 1)
    low = lane < HEAD_DIM
    sel = jnp.where((lane >= HEAD_DIM) & (lane < HEAD_DIM + BIAS_TERMS), 1.0, 0.0)

    for c0 in range(0, seq, chunk):
        rows = pl.ds(c0, chunk)
        for g in range(N_PAIRS):
            lanes = slice(g * LANES, (g + 1) * LANES)
            qn = _qk_norm(q_ref[0, rows, lanes].astype(F32), gq_ref[...], gmat) * Q_SCALE_LOG2
            for h, qa in ((2 * g, jnp.where(low, qn, sel)),
                          (2 * g + 1, jnp.where(low, pltpu.roll(qn, HEAD_DIM, axis=1), sel))):
                qt = qa.T
                for cc in range(chunk // TK):
                    qat_s[c0 // TK + cc, h * LANES:(h + 1) * LANES, :] = qt[:, cc * TK:(cc + 1) * TK].astype(BF16)
            kn =_qk_norm(k_ref[0, rows, lanes].astype(F32), gk_ref[...], gmat)
            kb = kb_ref[0, rows, lanes].astype(F32)
            ka_s[2 * g, rows, :] = jnp.where(low, kn, kb).astype(BF16)
            ka_s[2 * g + 1, rows, :] = pltpu.roll(jnp.where(low, kb, kn), HEAD_DIM, axis=1).astype(BF16)
            vt = v_ref[0, rows, lanes].astype(F32).T
            for cc in range(chunk // TK):
                vt_s[c0 // TK + cc, lanes, :] = vt[:, cc * TK:(cc + 1) * TK].astype(BF16)

    r = lax.broadcasted_iota(jnp.int32, (TK, TK), 0)
    c = lax.broadcasted_iota(jnp.int32, (TK, TK), 1)
    tri_s[...] = jnp.where(r <= c, 0.0, NEG)

    def q_block(qi, carry):
        q_rows = pl.ds(pl.multiple_of(qi * TK, TK), TK)
        for h in range(N_HEADS):
            acc_s[h] = jnp.zeros((HEAD_DIM, TK), F32)

        def step(j, st, diagonal):
            k_rows = pl.ds(pl.multiple_of(j * TK, TK), TK)

            def scores(h):
                s = jnp.dot(ka_s[h, k_rows, :], qat_s[qi, h * LANES:(h + 1) * LANES, :], preferred_element_type=F32)
                return s + tri_s[...] if diagonal else s

            ahead = 6
            pending = [scores(h) for h in range(ahead)]
            new = []
            for h in range(N_HEADS):
                m, l = st[h]
                s = pending.pop(0)
                if h + ahead < N_HEADS:
                    pending.append(scores(h + ahead))
                m_new = jnp.maximum(m, jnp.max(s, axis=0, keepdims=True))
                alpha = jnp.exp2(m - m_new)
                pr = jnp.exp2(s - m_new)
                l_new = alpha * l + jnp.sum(pr, axis=0, keepdims=True)
                pv = jnp.dot(vt_s[j, h * HEAD_DIM:(h + 1) * HEAD_DIM, :], pr.astype(BF16),
                             preferred_element_type=F32)
                acc_s[h] = alpha * acc_s[h] + pv
                new.append((m_new, l_new))
            return tuple(new)

        init = tuple((jnp.full((1, TK), NEG, F32), jnp.zeros((1, TK), F32)) for _ in range(N_HEADS))
        st = lax.fori_loop(0, qi // 2, lambda jj, st: step(2 * jj + 1, step(2 * jj, st, False), False), init)
        st = lax.cond(qi % 2 == 1, lambda st: step(qi - 1, st, False), lambda st: st, st)
        st = step(qi, st, True)
        for g in range(N_PAIRS):
            o2 = jnp.concatenate([acc_s[2 * g] / st[2 * g][1], acc_s[2 * g + 1] / st[2 * g + 1][1]], axis=0)
            o_ref[0, q_rows, g * LANES:(g + 1) * LANES] = o2.T.astype(BF16)
        return carry

    lax.fori_loop(0, seq // TK, q_block, 0)


def _fox_attn(qkv3, kbias, gq, gk):
    batch, seq, _ = qkv3.shape
    blk = lambda s: pl.BlockSpec((1, seq, D_SEC), lambda b, s=s: (b, 0, s))
    vec = pl.BlockSpec((1, LANES), lambda b: (0, 0))
    return pl.pallas_call(
        functools.partial(_fox_kernel, seq=seq),
        grid=(batch,),
        in_specs=[blk(3), blk(4), blk(5), pl.BlockSpec((1, seq, D_SEC), lambda b: (b, 0, 0)), vec, vec],
        out_specs=pl.BlockSpec((1, seq, D_SEC), lambda b: (b, 0, 0)),
        out_shape=jax.ShapeDtypeStruct((batch, seq, D_SEC), BF16),
        scratch_shapes=[
            pltpu.VMEM((seq // TK, N_HEADS * LANES, TK), BF16),
            pltpu.VMEM((N_HEADS, seq, LANES), BF16),
            pltpu.VMEM((seq // TK, D_SEC, TK), BF16),
            pltpu.VMEM((N_HEADS, HEAD_DIM, TK), F32),
            pltpu.VMEM((TK, TK), F32),
        ],
        compiler_params=_cparams("parallel"),
        name="fox_attn",
    )(qkv3, qkv3, qkv3, kbias, gq, gk)


def _out_router_kernel(oa_ref, ob_ref, x_ref, wo_ref, g_ref, wr_ref, br_ref,
                       h1_ref, msort_ref, route_ref, cnt_ref, wo_s, pick_s, mn_s):
    tm = TM_TOK

    @pl.when(pl.program_id(0) == 0)
    def _():
        wo_s[...] = wo_ref[...].astype(BF16)
        pick_s[...] = jnp.zeros_like(pick_s)
        mn_s[...] = jnp.zeros_like(mn_s)

    h1 = (x_ref[...]
          + jnp.dot(oa_ref[...], wo_s[0:D_SEC, :], preferred_element_type=F32)
          + jnp.dot(ob_ref[...], wo_s[D_SEC:2 * D_SEC, :], preferred_element_type=F32))
    h1_ref[...] = h1
    ms = jnp.mean(h1 * h1, axis=-1, keepdims=True)
    mn = h1 * lax.rsqrt(ms + EPS) * g_ref[...]

    logits = jnp.dot(mn.astype(BF16), wr_ref[...], preferred_element_type=F32) + br_ref[...]

    msort_ref[...] = jnp.dot(pick_s[...], mn_s[...], preferred_element_type=F32).astype(BF16)

    lane = lax.broadcasted_iota(jnp.int32, (tm, LANES), 1).astype(F32)
    big = float(LANES)

    def first_argmax(vals):
        vmax = jnp.max(vals, axis=-1, keepdims=True)
        idx = jnp.min(jnp.where(vals == vmax, lane, big), axis=-1, keepdims=True)
        return vmax, idx

    lg = jnp.where(lane < N_GROUPS, logits, -jnp.inf)
    gmax, gidx = first_argmax(lg)
    gw = 1.0 / jnp.sum(jnp.exp(lg - gmax), axis=-1, keepdims=True)
    lo = ROUTER_LANE0 + EXPERTS_PER_GROUP * gidx
    le = jnp.where((lane >= lo) & (lane < lo + EXPERTS_PER_GROUP), logits, -jnp.inf)
    v0, i0 = first_argmax(le)
    v1, i1 = first_argmax(jnp.where(lane == i0, -jnp.inf, le))
    ex = jnp.exp(v1 - v0)
    w0 = gw / (1.0 + ex)
    w1 = gw * ex / (1.0 + ex)

    sel0 = lane == i0
    sel1 = lane == i1
    onehot = jnp.where(sel0 | sel1, 1.0, 0.0)
    r = lax.broadcasted_iota(jnp.int32, (tm, tm), 0)
    c = lax.broadcasted_iota(jnp.int32, (tm, tm), 1)
    ltri = (c <= r).astype(BF16)
    incl = jnp.dot(ltri, onehot.astype(BF16), preferred_element_type=F32)
    excl = incl - onehot
    counts = incl[tm - 1:tm, :]
    cnt_ref[0] = jnp.broadcast_to(counts, (SUBLANES, LANES))

    seg_rows = jnp.floor((counts + (SEG_ALIGN - 1.0)) * (1.0 / SEG_ALIGN)) * SEG_ALIGN
    seg_rows = jnp.broadcast_to(seg_rows, (tm, LANES))
    lp0 = jnp.sum(jnp.where(lane < i0, seg_rows, 0.0) + jnp.where(sel0, excl, 0.0), axis=-1, keepdims=True)
    lp1 = jnp.sum(jnp.where(lane < i1, seg_rows, 0.0) + jnp.where(sel1, excl, 0.0), axis=-1, keepdims=True)

    rec = jnp.zeros((tm, LANES), F32)
    for pos, val in enumerate((i0 - ROUTER_LANE0, i1 - ROUTER_LANE0, lp0, lp1, w0, w1)):
        rec = jnp.where(lane == float(pos), val, rec)
    route_ref[...] = rec[:, 0:SUBLANES]

    rec_t = rec.T
    out_row = lax.broadcasted_iota(jnp.int32, (SORT_ROWS, tm), 0).astype(F32)
    pick_s[...] = jnp.where((out_row == rec_t[2:3, :]) | (out_row == rec_t[3:4, :]), 1.0, 0.0).astype(BF16)
    mn_s[...] = mn.astype(BF16)


def _out_router(oa, ob, x2, w_o, g_ffn, w_r, b_r):
    T = x2.shape[0]
    tm = TM_TOK
    last = T // tm - 1
    row = lambda w: pl.BlockSpec((tm, w), lambda i: (jnp.minimum(i, last), 0))
    const = lambda a, b: pl.BlockSpec((a, b), lambda i: (0, 0))
    return pl.pallas_call(
        _out_router_kernel,
        grid=(T // tm + 1,),
        in_specs=[row(D_SEC), row(D_SEC), row(D_MODEL), _resident((D_MODEL, D_MODEL)), const(1, D_MODEL),
                  const(D_MODEL, LANES), const(1, LANES)],
        out_specs=[row(D_MODEL), pl.BlockSpec((SORT_ROWS, D_MODEL), lambda i: (jnp.maximum(i - 1, 0), 0)),
                   row(SUBLANES), pl.BlockSpec((1, SUBLANES, LANES), lambda i: (jnp.minimum(i, last), 0, 0))],
        out_shape=[
            jax.ShapeDtypeStruct((T, D_MODEL), F32),
            jax.ShapeDtypeStruct((T // tm * SORT_ROWS, D_MODEL), BF16),
            jax.ShapeDtypeStruct((T, SUBLANES), F32),
            jax.ShapeDtypeStruct((T // tm, SUBLANES, LANES), F32),
        ],
        scratch_shapes=[pltpu.VMEM((D_MODEL, D_MODEL), BF16), pltpu.VMEM((SORT_ROWS, tm), BF16),
                        pltpu.VMEM((tm, D_MODEL), BF16)],
        compiler_params=_cparams("arbitrary"),
        name="out_router",
    )(oa, ob, x2, w_o, g_ffn, w_r, b_r)


MOE_CHUNKS = TM_MOE // SEG_ALIGN


def _chunk_row(c):
    return pl.multiple_of(c * SEG_ALIGN, SEG_ALIGN)


def _moe_kernel(tile_e_ref, nused_ref, src_ref, next_e_ref, slot_e_ref, msort_ref, wup_ref, wdn_ref, ys_ref,
                x_s, wup_f, wdn_f, wup_s, wdn_s, xsem, wsem):
    i = pl.program_id(0)
    nused = nused_ref[0]

    def gather(k, act):
        def chunk(c, carry):
            src = pl.multiple_of(src_ref[k * MOE_CHUNKS + c], SEG_ALIGN)
            act(pltpu.make_async_copy(msort_ref.at[pl.ds(src, SEG_ALIGN)],
                                      x_s.at[k % 2, pl.ds(_chunk_row(c), SEG_ALIGN)], xsem.at[k % 2]))
            return carry
        lax.fori_loop(0, MOE_CHUNKS, chunk, 0, unroll=8)

    def weights(e, s, act):
        act(pltpu.make_async_copy(wup_ref.at[e], wup_f.at[s], wsem.at[0, s]))
        act(pltpu.make_async_copy(wdn_ref.at[e], wdn_f.at[s], wsem.at[1, s]))

    @pl.when(i == 0)
    def _():
        gather(0, lambda c: c.start())
        weights(tile_e_ref[0], 0, lambda c: c.start())

    @pl.when(i + 1 < nused)
    def _():
        gather(i + 1, lambda c: c.start())

    @pl.when(i < nused)
    def _():
        e = tile_e_ref[i]

        @pl.when((i == 0) | (tile_e_ref[jnp.maximum(i - 1, 0)] != e))
        def _():
            s = slot_e_ref[e]
            weights(e, s, lambda c: c.wait())

            @pl.when(next_e_ref[e] >= 0)
            def _():
                weights(next_e_ref[e], 1 - s, lambda c: c.start())
            wup_s[...] = wup_f[s].astype(BF16)
            wdn_s[...] = wdn_f[s].astype(BF16)

        gather(i, lambda c: c.wait())
        hu = jnp.dot(x_s[i % 2], wup_s[...], preferred_element_type=F32)
        gate = hu[:, :D_EXPERT]
        hid = gate * (1.0 / (1.0 + jnp.exp(-gate))) * hu[:, D_EXPERT:]
        ys_ref[...] = jnp.dot(hid.astype(BF16), wdn_s[...], preferred_element_type=F32).astype(BF16)

    @pl.when(i >= nused)
    def _():
        ys_ref[...] = jnp.zeros_like(ys_ref)


def _moe_experts(tile_e, nused, src_chunk, next_e, slot_e, msort, w_up, w_down):
    n_tiles = tile_e.shape[0]
    grid_spec = pltpu.PrefetchScalarGridSpec(
        num_scalar_prefetch=5,
        grid=(n_tiles,),
        in_specs=[pl.BlockSpec(memory_space=pl.ANY)] * 3,
        out_specs=pl.BlockSpec((TM_MOE, D_MODEL), lambda i, *_: (i, 0)),
        scratch_shapes=[
            pltpu.VMEM((2, TM_MOE, D_MODEL), BF16),
            pltpu.VMEM((2, D_MODEL, 2 * D_EXPERT), F32), pltpu.VMEM((2, D_EXPERT, D_MODEL), F32),
            pltpu.VMEM((D_MODEL, 2 * D_EXPERT), BF16), pltpu.VMEM((D_EXPERT, D_MODEL), BF16),
            pltpu.SemaphoreType.DMA((2,)), pltpu.SemaphoreType.DMA((2, 2)),
        ],
    )
    return pl.pallas_call(
        _moe_kernel,
        grid_spec=grid_spec,
        out_shape=jax.ShapeDtypeStruct((n_tiles * TM_MOE, D_MODEL), BF16),
        compiler_params=_cparams("arbitrary"),
        name="moe_experts",
    )(tile_e, nused, src_chunk, next_e, slot_e, msort, w_up, w_down)


def _combine_kernel(src_ref, ys_ref, h1_ref, route_ref, p_ref,
                    g_ref, wg_ref, wp_ref, o_ref, gath_s, sems, wg_s, wp_s, h2_s, n_s):
    i = pl.program_id(0)
    n_steps = pl.num_programs(0)
    last_tile = n_steps - 2

    tm = TM_TOK
    slot = i % 2

    def gather(t, s, act):
        for c in range(SORT_CHUNKS):
            src = pl.multiple_of(src_ref[t * SORT_CHUNKS + c], SEG_ALIGN)
            act(pltpu.make_async_copy(ys_ref.at[pl.ds(src, SEG_ALIGN)],
                                      gath_s.at[s, pl.ds(c * SEG_ALIGN, SEG_ALIGN)], sems.at[s]))

    @pl.when(i == 0)
    def _():
        wg_s[...] = wg_ref[...].astype(BF16)
        wp_s[...] = wp_ref[...].astype(BF16)
        h2_s[...] = jnp.zeros_like(h2_s)
        n_s[...] = jnp.zeros_like(n_s)
        gather(0, 0, lambda c: c.start())

    cur = jnp.minimum(i, last_tile)
    nxt = jnp.minimum(i + 1, last_tile)
    gather(cur, slot, lambda c: c.wait())
    gather(nxt, 1 - slot, lambda c: c.start())

    ple = jnp.dot(p_ref[...].astype(BF16), wp_s[...], preferred_element_type=F32)
    z = jnp.dot(n_s[...], wg_s[...], preferred_element_type=F32)
    o_ref[...] = h2_s[...] + (1.0 / (1.0 + jnp.exp(-z))) * ple

    route = route_ref[...]
    col = lax.broadcasted_iota(jnp.int32, (tm, SORT_ROWS), 1).astype(F32)
    weights = (jnp.where(col == route[:, 2:3], route[:, 4:5], 0.0)
               + jnp.where(col == route[:, 3:4], route[:, 5:6], 0.0)).astype(BF16)
    y = jnp.dot(weights, gath_s[slot], preferred_element_type=F32)
    h2 = h1_ref[...] + y
    ms = jnp.mean(h2 * h2, axis=-1, keepdims=True)
    h2_s[...] = h2
    n_s[...] = (h2 * lax.rsqrt(ms + EPS) * g_ref[...]).astype(BF16)

    @pl.when(i == n_steps - 1)
    def _():
        gather(nxt, 1 - slot, lambda c: c.wait())


def _combine_ple(chunk_dst, ys, h1, route, p2, g_ple, w_gate, w_proj):
    T = h1.shape[0]
    tm = TM_TOK
    last = T // tm - 1
    cur = lambda w: pl.BlockSpec((tm, w), lambda i, *_: (jnp.minimum(i, last), 0))
    prev = lambda w: pl.BlockSpec((tm, w), lambda i, *_: (jnp.maximum(i - 1, 0), 0))
    const = lambda a, b: pl.BlockSpec((a, b), lambda i, *_: (0, 0))
    grid_spec = pltpu.PrefetchScalarGridSpec(
        num_scalar_prefetch=1,
        grid=(T // tm + 1,),
        in_specs=[pl.BlockSpec(memory_space=pl.ANY), cur(D_MODEL), cur(SUBLANES), prev(PLE_DIM),
                  const(1, D_MODEL), _resident((D_MODEL, D_MODEL)), _resident((PLE_DIM, D_MODEL))],
        out_specs=prev(D_MODEL),
        scratch_shapes=[pltpu.VMEM((2, SORT_ROWS, D_MODEL), BF16), pltpu.SemaphoreType.DMA((2,)),
                        pltpu.VMEM((D_MODEL, D_MODEL), BF16), pltpu.VMEM((PLE_DIM, D_MODEL), BF16),
                        pltpu.VMEM((tm, D_MODEL), F32), pltpu.VMEM((tm, D_MODEL), BF16)],
    )
    return pl.pallas_call(
        _combine_kernel,
        grid_spec=grid_spec,
        out_shape=jax.ShapeDtypeStruct((T, D_MODEL), F32),
        compiler_params=_cparams("arbitrary"),
        name="combine_ple",
    )(chunk_dst, ys, h1, route, p2, g_ple, w_gate, w_proj)


def _rope_tables(positions):
    half = ROT_DIM // 2
    inv = ROPE_THETA ** (-jnp.arange(0, ROT_DIM, 2, dtype=F32) / ROT_DIM)
    ang = positions.astype(F32)[:, None, :] * inv[None, :, None]
    lane = jnp.arange(LANES) % HEAD_DIM
    freq = jnp.arange(half)[:, None]
    rot = (lane[None, :] < ROT_DIM) & (lane[None, :] % half == freq)
    e_cos = rot.astype(BF16)
    e_sin = jnp.where(rot, jnp.where(lane[None, :] < half, -1.0, 1.0), 0.0).astype(BF16)

    def expand(tab, sel):
        hi = tab.astype(BF16)
        lo = (tab - hi.astype(F32)).astype(BF16)
        pick = functools.partial(jnp.einsum, "bfs,fl->bsl", preferred_element_type=F32)
        return pick(hi, sel) + pick(lo, sel)

    return expand(jnp.cos(ang), e_cos) + (lane >= ROT_DIM).astype(F32), expand(jnp.sin(ang), e_sin)


def _layer(i, h, p, cos_t, sin_t, g_mix, w_in, b_f, qn_a, kn_a, qn_b, kn_b, w_o, g_ffn, w_rg, b_rg,
           w_re, b_re, w_up, w_down, g_ple, w_ple_gate, w_ple_proj):
    B, S, _ = h.shape
    T = B * S
    x2 = h.reshape(T, D_MODEL)

    bf = jnp.zeros((1, LANES), F32).at[0, :N_HEADS].set(b_f[i])
    pair = lambda g: jnp.tile(g, 2).reshape(1, LANES)
    w_r = jnp.zeros((D_MODEL, LANES), F32)
    w_r = w_r.at[:, :N_GROUPS].set(w_rg[i])
    w_r = w_r.at[:, ROUTER_LANE0:ROUTER_LANE0 + N_EXPERTS].set(
        jnp.transpose(w_re[i], (1, 0, 2)).reshape(D_MODEL, N_EXPERTS))
    b_r = jnp.zeros((1, LANES), F32).at[0, :N_GROUPS].set(b_rg[i])
    b_r = b_r.at[0, ROUTER_LANE0:ROUTER_LANE0 + N_EXPERTS].set(b_re[i].reshape(-1))

    qkv, f = _in_proj(x2, g_mix[i].reshape(1, -1), w_in, i)
    kbias = _forget_scan(f, bf, B, S)
    qkv3 = qkv.reshape(B, S, N_QKV)
    oa = _dilated_attn(qkv3, cos_t, sin_t, pair(qn_a[i]), pair(kn_a[i]))
    ob = _fox_attn(qkv3, kbias, pair(qn_b[i]), pair(kn_b[i]))

    h1, msort, route, cnt = _out_router(oa.reshape(T, D_SEC), ob.reshape(T, D_SEC), x2, w_o[i],
                                        g_ffn[i].reshape(1, -1), w_r.astype(BF16), b_r)

    n_tok_tiles = T // TM_TOK
    counts = cnt[:, 0, ROUTER_LANE0:ROUTER_LANE0 + N_EXPERTS].astype(jnp.int32)
    seg_rows = (counts + SEG_ALIGN - 1) // SEG_ALIGN * SEG_ALIGN
    seg_local = jnp.cumsum(seg_rows, axis=1) - seg_rows
    rows_e = jnp.sum(seg_rows, axis=0)
    tile_end = jnp.cumsum((rows_e + TM_MOE - 1) // TM_MOE)
    off = jnp.concatenate([jnp.zeros((1,), jnp.int32), tile_end * TM_MOE]).astype(jnp.int32)
    seg_global = off[None, :N_EXPERTS] + jnp.cumsum(seg_rows, axis=0) - seg_rows
    n_tiles = (2 * T + n_tok_tiles * N_EXPERTS * (SEG_ALIGN - 1)) // TM_MOE + N_EXPERTS
    nused = tile_end[-1:].astype(jnp.int32)
    tile_ids = jnp.minimum(jnp.arange(n_tiles, dtype=jnp.int32), nused[0] - 1)
    tile_e = jnp.sum((tile_ids[:, None] >= tile_end[None, :]).astype(jnp.int32), axis=1).astype(jnp.int32)
    chunk_row = jnp.arange(SORT_CHUNKS, dtype=jnp.int32)[None, :, None] * SEG_ALIGN
    lo, hi = seg_local[:, None, :], (seg_local + seg_rows)[:, None, :]
    chunk_dst = jnp.sum(jnp.where((chunk_row >= lo) & (chunk_row < hi), seg_global[:, None, :] + chunk_row - lo, 0),
                        axis=2).reshape(-1).astype(jnp.int32)
    sorted_row = jnp.arange(n_tiles * MOE_CHUNKS, dtype=jnp.int32)[:, None] * SEG_ALIGN
    seg_lo = seg_global.reshape(1, -1)
    seg_src = (jnp.arange(n_tok_tiles, dtype=jnp.int32)[:, None] * SORT_ROWS + seg_local).reshape(1, -1)
    in_seg = (sorted_row >= seg_lo) & (sorted_row < seg_lo + seg_rows.reshape(1, -1))
    src_chunk = jnp.where(jnp.any(in_seg, axis=1),
                          jnp.sum(jnp.where(in_seg, seg_src + sorted_row - seg_lo, 0), axis=1),
                          SORT_ROWS - SEG_ALIGN).astype(jnp.int32)
    e_ids = jnp.arange(N_EXPERTS, dtype=jnp.int32)
    active = rows_e > 0
    later_active = jnp.where(active[None, :] & (e_ids[None, :] > e_ids[:, None]), e_ids[None, :], N_EXPERTS)
    next_e = jnp.min(later_active, axis=1)
    next_e = jnp.where(next_e == N_EXPERTS, -1, next_e).astype(jnp.int32)
    slot_e = ((jnp.cumsum(active.astype(jnp.int32)) - 1) % 2).astype(jnp.int32)

    ys = _moe_experts(tile_e, nused, src_chunk, next_e, slot_e, msort, w_up[i], w_down[i])
    out = _combine_ple(chunk_dst, ys, h1, route, p[i].reshape(T, PLE_DIM),
                       g_ple[i].reshape(1, -1), w_ple_gate[i], w_ple_proj[i])
    return out.reshape(B, S, D_MODEL)


def kernel(x, p, positions, g_mix, w_in, b_f, qn_a, kn_a, qn_b, kn_b, w_o, g_ffn, w_rg, b_rg, w_re, b_re,
           w_up, w_down, g_ple, w_ple_gate, w_ple_proj):
    cos_t, sin_t = _rope_tables(positions)
    h = x
    for i in range(p.shape[0]):
        h = _layer(i, h, p, cos_t, sin_t, g_mix, w_in, b_f, qn_a, kn_a, qn_b, kn_b, w_o, g_ffn, w_rg, b_rg,
                   w_re, b_re, w_up, w_down, g_ple, w_ple_gate, w_ple_proj)
    return h
```

```python
import functools
import math

import jax
import jax.numpy as jnp
from jax import lax
from jax.experimental import pallas as pl
from jax.experimental.pallas import tpu as pltpu

D_MODEL = 1024
HEAD_DIM = 64
N_HEADS = 8
D_SEC = N_HEADS * HEAD_DIM
N_QKV = 6 * D_SEC
ROT_DIM = HEAD_DIM // 4
ROPE_THETA = 500000.0
N_GROUPS = 4
EXPERTS_PER_GROUP = 8
N_EXPERTS = N_GROUPS * EXPERTS_PER_GROUP
D_EXPERT = 512
PLE_DIM = 256
EPS = 1e-6
NEG = -1e30
WINDOW = 128

LANES = 128
SUBLANES = 8
VMEM_LIMIT = 48 * 1024 * 1024

TM_PROJ = 512
TQ = 128
TK = 256
N_PAIRS = N_HEADS // 2
TM_MOE = 512
TM_TOK = 512
SEG_ALIGN = 2 * SUBLANES
SORT_ROWS = 2 * TM_TOK + N_EXPERTS * SEG_ALIGN
SORT_CHUNKS = SORT_ROWS // SEG_ALIGN
ROUTER_LANE0 = N_GROUPS

Q_SCALE_LOG2 = math.log2(math.e) / math.sqrt(HEAD_DIM)

F32 = jnp.float32
BF16 = jnp.bfloat16
NT_DIMS = (((1,), (1,)), ((), ()))


def _cparams(*sem):
    return pltpu.CompilerParams(dimension_semantics=sem, vmem_limit_bytes=VMEM_LIMIT)


def _resident(shape):
    return pl.BlockSpec(shape, lambda i, *_: (0,) * len(shape), pipeline_mode=pl.Buffered(1))


def _in_proj_kernel(x_ref, g_ref, w_ref, qkv_ref, f_ref, w_s, wf_s):
    @pl.when(pl.program_id(0) == 0)
    def _():
        w_s[...] = w_ref[0, :, 0:N_QKV].astype(BF16)
        wf_s[...] = jnp.zeros_like(wf_s)
        wf_s[:, 0:N_HEADS] = w_ref[0, :, N_QKV:N_QKV + N_HEADS].astype(BF16)

    x = x_ref[...]
    ms = jnp.mean(x * x, axis=-1, keepdims=True)
    a = (x * lax.rsqrt(ms + EPS) * g_ref[...]).astype(BF16)
    qkv_ref[...] = jnp.dot(a, w_s[...], preferred_element_type=F32).astype(BF16)
    f_ref[...] = jnp.dot(a, wf_s[...], preferred_element_type=F32)


def _in_proj(x2, g_mix, w_in, layer):
    T = x2.shape[0]
    w_block = (1,) + w_in.shape[1:]
    return pl.pallas_call(
        _in_proj_kernel,
        grid=(T // TM_PROJ,),
        in_specs=[
            pl.BlockSpec((TM_PROJ, D_MODEL), lambda i: (i, 0)),
            pl.BlockSpec((1, D_MODEL), lambda i: (0, 0)),
            pl.BlockSpec(w_block, lambda i: (layer, 0, 0), pipeline_mode=pl.Buffered(1)),
        ],
        out_specs=[
            pl.BlockSpec((TM_PROJ, N_QKV), lambda i: (i, 0)),
            pl.BlockSpec((TM_PROJ, LANES), lambda i: (i, 0)),
        ],
        out_shape=[
            jax.ShapeDtypeStruct((T, N_QKV), BF16),
            jax.ShapeDtypeStruct((T, LANES), F32),
        ],
        scratch_shapes=[pltpu.VMEM((D_MODEL, N_QKV), BF16), pltpu.VMEM((D_MODEL, LANES), BF16)],
        compiler_params=_cparams("arbitrary"),
        name="in_proj",
    )(x2, g_mix, w_in)


BIAS_TERMS = 3


def _bias_lane(head):
    return (head // 2) * LANES + (HEAD_DIM if head % 2 == 0 else 0)


def _forget_scan_kernel(f_ref, bf_ref, kb_ref, *, seq):
    f = f_ref[...] + bf_ref[...]
    logf = jnp.minimum(f, 0.0) - jnp.log1p(jnp.exp(-jnp.abs(f)))

    def split_bf16(v):
        terms = []
        for _ in range(BIAS_TERMS):
            t = v.astype(BF16)
            terms.append(t)
            v = v - t.astype(F32)
        return terms

    tri = (lax.broadcasted_iota(jnp.int32, (TK, TK), 1) <= lax.broadcasted_iota(jnp.int32, (TK, TK), 0)).astype(BF16)
    carry = jnp.zeros((1, LANES), F32)
    chunks = []
    for j in range(seq // TK):
        terms = jnp.concatenate(split_bf16(logf[j * TK:(j + 1) * TK, :]), axis=1)
        pre = jnp.dot(tri, terms, preferred_element_type=F32)
        cj = carry + sum(pre[:, t * LANES:(t + 1) * LANES] for t in range(BIAS_TERMS))
        carry = cj[TK - 1:TK, :]
        chunks.append(cj)
    c = jnp.concatenate(chunks, axis=0)
    rest = c * (-math.log2(math.e))
    r_idx = lax.broadcasted_iota(jnp.int32, (LANES, D_SEC), 0)
    c_idx = lax.broadcasted_iota(jnp.int32, (LANES, D_SEC), 1)
    base = (r_idx // 2) * LANES + jnp.where(r_idx % 2 == 0, HEAD_DIM, 0)
    out = jnp.zeros((seq, D_SEC), F32)
    for t, term in enumerate(split_bf16(rest)):
        place = ((r_idx < N_HEADS) & (c_idx == base + t)).astype(BF16)
        out = out + jnp.dot(term, place, preferred_element_type=F32)
    kb_ref[0] = out.astype(BF16)


def _forget_scan(f, bf, batch, seq):
    return pl.pallas_call(
        functools.partial(_forget_scan_kernel, seq=seq),
        grid=(batch,),
        in_specs=[
            pl.BlockSpec((seq, LANES), lambda b: (b, 0)),
            pl.BlockSpec((1, LANES), lambda b: (0, 0)),
        ],
        out_specs=pl.BlockSpec((1, seq, D_SEC), lambda b: (b, 0, 0)),
        out_shape=jax.ShapeDtypeStruct((batch, seq, D_SEC), BF16),
        compiler_params=_cparams("parallel"),
        name="forget_scan",
    )(f, bf)


def _head_sumsq_matrix():
    r = lax.broadcasted_iota(jnp.int32, (LANES, LANES), 0) // HEAD_DIM
    c = lax.broadcasted_iota(jnp.int32, (LANES, LANES), 1) // HEAD_DIM
    return (r == c).astype(BF16)


def _qk_norm(x, gain, gmat):
    ss = jnp.dot((x * x).astype(BF16), gmat, preferred_element_type=F32)
    return x * lax.rsqrt(ss * (1.0 / HEAD_DIM) + EPS) * gain


def _stack_heads(qb, head0):
    zero = jnp.zeros_like(qb)
    return jnp.concatenate([jnp.where(head0, qb, zero), jnp.where(head0, zero, qb)], axis=0)


def _unstack(a, head0):
    return jnp.where(head0, a[:TQ], a[TQ:])


def _dilated_kernel(q_ref, k_ref, v_ref, cos_ref, sin_ref, gq_ref, gk_ref, o_ref,
                    qn_s, kn_s, v_s, acc_s, m_s, l_s, bias_s, *, seq):
    gmat = _head_sumsq_matrix()
    lane = lax.broadcasted_iota(jnp.int32, (TQ, LANES), 1)
    head0 = lane < HEAD_DIM
    chunk = 512
    lane_c = lax.broadcasted_iota(jnp.int32, (chunk, LANES), 1) % HEAD_DIM
    first_half = lane_c < ROT_DIM // 2

    def rope(t, cs, sn):
        partner = jnp.where(first_half, pltpu.roll(t, LANES - ROT_DIM // 2, axis=1),
                            pltpu.roll(t, ROT_DIM // 2, axis=1))
        return t * cs + partner * sn

    for c0 in range(0, seq, chunk):
        rows = pl.ds(c0, chunk)
        cs = cos_ref[0, rows, :]
        sn = sin_ref[0, rows, :]
        qn = _qk_norm(q_ref[0, rows, :].astype(F32), gq_ref[...], gmat)
        qn_s[rows, :] = rope(qn, cs, sn) * Q_SCALE_LOG2
        kn = _qk_norm(k_ref[0, rows, :].astype(F32), gk_ref[...], gmat)
        kn_s[rows, :] = rope(kn, cs, sn)
        v_s[rows, :] = v_ref[0, rows, :].astype(F32)

    u = lax.broadcasted_iota(jnp.int32, (2 * TQ, 2 * TQ), 0) % TQ
    c = lax.broadcasted_iota(jnp.int32, (2 * TQ, 2 * TQ), 1)
    for slot, dist0 in enumerate((0, TQ)):
        dist = dist0 + u - c
        bias_s[slot] = jnp.where((dist >= 0) & (dist <= WINDOW), 0.0, NEG)

    def scores(blk):
        p, q_rows, k_rows, nk, dist0 = blk
        qst = _stack_heads(qn_s[q_rows, :].astype(BF16), head0)
        s = lax.dot_general(qst, kn_s[k_rows, :].astype(BF16), NT_DIMS, preferred_element_type=F32)
        return s + bias_s[dist0 // TQ, :, 0:nk]

    def finish(blk, s):
        p, q_rows, k_rows, nk, dist0 = blk
        m = jnp.max(s, axis=-1, keepdims=True)
        pr = jnp.exp2(s - m)
        l = jnp.sum(pr, axis=-1, keepdims=True)
        acc = jnp.dot(pr.astype(BF16), v_s[k_rows, :].astype(BF16), preferred_element_type=F32)
        acc_s[p, q_rows, :] = _unstack(acc, head0)
        m_s[p, q_rows, :] = _unstack(jnp.broadcast_to(m, (2 * TQ, LANES)), head0)
        l_s[p, q_rows, :] = _unstack(jnp.broadcast_to(l, (2 * TQ, LANES)), head0)

    blocks = []
    for i in range(seq // TQ):
        kb0 = max(i - 1, 0)
        blocks.append((0, pl.ds(i * TQ, TQ), pl.ds(kb0 * TQ, 2 * TQ), 2 * TQ, (i - kb0) * TQ))
    for r in range(4):
        for n in range(seq // (4 * TQ)):
            kb0 = max(n - 1, 0)
            blocks.append((1, pl.ds(4 * TQ * n + r, TQ, stride=4), pl.ds(4 * TQ * kb0 + r, 2 * TQ, stride=4),
                           2 * TQ, (n - kb0) * TQ))
    for r in range(16):
        rows = pl.ds(r, TQ, stride=16)
        blocks.append((2, rows, rows, TQ, 0))

    ahead = 4
    pending = [scores(b) for b in blocks[:ahead]]
    for idx, b in enumerate(blocks):
        s = pending.pop(0)
        if idx + ahead < len(blocks):
            pending.append(scores(blocks[idx + ahead]))
        finish(b, s)

    for c0 in range(0, seq, chunk):
        rows = pl.ds(c0, chunk)
        m0, m1, m2 = m_s[0, rows, :], m_s[1, rows, :], m_s[2, rows, :]
        mm = jnp.maximum(jnp.maximum(m0, m1), m2)
        e0, e1, e2 = jnp.exp2(m0 - mm), jnp.exp2(m1 - mm), jnp.exp2(m2 - mm)
        num = acc_s[0, rows, :] * e0 + acc_s[1, rows, :] * e1 + acc_s[2, rows, :] * e2
        den = l_s[0, rows, :] * e0 + l_s[1, rows, :] * e1 + l_s[2, rows, :] * e2
        o_ref[0, rows, :] = (num / den).astype(BF16)


def _dilated_attn(qkv3, cos_t, sin_t, gq, gk):
    batch, seq, _ = qkv3.shape
    sec = D_SEC // LANES
    blk = lambda s: pl.BlockSpec((1, seq, LANES), lambda b, h, s=s: (b, 0, s * sec + h))
    tab = pl.BlockSpec((1, seq, LANES), lambda b, h: (b, 0, 0))
    vec = pl.BlockSpec((1, LANES), lambda b, h: (0, 0))
    return pl.pallas_call(
        functools.partial(_dilated_kernel, seq=seq),
        grid=(batch, N_PAIRS),
        in_specs=[blk(0), blk(1), blk(2), tab, tab, vec, vec],
        out_specs=pl.BlockSpec((1, seq, LANES), lambda b, h: (b, 0, h)),
        out_shape=jax.ShapeDtypeStruct((batch, seq, D_SEC), BF16),
        scratch_shapes=[
            pltpu.VMEM((seq, LANES), F32), pltpu.VMEM((seq, LANES), F32), pltpu.VMEM((seq, LANES), F32),
            pltpu.VMEM((3, seq, LANES), F32), pltpu.VMEM((3, seq, LANES), F32),
            pltpu.VMEM((3, seq, LANES), F32), pltpu.VMEM((2, 2 * TQ, 2 * TQ), F32),
        ],
        compiler_params=_cparams("parallel", "parallel"),
        name="dilated_attn",
    )(qkv3, qkv3, qkv3, cos_t, sin_t, gq, gk)


def _fox_kernel(q_ref, k_ref, v_ref, kb_ref, gq_ref, gk_ref, o_ref, qat_s, ka_s, vt_s, acc_s, tri_s, *, seq):
    gmat = _head_sumsq_matrix()
    chunk = 512
    lane = lax.broadcasted_iota(jnp.int32, (chunk, LANES), 1)
    low = lane < HEAD_DIM
    sel = jnp.where((lane >= HEAD_DIM) & (lane < HEAD_DIM + BIAS_TERMS), 1.0, 0.0)

    for c0 in range(0, seq, chunk):
        rows = pl.ds(c0, chunk)
        for g in range(N_PAIRS):
            lanes = slice(g * LANES, (g + 1) * LANES)
            qn = _qk_norm(q_ref[0, rows, lanes].astype(F32), gq_ref[...], gmat) * Q_SCALE_LOG2
            for h, qa in ((2 * g, jnp.where(low, qn, sel)),
                          (2 * g + 1, jnp.where(low, pltpu.roll(qn, HEAD_DIM, axis=1), sel))):
                qt = qa.T
                for cc in range(chunk // TK):
                    qat_s[c0 // TK + cc, h * LANES:(h + 1) * LANES, :] = qt[:, cc * TK:(cc + 1) * TK].astype(BF16)
            kn =_qk_norm(k_ref[0, rows, lanes].astype(F32), gk_ref[...], gmat)
            kb = kb_ref[0, rows, lanes].astype(F32)
            ka_s[2 * g, rows, :] = jnp.where(low, kn, kb).astype(BF16)
            ka_s[2 * g + 1, rows, :] = pltpu.roll(jnp.where(low, kb, kn), HEAD_DIM, axis=1).astype(BF16)
            vt = v_ref[0, rows, lanes].astype(F32).T
            for cc in range(chunk // TK):
                vt_s[c0 // TK + cc, lanes, :] = vt[:, cc * TK:(cc + 1) * TK].astype(BF16)

    r = lax.broadcasted_iota(jnp.int32, (TK, TK), 0)
    c = lax.broadcasted_iota(jnp.int32, (TK, TK), 1)
    tri_s[...] = jnp.where(r <= c, 0.0, NEG)

    def q_block(qi, carry):
        q_rows = pl.ds(pl.multiple_of(qi * TK, TK), TK)
        for h in range(N_HEADS):
            acc_s[h] = jnp.zeros((HEAD_DIM, TK), F32)

        def step(j, st, diagonal):
            k_rows = pl.ds(pl.multiple_of(j * TK, TK), TK)

            def scores(h):
                s = jnp.dot(ka_s[h, k_rows, :], qat_s[qi, h * LANES:(h + 1) * LANES, :], preferred_element_type=F32)
                return s + tri_s[...] if diagonal else s

            ahead = 6
            pending = [scores(h) for h in range(ahead)]
            new = []
            for h in range(N_HEADS):
                m, l = st[h]
                s = pending.pop(0)
                if h + ahead < N_HEADS:
                    pending.append(scores(h + ahead))
                m_new = jnp.maximum(m, jnp.max(s, axis=0, keepdims=True))
                alpha = jnp.exp2(m - m_new)
                pr = jnp.exp2(s - m_new)
                l_new = alpha * l + jnp.sum(pr, axis=0, keepdims=True)
                pv = jnp.dot(vt_s[j, h * HEAD_DIM:(h + 1) * HEAD_DIM, :], pr.astype(BF16),
                             preferred_element_type=F32)
                acc_s[h] = alpha * acc_s[h] + pv
                new.append((m_new, l_new))
            return tuple(new)

        init = tuple((jnp.full((1, TK), NEG, F32), jnp.zeros((1, TK), F32)) for _ in range(N_HEADS))
        st = lax.fori_loop(0, qi // 2, lambda jj, st: step(2 * jj + 1, step(2 * jj, st, False), False), init)
        st = lax.cond(qi % 2 == 1, lambda st: step(qi - 1, st, False), lambda st: st, st)
        st = step(qi, st, True)
        for g in range(N_PAIRS):
            o2 = jnp.concatenate([acc_s[2 * g] / st[2 * g][1], acc_s[2 * g + 1] / st[2 * g + 1][1]], axis=0)
            o_ref[0, q_rows, g * LANES:(g + 1) * LANES] = o2.T.astype(BF16)
        return carry

    lax.fori_loop(0, seq // TK, q_block, 0)


def _fox_attn(qkv3, kbias, gq, gk):
    batch, seq, _ = qkv3.shape
    blk = lambda s: pl.BlockSpec((1, seq, D_SEC), lambda b, s=s: (b, 0, s))
    vec = pl.BlockSpec((1, LANES), lambda b: (0, 0))
    return pl.pallas_call(
        functools.partial(_fox_kernel, seq=seq),
        grid=(batch,),
        in_specs=[blk(3), blk(4), blk(5), pl.BlockSpec((1, seq, D_SEC), lambda b: (b, 0, 0)), vec, vec],
        out_specs=pl.BlockSpec((1, seq, D_SEC), lambda b: (b, 0, 0)),
        out_shape=jax.ShapeDtypeStruct((batch, seq, D_SEC), BF16),
        scratch_shapes=[
            pltpu.VMEM((seq // TK, N_HEADS * LANES, TK), BF16),
            pltpu.VMEM((N_HEADS, seq, LANES), BF16),
            pltpu.VMEM((seq // TK, D_SEC, TK), BF16),
            pltpu.VMEM((N_HEADS, HEAD_DIM, TK), F32),
            pltpu.VMEM((TK, TK), F32),
        ],
        compiler_params=_cparams("parallel"),
        name="fox_attn",
    )(qkv3, qkv3, qkv3, kbias, gq, gk)


def _out_router_kernel(oa_ref, ob_ref, x_ref, wo_ref, g_ref, wr_ref, br_ref,
                       h1_ref, msort_ref, route_ref, cnt_ref, wo_s, pick_s, mn_s):
    tm = TM_TOK

    @pl.when(pl.program_id(0) == 0)
    def _():
        wo_s[...] = wo_ref[...].astype(BF16)
        pick_s[...] = jnp.zeros_like(pick_s)
        mn_s[...] = jnp.zeros_like(mn_s)

    h1 = (x_ref[...]
          + jnp.dot(oa_ref[...], wo_s[0:D_SEC, :], preferred_element_type=F32)
          + jnp.dot(ob_ref[...], wo_s[D_SEC:2 * D_SEC, :], preferred_element_type=F32))
    h1_ref[...] = h1
    ms = jnp.mean(h1 * h1, axis=-1, keepdims=True)
    mn = h1 * lax.rsqrt(ms + EPS) * g_ref[...]

    logits = jnp.dot(mn.astype(BF16), wr_ref[...], preferred_element_type=F32) + br_ref[...]

    msort_ref[...] = jnp.dot(pick_s[...], mn_s[...], preferred_element_type=F32).astype(BF16)

    lane = lax.broadcasted_iota(jnp.int32, (tm, LANES), 1).astype(F32)
    big = float(LANES)

    def first_argmax(vals):
        vmax = jnp.max(vals, axis=-1, keepdims=True)
        idx = jnp.min(jnp.where(vals == vmax, lane, big), axis=-1, keepdims=True)
        return vmax, idx

    lg = jnp.where(lane < N_GROUPS, logits, -jnp.inf)
    gmax, gidx = first_argmax(lg)
    gw = 1.0 / jnp.sum(jnp.exp(lg - gmax), axis=-1, keepdims=True)
    lo = ROUTER_LANE0 + EXPERTS_PER_GROUP * gidx
    le = jnp.where((lane >= lo) & (lane < lo + EXPERTS_PER_GROUP), logits, -jnp.inf)
    v0, i0 = first_argmax(le)
    v1, i1 = first_argmax(jnp.where(lane == i0, -jnp.inf, le))
    ex = jnp.exp(v1 - v0)
    w0 = gw / (1.0 + ex)
    w1 = gw * ex / (1.0 + ex)

    sel0 = lane == i0
    sel1 = lane == i1
    onehot = jnp.where(sel0 | sel1, 1.0, 0.0)
    r = lax.broadcasted_iota(jnp.int32, (tm, tm), 0)
    c = lax.broadcasted_iota(jnp.int32, (tm, tm), 1)
    ltri = (c <= r).astype(BF16)
    incl = jnp.dot(ltri, onehot.astype(BF16), preferred_element_type=F32)
    excl = incl - onehot
    counts = incl[tm - 1:tm, :]
    cnt_ref[0] = jnp.broadcast_to(counts, (SUBLANES, LANES))

    seg_rows = jnp.floor((counts + (SEG_ALIGN - 1.0)) * (1.0 / SEG_ALIGN)) * SEG_ALIGN
    seg_rows = jnp.broadcast_to(seg_rows, (tm, LANES))
    lp0 = jnp.sum(jnp.where(lane < i0, seg_rows, 0.0) + jnp.where(sel0, excl, 0.0), axis=-1, keepdims=True)
    lp1 = jnp.sum(jnp.where(lane < i1, seg_rows, 0.0) + jnp.where(sel1, excl, 0.0), axis=-1, keepdims=True)

    rec = jnp.zeros((tm, LANES), F32)
    for pos, val in enumerate((i0 - ROUTER_LANE0, i1 - ROUTER_LANE0, lp0, lp1, w0, w1)):
        rec = jnp.where(lane == float(pos), val, rec)
    route_ref[...] = rec[:, 0:SUBLANES]

    rec_t = rec.T
    out_row = lax.broadcasted_iota(jnp.int32, (SORT_ROWS, tm), 0).astype(F32)
    pick_s[...] = jnp.where((out_row == rec_t[2:3, :]) | (out_row == rec_t[3:4, :]), 1.0, 0.0).astype(BF16)
    mn_s[...] = mn.astype(BF16)


def _out_router(oa, ob, x2, w_o, g_ffn, w_r, b_r):
    T = x2.shape[0]
    tm = TM_TOK
    last = T // tm - 1
    row = lambda w: pl.BlockSpec((tm, w), lambda i: (jnp.minimum(i, last), 0))
    const = lambda a, b: pl.BlockSpec((a, b), lambda i: (0, 0))
    return pl.pallas_call(
        _out_router_kernel,
        grid=(T // tm + 1,),
        in_specs=[row(D_SEC), row(D_SEC), row(D_MODEL), _resident((D_MODEL, D_MODEL)), const(1, D_MODEL),
                  const(D_MODEL, LANES), const(1, LANES)],
        out_specs=[row(D_MODEL), pl.BlockSpec((SORT_ROWS, D_MODEL), lambda i: (jnp.maximum(i - 1, 0), 0)),
                   row(SUBLANES), pl.BlockSpec((1, SUBLANES, LANES), lambda i: (jnp.minimum(i, last), 0, 0))],
        out_shape=[
            jax.ShapeDtypeStruct((T, D_MODEL), F32),
            jax.ShapeDtypeStruct((T // tm * SORT_ROWS, D_MODEL), BF16),
            jax.ShapeDtypeStruct((T, SUBLANES), F32),
            jax.ShapeDtypeStruct((T // tm, SUBLANES, LANES), F32),
        ],
        scratch_shapes=[pltpu.VMEM((D_MODEL, D_MODEL), BF16), pltpu.VMEM((SORT_ROWS, tm), BF16),
                        pltpu.VMEM((tm, D_MODEL), BF16)],
        compiler_params=_cparams("arbitrary"),
        name="out_router",
    )(oa, ob, x2, w_o, g_ffn, w_r, b_r)


MOE_CHUNKS = TM_MOE // SEG_ALIGN


def _chunk_row(c):
    return pl.multiple_of(c * SEG_ALIGN, SEG_ALIGN)


def _moe_kernel(tile_e_ref, nused_ref, src_ref, next_e_ref, slot_e_ref, msort_ref, wup_ref, wdn_ref, ys_ref,
                x_s, wup_f, wdn_f, wup_s, wdn_s, xsem, wsem):
    i = pl.program_id(0)
    nused = nused_ref[0]

    def gather(k, act):
        def chunk(c, carry):
            src = pl.multiple_of(src_ref[k * MOE_CHUNKS + c], SEG_ALIGN)
            act(pltpu.make_async_copy(msort_ref.at[pl.ds(src, SEG_ALIGN)],
                                      x_s.at[k % 2, pl.ds(_chunk_row(c), SEG_ALIGN)], xsem.at[k % 2]))
            return carry
        lax.fori_loop(0, MOE_CHUNKS, chunk, 0, unroll=8)

    def weights(e, s, act):
        act(pltpu.make_async_copy(wup_ref.at[e], wup_f.at[s], wsem.at[0, s]))
        act(pltpu.make_async_copy(wdn_ref.at[e], wdn_f.at[s], wsem.at[1, s]))

    @pl.when(i == 0)
    def _():
        gather(0, lambda c: c.start())
        weights(tile_e_ref[0], 0, lambda c: c.start())

    @pl.when(i + 1 < nused)
    def _():
        gather(i + 1, lambda c: c.start())

    @pl.when(i < nused)
    def _():
        e = tile_e_ref[i]

        @pl.when((i == 0) | (tile_e_ref[jnp.maximum(i - 1, 0)] != e))
        def _():
            s = slot_e_ref[e]
            weights(e, s, lambda c: c.wait())

            @pl.when(next_e_ref[e] >= 0)
            def _():
                weights(next_e_ref[e], 1 - s, lambda c: c.start())
            wup_s[...] = wup_f[s].astype(BF16)
            wdn_s[...] = wdn_f[s].astype(BF16)

        gather(i, lambda c: c.wait())
        hu = jnp.dot(x_s[i % 2], wup_s[...], preferred_element_type=F32)
        gate = hu[:, :D_EXPERT]
        hid = gate * (1.0 / (1.0 + jnp.exp(-gate))) * hu[:, D_EXPERT:]
        ys_ref[...] = jnp.dot(hid.astype(BF16), wdn_s[...], preferred_element_type=F32).astype(BF16)

    @pl.when(i >= nused)
    def _():
        ys_ref[...] = jnp.zeros_like(ys_ref)


def _moe_experts(tile_e, nused, src_chunk, next_e, slot_e, msort, w_up, w_down):
    n_tiles = tile_e.shape[0]
    grid_spec = pltpu.PrefetchScalarGridSpec(
        num_scalar_prefetch=5,
        grid=(n_tiles,),
        in_specs=[pl.BlockSpec(memory_space=pl.ANY)] * 3,
        out_specs=pl.BlockSpec((TM_MOE, D_MODEL), lambda i, *_: (i, 0)),
        scratch_shapes=[
            pltpu.VMEM((2, TM_MOE, D_MODEL), BF16),
            pltpu.VMEM((2, D_MODEL, 2 * D_EXPERT), F32), pltpu.VMEM((2, D_EXPERT, D_MODEL), F32),
            pltpu.VMEM((D_MODEL, 2 * D_EXPERT), BF16), pltpu.VMEM((D_EXPERT, D_MODEL), BF16),
            pltpu.SemaphoreType.DMA((2,)), pltpu.SemaphoreType.DMA((2, 2)),
        ],
    )
    return pl.pallas_call(
        _moe_kernel,
        grid_spec=grid_spec,
        out_shape=jax.ShapeDtypeStruct((n_tiles * TM_MOE, D_MODEL), BF16),
        compiler_params=_cparams("arbitrary"),
        name="moe_experts",
    )(tile_e, nused, src_chunk, next_e, slot_e, msort, w_up, w_down)


def _combine_kernel(src_ref, ys_ref, h1_ref, route_ref, p_ref,
                    g_ref, wg_ref, wp_ref, o_ref, gath_s, sems, wg_s, wp_s, h2_s, n_s):
    i = pl.program_id(0)
    n_steps = pl.num_programs(0)
    last_tile = n_steps - 2

    tm = TM_TOK
    slot = i % 2

    def gather(t, s, act):
        for c in range(SORT_CHUNKS):
            src = pl.multiple_of(src_ref[t * SORT_CHUNKS + c], SEG_ALIGN)
            act(pltpu.make_async_copy(ys_ref.at[pl.ds(src, SEG_ALIGN)],
                                      gath_s.at[s, pl.ds(c * SEG_ALIGN, SEG_ALIGN)], sems.at[s]))

    @pl.when(i == 0)
    def _():
        wg_s[...] = wg_ref[...].astype(BF16)
        wp_s[...] = wp_ref[...].astype(BF16)
        h2_s[...] = jnp.zeros_like(h2_s)
        n_s[...] = jnp.zeros_like(n_s)
        gather(0, 0, lambda c: c.start())

    cur = jnp.minimum(i, last_tile)
    nxt = jnp.minimum(i + 1, last_tile)
    gather(cur, slot, lambda c: c.wait())
    gather(nxt, 1 - slot, lambda c: c.start())

    ple = jnp.dot(p_ref[...].astype(BF16), wp_s[...], preferred_element_type=F32)
    z = jnp.dot(n_s[...], wg_s[...], preferred_element_type=F32)
    o_ref[...] = h2_s[...] + (1.0 / (1.0 + jnp.exp(-z))) * ple

    route = route_ref[...]
    col = lax.broadcasted_iota(jnp.int32, (tm, SORT_ROWS), 1).astype(F32)
    weights = (jnp.where(col == route[:, 2:3], route[:, 4:5], 0.0)
               + jnp.where(col == route[:, 3:4], route[:, 5:6], 0.0)).astype(BF16)
    y = jnp.dot(weights, gath_s[slot], preferred_element_type=F32)
    h2 = h1_ref[...] + y
    ms = jnp.mean(h2 * h2, axis=-1, keepdims=True)
    h2_s[...] = h2
    n_s[...] = (h2 * lax.rsqrt(ms + EPS) * g_ref[...]).astype(BF16)

    @pl.when(i == n_steps - 1)
    def _():
        gather(nxt, 1 - slot, lambda c: c.wait())


def _combine_ple(chunk_dst, ys, h1, route, p2, g_ple, w_gate, w_proj):
    T = h1.shape[0]
    tm = TM_TOK
    last = T // tm - 1
    cur = lambda w: pl.BlockSpec((tm, w), lambda i, *_: (jnp.minimum(i, last), 0))
    prev = lambda w: pl.BlockSpec((tm, w), lambda i, *_: (jnp.maximum(i - 1, 0), 0))
    const = lambda a, b: pl.BlockSpec((a, b), lambda i, *_: (0, 0))
    grid_spec = pltpu.PrefetchScalarGridSpec(
        num_scalar_prefetch=1,
        grid=(T // tm + 1,),
        in_specs=[pl.BlockSpec(memory_space=pl.ANY), cur(D_MODEL), cur(SUBLANES), prev(PLE_DIM),
                  const(1, D_MODEL), _resident((D_MODEL, D_MODEL)), _resident((PLE_DIM, D_MODEL))],
        out_specs=prev(D_MODEL),
        scratch_shapes=[pltpu.VMEM((2, SORT_ROWS, D_MODEL), BF16), pltpu.SemaphoreType.DMA((2,)),
                        pltpu.VMEM((D_MODEL, D_MODEL), BF16), pltpu.VMEM((PLE_DIM, D_MODEL), BF16),
                        pltpu.VMEM((tm, D_MODEL), F32), pltpu.VMEM((tm, D_MODEL), BF16)],
    )
    return pl.pallas_call(
        _combine_kernel,
        grid_spec=grid_spec,
        out_shape=jax.ShapeDtypeStruct((T, D_MODEL), F32),
        compiler_params=_cparams("arbitrary"),
        name="combine_ple",
    )(chunk_dst, ys, h1, route, p2, g_ple, w_gate, w_proj)


def _rope_tables(positions):
    half = ROT_DIM // 2
    inv = ROPE_THETA ** (-jnp.arange(0, ROT_DIM, 2, dtype=F32) / ROT_DIM)
    ang = positions.astype(F32)[:, None, :] * inv[None, :, None]
    lane = jnp.arange(LANES) % HEAD_DIM
    freq = jnp.arange(half)[:, None]
    rot = (lane[None, :] < ROT_DIM) & (lane[None, :] % half == freq)
    e_cos = rot.astype(BF16)
    e_sin = jnp.where(rot, jnp.where(lane[None, :] < half, -1.0, 1.0), 0.0).astype(BF16)

    def expand(tab, sel):
        hi = tab.astype(BF16)
        lo = (tab - hi.astype(F32)).astype(BF16)
        pick = functools.partial(jnp.einsum, "bfs,fl->bsl", preferred_element_type=F32)
        return pick(hi, sel) + pick(lo, sel)

    return expand(jnp.cos(ang), e_cos) + (lane >= ROT_DIM).astype(F32), expand(jnp.sin(ang), e_sin)


def _layer(i, h, p, cos_t, sin_t, g_mix, w_in, b_f, qn_a, kn_a, qn_b, kn_b, w_o, g_ffn, w_rg, b_rg,
           w_re, b_re, w_up, w_down, g_ple, w_ple_gate, w_ple_proj):
    B, S, _ = h.shape
    T = B * S
    x2 = h.reshape(T, D_MODEL)

    bf = jnp.zeros((1, LANES), F32).at[0, :N_HEADS].set(b_f[i])
    pair = lambda g: jnp.tile(g, 2).reshape(1, LANES)
    w_r = jnp.zeros((D_MODEL, LANES), F32)
    w_r = w_r.at[:, :N_GROUPS].set(w_rg[i])
    w_r = w_r.at[:, ROUTER_LANE0:ROUTER_LANE0 + N_EXPERTS].set(
        jnp.transpose(w_re[i], (1, 0, 2)).reshape(D_MODEL, N_EXPERTS))
    b_r = jnp.zeros((1, LANES), F32).at[0, :N_GROUPS].set(b_rg[i])
    b_r = b_r.at[0, ROUTER_LANE0:ROUTER_LANE0 + N_EXPERTS].set(b_re[i].reshape(-1))

    qkv, f = _in_proj(x2, g_mix[i].reshape(1, -1), w_in, i)
    kbias = _forget_scan(f, bf, B, S)
    qkv3 = qkv.reshape(B, S, N_QKV)
    oa = _dilated_attn(qkv3, cos_t, sin_t, pair(qn_a[i]), pair(kn_a[i]))
    ob = _fox_attn(qkv3, kbias, pair(qn_b[i]), pair(kn_b[i]))

    h1, msort, route, cnt = _out_router(oa.reshape(T, D_SEC), ob.reshape(T, D_SEC), x2, w_o[i],
                                        g_ffn[i].reshape(1, -1), w_r.astype(BF16), b_r)

    n_tok_tiles = T // TM_TOK
    counts = cnt[:, 0, ROUTER_LANE0:ROUTER_LANE0 + N_EXPERTS].astype(jnp.int32)
    seg_rows = (counts + SEG_ALIGN - 1) // SEG_ALIGN * SEG_ALIGN
    seg_local = jnp.cumsum(seg_rows, axis=1) - seg_rows
    rows_e = jnp.sum(seg_rows, axis=0)
    tile_end = jnp.cumsum((rows_e + TM_MOE - 1) // TM_MOE)
    off = jnp.concatenate([jnp.zeros((1,), jnp.int32), tile_end * TM_MOE]).astype(jnp.int32)
    seg_global = off[None, :N_EXPERTS] + jnp.cumsum(seg_rows, axis=0) - seg_rows
    n_tiles = (2 * T + n_tok_tiles * N_EXPERTS * (SEG_ALIGN - 1)) // TM_MOE + N_EXPERTS
    nused = tile_end[-1:].astype(jnp.int32)
    tile_ids = jnp.minimum(jnp.arange(n_tiles, dtype=jnp.int32), nused[0] - 1)
    tile_e = jnp.sum((tile_ids[:, None] >= tile_end[None, :]).astype(jnp.int32), axis=1).astype(jnp.int32)
    chunk_row = jnp.arange(SORT_CHUNKS, dtype=jnp.int32)[None, :, None] * SEG_ALIGN
    lo, hi = seg_local[:, None, :], (seg_local + seg_rows)[:, None, :]
    chunk_dst = jnp.sum(jnp.where((chunk_row >= lo) & (chunk_row < hi), seg_global[:, None, :] + chunk_row - lo, 0),
                        axis=2).reshape(-1).astype(jnp.int32)
    sorted_row = jnp.arange(n_tiles * MOE_CHUNKS, dtype=jnp.int32)[:, None] * SEG_ALIGN
    seg_lo = seg_global.reshape(1, -1)
    seg_src = (jnp.arange(n_tok_tiles, dtype=jnp.int32)[:, None] * SORT_ROWS + seg_local).reshape(1, -1)
    in_seg = (sorted_row >= seg_lo) & (sorted_row < seg_lo + seg_rows.reshape(1, -1))
    src_chunk = jnp.where(jnp.any(in_seg, axis=1),
                          jnp.sum(jnp.where(in_seg, seg_src + sorted_row - seg_lo, 0), axis=1),
                          SORT_ROWS - SEG_ALIGN).astype(jnp.int32)
    e_ids = jnp.arange(N_EXPERTS, dtype=jnp.int32)
    active = rows_e > 0
    later_active = jnp.where(active[None, :] & (e_ids[None, :] > e_ids[:, None]), e_ids[None, :], N_EXPERTS)
    next_e = jnp.min(later_active, axis=1)
    next_e = jnp.where(next_e == N_EXPERTS, -1, next_e).astype(jnp.int32)
    slot_e = ((jnp.cumsum(active.astype(jnp.int32)) - 1) % 2).astype(jnp.int32)

    ys = _moe_experts(tile_e, nused, src_chunk, next_e, slot_e, msort, w_up[i], w_down[i])
    out = _combine_ple(chunk_dst, ys, h1, route, p[i].reshape(T, PLE_DIM),
                       g_ple[i].reshape(1, -1), w_ple_gate[i], w_ple_proj[i])
    return out.reshape(B, S, D_MODEL)


def kernel(x, p, positions, g_mix, w_in, b_f, qn_a, kn_a, qn_b, kn_b, w_o, g_ffn, w_rg, b_rg, w_re, b_re,
           w_up, w_down, g_ple, w_ple_gate, w_ple_proj):
    cos_t, sin_t = _rope_tables(positions)
    h = x
    for i in range(p.shape[0]):
        h = _layer(i, h, p, cos_t, sin_t, g_mix, w_in, b_f, qn_a, kn_a, qn_b, kn_b, w_o, g_ffn, w_rg, b_rg,
                   w_re, b_re, w_up, w_down, g_ple, w_ple_gate, w_ple_proj)
    return h
```

```python
import functools
import math

import jax
import jax.numpy as jnp
from jax import lax
from jax.experimental import pallas as pl
from jax.experimental.pallas import tpu as pltpu

D_MODEL = 1024
HEAD_DIM = 64
N_HEADS = 8
D_SEC = N_HEADS * HEAD_DIM
N_QKV = 6 * D_SEC
ROT_DIM = HEAD_DIM // 4
ROPE_THETA = 500000.0
N_GROUPS = 4
EXPERTS_PER_GROUP = 8
N_EXPERTS = N_GROUPS * EXPERTS_PER_GROUP
D_EXPERT = 512
PLE_DIM = 256
EPS = 1e-6
NEG = -1e30
WINDOW = 128

LANES = 128
SUBLANES = 8
VMEM_LIMIT = 48 * 1024 * 1024

TM_PROJ = 512
TQ = 128
TK = 256
N_PAIRS = N_HEADS // 2
TM_MOE = 512
TM_TOK = 512
SEG_ALIGN = 2 * SUBLANES
SORT_ROWS = 2 * TM_TOK + N_EXPERTS * SEG_ALIGN
SORT_CHUNKS = SORT_ROWS // SEG_ALIGN
ROUTER_LANE0 = N_GROUPS

Q_SCALE_LOG2 = math.log2(math.e) / math.sqrt(HEAD_DIM)

F32 = jnp.float32
BF16 = jnp.bfloat16
NT_DIMS = (((1,), (1,)), ((), ()))


def _cparams(*sem):
    return pltpu.CompilerParams(dimension_semantics=sem, vmem_limit_bytes=VMEM_LIMIT)


def _resident(shape):
    return pl.BlockSpec(shape, lambda i, *_: (0,) * len(shape), pipeline_mode=pl.Buffered(1))


def _in_proj_kernel(x_ref, g_ref, w_ref, qkv_ref, f_ref, w_s, wf_s):
    @pl.when(pl.program_id(0) == 0)
    def _():
        w_s[...] = w_ref[0, :, 0:N_QKV].astype(BF16)
        wf_s[...] = jnp.zeros_like(wf_s)
        wf_s[:, 0:N_HEADS] = w_ref[0, :, N_QKV:N_QKV + N_HEADS].astype(BF16)

    x = x_ref[...]
    ms = jnp.mean(x * x, axis=-1, keepdims=True)
    a = (x * lax.rsqrt(ms + EPS) * g_ref[...]).astype(BF16)
    qkv_ref[...] = jnp.dot(a, w_s[...], preferred_element_type=F32).astype(BF16)
    f_ref[...] = jnp.dot(a, wf_s[...], preferred_element_type=F32)


def _in_proj(x2, g_mix, w_in, layer):
    T = x2.shape[0]
    w_block = (1,) + w_in.shape[1:]
    return pl.pallas_call(
        _in_proj_kernel,
        grid=(T // TM_PROJ,),
        in_specs=[
            pl.BlockSpec((TM_PROJ, D_MODEL), lambda i: (i, 0)),
            pl.BlockSpec((1, D_MODEL), lambda i: (0, 0)),
            pl.BlockSpec(w_block, lambda i: (layer, 0, 0), pipeline_mode=pl.Buffered(1)),
        ],
        out_specs=[
            pl.BlockSpec((TM_PROJ, N_QKV), lambda i: (i, 0)),
            pl.BlockSpec((TM_PROJ, LANES), lambda i: (i, 0)),
        ],
        out_shape=[
            jax.ShapeDtypeStruct((T, N_QKV), BF16),
            jax.ShapeDtypeStruct((T, LANES), F32),
        ],
        scratch_shapes=[pltpu.VMEM((D_MODEL, N_QKV), BF16), pltpu.VMEM((D_MODEL, LANES), BF16)],
        compiler_params=_cparams("arbitrary"),
        name="in_proj",
    )(x2, g_mix, w_in)


BIAS_TERMS = 3


def _bias_lane(head):
    return (head // 2) * LANES + (HEAD_DIM if head % 2 == 0 else 0)


def _forget_scan_kernel(f_ref, bf_ref, kb_ref, *, seq):
    f = f_ref[...] + bf_ref[...]
    logf = jnp.minimum(f, 0.0) - jnp.log1p(jnp.exp(-jnp.abs(f)))

    def split_bf16(v):
        terms = []
        for _ in range(BIAS_TERMS):
            t = v.astype(BF16)
            terms.append(t)
            v = v - t.astype(F32)
        return terms

    tri = (lax.broadcasted_iota(jnp.int32, (TK, TK), 1) <= lax.broadcasted_iota(jnp.int32, (TK, TK), 0)).astype(BF16)
    carry = jnp.zeros((1, LANES), F32)
    chunks = []
    for j in range(seq // TK):
        terms = jnp.concatenate(split_bf16(logf[j * TK:(j + 1) * TK, :]), axis=1)
        pre = jnp.dot(tri, terms, preferred_element_type=F32)
        cj = carry + sum(pre[:, t * LANES:(t + 1) * LANES] for t in range(BIAS_TERMS))
        carry = cj[TK - 1:TK, :]
        chunks.append(cj)
    c = jnp.concatenate(chunks, axis=0)
    rest = c * (-math.log2(math.e))
    r_idx = lax.broadcasted_iota(jnp.int32, (LANES, D_SEC), 0)
    c_idx = lax.broadcasted_iota(jnp.int32, (LANES, D_SEC), 1)
    base = (r_idx // 2) * LANES + jnp.where(r_idx % 2 == 0, HEAD_DIM, 0)
    out = jnp.zeros((seq, D_SEC), F32)
    for t, term in enumerate(split_bf16(rest)):
        place = ((r_idx < N_HEADS) & (c_idx == base + t)).astype(BF16)
        out = out + jnp.dot(term, place, preferred_element_type=F32)
    kb_ref[0] = out.astype(BF16)


def _forget_scan(f, bf, batch, seq):
    return pl.pallas_call(
        functools.partial(_forget_scan_kernel, seq=seq),
        grid=(batch,),
        in_specs=[
            pl.BlockSpec((seq, LANES), lambda b: (b, 0)),
            pl.BlockSpec((1, LANES), lambda b: (0, 0)),
        ],
        out_specs=pl.BlockSpec((1, seq, D_SEC), lambda b: (b, 0, 0)),
        out_shape=jax.ShapeDtypeStruct((batch, seq, D_SEC), BF16),
        compiler_params=_cparams("parallel"),
        name="forget_scan",
    )(f, bf)


def _head_sumsq_matrix():
    r = lax.broadcasted_iota(jnp.int32, (LANES, LANES), 0) // HEAD_DIM
    c = lax.broadcasted_iota(jnp.int32, (LANES, LANES), 1) // HEAD_DIM
    return (r == c).astype(BF16)


def _qk_norm(x, gain, gmat):
    ss = jnp.dot((x * x).astype(BF16), gmat, preferred_element_type=F32)
    return x * lax.rsqrt(ss * (1.0 / HEAD_DIM) + EPS) * gain


def _stack_heads(qb, head0):
    zero = jnp.zeros_like(qb)
    return jnp.concatenate([jnp.where(head0, qb, zero), jnp.where(head0, zero, qb)], axis=0)


def _unstack(a, head0):
    return jnp.where(head0, a[:TQ], a[TQ:])


def _dilated_kernel(q_ref, k_ref, v_ref, cos_ref, sin_ref, gq_ref, gk_ref, o_ref,
                    qn_s, kn_s, v_s, acc_s, m_s, l_s, bias_s, *, seq):
    gmat = _head_sumsq_matrix()
    lane = lax.broadcasted_iota(jnp.int32, (TQ, LANES), 1)
    head0 = lane < HEAD_DIM
    chunk = 512
    half = ROT_DIM // 2
    src = lax.broadcasted_iota(jnp.int32, (LANES, LANES), 0)
    dst = lax.broadcasted_iota(jnp.int32, (LANES, LANES), 1)
    dst_h = dst % HEAD_DIM
    perm = (((dst_h < half) & (src == dst + half))
            | ((dst_h >= half) & (dst_h < ROT_DIM) & (src == dst - half))).astype(BF16)

    def rope(t, cs, sn):
        hi = t.astype(BF16)
        lo = (t - hi.astype(F32)).astype(BF16)
        partner = (jnp.dot(hi, perm, preferred_element_type=F32) + jnp.dot(lo, perm, preferred_element_type=F32))
        return t * cs + partner * sn

    for c0 in range(0, seq, chunk):
        rows = pl.ds(c0, chunk)
        cs = cos_ref[0, rows, :]
        sn = sin_ref[0, rows, :]
        qn = _qk_norm(q_ref[0, rows, :].astype(F32), gq_ref[...], gmat)
        qn_s[rows, :] = rope(qn, cs, sn) * Q_SCALE_LOG2
        kn = _qk_norm(k_ref[0, rows, :].astype(F32), gk_ref[...], gmat)
        kn_s[rows, :] = rope(kn, cs, sn)
        v_s[rows, :] = v_ref[0, rows, :].astype(F32)

    u = lax.broadcasted_iota(jnp.int32, (2 * TQ, 2 * TQ), 0) % TQ
    c = lax.broadcasted_iota(jnp.int32, (2 * TQ, 2 * TQ), 1)
    for slot, dist0 in enumerate((0, TQ)):
        dist = dist0 + u - c
        bias_s[slot] = jnp.where((dist >= 0) & (dist <= WINDOW), 0.0, NEG)

    def scores(blk):
        p, q_rows, k_rows, nk, dist0 = blk
        qst = _stack_heads(qn_s[q_rows, :].astype(BF16), head0)
        s = lax.dot_general(qst, kn_s[k_rows, :].astype(BF16), NT_DIMS, preferred_element_type=F32)
        return s + bias_s[dist0 // TQ, :, 0:nk]

    def finish(blk, s):
        p, q_rows, k_rows, nk, dist0 = blk
        m = jnp.max(s, axis=-1, keepdims=True)
        pr = jnp.exp2(s - m)
        v_ones = jnp.concatenate([v_s[k_rows, :].astype(BF16), jnp.ones((nk, LANES), BF16)], axis=1)
        acc = jnp.dot(pr.astype(BF16), v_ones, preferred_element_type=F32)
        acc_s[p, q_rows, :] = _unstack(acc[:, :LANES], head0)
        m_s[p, q_rows, :] = _unstack(jnp.broadcast_to(m, (2 * TQ, LANES)), head0)
        l_s[p, q_rows, :] = _unstack(acc[:, LANES:], head0)

    blocks = []
    for i in range(seq // TQ):
        kb0 = max(i - 1, 0)
        blocks.append((0, pl.ds(i * TQ, TQ), pl.ds(kb0 * TQ, 2 * TQ), 2 * TQ, (i - kb0) * TQ))
    for r in range(4):
        for n in range(seq // (4 * TQ)):
            kb0 = max(n - 1, 0)
            blocks.append((1, pl.ds(4 * TQ * n + r, TQ, stride=4), pl.ds(4 * TQ * kb0 + r, 2 * TQ, stride=4),
                           2 * TQ, (n - kb0) * TQ))
    for r in range(16):
        rows = pl.ds(r, TQ, stride=16)
        blocks.append((2, rows, rows, TQ, 0))

    ahead = 4
    pending = [scores(b) for b in blocks[:ahead]]
    for idx, b in enumerate(blocks):
        s = pending.pop(0)
        if idx + ahead < len(blocks):
            pending.append(scores(blocks[idx + ahead]))
        finish(b, s)

    for c0 in range(0, seq, chunk):
        rows = pl.ds(c0, chunk)
        m0, m1, m2 = m_s[0, rows, :], m_s[1, rows, :], m_s[2, rows, :]
        mm = jnp.maximum(jnp.maximum(m0, m1), m2)
        e0, e1, e2 = jnp.exp2(m0 - mm), jnp.exp2(m1 - mm), jnp.exp2(m2 - mm)
        num = acc_s[0, rows, :] * e0 + acc_s[1, rows, :] * e1 + acc_s[2, rows, :] * e2
        den = l_s[0, rows, :] * e0 + l_s[1, rows, :] * e1 + l_s[2, rows, :] * e2
        o_ref[0, rows, :] = (num / den).astype(BF16)


def _dilated_attn(qkv3, cos_t, sin_t, gq, gk):
    batch, seq, _ = qkv3.shape
    sec = D_SEC // LANES
    blk = lambda s: pl.BlockSpec((1, seq, LANES), lambda b, h, s=s: (b, 0, s * sec + h))
    tab = pl.BlockSpec((1, seq, LANES), lambda b, h: (b, 0, 0))
    vec = pl.BlockSpec((1, LANES), lambda b, h: (0, 0))
    return pl.pallas_call(
        functools.partial(_dilated_kernel, seq=seq),
        grid=(batch, N_PAIRS),
        in_specs=[blk(0), blk(1), blk(2), tab, tab, vec, vec],
        out_specs=pl.BlockSpec((1, seq, LANES), lambda b, h: (b, 0, h)),
        out_shape=jax.ShapeDtypeStruct((batch, seq, D_SEC), BF16),
        scratch_shapes=[
            pltpu.VMEM((seq, LANES), F32), pltpu.VMEM((seq, LANES), F32), pltpu.VMEM((seq, LANES), F32),
            pltpu.VMEM((3, seq, LANES), F32), pltpu.VMEM((3, seq, LANES), F32),
            pltpu.VMEM((3, seq, LANES), F32), pltpu.VMEM((2, 2 * TQ, 2 * TQ), F32),
        ],
        compiler_params=_cparams("parallel", "parallel"),
        name="dilated_attn",
    )(qkv3, qkv3, qkv3, cos_t, sin_t, gq, gk)


def _fox_kernel(q_ref, k_ref, v_ref, kb_ref, gq_ref, gk_ref, o_ref, qat_s, ka_s, vt_s, acc_s, tri_s, *, seq):
    gmat = _head_sumsq_matrix()
    chunk = 512
    lane = lax.broadcasted_iota(jnp.int32, (chunk, LANES), 1)
    low = lane < HEAD_DIM
    sel_t = jnp.where(lax.broadcasted_iota(jnp.int32, (HEAD_DIM, TK), 0) < BIAS_TERMS, 1.0, 0.0).astype(BF16)

    for c0 in range(0, seq, chunk):
        rows = pl.ds(c0, chunk)
        for g in range(N_PAIRS):
            lanes = slice(g * LANES, (g + 1) * LANES)
            qn = _qk_norm(q_ref[0, rows, lanes].astype(F32), gq_ref[...], gmat) * Q_SCALE_LOG2
            qt = qn.T
            for cc in range(chunk // TK):
                ci, cols = c0 // TK + cc, slice(cc * TK, (cc + 1) * TK)
                even, odd = 2 * g * LANES, (2 * g + 1) * LANES
                qat_s[ci, even:even + HEAD_DIM, :] = qt[0:HEAD_DIM, cols].astype(BF16)
                qat_s[ci, even + HEAD_DIM:even + LANES, :] = sel_t
                qat_s[ci, odd:odd + HEAD_DIM, :] = sel_t
                qat_s[ci, odd + HEAD_DIM:odd + LANES, :] = qt[HEAD_DIM:LANES, cols].astype(BF16)
            kn = _qk_norm(k_ref[0, rows, lanes].astype(F32), gk_ref[...], gmat)
            kb = kb_ref[0, rows, lanes].astype(F32)
            ka_s[2 * g, rows, :] = jnp.where(low, kn, kb).astype(BF16)
            ka_s[2 * g + 1, rows, :] = jnp.where(low, kb, kn).astype(BF16)
            vt = v_ref[0, rows, lanes].astype(F32).T
            for cc in range(chunk // TK):
                vt_s[c0 // TK + cc, lanes, :] = vt[:, cc * TK:(cc + 1) * TK].astype(BF16)

    r = lax.broadcasted_iota(jnp.int32, (TK, TK), 0)
    c = lax.broadcasted_iota(jnp.int32, (TK, TK), 1)
    tri_s[...] = jnp.where(r <= c, 0.0, NEG)

    def q_block(qi, carry):
        q_rows = pl.ds(pl.multiple_of(qi * TK, TK), TK)
        for h in range(N_HEADS):
            acc_s[h] = jnp.zeros((HEAD_DIM, TK), F32)

        def step(j, st, diagonal):
            k_rows = pl.ds(pl.multiple_of(j * TK, TK), TK)

            def scores(h):
                s = jnp.dot(ka_s[h, k_rows, :], qat_s[qi, h * LANES:(h + 1) * LANES, :], preferred_element_type=F32)
                return s + tri_s[...] if diagonal else s

            ahead = 6
            pending = [scores(h) for h in range(ahead)]
            new = []
            for h in range(N_HEADS):
                m, l = st[h]
                s = pending.pop(0)
                if h + ahead < N_HEADS:
                    pending.append(scores(h + ahead))
                m_new = jnp.maximum(m, jnp.max(s, axis=0, keepdims=True))
                alpha = jnp.exp2(m - m_new)
                pr = jnp.exp2(s - m_new)
                l_new = alpha * l + jnp.sum(pr, axis=0, keepdims=True)
                pv = jnp.dot(vt_s[j, h * HEAD_DIM:(h + 1) * HEAD_DIM, :], pr.astype(BF16),
                             preferred_element_type=F32)
                acc_s[h] = alpha * acc_s[h] + pv
                new.append((m_new, l_new))
            return tuple(new)

        init = tuple((jnp.full((1, TK), NEG, F32), jnp.zeros((1, TK), F32)) for _ in range(N_HEADS))
        st = lax.fori_loop(0, qi // 2, lambda jj, st: step(2 * jj + 1, step(2 * jj, st, False), False), init)
        st = lax.cond(qi % 2 == 1, lambda st: step(qi - 1, st, False), lambda st: st, st)
        st = step(qi, st, True)
        for g in range(N_PAIRS):
            o2 = jnp.concatenate([acc_s[2 * g] / st[2 * g][1], acc_s[2 * g + 1] / st[2 * g + 1][1]], axis=0)
            o_ref[0, q_rows, g * LANES:(g + 1) * LANES] = o2.T.astype(BF16)
        return carry

    lax.fori_loop(0, seq // TK, q_block, 0)


def _fox_attn(qkv3, kbias, gq, gk):
    batch, seq, _ = qkv3.shape
    blk = lambda s: pl.BlockSpec((1, seq, D_SEC), lambda b, s=s: (b, 0, s))
    vec = pl.BlockSpec((1, LANES), lambda b: (0, 0))
    return pl.pallas_call(
        functools.partial(_fox_kernel, seq=seq),
        grid=(batch,),
        in_specs=[blk(3), blk(4), blk(5), pl.BlockSpec((1, seq, D_SEC), lambda b: (b, 0, 0)), vec, vec],
        out_specs=pl.BlockSpec((1, seq, D_SEC), lambda b: (b, 0, 0)),
        out_shape=jax.ShapeDtypeStruct((batch, seq, D_SEC), BF16),
        scratch_shapes=[
            pltpu.VMEM((seq // TK, N_HEADS * LANES, TK), BF16),
            pltpu.VMEM((N_HEADS, seq, LANES), BF16),
            pltpu.VMEM((seq // TK, D_SEC, TK), BF16),
            pltpu.VMEM((N_HEADS, HEAD_DIM, TK), F32),
            pltpu.VMEM((TK, TK), F32),
        ],
        compiler_params=_cparams("parallel"),
        name="fox_attn",
    )(qkv3, qkv3, qkv3, kbias, gq, gk)


def _out_router_kernel(oa_ref, ob_ref, x_ref, wo_ref, g_ref, wr_ref, br_ref,
                       h1_ref, msort_ref, route_ref, cnt_ref, wo_s, pick_s, mn_s):
    tm = TM_TOK

    @pl.when(pl.program_id(0) == 0)
    def _():
        wo_s[...] = wo_ref[...].astype(BF16)
        pick_s[...] = jnp.zeros_like(pick_s)
        mn_s[...] = jnp.zeros_like(mn_s)

    h1 = (x_ref[...]
          + jnp.dot(oa_ref[...], wo_s[0:D_SEC, :], preferred_element_type=F32)
          + jnp.dot(ob_ref[...], wo_s[D_SEC:2 * D_SEC, :], preferred_element_type=F32))
    h1_ref[...] = h1
    ms = jnp.mean(h1 * h1, axis=-1, keepdims=True)
    mn = h1 * lax.rsqrt(ms + EPS) * g_ref[...]

    logits = jnp.dot(mn.astype(BF16), wr_ref[...], preferred_element_type=F32) + br_ref[...]

    msort_ref[...] = jnp.dot(pick_s[...], mn_s[...], preferred_element_type=F32).astype(BF16)

    lane = lax.broadcasted_iota(jnp.int32, (tm, LANES), 1).astype(F32)
    big = float(LANES)

    def first_argmax(vals):
        vmax = jnp.max(vals, axis=-1, keepdims=True)
        idx = jnp.min(jnp.where(vals == vmax, lane, big), axis=-1, keepdims=True)
        return vmax, idx

    lg = jnp.where(lane < N_GROUPS, logits, -jnp.inf)
    gmax, gidx = first_argmax(lg)
    gw = 1.0 / jnp.sum(jnp.exp(lg - gmax), axis=-1, keepdims=True)
    lo = ROUTER_LANE0 + EXPERTS_PER_GROUP * gidx
    le = jnp.where((lane >= lo) & (lane < lo + EXPERTS_PER_GROUP), logits, -jnp.inf)
    v0, i0 = first_argmax(le)
    v1, i1 = first_argmax(jnp.where(lane == i0, -jnp.inf, le))
    ex = jnp.exp(v1 - v0)
    w0 = gw / (1.0 + ex)
    w1 = gw * ex / (1.0 + ex)

    sel0 = lane == i0
    sel1 = lane == i1
    onehot = jnp.where(sel0 | sel1, 1.0, 0.0)
    r = lax.broadcasted_iota(jnp.int32, (tm, tm), 0)
    c = lax.broadcasted_iota(jnp.int32, (tm, tm), 1)
    ltri = (c <= r).astype(BF16)
    incl = jnp.dot(ltri, onehot.astype(BF16), preferred_element_type=F32)
    excl = incl - onehot
    counts = incl[tm - 1:tm, :]
    cnt_ref[0] = jnp.broadcast_to(counts, (SUBLANES, LANES))

    seg_rows = jnp.floor((counts + (SEG_ALIGN - 1.0)) * (1.0 / SEG_ALIGN)) * SEG_ALIGN
    seg_rows = jnp.broadcast_to(seg_rows, (tm, LANES))
    lp0 = jnp.sum(jnp.where(lane < i0, seg_rows, 0.0) + jnp.where(sel0, excl, 0.0), axis=-1, keepdims=True)
    lp1 = jnp.sum(jnp.where(lane < i1, seg_rows, 0.0) + jnp.where(sel1, excl, 0.0), axis=-1, keepdims=True)

    rec = jnp.zeros((tm, LANES), F32)
    for pos, val in enumerate((i0 - ROUTER_LANE0, i1 - ROUTER_LANE0, lp0, lp1, w0, w1)):
        rec = jnp.where(lane == float(pos), val, rec)
    route_ref[...] = rec[:, 0:SUBLANES]

    rec_t = rec.T
    out_row = lax.broadcasted_iota(jnp.int32, (SORT_ROWS, tm), 0).astype(F32)
    pick_s[...] = jnp.where((out_row == rec_t[2:3, :]) | (out_row == rec_t[3:4, :]), 1.0, 0.0).astype(BF16)
    mn_s[...] = mn.astype(BF16)


def _out_router(oa, ob, x2, w_o, g_ffn, w_r, b_r):
    T = x2.shape[0]
    tm = TM_TOK
    last = T // tm - 1
    row = lambda w: pl.BlockSpec((tm, w), lambda i: (jnp.minimum(i, last), 0))
    const = lambda a, b: pl.BlockSpec((a, b), lambda i: (0, 0))
    return pl.pallas_call(
        _out_router_kernel,
        grid=(T // tm + 1,),
        in_specs=[row(D_SEC), row(D_SEC), row(D_MODEL), _resident((D_MODEL, D_MODEL)), const(1, D_MODEL),
                  const(D_MODEL, LANES), const(1, LANES)],
        out_specs=[row(D_MODEL), pl.BlockSpec((SORT_ROWS, D_MODEL), lambda i: (jnp.maximum(i - 1, 0), 0)),
                   row(SUBLANES), pl.BlockSpec((1, SUBLANES, LANES), lambda i: (jnp.minimum(i, last), 0, 0))],
        out_shape=[
            jax.ShapeDtypeStruct((T, D_MODEL), F32),
            jax.ShapeDtypeStruct((T // tm * SORT_ROWS, D_MODEL), BF16),
            jax.ShapeDtypeStruct((T, SUBLANES), F32),
            jax.ShapeDtypeStruct((T // tm, SUBLANES, LANES), F32),
        ],
        scratch_shapes=[pltpu.VMEM((D_MODEL, D_MODEL), BF16), pltpu.VMEM((SORT_ROWS, tm), BF16),
                        pltpu.VMEM((tm, D_MODEL), BF16)],
        compiler_params=_cparams("arbitrary"),
        name="out_router",
    )(oa, ob, x2, w_o, g_ffn, w_r, b_r)


MOE_CHUNKS = TM_MOE // SEG_ALIGN


def _chunk_row(c):
    return pl.multiple_of(c * SEG_ALIGN, SEG_ALIGN)


def _moe_kernel(tile_e_ref, nused_ref, src_ref, next_e_ref, slot_e_ref, msort_ref, wup_ref, wdn_ref, ys_ref,
                x_s, wup_f, wdn_f, wup_s, wdn_s, xsem, wsem):
    i = pl.program_id(0)
    nused = nused_ref[0]

    def gather(k, act):
        def chunk(c, carry):
            src = pl.multiple_of(src_ref[k * MOE_CHUNKS + c], SEG_ALIGN)
            act(pltpu.make_async_copy(msort_ref.at[pl.ds(src, SEG_ALIGN)],
                                      x_s.at[k % 2, pl.ds(_chunk_row(c), SEG_ALIGN)], xsem.at[k % 2]))
            return carry
        lax.fori_loop(0, MOE_CHUNKS, chunk, 0, unroll=8)

    def weights(e, s, act):
        act(pltpu.make_async_copy(wup_ref.at[e], wup_f.at[s], wsem.at[0, s]))
        act(pltpu.make_async_copy(wdn_ref.at[e], wdn_f.at[s], wsem.at[1, s]))

    @pl.when(i == 0)
    def _():
        gather(0, lambda c: c.start())
        weights(tile_e_ref[0], 0, lambda c: c.start())

    @pl.when(i + 1 < nused)
    def _():
        gather(i + 1, lambda c: c.start())

    @pl.when(i < nused)
    def _():
        e = tile_e_ref[i]

        @pl.when((i == 0) | (tile_e_ref[jnp.maximum(i - 1, 0)] != e))
        def _():
            s = slot_e_ref[e]
            weights(e, s, lambda c: c.wait())

            @pl.when(next_e_ref[e] >= 0)
            def _():
                weights(next_e_ref[e], 1 - s, lambda c: c.start())
            wup_s[...] = wup_f[s].astype(BF16)
            wdn_s[...] = wdn_f[s].astype(BF16)

        gather(i, lambda c: c.wait())
        hu = jnp.dot(x_s[i % 2], wup_s[...], preferred_element_type=F32)
        gate = hu[:, :D_EXPERT]
        hid = gate * (1.0 / (1.0 + jnp.exp(-gate))) * hu[:, D_EXPERT:]
        ys_ref[...] = jnp.dot(hid.astype(BF16), wdn_s[...], preferred_element_type=F32).astype(BF16)

    @pl.when(i >= nused)
    def _():
        ys_ref[...] = jnp.zeros_like(ys_ref)


def _moe_experts(tile_e, nused, src_chunk, next_e, slot_e, msort, w_up, w_down):
    n_tiles = tile_e.shape[0]
    grid_spec = pltpu.PrefetchScalarGridSpec(
        num_scalar_prefetch=5,
        grid=(n_tiles,),
        in_specs=[pl.BlockSpec(memory_space=pl.ANY)] * 3,
        out_specs=pl.BlockSpec((TM_MOE, D_MODEL), lambda i, *_: (i, 0)),
        scratch_shapes=[
            pltpu.VMEM((2, TM_MOE, D_MODEL), BF16),
            pltpu.VMEM((2, D_MODEL, 2 * D_EXPERT), F32), pltpu.VMEM((2, D_EXPERT, D_MODEL), F32),
            pltpu.VMEM((D_MODEL, 2 * D_EXPERT), BF16), pltpu.VMEM((D_EXPERT, D_MODEL), BF16),
            pltpu.SemaphoreType.DMA((2,)), pltpu.SemaphoreType.DMA((2, 2)),
        ],
    )
    return pl.pallas_call(
        _moe_kernel,
        grid_spec=grid_spec,
        out_shape=jax.ShapeDtypeStruct((n_tiles * TM_MOE, D_MODEL), BF16),
        compiler_params=_cparams("arbitrary"),
        name="moe_experts",
    )(tile_e, nused, src_chunk, next_e, slot_e, msort, w_up, w_down)


def _combine_kernel(src_ref, ys_ref, h1_ref, route_ref, p_ref,
                    g_ref, wg_ref, wp_ref, o_ref, gath_s, sems, wg_s, wp_s, h2_s, n_s):
    i = pl.program_id(0)
    n_steps = pl.num_programs(0)
    last_tile = n_steps - 2

    tm = TM_TOK
    slot = i % 2

    def gather(t, s, act):
        for c in range(SORT_CHUNKS):
            src = pl.multiple_of(src_ref[t * SORT_CHUNKS + c], SEG_ALIGN)
            act(pltpu.make_async_copy(ys_ref.at[pl.ds(src, SEG_ALIGN)],
                                      gath_s.at[s, pl.ds(c * SEG_ALIGN, SEG_ALIGN)], sems.at[s]))

    @pl.when(i == 0)
    def _():
        wg_s[...] = wg_ref[...].astype(BF16)
        wp_s[...] = wp_ref[...].astype(BF16)
        h2_s[...] = jnp.zeros_like(h2_s)
        n_s[...] = jnp.zeros_like(n_s)
        gather(0, 0, lambda c: c.start())

    cur = jnp.minimum(i, last_tile)
    nxt = jnp.minimum(i + 1, last_tile)
    gather(cur, slot, lambda c: c.wait())
    gather(nxt, 1 - slot, lambda c: c.start())

    ple = jnp.dot(p_ref[...].astype(BF16), wp_s[...], preferred_element_type=F32)
    z = jnp.dot(n_s[...], wg_s[...], preferred_element_type=F32)
    o_ref[...] = h2_s[...] + (1.0 / (1.0 + jnp.exp(-z))) * ple

    route = route_ref[...]
    col = lax.broadcasted_iota(jnp.int32, (tm, SORT_ROWS), 1).astype(F32)
    weights = (jnp.where(col == route[:, 2:3], route[:, 4:5], 0.0)
               + jnp.where(col == route[:, 3:4], route[:, 5:6], 0.0)).astype(BF16)
    y = jnp.dot(weights, gath_s[slot], preferred_element_type=F32)
    h2 = h1_ref[...] + y
    ms = jnp.mean(h2 * h2, axis=-1, keepdims=True)
    h2_s[...] = h2
    n_s[...] = (h2 * lax.rsqrt(ms + EPS) * g_ref[...]).astype(BF16)

    @pl.when(i == n_steps - 1)
    def _():
        gather(nxt, 1 - slot, lambda c: c.wait())


def _combine_ple(chunk_dst, ys, h1, route, p2, g_ple, w_gate, w_proj):
    T = h1.shape[0]
    tm = TM_TOK
    last = T // tm - 1
    cur = lambda w: pl.BlockSpec((tm, w), lambda i, *_: (jnp.minimum(i, last), 0))
    prev = lambda w: pl.BlockSpec((tm, w), lambda i, *_: (jnp.maximum(i - 1, 0), 0))
    const = lambda a, b: pl.BlockSpec((a, b), lambda i, *_: (0, 0))
    grid_spec = pltpu.PrefetchScalarGridSpec(
        num_scalar_prefetch=1,
        grid=(T // tm + 1,),
        in_specs=[pl.BlockSpec(memory_space=pl.ANY), cur(D_MODEL), cur(SUBLANES), prev(PLE_DIM),
                  const(1, D_MODEL), _resident((D_MODEL, D_MODEL)), _resident((PLE_DIM, D_MODEL))],
        out_specs=prev(D_MODEL),
        scratch_shapes=[pltpu.VMEM((2, SORT_ROWS, D_MODEL), BF16), pltpu.SemaphoreType.DMA((2,)),
                        pltpu.VMEM((D_MODEL, D_MODEL), BF16), pltpu.VMEM((PLE_DIM, D_MODEL), BF16),
                        pltpu.VMEM((tm, D_MODEL), F32), pltpu.VMEM((tm, D_MODEL), BF16)],
    )
    return pl.pallas_call(
        _combine_kernel,
        grid_spec=grid_spec,
        out_shape=jax.ShapeDtypeStruct((T, D_MODEL), F32),
        compiler_params=_cparams("arbitrary"),
        name="combine_ple",
    )(chunk_dst, ys, h1, route, p2, g_ple, w_gate, w_proj)


def _rope_tables(positions):
    half = ROT_DIM // 2
    inv = ROPE_THETA ** (-jnp.arange(0, ROT_DIM, 2, dtype=F32) / ROT_DIM)
    ang = positions.astype(F32)[:, None, :] * inv[None, :, None]
    lane = jnp.arange(LANES) % HEAD_DIM
    freq = jnp.arange(half)[:, None]
    rot = (lane[None, :] < ROT_DIM) & (lane[None, :] % half == freq)
    e_cos = rot.astype(BF16)
    e_sin = jnp.where(rot, jnp.where(lane[None, :] < half, -1.0, 1.0), 0.0).astype(BF16)

    def expand(tab, sel):
        hi = tab.astype(BF16)
        lo = (tab - hi.astype(F32)).astype(BF16)
        pick = functools.partial(jnp.einsum, "bfs,fl->bsl", preferred_element_type=F32)
        return pick(hi, sel) + pick(lo, sel)

    return expand(jnp.cos(ang), e_cos) + (lane >= ROT_DIM).astype(F32), expand(jnp.sin(ang), e_sin)


def _layer(i, h, p, cos_t, sin_t, g_mix, w_in, b_f, qn_a, kn_a, qn_b, kn_b, w_o, g_ffn, w_rg, b_rg,
           w_re, b_re, w_up, w_down, g_ple, w_ple_gate, w_ple_proj):
    B, S, _ = h.shape
    T = B * S
    x2 = h.reshape(T, D_MODEL)

    bf = jnp.zeros((1, LANES), F32).at[0, :N_HEADS].set(b_f[i])
    pair = lambda g: jnp.tile(g, 2).reshape(1, LANES)
    w_r = jnp.zeros((D_MODEL, LANES), F32)
    w_r = w_r.at[:, :N_GROUPS].set(w_rg[i])
    w_r = w_r.at[:, ROUTER_LANE0:ROUTER_LANE0 + N_EXPERTS].set(
        jnp.transpose(w_re[i], (1, 0, 2)).reshape(D_MODEL, N_EXPERTS))
    b_r = jnp.zeros((1, LANES), F32).at[0, :N_GROUPS].set(b_rg[i])
    b_r = b_r.at[0, ROUTER_LANE0:ROUTER_LANE0 + N_EXPERTS].set(b_re[i].reshape(-1))

    qkv, f = _in_proj(x2, g_mix[i].reshape(1, -1), w_in, i)
    kbias = _forget_scan(f, bf, B, S)
    qkv3 = qkv.reshape(B, S, N_QKV)
    oa = _dilated_attn(qkv3, cos_t, sin_t, pair(qn_a[i]), pair(kn_a[i]))
    ob = _fox_attn(qkv3, kbias, pair(qn_b[i]), pair(kn_b[i]))

    h1, msort, route, cnt = _out_router(oa.reshape(T, D_SEC), ob.reshape(T, D_SEC), x2, w_o[i],
                                        g_ffn[i].reshape(1, -1), w_r.astype(BF16), b_r)

    n_tok_tiles = T // TM_TOK
    counts = cnt[:, 0, ROUTER_LANE0:ROUTER_LANE0 + N_EXPERTS].astype(jnp.int32)
    seg_rows = (counts + SEG_ALIGN - 1) // SEG_ALIGN * SEG_ALIGN
    seg_local = jnp.cumsum(seg_rows, axis=1) - seg_rows
    rows_e = jnp.sum(seg_rows, axis=0)
    tile_end = jnp.cumsum((rows_e + TM_MOE - 1) // TM_MOE)
    off = jnp.concatenate([jnp.zeros((1,), jnp.int32), tile_end * TM_MOE]).astype(jnp.int32)
    seg_global = off[None, :N_EXPERTS] + jnp.cumsum(seg_rows, axis=0) - seg_rows
    n_tiles = (2 * T + n_tok_tiles * N_EXPERTS * (SEG_ALIGN - 1)) // TM_MOE + N_EXPERTS
    nused = tile_end[-1:].astype(jnp.int32)
    tile_ids = jnp.minimum(jnp.arange(n_tiles, dtype=jnp.int32), nused[0] - 1)
    tile_e = jnp.sum((tile_ids[:, None] >= tile_end[None, :]).astype(jnp.int32), axis=1).astype(jnp.int32)
    chunk_row = jnp.arange(SORT_CHUNKS, dtype=jnp.int32)[None, :, None] * SEG_ALIGN
    lo, hi = seg_local[:, None, :], (seg_local + seg_rows)[:, None, :]
    chunk_dst = jnp.sum(jnp.where((chunk_row >= lo) & (chunk_row < hi), seg_global[:, None, :] + chunk_row - lo, 0),
                        axis=2).reshape(-1).astype(jnp.int32)
    sorted_row = jnp.arange(n_tiles * MOE_CHUNKS, dtype=jnp.int32)[:, None] * SEG_ALIGN
    seg_lo = seg_global.reshape(1, -1)
    seg_src = (jnp.arange(n_tok_tiles, dtype=jnp.int32)[:, None] * SORT_ROWS + seg_local).reshape(1, -1)
    in_seg = (sorted_row >= seg_lo) & (sorted_row < seg_lo + seg_rows.reshape(1, -1))
    src_chunk = jnp.where(jnp.any(in_seg, axis=1),
                          jnp.sum(jnp.where(in_seg, seg_src + sorted_row - seg_lo, 0), axis=1),
                          SORT_ROWS - SEG_ALIGN).astype(jnp.int32)
    e_ids = jnp.arange(N_EXPERTS, dtype=jnp.int32)
    active = rows_e > 0
    later_active = jnp.where(active[None, :] & (e_ids[None, :] > e_ids[:, None]), e_ids[None, :], N_EXPERTS)
    next_e = jnp.min(later_active, axis=1)
    next_e = jnp.where(next_e == N_EXPERTS, -1, next_e).astype(jnp.int32)
    slot_e = ((jnp.cumsum(active.astype(jnp.int32)) - 1) % 2).astype(jnp.int32)

    ys = _moe_experts(tile_e, nused, src_chunk, next_e, slot_e, msort, w_up[i], w_down[i])
    out = _combine_ple(chunk_dst, ys, h1, route, p[i].reshape(T, PLE_DIM),
                       g_ple[i].reshape(1, -1), w_ple_gate[i], w_ple_proj[i])
    return out.reshape(B, S, D_MODEL)


def kernel(x, p, positions, g_mix, w_in, b_f, qn_a, kn_a, qn_b, kn_b, w_o, g_ffn, w_rg, b_rg, w_re, b_re,
           w_up, w_down, g_ple, w_ple_gate, w_ple_proj):
    cos_t, sin_t = _rope_tables(positions)
    h = x
    for i in range(p.shape[0]):
        h = _layer(i, h, p, cos_t, sin_t, g_mix, w_in, b_f, qn_a, kn_a, qn_b, kn_b, w_o, g_ffn, w_rg, b_rg,
                   w_re, b_re, w_up, w_down, g_ple, w_ple_gate, w_ple_proj)
    return h
```

```python
import functools
import math

import jax
import jax.numpy as jnp
from jax import lax
from jax.experimental import pallas as pl
from jax.experimental.pallas import tpu as pltpu

D_MODEL = 1024
HEAD_DIM = 64
N_HEADS = 8
D_SEC = N_HEADS * HEAD_DIM
N_QKV = 6 * D_SEC
ROT_DIM = HEAD_DIM // 4
ROPE_THETA = 500000.0
N_GROUPS = 4
EXPERTS_PER_GROUP = 8
N_EXPERTS = N_GROUPS * EXPERTS_PER_GROUP
D_EXPERT = 512
PLE_DIM = 256
EPS = 1e-6
NEG = -1e30
WINDOW = 128

LANES = 128
SUBLANES = 8
VMEM_LIMIT = 48 * 1024 * 1024

TM_PROJ = 512
TQ = 128
TK = 256
N_PAIRS = N_HEADS // 2
TM_MOE = 512
TM_TOK = 512
SEG_ALIGN = 2 * SUBLANES
SORT_ROWS = 2 * TM_TOK + N_EXPERTS * SEG_ALIGN
SORT_CHUNKS = SORT_ROWS // SEG_ALIGN
ROUTER_LANE0 = N_GROUPS

Q_SCALE_LOG2 = math.log2(math.e) / math.sqrt(HEAD_DIM)

F32 = jnp.float32
BF16 = jnp.bfloat16
NT_DIMS = (((1,), (1,)), ((), ()))


def _cparams(*sem):
    return pltpu.CompilerParams(dimension_semantics=sem, vmem_limit_bytes=VMEM_LIMIT)


def _resident(shape):
    return pl.BlockSpec(shape, lambda i, *_: (0,) * len(shape), pipeline_mode=pl.Buffered(1))


def _in_proj_kernel(x_ref, g_ref, w_ref, qkv_ref, f_ref, w_s, wf_s):
    @pl.when(pl.program_id(0) == 0)
    def _():
        w_s[...] = w_ref[0, :, 0:N_QKV].astype(BF16)
        wf_s[...] = jnp.zeros_like(wf_s)
        wf_s[:, 0:N_HEADS] = w_ref[0, :, N_QKV:N_QKV + N_HEADS].astype(BF16)

    x = x_ref[...]
    ms = jnp.mean(x * x, axis=-1, keepdims=True)
    a = (x * lax.rsqrt(ms + EPS) * g_ref[...]).astype(BF16)
    qkv_ref[...] = jnp.dot(a, w_s[...], preferred_element_type=F32).astype(BF16)
    f_ref[...] = jnp.dot(a, wf_s[...], preferred_element_type=F32)


def _in_proj(x2, g_mix, w_in, layer):
    T = x2.shape[0]
    w_block = (1,) + w_in.shape[1:]
    return pl.pallas_call(
        _in_proj_kernel,
        grid=(T // TM_PROJ,),
        in_specs=[
            pl.BlockSpec((TM_PROJ, D_MODEL), lambda i: (i, 0)),
            pl.BlockSpec((1, D_MODEL), lambda i: (0, 0)),
            pl.BlockSpec(w_block, lambda i: (layer, 0, 0), pipeline_mode=pl.Buffered(1)),
        ],
        out_specs=[
            pl.BlockSpec((TM_PROJ, N_QKV), lambda i: (i, 0)),
            pl.BlockSpec((TM_PROJ, LANES), lambda i: (i, 0)),
        ],
        out_shape=[
            jax.ShapeDtypeStruct((T, N_QKV), BF16),
            jax.ShapeDtypeStruct((T, LANES), F32),
        ],
        scratch_shapes=[pltpu.VMEM((D_MODEL, N_QKV), BF16), pltpu.VMEM((D_MODEL, LANES), BF16)],
        compiler_params=_cparams("arbitrary"),
        name="in_proj",
    )(x2, g_mix, w_in)


BIAS_TERMS = 3


def _bias_lane(head):
    return (head // 2) * LANES + (HEAD_DIM if head % 2 == 0 else 0)


def _forget_scan_kernel(f_ref, bf_ref, kb_ref, *, seq):
    f = f_ref[...] + bf_ref[...]
    logf = jnp.minimum(f, 0.0) - jnp.log1p(jnp.exp(-jnp.abs(f)))

    def split_bf16(v):
        terms = []
        for _ in range(BIAS_TERMS):
            t = v.astype(BF16)
            terms.append(t)
            v = v - t.astype(F32)
        return terms

    tri = (lax.broadcasted_iota(jnp.int32, (TK, TK), 1) <= lax.broadcasted_iota(jnp.int32, (TK, TK), 0)).astype(BF16)
    carry = jnp.zeros((1, LANES), F32)
    chunks = []
    for j in range(seq // TK):
        terms = jnp.concatenate(split_bf16(logf[j * TK:(j + 1) * TK, :]), axis=1)
        pre = jnp.dot(tri, terms, preferred_element_type=F32)
        cj = carry + sum(pre[:, t * LANES:(t + 1) * LANES] for t in range(BIAS_TERMS))
        carry = cj[TK - 1:TK, :]
        chunks.append(cj)
    c = jnp.concatenate(chunks, axis=0)
    rest = c * (-math.log2(math.e))
    r_idx = lax.broadcasted_iota(jnp.int32, (LANES, D_SEC), 0)
    c_idx = lax.broadcasted_iota(jnp.int32, (LANES, D_SEC), 1)
    base = (r_idx // 2) * LANES + jnp.where(r_idx % 2 == 0, HEAD_DIM, 0)
    out = jnp.zeros((seq, D_SEC), F32)
    for t, term in enumerate(split_bf16(rest)):
        place = ((r_idx < N_HEADS) & (c_idx == base + t)).astype(BF16)
        out = out + jnp.dot(term, place, preferred_element_type=F32)
    kb_ref[0] = out.astype(BF16)


def _forget_scan(f, bf, batch, seq):
    return pl.pallas_call(
        functools.partial(_forget_scan_kernel, seq=seq),
        grid=(batch,),
        in_specs=[
            pl.BlockSpec((seq, LANES), lambda b: (b, 0)),
            pl.BlockSpec((1, LANES), lambda b: (0, 0)),
        ],
        out_specs=pl.BlockSpec((1, seq, D_SEC), lambda b: (b, 0, 0)),
        out_shape=jax.ShapeDtypeStruct((batch, seq, D_SEC), BF16),
        compiler_params=_cparams("parallel"),
        name="forget_scan",
    )(f, bf)


def _head_sumsq_matrix():
    r = lax.broadcasted_iota(jnp.int32, (LANES, LANES), 0) // HEAD_DIM
    c = lax.broadcasted_iota(jnp.int32, (LANES, LANES), 1) // HEAD_DIM
    return (r == c).astype(BF16)


def _qk_norm(x, gain, gmat):
    ss = jnp.dot((x * x).astype(BF16), gmat, preferred_element_type=F32)
    return x * lax.rsqrt(ss * (1.0 / HEAD_DIM) + EPS) * gain


def _stack_heads(qb, head0):
    zero = jnp.zeros_like(qb)
    return jnp.concatenate([jnp.where(head0, qb, zero), jnp.where(head0, zero, qb)], axis=0)


def _unstack(a, head0):
    return jnp.where(head0, a[:TQ], a[TQ:])


def _dilated_kernel(q_ref, k_ref, v_ref, cos_ref, sin_ref, gq_ref, gk_ref, o_ref,
                    qn_s, kn_s, v_s, acc_s, m_s, l_s, bias_s, *, seq):
    gmat = _head_sumsq_matrix()
    lane = lax.broadcasted_iota(jnp.int32, (TQ, LANES), 1)
    head0 = lane < HEAD_DIM
    chunk = 512
    half = ROT_DIM // 2
    src = lax.broadcasted_iota(jnp.int32, (LANES, LANES), 0)
    dst = lax.broadcasted_iota(jnp.int32, (LANES, LANES), 1)
    dst_h = dst % HEAD_DIM
    perm = (((dst_h < half) & (src == dst + half))
            | ((dst_h >= half) & (dst_h < ROT_DIM) & (src == dst - half))).astype(BF16)

    def rope(t, cs, sn):
        hi = t.astype(BF16)
        lo = (t - hi.astype(F32)).astype(BF16)
        partner = (jnp.dot(hi, perm, preferred_element_type=F32) + jnp.dot(lo, perm, preferred_element_type=F32))
        return t * cs + partner * sn

    for c0 in range(0, seq, chunk):
        rows = pl.ds(c0, chunk)
        cs = cos_ref[0, rows, :]
        sn = sin_ref[0, rows, :]
        qn = _qk_norm(q_ref[0, rows, :].astype(F32), gq_ref[...], gmat)
        qn_s[rows, :] = rope(qn, cs, sn) * Q_SCALE_LOG2
        kn = _qk_norm(k_ref[0, rows, :].astype(F32), gk_ref[...], gmat)
        kn_s[rows, :] = rope(kn, cs, sn)
        v_s[rows, :] = v_ref[0, rows, :].astype(F32)

    u = lax.broadcasted_iota(jnp.int32, (2 * TQ, 2 * TQ), 0) % TQ
    c = lax.broadcasted_iota(jnp.int32, (2 * TQ, 2 * TQ), 1)
    for slot, dist0 in enumerate((0, TQ)):
        dist = dist0 + u - c
        bias_s[slot] = jnp.where((dist >= 0) & (dist <= WINDOW), 0.0, NEG)

    def scores(blk):
        p, q_rows, k_rows, nk, dist0 = blk
        qst = _stack_heads(qn_s[q_rows, :].astype(BF16), head0)
        s = lax.dot_general(qst, kn_s[k_rows, :].astype(BF16), NT_DIMS, preferred_element_type=F32)
        return s + bias_s[dist0 // TQ, :, 0:nk]

    def finish(blk, s):
        p, q_rows, k_rows, nk, dist0 = blk
        m = jnp.max(s, axis=-1, keepdims=True)
        pr = jnp.exp2(s - m)
        v_ones = jnp.concatenate([v_s[k_rows, :].astype(BF16), jnp.ones((nk, LANES), BF16)], axis=1)
        acc = jnp.dot(pr.astype(BF16), v_ones, preferred_element_type=F32)
        acc_s[p, q_rows, :] = _unstack(acc[:, :LANES], head0)
        m_s[p, q_rows, :] = _unstack(jnp.broadcast_to(m, (2 * TQ, LANES)), head0)
        l_s[p, q_rows, :] = _unstack(acc[:, LANES:], head0)

    blocks = []
    for i in range(seq // TQ):
        kb0 = max(i - 1, 0)
        blocks.append((0, pl.ds(i * TQ, TQ), pl.ds(kb0 * TQ, 2 * TQ), 2 * TQ, (i - kb0) * TQ))
    for r in range(4):
        for n in range(seq // (4 * TQ)):
            kb0 = max(n - 1, 0)
            blocks.append((1, pl.ds(4 * TQ * n + r, TQ, stride=4), pl.ds(4 * TQ * kb0 + r, 2 * TQ, stride=4),
                           2 * TQ, (n - kb0) * TQ))
    for r in range(16):
        rows = pl.ds(r, TQ, stride=16)
        blocks.append((2, rows, rows, TQ, 0))

    ahead = 2
    pending = [scores(b) for b in blocks[:ahead]]
    for idx, b in enumerate(blocks):
        s = pending.pop(0)
        if idx + ahead < len(blocks):
            pending.append(scores(blocks[idx + ahead]))
        finish(b, s)

    for c0 in range(0, seq, chunk):
        rows = pl.ds(c0, chunk)
        m0, m1, m2 = m_s[0, rows, :], m_s[1, rows, :], m_s[2, rows, :]
        mm = jnp.maximum(jnp.maximum(m0, m1), m2)
        e0, e1, e2 = jnp.exp2(m0 - mm), jnp.exp2(m1 - mm), jnp.exp2(m2 - mm)
        num = acc_s[0, rows, :] * e0 + acc_s[1, rows, :] * e1 + acc_s[2, rows, :] * e2
        den = l_s[0, rows, :] * e0 + l_s[1, rows, :] * e1 + l_s[2, rows, :] * e2
        o_ref[0, rows, :] = (num / den).astype(BF16)


def _dilated_attn(qkv3, cos_t, sin_t, gq, gk):
    batch, seq, _ = qkv3.shape
    sec = D_SEC // LANES
    blk = lambda s: pl.BlockSpec((1, seq, LANES), lambda b, h, s=s: (b, 0, s * sec + h))
    tab = pl.BlockSpec((1, seq, LANES), lambda b, h: (b, 0, 0))
    vec = pl.BlockSpec((1, LANES), lambda b, h: (0, 0))
    return pl.pallas_call(
        functools.partial(_dilated_kernel, seq=seq),
        grid=(batch, N_PAIRS),
        in_specs=[blk(0), blk(1), blk(2), tab, tab, vec, vec],
        out_specs=pl.BlockSpec((1, seq, LANES), lambda b, h: (b, 0, h)),
        out_shape=jax.ShapeDtypeStruct((batch, seq, D_SEC), BF16),
        scratch_shapes=[
            pltpu.VMEM((seq, LANES), F32), pltpu.VMEM((seq, LANES), F32), pltpu.VMEM((seq, LANES), F32),
            pltpu.VMEM((3, seq, LANES), F32), pltpu.VMEM((3, seq, LANES), F32),
            pltpu.VMEM((3, seq, LANES), F32), pltpu.VMEM((2, 2 * TQ, 2 * TQ), F32),
        ],
        compiler_params=_cparams("parallel", "parallel"),
        name="dilated_attn",
    )(qkv3, qkv3, qkv3, cos_t, sin_t, gq, gk)


def _fox_kernel(q_ref, k_ref, v_ref, kb_ref, gq_ref, gk_ref, o_ref, qat_s, ka_s, vt_s, acc_s, tri_s, *, seq):
    gmat = _head_sumsq_matrix()
    chunk = 512
    lane = lax.broadcasted_iota(jnp.int32, (chunk, LANES), 1)
    low = lane < HEAD_DIM
    sel_t = jnp.where(lax.broadcasted_iota(jnp.int32, (HEAD_DIM, TK), 0) < BIAS_TERMS, 1.0, 0.0).astype(BF16)

    for c0 in range(0, seq, chunk):
        rows = pl.ds(c0, chunk)
        for g in range(N_PAIRS):
            lanes = slice(g * LANES, (g + 1) * LANES)
            qn = _qk_norm(q_ref[0, rows, lanes].astype(F32), gq_ref[...], gmat) * Q_SCALE_LOG2
            qt = qn.T
            for cc in range(chunk // TK):
                ci, cols = c0 // TK + cc, slice(cc * TK, (cc + 1) * TK)
                even, odd = 2 * g * LANES, (2 * g + 1) * LANES
                qat_s[ci, even:even + HEAD_DIM, :] = qt[0:HEAD_DIM, cols].astype(BF16)
                qat_s[ci, even + HEAD_DIM:even + LANES, :] = sel_t
                qat_s[ci, odd:odd + HEAD_DIM, :] = sel_t
                qat_s[ci, odd + HEAD_DIM:odd + LANES, :] = qt[HEAD_DIM:LANES, cols].astype(BF16)
            kn = _qk_norm(k_ref[0, rows, lanes].astype(F32), gk_ref[...], gmat)
            kb = kb_ref[0, rows, lanes].astype(F32)
            ka_s[2 * g, rows, :] = jnp.where(low, kn, kb).astype(BF16)
            ka_s[2 * g + 1, rows, :] = jnp.where(low, kb, kn).astype(BF16)
            vt = v_ref[0, rows, lanes].astype(F32).T
            for cc in range(chunk // TK):
                vt_s[c0 // TK + cc, lanes, :] = vt[:, cc * TK:(cc + 1) * TK].astype(BF16)

    r = lax.broadcasted_iota(jnp.int32, (TK, TK), 0)
    c = lax.broadcasted_iota(jnp.int32, (TK, TK), 1)
    tri_s[...] = jnp.where(r <= c, 0.0, NEG)

    def q_block(qi, carry):
        q_rows = pl.ds(pl.multiple_of(qi * TK, TK), TK)
        for h in range(N_HEADS):
            acc_s[h] = jnp.zeros((HEAD_DIM, TK), F32)

        def step(j, st, diagonal):
            k_rows = pl.ds(pl.multiple_of(j * TK, TK), TK)

            def scores(h):
                s = jnp.dot(ka_s[h, k_rows, :], qat_s[qi, h * LANES:(h + 1) * LANES, :], preferred_element_type=F32)
                return s + tri_s[...] if diagonal else s

            ahead = 6
            pending = [scores(h) for h in range(ahead)]
            new = []
            for h in range(N_HEADS):
                m, l = st[h]
                s = pending.pop(0)
                if h + ahead < N_HEADS:
                    pending.append(scores(h + ahead))
                m_new = jnp.maximum(m, jnp.max(s, axis=0, keepdims=True))
                alpha = jnp.exp2(m - m_new)
                pr = jnp.exp2(s - m_new)
                l_new = alpha * l + jnp.sum(pr, axis=0, keepdims=True)
                pv = jnp.dot(vt_s[j, h * HEAD_DIM:(h + 1) * HEAD_DIM, :], pr.astype(BF16),
                             preferred_element_type=F32)
                acc_s[h] = alpha * acc_s[h] + pv
                new.append((m_new, l_new))
            return tuple(new)

        init = tuple((jnp.full((1, TK), NEG, F32), jnp.zeros((1, TK), F32)) for _ in range(N_HEADS))
        st = lax.fori_loop(0, qi // 2, lambda jj, st: step(2 * jj + 1, step(2 * jj, st, False), False), init)
        st = lax.cond(qi % 2 == 1, lambda st: step(qi - 1, st, False), lambda st: st, st)
        st = step(qi, st, True)
        for g in range(N_PAIRS):
            o2 = jnp.concatenate([acc_s[2 * g] / st[2 * g][1], acc_s[2 * g + 1] / st[2 * g + 1][1]], axis=0)
            o_ref[0, q_rows, g * LANES:(g + 1) * LANES] = o2.T.astype(BF16)
        return carry

    lax.fori_loop(0, seq // TK, q_block, 0)


def _fox_attn(qkv3, kbias, gq, gk):
    batch, seq, _ = qkv3.shape
    blk = lambda s: pl.BlockSpec((1, seq, D_SEC), lambda b, s=s: (b, 0, s))
    vec = pl.BlockSpec((1, LANES), lambda b: (0, 0))
    return pl.pallas_call(
        functools.partial(_fox_kernel, seq=seq),
        grid=(batch,),
        in_specs=[blk(3), blk(4), blk(5), pl.BlockSpec((1, seq, D_SEC), lambda b: (b, 0, 0)), vec, vec],
        out_specs=pl.BlockSpec((1, seq, D_SEC), lambda b: (b, 0, 0)),
        out_shape=jax.ShapeDtypeStruct((batch, seq, D_SEC), BF16),
        scratch_shapes=[
            pltpu.VMEM((seq // TK, N_HEADS * LANES, TK), BF16),
            pltpu.VMEM((N_HEADS, seq, LANES), BF16),
            pltpu.VMEM((seq // TK, D_SEC, TK), BF16),
            pltpu.VMEM((N_HEADS, HEAD_DIM, TK), F32),
            pltpu.VMEM((TK, TK), F32),
        ],
        compiler_params=_cparams("parallel"),
        name="fox_attn",
    )(qkv3, qkv3, qkv3, kbias, gq, gk)


def _out_router_kernel(oa_ref, ob_ref, x_ref, wo_ref, g_ref, wr_ref, br_ref,
                       h1_ref, msort_ref, route_ref, cnt_ref, wo_s, pick_s, mn_s, tri_s):
    tm = TM_TOK

    @pl.when(pl.program_id(0) == 0)
    def _():
        wo_s[...] = wo_ref[...].astype(BF16)
        pick_s[...] = jnp.zeros_like(pick_s)
        mn_s[...] = jnp.zeros_like(mn_s)
        r = lax.broadcasted_iota(jnp.int32, (tm, tm), 0)
        c = lax.broadcasted_iota(jnp.int32, (tm, tm), 1)
        tri_s[...] = (c <= r).astype(BF16)

    h1 = (x_ref[...]
          + jnp.dot(oa_ref[...], wo_s[0:D_SEC, :], preferred_element_type=F32)
          + jnp.dot(ob_ref[...], wo_s[D_SEC:2 * D_SEC, :], preferred_element_type=F32))
    h1_ref[...] = h1
    ms = jnp.mean(h1 * h1, axis=-1, keepdims=True)
    mn = h1 * lax.rsqrt(ms + EPS) * g_ref[...]

    logits = jnp.dot(mn.astype(BF16), wr_ref[...], preferred_element_type=F32) + br_ref[...]

    msort_ref[...] = jnp.dot(pick_s[...], mn_s[...], preferred_element_type=F32).astype(BF16)

    lane = lax.broadcasted_iota(jnp.int32, (tm, LANES), 1).astype(F32)
    big = float(LANES)

    def first_argmax(vals):
        vmax = jnp.max(vals, axis=-1, keepdims=True)
        idx = jnp.min(jnp.where(vals == vmax, lane, big), axis=-1, keepdims=True)
        return vmax, idx

    lg = jnp.where(lane < N_GROUPS, logits, -jnp.inf)
    gmax, gidx = first_argmax(lg)
    gw = 1.0 / jnp.sum(jnp.exp(lg - gmax), axis=-1, keepdims=True)
    lo = ROUTER_LANE0 + EXPERTS_PER_GROUP * gidx
    le = jnp.where((lane >= lo) & (lane < lo + EXPERTS_PER_GROUP), logits, -jnp.inf)
    v0, i0 = first_argmax(le)
    v1, i1 = first_argmax(jnp.where(lane == i0, -jnp.inf, le))
    ex = jnp.exp(v1 - v0)
    w0 = gw / (1.0 + ex)
    w1 = gw * ex / (1.0 + ex)

    sel0 = lane == i0
    sel1 = lane == i1
    onehot = jnp.where(sel0 | sel1, 1.0, 0.0)
    incl = jnp.dot(tri_s[...], onehot.astype(BF16), preferred_element_type=F32)
    excl = incl - onehot
    counts = incl[tm - 1:tm, :]
    cnt_ref[0] = jnp.broadcast_to(counts, (SUBLANES, LANES))

    seg_rows = jnp.floor((counts + (SEG_ALIGN - 1.0)) * (1.0 / SEG_ALIGN)) * SEG_ALIGN
    seg_rows = jnp.broadcast_to(seg_rows, (tm, LANES))
    lp0 = jnp.sum(jnp.where(lane < i0, seg_rows, 0.0) + jnp.where(sel0, excl, 0.0), axis=-1, keepdims=True)
    lp1 = jnp.sum(jnp.where(lane < i1, seg_rows, 0.0) + jnp.where(sel1, excl, 0.0), axis=-1, keepdims=True)

    rec = jnp.zeros((tm, LANES), F32)
    for pos, val in enumerate((i0 - ROUTER_LANE0, i1 - ROUTER_LANE0, lp0, lp1, w0, w1)):
        rec = jnp.where(lane == float(pos), val, rec)
    route_ref[...] = rec[:, 0:SUBLANES]

    rec_t = rec.T
    out_row = lax.broadcasted_iota(jnp.int32, (SORT_ROWS, tm), 0).astype(F32)
    pick_s[...] = jnp.where((out_row == rec_t[2:3, :]) | (out_row == rec_t[3:4, :]), 1.0, 0.0).astype(BF16)
    mn_s[...] = mn.astype(BF16)


def _out_router(oa, ob, x2, w_o, g_ffn, w_r, b_r):
    T = x2.shape[0]
    tm = TM_TOK
    last = T // tm - 1
    row = lambda w: pl.BlockSpec((tm, w), lambda i: (jnp.minimum(i, last), 0))
    const = lambda a, b: pl.BlockSpec((a, b), lambda i: (0, 0))
    return pl.pallas_call(
        _out_router_kernel,
        grid=(T // tm + 1,),
        in_specs=[row(D_SEC), row(D_SEC), row(D_MODEL), _resident((D_MODEL, D_MODEL)), const(1, D_MODEL),
                  const(D_MODEL, LANES), const(1, LANES)],
        out_specs=[row(D_MODEL), pl.BlockSpec((SORT_ROWS, D_MODEL), lambda i: (jnp.maximum(i - 1, 0), 0)),
                   row(SUBLANES), pl.BlockSpec((1, SUBLANES, LANES), lambda i: (jnp.minimum(i, last), 0, 0))],
        out_shape=[
            jax.ShapeDtypeStruct((T, D_MODEL), F32),
            jax.ShapeDtypeStruct((T // tm * SORT_ROWS, D_MODEL), BF16),
            jax.ShapeDtypeStruct((T, SUBLANES), F32),
            jax.ShapeDtypeStruct((T // tm, SUBLANES, LANES), F32),
        ],
        scratch_shapes=[pltpu.VMEM((D_MODEL, D_MODEL), BF16), pltpu.VMEM((SORT_ROWS, tm), BF16),
                        pltpu.VMEM((tm, D_MODEL), BF16), pltpu.VMEM((tm, tm), BF16)],
        compiler_params=_cparams("arbitrary"),
        name="out_router",
    )(oa, ob, x2, w_o, g_ffn, w_r, b_r)


MOE_CHUNKS = TM_MOE // SEG_ALIGN


def _chunk_row(c):
    return pl.multiple_of(c * SEG_ALIGN, SEG_ALIGN)


def _moe_kernel(tile_e_ref, nused_ref, src_ref, next_e_ref, slot_e_ref, msort_ref, wup_ref, wdn_ref, ys_ref,
                x_s, wup_f, wdn_f, wup_s, wdn_s, xsem, wsem):
    i = pl.program_id(0)
    nused = nused_ref[0]

    def gather(k, act):
        def chunk(c, carry):
            src = pl.multiple_of(src_ref[k * MOE_CHUNKS + c], SEG_ALIGN)
            act(pltpu.make_async_copy(msort_ref.at[pl.ds(src, SEG_ALIGN)],
                                      x_s.at[k % 2, pl.ds(_chunk_row(c), SEG_ALIGN)], xsem.at[k % 2]))
            return carry
        lax.fori_loop(0, MOE_CHUNKS, chunk, 0, unroll=8)

    def weights(e, s, act):
        act(pltpu.make_async_copy(wup_ref.at[e], wup_f.at[s], wsem.at[0, s]))
        act(pltpu.make_async_copy(wdn_ref.at[e], wdn_f.at[s], wsem.at[1, s]))

    @pl.when(i == 0)
    def _():
        gather(0, lambda c: c.start())
        weights(tile_e_ref[0], 0, lambda c: c.start())

    @pl.when(i + 1 < nused)
    def _():
        gather(i + 1, lambda c: c.start())

    @pl.when(i < nused)
    def _():
        e = tile_e_ref[i]

        @pl.when((i == 0) | (tile_e_ref[jnp.maximum(i - 1, 0)] != e))
        def _():
            s = slot_e_ref[e]
            weights(e, s, lambda c: c.wait())

            @pl.when(next_e_ref[e] >= 0)
            def _():
                weights(next_e_ref[e], 1 - s, lambda c: c.start())
            wup_s[...] = wup_f[s].astype(BF16)
            wdn_s[...] = wdn_f[s].astype(BF16)

        gather(i, lambda c: c.wait())
        hu = jnp.dot(x_s[i % 2], wup_s[...], preferred_element_type=F32)
        gate = hu[:, :D_EXPERT]
        hid = gate * (1.0 / (1.0 + jnp.exp(-gate))) * hu[:, D_EXPERT:]
        ys_ref[...] = jnp.dot(hid.astype(BF16), wdn_s[...], preferred_element_type=F32).astype(BF16)

    @pl.when(i >= nused)
    def _():
        ys_ref[...] = jnp.zeros_like(ys_ref)


def _moe_experts(tile_e, nused, src_chunk, next_e, slot_e, msort, w_up, w_down):
    n_tiles = tile_e.shape[0]
    grid_spec = pltpu.PrefetchScalarGridSpec(
        num_scalar_prefetch=5,
        grid=(n_tiles,),
        in_specs=[pl.BlockSpec(memory_space=pl.ANY)] * 3,
        out_specs=pl.BlockSpec((TM_MOE, D_MODEL), lambda i, *_: (i, 0)),
        scratch_shapes=[
            pltpu.VMEM((2, TM_MOE, D_MODEL), BF16),
            pltpu.VMEM((2, D_MODEL, 2 * D_EXPERT), F32), pltpu.VMEM((2, D_EXPERT, D_MODEL), F32),
            pltpu.VMEM((D_MODEL, 2 * D_EXPERT), BF16), pltpu.VMEM((D_EXPERT, D_MODEL), BF16),
            pltpu.SemaphoreType.DMA((2,)), pltpu.SemaphoreType.DMA((2, 2)),
        ],
    )
    return pl.pallas_call(
        _moe_kernel,
        grid_spec=grid_spec,
        out_shape=jax.ShapeDtypeStruct((n_tiles * TM_MOE, D_MODEL), BF16),
        compiler_params=_cparams("arbitrary"),
        name="moe_experts",
    )(tile_e, nused, src_chunk, next_e, slot_e, msort, w_up, w_down)


def _combine_kernel(src_ref, ys_ref, h1_ref, route_ref, p_ref,
                    g_ref, wg_ref, wp_ref, o_ref, gath_s, sems, wg_s, wp_s, h2_s, n_s):
    i = pl.program_id(0)
    n_steps = pl.num_programs(0)
    last_tile = n_steps - 2

    tm = TM_TOK
    slot = i % 2

    def gather(t, s, act):
        for c in range(SORT_CHUNKS):
            src = pl.multiple_of(src_ref[t * SORT_CHUNKS + c], SEG_ALIGN)
            act(pltpu.make_async_copy(ys_ref.at[pl.ds(src, SEG_ALIGN)],
                                      gath_s.at[s, pl.ds(c * SEG_ALIGN, SEG_ALIGN)], sems.at[s]))

    @pl.when(i == 0)
    def _():
        wg_s[...] = wg_ref[...].astype(BF16)
        wp_s[...] = wp_ref[...].astype(BF16)
        h2_s[...] = jnp.zeros_like(h2_s)
        n_s[...] = jnp.zeros_like(n_s)
        gather(0, 0, lambda c: c.start())

    cur = jnp.minimum(i, last_tile)
    nxt = jnp.minimum(i + 1, last_tile)
    gather(cur, slot, lambda c: c.wait())
    gather(nxt, 1 - slot, lambda c: c.start())

    ple = jnp.dot(p_ref[...].astype(BF16), wp_s[...], preferred_element_type=F32)
    z = jnp.dot(n_s[...], wg_s[...], preferred_element_type=F32)
    o_ref[...] = h2_s[...] + (1.0 / (1.0 + jnp.exp(-z))) * ple

    route = route_ref[...]
    col = lax.broadcasted_iota(jnp.int32, (tm, SORT_ROWS), 1).astype(F32)
    weights = (jnp.where(col == route[:, 2:3], route[:, 4:5], 0.0)
               + jnp.where(col == route[:, 3:4], route[:, 5:6], 0.0)).astype(BF16)
    y = jnp.dot(weights, gath_s[slot], preferred_element_type=F32)
    h2 = h1_ref[...] + y
    ms = jnp.mean(h2 * h2, axis=-1, keepdims=True)
    h2_s[...] = h2
    n_s[...] = (h2 * lax.rsqrt(ms + EPS) * g_ref[...]).astype(BF16)

    @pl.when(i == n_steps - 1)
    def _():
        gather(nxt, 1 - slot, lambda c: c.wait())


def _combine_ple(chunk_dst, ys, h1, route, p2, g_ple, w_gate, w_proj):
    T = h1.shape[0]
    tm = TM_TOK
    last = T // tm - 1
    cur = lambda w: pl.BlockSpec((tm, w), lambda i, *_: (jnp.minimum(i, last), 0))
    prev = lambda w: pl.BlockSpec((tm, w), lambda i, *_: (jnp.maximum(i - 1, 0), 0))
    const = lambda a, b: pl.BlockSpec((a, b), lambda i, *_: (0, 0))
    grid_spec = pltpu.PrefetchScalarGridSpec(
        num_scalar_prefetch=1,
        grid=(T // tm + 1,),
        in_specs=[pl.BlockSpec(memory_space=pl.ANY), cur(D_MODEL), cur(SUBLANES), prev(PLE_DIM),
                  const(1, D_MODEL), _resident((D_MODEL, D_MODEL)), _resident((PLE_DIM, D_MODEL))],
        out_specs=prev(D_MODEL),
        scratch_shapes=[pltpu.VMEM((2, SORT_ROWS, D_MODEL), BF16), pltpu.SemaphoreType.DMA((2,)),
                        pltpu.VMEM((D_MODEL, D_MODEL), BF16), pltpu.VMEM((PLE_DIM, D_MODEL), BF16),
                        pltpu.VMEM((tm, D_MODEL), F32), pltpu.VMEM((tm, D_MODEL), BF16)],
    )
    return pl.pallas_call(
        _combine_kernel,
        grid_spec=grid_spec,
        out_shape=jax.ShapeDtypeStruct((T, D_MODEL), F32),
        compiler_params=_cparams("arbitrary"),
        name="combine_ple",
    )(chunk_dst, ys, h1, route, p2, g_ple, w_gate, w_proj)


def _rope_tables(positions):
    half = ROT_DIM // 2
    inv = ROPE_THETA ** (-jnp.arange(0, ROT_DIM, 2, dtype=F32) / ROT_DIM)
    ang = positions.astype(F32)[:, None, :] * inv[None, :, None]
    lane = jnp.arange(LANES) % HEAD_DIM
    freq = jnp.arange(half)[:, None]
    rot = (lane[None, :] < ROT_DIM) & (lane[None, :] % half == freq)
    e_cos = rot.astype(BF16)
    e_sin = jnp.where(rot, jnp.where(lane[None, :] < half, -1.0, 1.0), 0.0).astype(BF16)

    def expand(tab, sel):
        hi = tab.astype(BF16)
        lo = (tab - hi.astype(F32)).astype(BF16)
        pick = functools.partial(jnp.einsum, "bfs,fl->bsl", preferred_element_type=F32)
        return pick(hi, sel) + pick(lo, sel)

    return expand(jnp.cos(ang), e_cos) + (lane >= ROT_DIM).astype(F32), expand(jnp.sin(ang), e_sin)


def _layer(i, h, p, cos_t, sin_t, g_mix, w_in, b_f, qn_a, kn_a, qn_b, kn_b, w_o, g_ffn, w_rg, b_rg,
           w_re, b_re, w_up, w_down, g_ple, w_ple_gate, w_ple_proj):
    B, S, _ = h.shape
    T = B * S
    x2 = h.reshape(T, D_MODEL)

    bf = jnp.zeros((1, LANES), F32).at[0, :N_HEADS].set(b_f[i])
    pair = lambda g: jnp.tile(g, 2).reshape(1, LANES)
    w_r = jnp.zeros((D_MODEL, LANES), F32)
    w_r = w_r.at[:, :N_GROUPS].set(w_rg[i])
    w_r = w_r.at[:, ROUTER_LANE0:ROUTER_LANE0 + N_EXPERTS].set(
        jnp.transpose(w_re[i], (1, 0, 2)).reshape(D_MODEL, N_EXPERTS))
    b_r = jnp.zeros((1, LANES), F32).at[0, :N_GROUPS].set(b_rg[i])
    b_r = b_r.at[0, ROUTER_LANE0:ROUTER_LANE0 + N_EXPERTS].set(b_re[i].reshape(-1))

    qkv, f = _in_proj(x2, g_mix[i].reshape(1, -1), w_in, i)
    kbias = _forget_scan(f, bf, B, S)
    qkv3 = qkv.reshape(B, S, N_QKV)
    oa = _dilated_attn(qkv3, cos_t, sin_t, pair(qn_a[i]), pair(kn_a[i]))
    ob = _fox_attn(qkv3, kbias, pair(qn_b[i]), pair(kn_b[i]))

    h1, msort, route, cnt = _out_router(oa.reshape(T, D_SEC), ob.reshape(T, D_SEC), x2, w_o[i],
                                        g_ffn[i].reshape(1, -1), w_r.astype(BF16), b_r)

    n_tok_tiles = T // TM_TOK
    counts = cnt[:, 0, ROUTER_LANE0:ROUTER_LANE0 + N_EXPERTS].astype(jnp.int32)
    seg_rows = (counts + SEG_ALIGN - 1) // SEG_ALIGN * SEG_ALIGN
    seg_local = jnp.cumsum(seg_rows, axis=1) - seg_rows
    rows_e = jnp.sum(seg_rows, axis=0)
    tile_end = jnp.cumsum((rows_e + TM_MOE - 1) // TM_MOE)
    off = jnp.concatenate([jnp.zeros((1,), jnp.int32), tile_end * TM_MOE]).astype(jnp.int32)
    seg_global = off[None, :N_EXPERTS] + jnp.cumsum(seg_rows, axis=0) - seg_rows
    n_tiles = (2 * T + n_tok_tiles * N_EXPERTS * (SEG_ALIGN - 1)) // TM_MOE + N_EXPERTS
    nused = tile_end[-1:].astype(jnp.int32)
    tile_ids = jnp.minimum(jnp.arange(n_tiles, dtype=jnp.int32), nused[0] - 1)
    tile_e = jnp.sum((tile_ids[:, None] >= tile_end[None, :]).astype(jnp.int32), axis=1).astype(jnp.int32)
    chunk_row = jnp.arange(SORT_CHUNKS, dtype=jnp.int32)[None, :, None] * SEG_ALIGN
    lo, hi = seg_local[:, None, :], (seg_local + seg_rows)[:, None, :]
    chunk_dst = jnp.sum(jnp.where((chunk_row >= lo) & (chunk_row < hi), seg_global[:, None, :] + chunk_row - lo, 0),
                        axis=2).reshape(-1).astype(jnp.int32)
    sorted_row = jnp.arange(n_tiles * MOE_CHUNKS, dtype=jnp.int32)[:, None] * SEG_ALIGN
    seg_lo = seg_global.reshape(1, -1)
    seg_src = (jnp.arange(n_tok_tiles, dtype=jnp.int32)[:, None] * SORT_ROWS + seg_local).reshape(1, -1)
    in_seg = (sorted_row >= seg_lo) & (sorted_row < seg_lo + seg_rows.reshape(1, -1))
    src_chunk = jnp.where(jnp.any(in_seg, axis=1),
                          jnp.sum(jnp.where(in_seg, seg_src + sorted_row - seg_lo, 0), axis=1),
                          SORT_ROWS - SEG_ALIGN).astype(jnp.int32)
    e_ids = jnp.arange(N_EXPERTS, dtype=jnp.int32)
    active = rows_e > 0
    later_active = jnp.where(active[None, :] & (e_ids[None, :] > e_ids[:, None]), e_ids[None, :], N_EXPERTS)
    next_e = jnp.min(later_active, axis=1)
    next_e = jnp.where(next_e == N_EXPERTS, -1, next_e).astype(jnp.int32)
    slot_e = ((jnp.cumsum(active.astype(jnp.int32)) - 1) % 2).astype(jnp.int32)

    ys = _moe_experts(tile_e, nused, src_chunk, next_e, slot_e, msort, w_up[i], w_down[i])
    out = _combine_ple(chunk_dst, ys, h1, route, p[i].reshape(T, PLE_DIM),
                       g_ple[i].reshape(1, -1), w_ple_gate[i], w_ple_proj[i])
    return out.reshape(B, S, D_MODEL)


def kernel(x, p, positions, g_mix, w_in, b_f, qn_a, kn_a, qn_b, kn_b, w_o, g_ffn, w_rg, b_rg, w_re, b_re,
           w_up, w_down, g_ple, w_ple_gate, w_ple_proj):
    cos_t, sin_t = _rope_tables(positions)
    h = x
    for i in range(p.shape[0]):
        h = _layer(i, h, p, cos_t, sin_t, g_mix, w_in, b_f, qn_a, kn_a, qn_b, kn_b, w_o, g_ffn, w_rg, b_rg,
                   w_re, b_re, w_up, w_down, g_ple, w_ple_gate, w_ple_proj)
    return h
```

```python
import functools
import math

import jax
import jax.numpy as jnp
from jax import lax
from jax.experimental import pallas as pl
from jax.experimental.pallas import tpu as pltpu

D_MODEL = 1024
HEAD_DIM = 64
N_HEADS = 8
D_SEC = N_HEADS * HEAD_DIM
N_QKV = 6 * D_SEC
ROT_DIM = HEAD_DIM // 4
ROPE_THETA = 500000.0
N_GROUPS = 4
EXPERTS_PER_GROUP = 8
N_EXPERTS = N_GROUPS * EXPERTS_PER_GROUP
D_EXPERT = 512
PLE_DIM = 256
EPS = 1e-6
NEG = -1e30
WINDOW = 128

LANES = 128
SUBLANES = 8
VMEM_LIMIT = 48 * 1024 * 1024

TM_PROJ = 512
TQ = 128
TK = 256
N_PAIRS = N_HEADS // 2
TM_MOE = 512
TM_TOK = 512
SEG_ALIGN = 2 * SUBLANES
SORT_ROWS = 2 * TM_TOK + N_EXPERTS * SEG_ALIGN
SORT_CHUNKS = SORT_ROWS // SEG_ALIGN
ROUTER_LANE0 = N_GROUPS

Q_SCALE_LOG2 = math.log2(math.e) / math.sqrt(HEAD_DIM)

F32 = jnp.float32
BF16 = jnp.bfloat16
NT_DIMS = (((1,), (1,)), ((), ()))


def _cparams(*sem):
    return pltpu.CompilerParams(dimension_semantics=sem, vmem_limit_bytes=VMEM_LIMIT)


def _resident(shape):
    return pl.BlockSpec(shape, lambda i, *_: (0,) * len(shape), pipeline_mode=pl.Buffered(1))


def _in_proj_kernel(x_ref, g_ref, w_ref, qkv_ref, f_ref, w_s, wf_s):
    @pl.when(pl.program_id(0) == 0)
    def _():
        w_s[...] = w_ref[0, :, 0:N_QKV].astype(BF16)
        wf_s[...] = jnp.zeros_like(wf_s)
        wf_s[:, 0:N_HEADS] = w_ref[0, :, N_QKV:N_QKV + N_HEADS].astype(BF16)

    x = x_ref[...]
    ms = jnp.mean(x * x, axis=-1, keepdims=True)
    a = (x * lax.rsqrt(ms + EPS) * g_ref[...]).astype(BF16)
    qkv_ref[...] = jnp.dot(a, w_s[...], preferred_element_type=F32).astype(BF16)
    f_ref[...] = jnp.dot(a, wf_s[...], preferred_element_type=F32)


def _in_proj(x2, g_mix, w_in, layer):
    T = x2.shape[0]
    w_block = (1,) + w_in.shape[1:]
    return pl.pallas_call(
        _in_proj_kernel,
        grid=(T // TM_PROJ,),
        in_specs=[
            pl.BlockSpec((TM_PROJ, D_MODEL), lambda i: (i, 0)),
            pl.BlockSpec((1, D_MODEL), lambda i: (0, 0)),
            pl.BlockSpec(w_block, lambda i: (layer, 0, 0), pipeline_mode=pl.Buffered(1)),
        ],
        out_specs=[
            pl.BlockSpec((TM_PROJ, N_QKV), lambda i: (i, 0)),
            pl.BlockSpec((TM_PROJ, LANES), lambda i: (i, 0)),
        ],
        out_shape=[
            jax.ShapeDtypeStruct((T, N_QKV), BF16),
            jax.ShapeDtypeStruct((T, LANES), F32),
        ],
        scratch_shapes=[pltpu.VMEM((D_MODEL, N_QKV), BF16), pltpu.VMEM((D_MODEL, LANES), BF16)],
        compiler_params=_cparams("arbitrary"),
        name="in_proj",
    )(x2, g_mix, w_in)


BIAS_TERMS = 3


def _forget_scan_kernel(f_ref, bf_ref, kb_ref, *, seq):
    f = f_ref[...] + bf_ref[...]
    logf = jnp.minimum(f, 0.0) - jnp.log1p(jnp.exp(-jnp.abs(f)))

    def split_bf16(v):
        terms = []
        for _ in range(BIAS_TERMS):
            t = v.astype(BF16)
            terms.append(t)
            v = v - t.astype(F32)
        return terms

    tri = (lax.broadcasted_iota(jnp.int32, (TK, TK), 1) <= lax.broadcasted_iota(jnp.int32, (TK, TK), 0)).astype(BF16)
    carry = jnp.zeros((1, LANES), F32)
    chunks = []
    for j in range(seq // TK):
        terms = jnp.concatenate(split_bf16(logf[j * TK:(j + 1) * TK, :]), axis=1)
        pre = jnp.dot(tri, terms, preferred_element_type=F32)
        cj = carry + sum(pre[:, t * LANES:(t + 1) * LANES] for t in range(BIAS_TERMS))
        carry = cj[TK - 1:TK, :]
        chunks.append(cj)
    c = jnp.concatenate(chunks, axis=0)
    rest = c * (-math.log2(math.e))
    r_idx = lax.broadcasted_iota(jnp.int32, (LANES, D_SEC), 0)
    c_idx = lax.broadcasted_iota(jnp.int32, (LANES, D_SEC), 1)
    base = (r_idx // 2) * LANES + jnp.where(r_idx % 2 == 0, HEAD_DIM, 0)
    out = jnp.zeros((seq, D_SEC), F32)
    for t, term in enumerate(split_bf16(rest)):
        place = ((r_idx < N_HEADS) & (c_idx == base + t)).astype(BF16)
        out = out + jnp.dot(term, place, preferred_element_type=F32)
    kb_ref[0] = out.astype(BF16)


def _forget_scan(f, bf, batch, seq):
    return pl.pallas_call(
        functools.partial(_forget_scan_kernel, seq=seq),
        grid=(batch,),
        in_specs=[
            pl.BlockSpec((seq, LANES), lambda b: (b, 0)),
            pl.BlockSpec((1, LANES), lambda b: (0, 0)),
        ],
        out_specs=pl.BlockSpec((1, seq, D_SEC), lambda b: (b, 0, 0)),
        out_shape=jax.ShapeDtypeStruct((batch, seq, D_SEC), BF16),
        compiler_params=_cparams("parallel"),
        name="forget_scan",
    )(f, bf)


def _head_sumsq_matrix():
    r = lax.broadcasted_iota(jnp.int32, (LANES, LANES), 0) // HEAD_DIM
    c = lax.broadcasted_iota(jnp.int32, (LANES, LANES), 1) // HEAD_DIM
    return (r == c).astype(BF16)


def _qk_norm(x, gain, gmat):
    ss = jnp.dot((x * x).astype(BF16), gmat, preferred_element_type=F32)
    return x * lax.rsqrt(ss * (1.0 / HEAD_DIM) + EPS) * gain


def _stack_heads(qb, head0):
    zero = jnp.zeros_like(qb)
    return jnp.concatenate([jnp.where(head0, qb, zero), jnp.where(head0, zero, qb)], axis=0)


def _unstack(a, head0):
    return jnp.where(head0, a[:TQ], a[TQ:])


def _dilated_kernel(q_ref, k_ref, v_ref, cos_ref, sin_ref, gq_ref, gk_ref, o_ref,
                    qn_s, kn_s, v_s, acc_s, m_s, l_s, bias_s, *, seq):
    gmat = _head_sumsq_matrix()
    lane = lax.broadcasted_iota(jnp.int32, (TQ, LANES), 1)
    head0 = lane < HEAD_DIM
    chunk = 512
    half = ROT_DIM // 2
    src = lax.broadcasted_iota(jnp.int32, (LANES, LANES), 0)
    dst = lax.broadcasted_iota(jnp.int32, (LANES, LANES), 1)
    dst_h = dst % HEAD_DIM
    perm = (((dst_h < half) & (src == dst + half))
            | ((dst_h >= half) & (dst_h < ROT_DIM) & (src == dst - half))).astype(BF16)

    def rope(t, cs, sn):
        hi = t.astype(BF16)
        lo = (t - hi.astype(F32)).astype(BF16)
        partner = (jnp.dot(hi, perm, preferred_element_type=F32) + jnp.dot(lo, perm, preferred_element_type=F32))
        return t * cs + partner * sn

    for c0 in range(0, seq, chunk):
        rows = pl.ds(c0, chunk)
        cs = cos_ref[0, rows, :]
        sn = sin_ref[0, rows, :]
        qn = _qk_norm(q_ref[0, rows, :].astype(F32), gq_ref[...], gmat)
        qn_s[rows, :] = rope(qn, cs, sn) * Q_SCALE_LOG2
        kn = _qk_norm(k_ref[0, rows, :].astype(F32), gk_ref[...], gmat)
        kn_s[rows, :] = rope(kn, cs, sn)
        v_s[rows, :] = v_ref[0, rows, :].astype(F32)

    u = lax.broadcasted_iota(jnp.int32, (2 * TQ, 2 * TQ), 0) % TQ
    c = lax.broadcasted_iota(jnp.int32, (2 * TQ, 2 * TQ), 1)
    for slot, dist0 in enumerate((0, TQ)):
        dist = dist0 + u - c
        bias_s[slot] = jnp.where((dist >= 0) & (dist <= WINDOW), 0.0, NEG)

    def scores(blk):
        p, q_rows, k_rows, nk, dist0 = blk
        qst = _stack_heads(qn_s[q_rows, :].astype(BF16), head0)
        s = lax.dot_general(qst, kn_s[k_rows, :].astype(BF16), NT_DIMS, preferred_element_type=F32)
        return s + bias_s[dist0 // TQ, :, 0:nk]

    def finish(blk, s):
        p, q_rows, k_rows, nk, dist0 = blk
        m = jnp.max(s, axis=-1, keepdims=True)
        pr = jnp.exp2(s - m)
        v_ones = jnp.concatenate([v_s[k_rows, :].astype(BF16), jnp.ones((nk, LANES), BF16)], axis=1)
        acc = jnp.dot(pr.astype(BF16), v_ones, preferred_element_type=F32)
        acc_s[p, q_rows, :] = _unstack(acc[:, :LANES], head0)
        m_s[p, q_rows, :] = _unstack(jnp.broadcast_to(m, (2 * TQ, LANES)), head0)
        l_s[p, q_rows, :] = _unstack(acc[:, LANES:], head0)

    blocks = []
    for i in range(seq // TQ):
        kb0 = max(i - 1, 0)
        blocks.append((0, pl.ds(i * TQ, TQ), pl.ds(kb0 * TQ, 2 * TQ), 2 * TQ, (i - kb0) * TQ))
    for r in range(4):
        for n in range(seq // (4 * TQ)):
            kb0 = max(n - 1, 0)
            blocks.append((1, pl.ds(4 * TQ * n + r, TQ, stride=4), pl.ds(4 * TQ * kb0 + r, 2 * TQ, stride=4),
                           2 * TQ, (n - kb0) * TQ))
    for r in range(16):
        rows = pl.ds(r, TQ, stride=16)
        blocks.append((2, rows, rows, TQ, 0))

    ahead = 2
    pending = [scores(b) for b in blocks[:ahead]]
    for idx, b in enumerate(blocks):
        s = pending.pop(0)
        if idx + ahead < len(blocks):
            pending.append(scores(blocks[idx + ahead]))
        finish(b, s)

    for c0 in range(0, seq, chunk):
        rows = pl.ds(c0, chunk)
        m0, m1, m2 = m_s[0, rows, :], m_s[1, rows, :], m_s[2, rows, :]
        mm = jnp.maximum(jnp.maximum(m0, m1), m2)
        e0, e1, e2 = jnp.exp2(m0 - mm), jnp.exp2(m1 - mm), jnp.exp2(m2 - mm)
        num = acc_s[0, rows, :] * e0 + acc_s[1, rows, :] * e1 + acc_s[2, rows, :] * e2
        den = l_s[0, rows, :] * e0 + l_s[1, rows, :] * e1 + l_s[2, rows, :] * e2
        o_ref[0, rows, :] = (num / den).astype(BF16)


def _dilated_attn(qkv3, cos_t, sin_t, gq, gk):
    batch, seq, _ = qkv3.shape
    sec = D_SEC // LANES
    blk = lambda s: pl.BlockSpec((1, seq, LANES), lambda b, h, s=s: (b, 0, s * sec + h))
    tab = pl.BlockSpec((1, seq, LANES), lambda b, h: (b, 0, 0))
    vec = pl.BlockSpec((1, LANES), lambda b, h: (0, 0))
    return pl.pallas_call(
        functools.partial(_dilated_kernel, seq=seq),
        grid=(batch, N_PAIRS),
        in_specs=[blk(0), blk(1), blk(2), tab, tab, vec, vec],
        out_specs=pl.BlockSpec((1, seq, LANES), lambda b, h: (b, 0, h)),
        out_shape=jax.ShapeDtypeStruct((batch, seq, D_SEC), BF16),
        scratch_shapes=[
            pltpu.VMEM((seq, LANES), F32), pltpu.VMEM((seq, LANES), F32), pltpu.VMEM((seq, LANES), F32),
            pltpu.VMEM((3, seq, LANES), F32), pltpu.VMEM((3, seq, LANES), F32),
            pltpu.VMEM((3, seq, LANES), F32), pltpu.VMEM((2, 2 * TQ, 2 * TQ), F32),
        ],
        compiler_params=_cparams("parallel", "parallel"),
        name="dilated_attn",
    )(qkv3, qkv3, qkv3, cos_t, sin_t, gq, gk)


def _fox_kernel(q_ref, k_ref, v_ref, kb_ref, gq_ref, gk_ref, o_ref, qat_s, ka_s, vt_s, acc_s, tri_s, *, seq):
    gmat = _head_sumsq_matrix()
    chunk = 512
    lane = lax.broadcasted_iota(jnp.int32, (chunk, LANES), 1)
    low = lane < HEAD_DIM
    sel_t = jnp.where(lax.broadcasted_iota(jnp.int32, (HEAD_DIM, TK), 0) < BIAS_TERMS, 1.0, 0.0).astype(BF16)
    eye = (lax.broadcasted_iota(jnp.int32, (LANES, LANES), 0)
           == lax.broadcasted_iota(jnp.int32, (LANES, LANES), 1)).astype(BF16)

    def transpose_bf16(a):
        return lax.dot_general(eye, a, NT_DIMS, preferred_element_type=F32)

    for c0 in range(0, seq, chunk):
        rows = pl.ds(c0, chunk)
        for g in range(N_PAIRS):
            lanes = slice(g * LANES, (g + 1) * LANES)
            qn = _qk_norm(q_ref[0, rows, lanes].astype(F32), gq_ref[...], gmat) * Q_SCALE_LOG2
            qt = transpose_bf16(qn.astype(BF16))
            for cc in range(chunk // TK):
                ci, cols = c0 // TK + cc, slice(cc * TK, (cc + 1) * TK)
                even, odd = 2 * g * LANES, (2 * g + 1) * LANES
                qat_s[ci, even:even + HEAD_DIM, :] = qt[0:HEAD_DIM, cols].astype(BF16)
                qat_s[ci, even + HEAD_DIM:even + LANES, :] = sel_t
                qat_s[ci, odd:odd + HEAD_DIM, :] = sel_t
                qat_s[ci, odd + HEAD_DIM:odd + LANES, :] = qt[HEAD_DIM:LANES, cols].astype(BF16)
            kn = _qk_norm(k_ref[0, rows, lanes].astype(F32), gk_ref[...], gmat)
            kb = kb_ref[0, rows, lanes].astype(F32)
            ka_s[2 * g, rows, :] = jnp.where(low, kn, kb).astype(BF16)
            ka_s[2 * g + 1, rows, :] = jnp.where(low, kb, kn).astype(BF16)
            vt = transpose_bf16(v_ref[0, rows, lanes])
            for cc in range(chunk // TK):
                vt_s[c0 // TK + cc, lanes, :] = vt[:, cc * TK:(cc + 1) * TK].astype(BF16)

    r = lax.broadcasted_iota(jnp.int32, (TK, TK), 0)
    c = lax.broadcasted_iota(jnp.int32, (TK, TK), 1)
    tri_s[...] = jnp.where(r <= c, 0.0, NEG)

    def q_block(qi, carry):
        q_rows = pl.ds(pl.multiple_of(qi * TK, TK), TK)
        for h in range(N_HEADS):
            acc_s[h] = jnp.zeros((HEAD_DIM, TK), F32)

        def step(j, st, diagonal):
            k_rows = pl.ds(pl.multiple_of(j * TK, TK), TK)

            def scores(h):
                s = jnp.dot(ka_s[h, k_rows, :], qat_s[qi, h * LANES:(h + 1) * LANES, :], preferred_element_type=F32)
                return s + tri_s[...] if diagonal else s

            ahead = 6
            pending = [scores(h) for h in range(ahead)]
            new = []
            for h in range(N_HEADS):
                m, l = st[h]
                s = pending.pop(0)
                if h + ahead < N_HEADS:
                    pending.append(scores(h + ahead))
                m_new = jnp.maximum(m, jnp.max(s, axis=0, keepdims=True))
                alpha = jnp.exp2(m - m_new)
                pr = jnp.exp2(s - m_new)
                l_new = alpha * l + jnp.sum(pr, axis=0, keepdims=True)
                pv = jnp.dot(vt_s[j, h * HEAD_DIM:(h + 1) * HEAD_DIM, :], pr.astype(BF16),
                             preferred_element_type=F32)
                acc_s[h] = alpha * acc_s[h] + pv
                new.append((m_new, l_new))
            return tuple(new)

        init = tuple((jnp.full((1, TK), NEG, F32), jnp.zeros((1, TK), F32)) for _ in range(N_HEADS))
        st = lax.fori_loop(0, qi // 2, lambda jj, st: step(2 * jj + 1, step(2 * jj, st, False), False), init)
        st = lax.cond(qi % 2 == 1, lambda st: step(qi - 1, st, False), lambda st: st, st)
        st = step(qi, st, True)
        for g in range(N_PAIRS):
            o2 = jnp.concatenate([acc_s[2 * g] / st[2 * g][1], acc_s[2 * g + 1] / st[2 * g + 1][1]], axis=0)
            o_ref[0, q_rows, g * LANES:(g + 1) * LANES] = o2.T.astype(BF16)
        return carry

    lax.fori_loop(0, seq // TK, q_block, 0)


def _fox_attn(qkv3, kbias, gq, gk):
    batch, seq, _ = qkv3.shape
    blk = lambda s: pl.BlockSpec((1, seq, D_SEC), lambda b, s=s: (b, 0, s))
    vec = pl.BlockSpec((1, LANES), lambda b: (0, 0))
    return pl.pallas_call(
        functools.partial(_fox_kernel, seq=seq),
        grid=(batch,),
        in_specs=[blk(3), blk(4), blk(5), pl.BlockSpec((1, seq, D_SEC), lambda b: (b, 0, 0)), vec, vec],
        out_specs=pl.BlockSpec((1, seq, D_SEC), lambda b: (b, 0, 0)),
        out_shape=jax.ShapeDtypeStruct((batch, seq, D_SEC), BF16),
        scratch_shapes=[
            pltpu.VMEM((seq // TK, N_HEADS * LANES, TK), BF16),
            pltpu.VMEM((N_HEADS, seq, LANES), BF16),
            pltpu.VMEM((seq // TK, D_SEC, TK), BF16),
            pltpu.VMEM((N_HEADS, HEAD_DIM, TK), F32),
            pltpu.VMEM((TK, TK), F32),
        ],
        compiler_params=_cparams("parallel"),
        name="fox_attn",
    )(qkv3, qkv3, qkv3, kbias, gq, gk)


def _out_router_kernel(oa_ref, ob_ref, x_ref, wo_ref, g_ref, wr_ref, br_ref,
                       h1_ref, msort_ref, route_ref, cnt_ref, wo_s, pick_s, mn_s, tri_s):
    tm = TM_TOK

    @pl.when(pl.program_id(0) == 0)
    def _():
        wo_s[...] = wo_ref[...].astype(BF16)
        pick_s[...] = jnp.zeros_like(pick_s)
        mn_s[...] = jnp.zeros_like(mn_s)
        r = lax.broadcasted_iota(jnp.int32, (tm, tm), 0)
        c = lax.broadcasted_iota(jnp.int32, (tm, tm), 1)
        tri_s[...] = (c <= r).astype(BF16)

    h1 = (x_ref[...]
          + jnp.dot(oa_ref[...], wo_s[0:D_SEC, :], preferred_element_type=F32)
          + jnp.dot(ob_ref[...], wo_s[D_SEC:2 * D_SEC, :], preferred_element_type=F32))
    h1_ref[...] = h1
    ms = jnp.mean(h1 * h1, axis=-1, keepdims=True)
    mn = h1 * lax.rsqrt(ms + EPS) * g_ref[...]

    logits = jnp.dot(mn.astype(BF16), wr_ref[...], preferred_element_type=F32) + br_ref[...]

    msort_ref[...] = jnp.dot(pick_s[...], mn_s[...], preferred_element_type=F32).astype(BF16)

    lane = lax.broadcasted_iota(jnp.int32, (tm, LANES), 1).astype(F32)
    big = float(LANES)

    def first_argmax(vals):
        vmax = jnp.max(vals, axis=-1, keepdims=True)
        idx = jnp.min(jnp.where(vals == vmax, lane, big), axis=-1, keepdims=True)
        return vmax, idx

    lg = jnp.where(lane < N_GROUPS, logits, -jnp.inf)
    gmax, gidx = first_argmax(lg)
    gw = 1.0 / jnp.sum(jnp.exp(lg - gmax), axis=-1, keepdims=True)
    lo = ROUTER_LANE0 + EXPERTS_PER_GROUP * gidx
    le = jnp.where((lane >= lo) & (lane < lo + EXPERTS_PER_GROUP), logits, -jnp.inf)
    v0, i0 = first_argmax(le)
    v1, i1 = first_argmax(jnp.where(lane == i0, -jnp.inf, le))
    ex = jnp.exp(v1 - v0)
    w0 = gw / (1.0 + ex)
    w1 = gw * ex / (1.0 + ex)

    sel0 = lane == i0
    sel1 = lane == i1
    onehot = jnp.where(sel0 | sel1, 1.0, 0.0)
    incl = jnp.dot(tri_s[...], onehot.astype(BF16), preferred_element_type=F32)
    excl = incl - onehot
    counts = incl[tm - 1:tm, :]
    cnt_ref[0] = jnp.broadcast_to(counts, (SUBLANES, LANES))

    seg_rows = jnp.floor((counts + (SEG_ALIGN - 1.0)) * (1.0 / SEG_ALIGN)) * SEG_ALIGN
    seg_rows = jnp.broadcast_to(seg_rows, (tm, LANES))
    lp0 = jnp.sum(jnp.where(lane < i0, seg_rows, 0.0) + jnp.where(sel0, excl, 0.0), axis=-1, keepdims=True)
    lp1 = jnp.sum(jnp.where(lane < i1, seg_rows, 0.0) + jnp.where(sel1, excl, 0.0), axis=-1, keepdims=True)

    rec = jnp.zeros((tm, LANES), F32)
    for pos, val in enumerate((i0 - ROUTER_LANE0, i1 - ROUTER_LANE0, lp0, lp1, w0, w1)):
        rec = jnp.where(lane == float(pos), val, rec)
    route_ref[...] = rec[:, 0:SUBLANES]

    rec_t = rec.T
    out_row = lax.broadcasted_iota(jnp.int32, (SORT_ROWS, tm), 0).astype(F32)
    pick_s[...] = jnp.where((out_row == rec_t[2:3, :]) | (out_row == rec_t[3:4, :]), 1.0, 0.0).astype(BF16)
    mn_s[...] = mn.astype(BF16)


def _out_router(oa, ob, x2, w_o, g_ffn, w_r, b_r):
    T = x2.shape[0]
    tm = TM_TOK
    last = T // tm - 1
    row = lambda w: pl.BlockSpec((tm, w), lambda i: (jnp.minimum(i, last), 0))
    const = lambda a, b: pl.BlockSpec((a, b), lambda i: (0, 0))
    return pl.pallas_call(
        _out_router_kernel,
        grid=(T // tm + 1,),
        in_specs=[row(D_SEC), row(D_SEC), row(D_MODEL), _resident((D_MODEL, D_MODEL)), const(1, D_MODEL),
                  const(D_MODEL, LANES), const(1, LANES)],
        out_specs=[row(D_MODEL), pl.BlockSpec((SORT_ROWS, D_MODEL), lambda i: (jnp.maximum(i - 1, 0), 0)),
                   row(SUBLANES), pl.BlockSpec((1, SUBLANES, LANES), lambda i: (jnp.minimum(i, last), 0, 0))],
        out_shape=[
            jax.ShapeDtypeStruct((T, D_MODEL), F32),
            jax.ShapeDtypeStruct((T // tm * SORT_ROWS, D_MODEL), BF16),
            jax.ShapeDtypeStruct((T, SUBLANES), F32),
            jax.ShapeDtypeStruct((T // tm, SUBLANES, LANES), F32),
        ],
        scratch_shapes=[pltpu.VMEM((D_MODEL, D_MODEL), BF16), pltpu.VMEM((SORT_ROWS, tm), BF16),
                        pltpu.VMEM((tm, D_MODEL), BF16), pltpu.VMEM((tm, tm), BF16)],
        compiler_params=_cparams("arbitrary"),
        name="out_router",
    )(oa, ob, x2, w_o, g_ffn, w_r, b_r)


MOE_CHUNKS = TM_MOE // SEG_ALIGN


def _chunk_row(c):
    return pl.multiple_of(c * SEG_ALIGN, SEG_ALIGN)


def _moe_kernel(tile_e_ref, nused_ref, src_ref, next_e_ref, slot_e_ref, msort_ref, wup_ref, wdn_ref, ys_ref,
                x_s, wup_f, wdn_f, wup_s, wdn_s, xsem, wsem):
    i = pl.program_id(0)
    nused = nused_ref[0]

    def gather(k, act):
        def chunk(c, carry):
            src = pl.multiple_of(src_ref[k * MOE_CHUNKS + c], SEG_ALIGN)
            act(pltpu.make_async_copy(msort_ref.at[pl.ds(src, SEG_ALIGN)],
                                      x_s.at[k % 2, pl.ds(_chunk_row(c), SEG_ALIGN)], xsem.at[k % 2]))
            return carry
        lax.fori_loop(0, MOE_CHUNKS, chunk, 0, unroll=8)

    def weights(e, s, act):
        act(pltpu.make_async_copy(wup_ref.at[e], wup_f.at[s], wsem.at[0, s]))
        act(pltpu.make_async_copy(wdn_ref.at[e], wdn_f.at[s], wsem.at[1, s]))

    @pl.when(i == 0)
    def _():
        gather(0, lambda c: c.start())
        weights(tile_e_ref[0], 0, lambda c: c.start())

    @pl.when(i + 1 < nused)
    def _():
        gather(i + 1, lambda c: c.start())

    @pl.when(i < nused)
    def _():
        e = tile_e_ref[i]

        @pl.when((i == 0) | (tile_e_ref[jnp.maximum(i - 1, 0)] != e))
        def _():
            s = slot_e_ref[e]
            weights(e, s, lambda c: c.wait())

            @pl.when(next_e_ref[e] >= 0)
            def _():
                weights(next_e_ref[e], 1 - s, lambda c: c.start())
            wup_s[...] = wup_f[s].astype(BF16)
            wdn_s[...] = wdn_f[s].astype(BF16)

        gather(i, lambda c: c.wait())
        hu = jnp.dot(x_s[i % 2], wup_s[...], preferred_element_type=F32)
        gate = hu[:, :D_EXPERT]
        hid = gate * (1.0 / (1.0 + jnp.exp(-gate))) * hu[:, D_EXPERT:]
        ys_ref[...] = jnp.dot(hid.astype(BF16), wdn_s[...], preferred_element_type=F32).astype(BF16)

    @pl.when(i >= nused)
    def _():
        ys_ref[...] = jnp.zeros_like(ys_ref)


def _moe_experts(tile_e, nused, src_chunk, next_e, slot_e, msort, w_up, w_down):
    n_tiles = tile_e.shape[0]
    grid_spec = pltpu.PrefetchScalarGridSpec(
        num_scalar_prefetch=5,
        grid=(n_tiles,),
        in_specs=[pl.BlockSpec(memory_space=pl.ANY)] * 3,
        out_specs=pl.BlockSpec((TM_MOE, D_MODEL), lambda i, *_: (i, 0)),
        scratch_shapes=[
            pltpu.VMEM((2, TM_MOE, D_MODEL), BF16),
            pltpu.VMEM((2, D_MODEL, 2 * D_EXPERT), F32), pltpu.VMEM((2, D_EXPERT, D_MODEL), F32),
            pltpu.VMEM((D_MODEL, 2 * D_EXPERT), BF16), pltpu.VMEM((D_EXPERT, D_MODEL), BF16),
            pltpu.SemaphoreType.DMA((2,)), pltpu.SemaphoreType.DMA((2, 2)),
        ],
    )
    return pl.pallas_call(
        _moe_kernel,
        grid_spec=grid_spec,
        out_shape=jax.ShapeDtypeStruct((n_tiles * TM_MOE, D_MODEL), BF16),
        compiler_params=_cparams("arbitrary"),
        name="moe_experts",
    )(tile_e, nused, src_chunk, next_e, slot_e, msort, w_up, w_down)


def _combine_kernel(src_ref, ys_ref, h1_ref, route_ref, p_ref,
                    g_ref, wg_ref, wp_ref, o_ref, gath_s, sems, wg_s, wp_s, h2_s, n_s):
    i = pl.program_id(0)
    n_steps = pl.num_programs(0)
    last_tile = n_steps - 2

    tm = TM_TOK
    slot = i % 2

    def gather(t, s, act):
        for c in range(SORT_CHUNKS):
            src = pl.multiple_of(src_ref[t * SORT_CHUNKS + c], SEG_ALIGN)
            act(pltpu.make_async_copy(ys_ref.at[pl.ds(src, SEG_ALIGN)],
                                      gath_s.at[s, pl.ds(c * SEG_ALIGN, SEG_ALIGN)], sems.at[s]))

    @pl.when(i == 0)
    def _():
        wg_s[...] = wg_ref[...].astype(BF16)
        wp_s[...] = wp_ref[...].astype(BF16)
        h2_s[...] = jnp.zeros_like(h2_s)
        n_s[...] = jnp.zeros_like(n_s)
        gather(0, 0, lambda c: c.start())

    cur = jnp.minimum(i, last_tile)
    nxt = jnp.minimum(i + 1, last_tile)
    gather(cur, slot, lambda c: c.wait())
    gather(nxt, 1 - slot, lambda c: c.start())

    ple = jnp.dot(p_ref[...].astype(BF16), wp_s[...], preferred_element_type=F32)
    z = jnp.dot(n_s[...], wg_s[...], preferred_element_type=F32)
    o_ref[...] = h2_s[...] + (1.0 / (1.0 + jnp.exp(-z))) * ple

    route = route_ref[...]
    col = lax.broadcasted_iota(jnp.int32, (tm, SORT_ROWS), 1).astype(F32)
    weights = (jnp.where(col == route[:, 2:3], route[:, 4:5], 0.0)
               + jnp.where(col == route[:, 3:4], route[:, 5:6], 0.0)).astype(BF16)
    y = jnp.dot(weights, gath_s[slot], preferred_element_type=F32)
    h2 = h1_ref[...] + y
    ms = jnp.mean(h2 * h2, axis=-1, keepdims=True)
    h2_s[...] = h2
    n_s[...] = (h2 * lax.rsqrt(ms + EPS) * g_ref[...]).astype(BF16)

    @pl.when(i == n_steps - 1)
    def _():
        gather(nxt, 1 - slot, lambda c: c.wait())


def _combine_ple(chunk_dst, ys, h1, route, p2, g_ple, w_gate, w_proj):
    T = h1.shape[0]
    tm = TM_TOK
    last = T // tm - 1
    cur = lambda w: pl.BlockSpec((tm, w), lambda i, *_: (jnp.minimum(i, last), 0))
    prev = lambda w: pl.BlockSpec((tm, w), lambda i, *_: (jnp.maximum(i - 1, 0), 0))
    const = lambda a, b: pl.BlockSpec((a, b), lambda i, *_: (0, 0))
    grid_spec = pltpu.PrefetchScalarGridSpec(
        num_scalar_prefetch=1,
        grid=(T // tm + 1,),
        in_specs=[pl.BlockSpec(memory_space=pl.ANY), cur(D_MODEL), cur(SUBLANES), prev(PLE_DIM),
                  const(1, D_MODEL), _resident((D_MODEL, D_MODEL)), _resident((PLE_DIM, D_MODEL))],
        out_specs=prev(D_MODEL),
        scratch_shapes=[pltpu.VMEM((2, SORT_ROWS, D_MODEL), BF16), pltpu.SemaphoreType.DMA((2,)),
                        pltpu.VMEM((D_MODEL, D_MODEL), BF16), pltpu.VMEM((PLE_DIM, D_MODEL), BF16),
                        pltpu.VMEM((tm, D_MODEL), F32), pltpu.VMEM((tm, D_MODEL), BF16)],
    )
    return pl.pallas_call(
        _combine_kernel,
        grid_spec=grid_spec,
        out_shape=jax.ShapeDtypeStruct((T, D_MODEL), F32),
        compiler_params=_cparams("arbitrary"),
        name="combine_ple",
    )(chunk_dst, ys, h1, route, p2, g_ple, w_gate, w_proj)


def _rope_tables(positions):
    half = ROT_DIM // 2
    inv = ROPE_THETA ** (-jnp.arange(0, ROT_DIM, 2, dtype=F32) / ROT_DIM)
    ang = positions.astype(F32)[:, None, :] * inv[None, :, None]
    lane = jnp.arange(LANES) % HEAD_DIM
    freq = jnp.arange(half)[:, None]
    rot = (lane[None, :] < ROT_DIM) & (lane[None, :] % half == freq)
    e_cos = rot.astype(BF16)
    e_sin = jnp.where(rot, jnp.where(lane[None, :] < half, -1.0, 1.0), 0.0).astype(BF16)

    def expand(tab, sel):
        hi = tab.astype(BF16)
        lo = (tab - hi.astype(F32)).astype(BF16)
        pick = functools.partial(jnp.einsum, "bfs,fl->bsl", preferred_element_type=F32)
        return pick(hi, sel) + pick(lo, sel)

    return expand(jnp.cos(ang), e_cos) + (lane >= ROT_DIM).astype(F32), expand(jnp.sin(ang), e_sin)


def _layer(i, h, p, cos_t, sin_t, g_mix, w_in, b_f, qn_a, kn_a, qn_b, kn_b, w_o, g_ffn, w_rg, b_rg,
           w_re, b_re, w_up, w_down, g_ple, w_ple_gate, w_ple_proj):
    B, S, _ = h.shape
    T = B * S
    x2 = h.reshape(T, D_MODEL)

    bf = jnp.zeros((1, LANES), F32).at[0, :N_HEADS].set(b_f[i])
    pair = lambda g: jnp.tile(g, 2).reshape(1, LANES)
    w_r = jnp.zeros((D_MODEL, LANES), F32)
    w_r = w_r.at[:, :N_GROUPS].set(w_rg[i])
    w_r = w_r.at[:, ROUTER_LANE0:ROUTER_LANE0 + N_EXPERTS].set(
        jnp.transpose(w_re[i], (1, 0, 2)).reshape(D_MODEL, N_EXPERTS))
    b_r = jnp.zeros((1, LANES), F32).at[0, :N_GROUPS].set(b_rg[i])
    b_r = b_r.at[0, ROUTER_LANE0:ROUTER_LANE0 + N_EXPERTS].set(b_re[i].reshape(-1))

    qkv, f = _in_proj(x2, g_mix[i].reshape(1, -1), w_in, i)
    kbias = _forget_scan(f, bf, B, S)
    qkv3 = qkv.reshape(B, S, N_QKV)
    oa = _dilated_attn(qkv3, cos_t, sin_t, pair(qn_a[i]), pair(kn_a[i]))
    ob = _fox_attn(qkv3, kbias, pair(qn_b[i]), pair(kn_b[i]))

    h1, msort, route, cnt = _out_router(oa.reshape(T, D_SEC), ob.reshape(T, D_SEC), x2, w_o[i],
                                        g_ffn[i].reshape(1, -1), w_r.astype(BF16), b_r)

    n_tok_tiles = T // TM_TOK
    counts = cnt[:, 0, ROUTER_LANE0:ROUTER_LANE0 + N_EXPERTS].astype(jnp.int32)
    seg_rows = (counts + SEG_ALIGN - 1) // SEG_ALIGN * SEG_ALIGN
    seg_local = jnp.cumsum(seg_rows, axis=1) - seg_rows
    rows_e = jnp.sum(seg_rows, axis=0)
    tile_end = jnp.cumsum((rows_e + TM_MOE - 1) // TM_MOE)
    off = jnp.concatenate([jnp.zeros((1,), jnp.int32), tile_end * TM_MOE]).astype(jnp.int32)
    seg_global = off[None, :N_EXPERTS] + jnp.cumsum(seg_rows, axis=0) - seg_rows
    n_tiles = (2 * T + n_tok_tiles * N_EXPERTS * (SEG_ALIGN - 1)) // TM_MOE + N_EXPERTS
    nused = tile_end[-1:].astype(jnp.int32)
    tile_ids = jnp.minimum(jnp.arange(n_tiles, dtype=jnp.int32), nused[0] - 1)
    tile_e = jnp.sum((tile_ids[:, None] >= tile_end[None, :]).astype(jnp.int32), axis=1).astype(jnp.int32)
    chunk_row = jnp.arange(SORT_CHUNKS, dtype=jnp.int32)[None, :, None] * SEG_ALIGN
    lo, hi = seg_local[:, None, :], (seg_local + seg_rows)[:, None, :]
    chunk_dst = jnp.sum(jnp.where((chunk_row >= lo) & (chunk_row < hi), seg_global[:, None, :] + chunk_row - lo, 0),
                        axis=2).reshape(-1).astype(jnp.int32)
    sorted_row = jnp.arange(n_tiles * MOE_CHUNKS, dtype=jnp.int32)[:, None] * SEG_ALIGN
    seg_lo = seg_global.reshape(1, -1)
    seg_src = (jnp.arange(n_tok_tiles, dtype=jnp.int32)[:, None] * SORT_ROWS + seg_local).reshape(1, -1)
    in_seg = (sorted_row >= seg_lo) & (sorted_row < seg_lo + seg_rows.reshape(1, -1))
    src_chunk = jnp.where(jnp.any(in_seg, axis=1),
                          jnp.sum(jnp.where(in_seg, seg_src + sorted_row - seg_lo, 0), axis=1),
                          SORT_ROWS - SEG_ALIGN).astype(jnp.int32)
    e_ids = jnp.arange(N_EXPERTS, dtype=jnp.int32)
    active = rows_e > 0
    later_active = jnp.where(active[None, :] & (e_ids[None, :] > e_ids[:, None]), e_ids[None, :], N_EXPERTS)
    next_e = jnp.min(later_active, axis=1)
    next_e = jnp.where(next_e == N_EXPERTS, -1, next_e).astype(jnp.int32)
    slot_e = ((jnp.cumsum(active.astype(jnp.int32)) - 1) % 2).astype(jnp.int32)

    ys = _moe_experts(tile_e, nused, src_chunk, next_e, slot_e, msort, w_up[i], w_down[i])
    out = _combine_ple(chunk_dst, ys, h1, route, p[i].reshape(T, PLE_DIM),
                       g_ple[i].reshape(1, -1), w_ple_gate[i], w_ple_proj[i])
    return out.reshape(B, S, D_MODEL)


def kernel(x, p, positions, g_mix, w_in, b_f, qn_a, kn_a, qn_b, kn_b, w_o, g_ffn, w_rg, b_rg, w_re, b_re,
           w_up, w_down, g_ple, w_ple_gate, w_ple_proj):
    cos_t, sin_t = _rope_tables(positions)
    h = x
    for i in range(p.shape[0]):
        h = _layer(i, h, p, cos_t, sin_t, g_mix, w_in, b_f, qn_a, kn_a, qn_b, kn_b, w_o, g_ffn, w_rg, b_rg,
                   w_re, b_re, w_up, w_down, g_ple, w_ple_gate, w_ple_proj)
    return h
```

```python
import functools
import math

import jax
import jax.numpy as jnp
from jax import lax
from jax.experimental import pallas as pl
from jax.experimental.pallas import tpu as pltpu

D_MODEL = 1024
HEAD_DIM = 64
N_HEADS = 8
D_SEC = N_HEADS * HEAD_DIM
N_QKV = 6 * D_SEC
ROT_DIM = HEAD_DIM // 4
ROPE_THETA = 500000.0
N_GROUPS = 4
EXPERTS_PER_GROUP = 8
N_EXPERTS = N_GROUPS * EXPERTS_PER_GROUP
D_EXPERT = 512
PLE_DIM = 256
EPS = 1e-6
NEG = -1e30
WINDOW = 128

LANES = 128
SUBLANES = 8
VMEM_LIMIT = 48 * 1024 * 1024

TM_PROJ = 512
TQ = 128
TK = 256
N_PAIRS = N_HEADS // 2
TM_MOE = 512
TM_TOK = 512
SEG_ALIGN = 2 * SUBLANES
SORT_ROWS = 2 * TM_TOK + N_EXPERTS * SEG_ALIGN
SORT_CHUNKS = SORT_ROWS // SEG_ALIGN
ROUTER_LANE0 = N_GROUPS

Q_SCALE_LOG2 = math.log2(math.e) / math.sqrt(HEAD_DIM)

F32 = jnp.float32
BF16 = jnp.bfloat16
NT_DIMS = (((1,), (1,)), ((), ()))


def _cparams(*sem):
    return pltpu.CompilerParams(dimension_semantics=sem, vmem_limit_bytes=VMEM_LIMIT)


def _resident(shape):
    return pl.BlockSpec(shape, lambda i, *_: (0,) * len(shape), pipeline_mode=pl.Buffered(1))


def _in_proj_kernel(x_ref, g_ref, w_ref, qkv_ref, f_ref, w_s, wf_s):
    @pl.when(pl.program_id(0) == 0)
    def _():
        w_s[...] = w_ref[0, :, 0:N_QKV].astype(BF16)
        wf_s[...] = jnp.zeros_like(wf_s)
        wf_s[:, 0:N_HEADS] = w_ref[0, :, N_QKV:N_QKV + N_HEADS].astype(BF16)

    x = x_ref[...]
    ms = jnp.mean(x * x, axis=-1, keepdims=True)
    a = (x * lax.rsqrt(ms + EPS) * g_ref[...]).astype(BF16)
    qkv_ref[...] = jnp.dot(a, w_s[...], preferred_element_type=F32).astype(BF16)
    f_ref[...] = jnp.dot(a, wf_s[...], preferred_element_type=F32)


def _in_proj(x2, g_mix, w_in, layer):
    T = x2.shape[0]
    w_block = (1,) + w_in.shape[1:]
    return pl.pallas_call(
        _in_proj_kernel,
        grid=(T // TM_PROJ,),
        in_specs=[
            pl.BlockSpec((TM_PROJ, D_MODEL), lambda i: (i, 0)),
            pl.BlockSpec((1, D_MODEL), lambda i: (0, 0)),
            pl.BlockSpec(w_block, lambda i: (layer, 0, 0), pipeline_mode=pl.Buffered(1)),
        ],
        out_specs=[
            pl.BlockSpec((TM_PROJ, N_QKV), lambda i: (i, 0)),
            pl.BlockSpec((TM_PROJ, LANES), lambda i: (i, 0)),
        ],
        out_shape=[
            jax.ShapeDtypeStruct((T, N_QKV), BF16),
            jax.ShapeDtypeStruct((T, LANES), F32),
        ],
        scratch_shapes=[pltpu.VMEM((D_MODEL, N_QKV), BF16), pltpu.VMEM((D_MODEL, LANES), BF16)],
        compiler_params=_cparams("arbitrary"),
        name="in_proj",
    )(x2, g_mix, w_in)


BIAS_TERMS = 3


def _forget_scan_kernel(f_ref, bf_ref, kb_ref, *, seq):
    f = f_ref[...] + bf_ref[...]
    logf = jnp.minimum(f, 0.0) - jnp.log1p(jnp.exp(-jnp.abs(f)))

    def split_bf16(v):
        terms = []
        for _ in range(BIAS_TERMS):
            t = v.astype(BF16)
            terms.append(t)
            v = v - t.astype(F32)
        return terms

    tri = (lax.broadcasted_iota(jnp.int32, (TK, TK), 1) <= lax.broadcasted_iota(jnp.int32, (TK, TK), 0)).astype(BF16)
    carry = jnp.zeros((1, LANES), F32)
    chunks = []
    for j in range(seq // TK):
        terms = jnp.concatenate(split_bf16(logf[j * TK:(j + 1) * TK, :]), axis=1)
        pre = jnp.dot(tri, terms, preferred_element_type=F32)
        cj = carry + sum(pre[:, t * LANES:(t + 1) * LANES] for t in range(BIAS_TERMS))
        carry = cj[TK - 1:TK, :]
        chunks.append(cj)
    c = jnp.concatenate(chunks, axis=0)
    rest = c * (-math.log2(math.e))
    r_idx = lax.broadcasted_iota(jnp.int32, (LANES, D_SEC), 0)
    c_idx = lax.broadcasted_iota(jnp.int32, (LANES, D_SEC), 1)
    base = (r_idx // 2) * LANES + jnp.where(r_idx % 2 == 0, HEAD_DIM, 0)
    out = jnp.zeros((seq, D_SEC), F32)
    for t, term in enumerate(split_bf16(rest)):
        place = ((r_idx < N_HEADS) & (c_idx == base + t)).astype(BF16)
        out = out + jnp.dot(term, place, preferred_element_type=F32)
    kb_ref[0] = out.astype(BF16)


def _forget_scan(f, bf, batch, seq):
    return pl.pallas_call(
        functools.partial(_forget_scan_kernel, seq=seq),
        grid=(batch,),
        in_specs=[
            pl.BlockSpec((seq, LANES), lambda b: (b, 0)),
            pl.BlockSpec((1, LANES), lambda b: (0, 0)),
        ],
        out_specs=pl.BlockSpec((1, seq, D_SEC), lambda b: (b, 0, 0)),
        out_shape=jax.ShapeDtypeStruct((batch, seq, D_SEC), BF16),
        compiler_params=_cparams("parallel"),
        name="forget_scan",
    )(f, bf)


def _head_sumsq_matrix():
    r = lax.broadcasted_iota(jnp.int32, (LANES, LANES), 0) // HEAD_DIM
    c = lax.broadcasted_iota(jnp.int32, (LANES, LANES), 1) // HEAD_DIM
    return (r == c).astype(BF16)


def _qk_norm(x, gain, gmat):
    ss = jnp.dot((x * x).astype(BF16), gmat, preferred_element_type=F32)
    return x * lax.rsqrt(ss * (1.0 / HEAD_DIM) + EPS) * gain


def _stack_heads(qb, head0):
    zero = jnp.zeros_like(qb)
    return jnp.concatenate([jnp.where(head0, qb, zero), jnp.where(head0, zero, qb)], axis=0)


def _unstack(a, head0):
    return jnp.where(head0, a[:TQ], a[TQ:])


def _dilated_kernel(q_ref, k_ref, v_ref, cos_ref, sin_ref, gq_ref, gk_ref, o_ref,
                    qn_s, kn_s, v_s, acc_s, m_s, l_s, bias_s, *, seq):
    gmat = _head_sumsq_matrix()
    lane = lax.broadcasted_iota(jnp.int32, (TQ, LANES), 1)
    head0 = lane < HEAD_DIM
    chunk = 512
    half = ROT_DIM // 2
    src = lax.broadcasted_iota(jnp.int32, (LANES, LANES), 0)
    dst = lax.broadcasted_iota(jnp.int32, (LANES, LANES), 1)
    dst_h = dst % HEAD_DIM
    perm = (((dst_h < half) & (src == dst + half))
            | ((dst_h >= half) & (dst_h < ROT_DIM) & (src == dst - half))).astype(BF16)

    def rope(t, cs, sn):
        hi = t.astype(BF16)
        lo = (t - hi.astype(F32)).astype(BF16)
        partner = (jnp.dot(hi, perm, preferred_element_type=F32) + jnp.dot(lo, perm, preferred_element_type=F32))
        return t * cs + partner * sn

    for c0 in range(0, seq, chunk):
        rows = pl.ds(c0, chunk)
        cs = cos_ref[0, rows, :]
        sn = sin_ref[0, rows, :]
        qn = _qk_norm(q_ref[0, rows, :].astype(F32), gq_ref[...], gmat)
        qn_s[rows, :] = rope(qn, cs, sn) * Q_SCALE_LOG2
        kn = _qk_norm(k_ref[0, rows, :].astype(F32), gk_ref[...], gmat)
        kn_s[rows, :] = rope(kn, cs, sn)
        v_s[rows, :] = v_ref[0, rows, :].astype(F32)

    u = lax.broadcasted_iota(jnp.int32, (2 * TQ, 2 * TQ), 0) % TQ
    c = lax.broadcasted_iota(jnp.int32, (2 * TQ, 2 * TQ), 1)
    for slot, dist0 in enumerate((0, TQ)):
        dist = dist0 + u - c
        bias_s[slot] = jnp.where((dist >= 0) & (dist <= WINDOW), 0.0, NEG)

    def scores(blk):
        p, q_rows, k_rows, nk, dist0 = blk
        qst = _stack_heads(qn_s[q_rows, :].astype(BF16), head0)
        s = lax.dot_general(qst, kn_s[k_rows, :].astype(BF16), NT_DIMS, preferred_element_type=F32)
        return s + bias_s[dist0 // TQ, :, 0:nk]

    def finish(blk, s):
        p, q_rows, k_rows, nk, dist0 = blk
        m = jnp.max(s, axis=-1, keepdims=True)
        pr = jnp.exp2(s - m)
        v_ones = jnp.concatenate([v_s[k_rows, :].astype(BF16), jnp.ones((nk, LANES), BF16)], axis=1)
        acc = jnp.dot(pr.astype(BF16), v_ones, preferred_element_type=F32)
        acc_s[p, q_rows, :] = _unstack(acc[:, :LANES], head0)
        m_s[p, q_rows, :] = _unstack(jnp.broadcast_to(m, (2 * TQ, LANES)), head0)
        l_s[p, q_rows, :] = _unstack(acc[:, LANES:], head0)

    blocks = []
    for i in range(seq // TQ):
        kb0 = max(i - 1, 0)
        blocks.append((0, pl.ds(i * TQ, TQ), pl.ds(kb0 * TQ, 2 * TQ), 2 * TQ, (i - kb0) * TQ))
    for r in range(4):
        for n in range(seq // (4 * TQ)):
            kb0 = max(n - 1, 0)
            blocks.append((1, pl.ds(4 * TQ * n + r, TQ, stride=4), pl.ds(4 * TQ * kb0 + r, 2 * TQ, stride=4),
                           2 * TQ, (n - kb0) * TQ))
    for r in range(16):
        rows = pl.ds(r, TQ, stride=16)
        blocks.append((2, rows, rows, TQ, 0))

    ahead = 2
    pending = [scores(b) for b in blocks[:ahead]]
    for idx, b in enumerate(blocks):
        s = pending.pop(0)
        if idx + ahead < len(blocks):
            pending.append(scores(blocks[idx + ahead]))
        finish(b, s)

    for c0 in range(0, seq, chunk):
        rows = pl.ds(c0, chunk)
        m0, m1, m2 = m_s[0, rows, :], m_s[1, rows, :], m_s[2, rows, :]
        mm = jnp.maximum(jnp.maximum(m0, m1), m2)
        e0, e1, e2 = jnp.exp2(m0 - mm), jnp.exp2(m1 - mm), jnp.exp2(m2 - mm)
        num = acc_s[0, rows, :] * e0 + acc_s[1, rows, :] * e1 + acc_s[2, rows, :] * e2
        den = l_s[0, rows, :] * e0 + l_s[1, rows, :] * e1 + l_s[2, rows, :] * e2
        o_ref[0, rows, :] = (num / den).astype(BF16)


def _dilated_attn(qkv3, cos_t, sin_t, gq, gk):
    batch, seq, _ = qkv3.shape
    sec = D_SEC // LANES
    blk = lambda s: pl.BlockSpec((1, seq, LANES), lambda b, h, s=s: (b, 0, s * sec + h))
    tab = pl.BlockSpec((1, seq, LANES), lambda b, h: (b, 0, 0))
    vec = pl.BlockSpec((1, LANES), lambda b, h: (0, 0))
    return pl.pallas_call(
        functools.partial(_dilated_kernel, seq=seq),
        grid=(batch, N_PAIRS),
        in_specs=[blk(0), blk(1), blk(2), tab, tab, vec, vec],
        out_specs=pl.BlockSpec((1, seq, LANES), lambda b, h: (b, 0, h)),
        out_shape=jax.ShapeDtypeStruct((batch, seq, D_SEC), BF16),
        scratch_shapes=[
            pltpu.VMEM((seq, LANES), F32), pltpu.VMEM((seq, LANES), F32), pltpu.VMEM((seq, LANES), F32),
            pltpu.VMEM((3, seq, LANES), F32), pltpu.VMEM((3, seq, LANES), F32),
            pltpu.VMEM((3, seq, LANES), F32), pltpu.VMEM((2, 2 * TQ, 2 * TQ), F32),
        ],
        compiler_params=_cparams("parallel", "parallel"),
        name="dilated_attn",
    )(qkv3, qkv3, qkv3, cos_t, sin_t, gq, gk)


def _fox_kernel(q_ref, k_ref, v_ref, kb_ref, gq_ref, gk_ref, o_ref, qat_s, ka_s, vt_s, acc_s, tri_s, *, seq):
    gmat = _head_sumsq_matrix()
    chunk = 512
    lane = lax.broadcasted_iota(jnp.int32, (chunk, LANES), 1)
    low = lane < HEAD_DIM
    sel_t = jnp.where(lax.broadcasted_iota(jnp.int32, (HEAD_DIM, TK), 0) < BIAS_TERMS, 1.0, 0.0).astype(BF16)
    eye = (lax.broadcasted_iota(jnp.int32, (LANES, LANES), 0)
           == lax.broadcasted_iota(jnp.int32, (LANES, LANES), 1)).astype(BF16)

    def transpose_bf16(a):
        return lax.dot_general(eye, a, NT_DIMS, preferred_element_type=F32)

    for c0 in range(0, seq, chunk):
        rows = pl.ds(c0, chunk)
        for g in range(N_PAIRS):
            lanes = slice(g * LANES, (g + 1) * LANES)
            qn = _qk_norm(q_ref[0, rows, lanes].astype(F32), gq_ref[...], gmat) * Q_SCALE_LOG2
            qt = transpose_bf16(qn.astype(BF16))
            for cc in range(chunk // TK):
                ci, cols = c0 // TK + cc, slice(cc * TK, (cc + 1) * TK)
                even, odd = 2 * g * LANES, (2 * g + 1) * LANES
                qat_s[ci, even:even + HEAD_DIM, :] = qt[0:HEAD_DIM, cols].astype(BF16)
                qat_s[ci, even + HEAD_DIM:even + LANES, :] = sel_t
                qat_s[ci, odd:odd + HEAD_DIM, :] = sel_t
                qat_s[ci, odd + HEAD_DIM:odd + LANES, :] = qt[HEAD_DIM:LANES, cols].astype(BF16)
            kn = _qk_norm(k_ref[0, rows, lanes].astype(F32), gk_ref[...], gmat)
            kb = kb_ref[0, rows, lanes].astype(F32)
            ka_s[2 * g, rows, :] = jnp.where(low, kn, kb).astype(BF16)
            ka_s[2 * g + 1, rows, :] = jnp.where(low, kb, kn).astype(BF16)
            vt = transpose_bf16(v_ref[0, rows, lanes])
            for cc in range(chunk // TK):
                vt_s[c0 // TK + cc, lanes, :] = vt[:, cc * TK:(cc + 1) * TK].astype(BF16)

    r = lax.broadcasted_iota(jnp.int32, (TK, TK), 0)
    c = lax.broadcasted_iota(jnp.int32, (TK, TK), 1)
    tri_s[...] = jnp.where(r <= c, 0.0, NEG)

    def q_block(qi, carry):
        q_rows = pl.ds(pl.multiple_of(qi * TK, TK), TK)
        for h in range(N_HEADS):
            acc_s[h] = jnp.zeros((HEAD_DIM, TK), F32)

        def step(j, st, diagonal):
            k_rows = pl.ds(pl.multiple_of(j * TK, TK), TK)

            def scores(h):
                s = jnp.dot(ka_s[h, k_rows, :], qat_s[qi, h * LANES:(h + 1) * LANES, :], preferred_element_type=F32)
                return s + tri_s[...] if diagonal else s

            ahead = 6
            pending = [scores(h) for h in range(ahead)]
            new = []
            for h in range(N_HEADS):
                m, l = st[h]
                s = pending.pop(0)
                if h + ahead < N_HEADS:
                    pending.append(scores(h + ahead))
                m_new = jnp.maximum(m, jnp.max(s, axis=0, keepdims=True))
                alpha = jnp.exp2(m - m_new)
                pr = jnp.exp2(s - m_new)
                l_new = alpha * l + jnp.sum(pr, axis=0, keepdims=True)
                pv = jnp.dot(vt_s[j, h * HEAD_DIM:(h + 1) * HEAD_DIM, :], pr.astype(BF16),
                             preferred_element_type=F32)
                acc_s[h] = alpha * acc_s[h] + pv
                new.append((m_new, l_new))
            return tuple(new)

        init = tuple((jnp.full((1, TK), NEG, F32), jnp.zeros((1, TK), F32)) for _ in range(N_HEADS))
        st = lax.fori_loop(0, qi // 2, lambda jj, st: step(2 * jj + 1, step(2 * jj, st, False), False), init)
        st = lax.cond(qi % 2 == 1, lambda st: step(qi - 1, st, False), lambda st: st, st)
        st = step(qi, st, True)
        for g in range(N_PAIRS):
            o2 = jnp.concatenate([acc_s[2 * g] / st[2 * g][1], acc_s[2 * g + 1] / st[2 * g + 1][1]], axis=0)
            o_ref[0, q_rows, g * LANES:(g + 1) * LANES] = o2.T.astype(BF16)
        return carry

    lax.fori_loop(0, seq // TK, q_block, 0)


def _fox_attn(qkv3, kbias, gq, gk):
    batch, seq, _ = qkv3.shape
    blk = lambda s: pl.BlockSpec((1, seq, D_SEC), lambda b, s=s: (b, 0, s))
    vec = pl.BlockSpec((1, LANES), lambda b: (0, 0))
    return pl.pallas_call(
        functools.partial(_fox_kernel, seq=seq),
        grid=(batch,),
        in_specs=[blk(3), blk(4), blk(5), pl.BlockSpec((1, seq, D_SEC), lambda b: (b, 0, 0)), vec, vec],
        out_specs=pl.BlockSpec((1, seq, D_SEC), lambda b: (b, 0, 0)),
        out_shape=jax.ShapeDtypeStruct((batch, seq, D_SEC), BF16),
        scratch_shapes=[
            pltpu.VMEM((seq // TK, N_HEADS * LANES, TK), BF16),
            pltpu.VMEM((N_HEADS, seq, LANES), BF16),
            pltpu.VMEM((seq // TK, D_SEC, TK), BF16),
            pltpu.VMEM((N_HEADS, HEAD_DIM, TK), F32),
            pltpu.VMEM((TK, TK), F32),
        ],
        compiler_params=_cparams("parallel"),
        name="fox_attn",
    )(qkv3, qkv3, qkv3, kbias, gq, gk)


def _out_router_kernel(oa_ref, ob_ref, x_ref, wo_ref, g_ref, wr_ref, br_ref,
                       h1_ref, msort_ref, route_ref, cnt_ref, wo_s, pick_s, mn_s, tri_s):
    tm = TM_TOK

    @pl.when(pl.program_id(0) == 0)
    def _():
        wo_s[...] = wo_ref[...].astype(BF16)
        pick_s[...] = jnp.zeros_like(pick_s)
        mn_s[...] = jnp.zeros_like(mn_s)
        r = lax.broadcasted_iota(jnp.int32, (tm, tm), 0)
        c = lax.broadcasted_iota(jnp.int32, (tm, tm), 1)
        tri_s[...] = (c <= r).astype(BF16)

    h1 = (x_ref[...]
          + jnp.dot(oa_ref[...], wo_s[0:D_SEC, :], preferred_element_type=F32)
          + jnp.dot(ob_ref[...], wo_s[D_SEC:2 * D_SEC, :], preferred_element_type=F32))
    h1_ref[...] = h1
    ms = jnp.mean(h1 * h1, axis=-1, keepdims=True)
    mn = h1 * lax.rsqrt(ms + EPS) * g_ref[...]

    logits = jnp.dot(mn.astype(BF16), wr_ref[...], preferred_element_type=F32) + br_ref[...]

    msort_ref[...] = jnp.dot(pick_s[...], mn_s[...], preferred_element_type=F32).astype(BF16)

    lane = lax.broadcasted_iota(jnp.int32, (tm, LANES), 1).astype(F32)
    big = float(LANES)

    def first_argmax(vals):
        vmax = jnp.max(vals, axis=-1, keepdims=True)
        idx = jnp.min(jnp.where(vals == vmax, lane, big), axis=-1, keepdims=True)
        return vmax, idx

    lg = jnp.where(lane < N_GROUPS, logits, -jnp.inf)
    gmax, gidx = first_argmax(lg)
    gw = 1.0 / jnp.sum(jnp.exp(lg - gmax), axis=-1, keepdims=True)
    lo = ROUTER_LANE0 + EXPERTS_PER_GROUP * gidx
    le = jnp.where((lane >= lo) & (lane < lo + EXPERTS_PER_GROUP), logits, -jnp.inf)
    v0, i0 = first_argmax(le)
    v1, i1 = first_argmax(jnp.where(lane == i0, -jnp.inf, le))
    ex = jnp.exp(v1 - v0)
    w0 = gw / (1.0 + ex)
    w1 = gw * ex / (1.0 + ex)

    sel0 = lane == i0
    sel1 = lane == i1
    onehot = jnp.where(sel0 | sel1, 1.0, 0.0)
    incl = jnp.dot(tri_s[...], onehot.astype(BF16), preferred_element_type=F32)
    excl = incl - onehot
    counts = incl[tm - 1:tm, :]
    cnt_ref[0] = jnp.broadcast_to(counts, (SUBLANES, LANES))

    seg_rows = jnp.floor((counts + (SEG_ALIGN - 1.0)) * (1.0 / SEG_ALIGN)) * SEG_ALIGN
    seg_rows = jnp.broadcast_to(seg_rows, (tm, LANES))
    lp0 = jnp.sum(jnp.where(lane < i0, seg_rows, 0.0) + jnp.where(sel0, excl, 0.0), axis=-1, keepdims=True)
    lp1 = jnp.sum(jnp.where(lane < i1, seg_rows, 0.0) + jnp.where(sel1, excl, 0.0), axis=-1, keepdims=True)

    rec = jnp.zeros((tm, LANES), F32)
    for pos, val in enumerate((i0 - ROUTER_LANE0, i1 - ROUTER_LANE0, lp0, lp1, w0, w1)):
        rec = jnp.where(lane == float(pos), val, rec)
    route_ref[...] = rec[:, 0:SUBLANES]

    rec_t = rec.T
    out_row = lax.broadcasted_iota(jnp.int32, (SORT_ROWS, tm), 0).astype(F32)
    pick_s[...] = jnp.where((out_row == rec_t[2:3, :]) | (out_row == rec_t[3:4, :]), 1.0, 0.0).astype(BF16)
    mn_s[...] = mn.astype(BF16)


def _out_router(oa, ob, x2, w_o, g_ffn, w_r, b_r):
    T = x2.shape[0]
    tm = TM_TOK
    last = T // tm - 1
    row = lambda w: pl.BlockSpec((tm, w), lambda i: (jnp.minimum(i, last), 0))
    const = lambda a, b: pl.BlockSpec((a, b), lambda i: (0, 0))
    return pl.pallas_call(
        _out_router_kernel,
        grid=(T // tm + 1,),
        in_specs=[row(D_SEC), row(D_SEC), row(D_MODEL), _resident((D_MODEL, D_MODEL)), const(1, D_MODEL),
                  const(D_MODEL, LANES), const(1, LANES)],
        out_specs=[row(D_MODEL), pl.BlockSpec((SORT_ROWS, D_MODEL), lambda i: (jnp.maximum(i - 1, 0), 0)),
                   row(SUBLANES), pl.BlockSpec((1, SUBLANES, LANES), lambda i: (jnp.minimum(i, last), 0, 0))],
        out_shape=[
            jax.ShapeDtypeStruct((T, D_MODEL), F32),
            jax.ShapeDtypeStruct((T // tm * SORT_ROWS, D_MODEL), BF16),
            jax.ShapeDtypeStruct((T, SUBLANES), F32),
            jax.ShapeDtypeStruct((T // tm, SUBLANES, LANES), F32),
        ],
        scratch_shapes=[pltpu.VMEM((D_MODEL, D_MODEL), BF16), pltpu.VMEM((SORT_ROWS, tm), BF16),
                        pltpu.VMEM((tm, D_MODEL), BF16), pltpu.VMEM((tm, tm), BF16)],
        compiler_params=_cparams("arbitrary"),
        name="out_router",
    )(oa, ob, x2, w_o, g_ffn, w_r, b_r)


MOE_CHUNKS = TM_MOE // SEG_ALIGN


def _chunk_row(c):
    return pl.multiple_of(c * SEG_ALIGN, SEG_ALIGN)


def _start_split(copy, parity):
    copy.start(priority=parity)


def _wait(copy, parity):
    del parity
    copy.wait()


def _moe_kernel(tile_e_ref, nused_ref, src_ref, next_e_ref, slot_e_ref, msort_ref, wup_ref, wdn_ref, ys_ref,
                x_s, wup_f, wdn_f, wup_s, wdn_s, xsem, wsem):
    i = pl.program_id(0)
    nused = nused_ref[0]

    def gather(k, act):
        def pair(g, carry):
            for parity in range(2):
                c = 2 * g + parity
                src = pl.multiple_of(src_ref[k * MOE_CHUNKS + c], SEG_ALIGN)
                act(pltpu.make_async_copy(msort_ref.at[pl.ds(src, SEG_ALIGN)],
                                          x_s.at[k % 2, pl.ds(_chunk_row(c), SEG_ALIGN)], xsem.at[k % 2]), parity)
            return carry
        lax.fori_loop(0, MOE_CHUNKS // 2, pair, 0, unroll=4)

    def weights(e, s, act):
        act(pltpu.make_async_copy(wup_ref.at[e], wup_f.at[s], wsem.at[0, s]))
        act(pltpu.make_async_copy(wdn_ref.at[e], wdn_f.at[s], wsem.at[1, s]))

    @pl.when(i == 0)
    def _():
        gather(0, _start_split)
        weights(tile_e_ref[0], 0, lambda c: c.start())

    @pl.when(i + 1 < nused)
    def _():
        gather(i + 1, _start_split)

    @pl.when(i < nused)
    def _():
        e = tile_e_ref[i]

        @pl.when((i == 0) | (tile_e_ref[jnp.maximum(i - 1, 0)] != e))
        def _():
            s = slot_e_ref[e]
            weights(e, s, lambda c: c.wait())

            @pl.when(next_e_ref[e] >= 0)
            def _():
                weights(next_e_ref[e], 1 - s, lambda c: c.start())
            wup_s[...] = wup_f[s].astype(BF16)
            wdn_s[...] = wdn_f[s].astype(BF16)

        gather(i, _wait)
        hu = jnp.dot(x_s[i % 2], wup_s[...], preferred_element_type=F32)
        gate = hu[:, :D_EXPERT]
        hid = gate * (1.0 / (1.0 + jnp.exp(-gate))) * hu[:, D_EXPERT:]
        ys_ref[...] = jnp.dot(hid.astype(BF16), wdn_s[...], preferred_element_type=F32).astype(BF16)

    @pl.when(i >= nused)
    def _():
        ys_ref[...] = jnp.zeros_like(ys_ref)


def _moe_experts(tile_e, nused, src_chunk, next_e, slot_e, msort, w_up, w_down):
    n_tiles = tile_e.shape[0]
    grid_spec = pltpu.PrefetchScalarGridSpec(
        num_scalar_prefetch=5,
        grid=(n_tiles,),
        in_specs=[pl.BlockSpec(memory_space=pl.ANY)] * 3,
        out_specs=pl.BlockSpec((TM_MOE, D_MODEL), lambda i, *_: (i, 0)),
        scratch_shapes=[
            pltpu.VMEM((2, TM_MOE, D_MODEL), BF16),
            pltpu.VMEM((2, D_MODEL, 2 * D_EXPERT), F32), pltpu.VMEM((2, D_EXPERT, D_MODEL), F32),
            pltpu.VMEM((D_MODEL, 2 * D_EXPERT), BF16), pltpu.VMEM((D_EXPERT, D_MODEL), BF16),
            pltpu.SemaphoreType.DMA((2,)), pltpu.SemaphoreType.DMA((2, 2)),
        ],
    )
    return pl.pallas_call(
        _moe_kernel,
        grid_spec=grid_spec,
        out_shape=jax.ShapeDtypeStruct((n_tiles * TM_MOE, D_MODEL), BF16),
        compiler_params=_cparams("arbitrary"),
        name="moe_experts",
    )(tile_e, nused, src_chunk, next_e, slot_e, msort, w_up, w_down)


def _combine_kernel(src_ref, ys_ref, h1_ref, route_ref, p_ref,
                    g_ref, wg_ref, wp_ref, o_ref, gath_s, sems, wg_s, wp_s, h2_s, n_s):
    i = pl.program_id(0)
    n_steps = pl.num_programs(0)
    last_tile = n_steps - 2

    tm = TM_TOK
    slot = i % 2

    def gather(t, s, act):
        for c in range(SORT_CHUNKS):
            src = pl.multiple_of(src_ref[t * SORT_CHUNKS + c], SEG_ALIGN)
            act(pltpu.make_async_copy(ys_ref.at[pl.ds(src, SEG_ALIGN)],
                                      gath_s.at[s, pl.ds(c * SEG_ALIGN, SEG_ALIGN)], sems.at[s]), c % 2)

    @pl.when(i == 0)
    def _():
        wg_s[...] = wg_ref[...].astype(BF16)
        wp_s[...] = wp_ref[...].astype(BF16)
        h2_s[...] = jnp.zeros_like(h2_s)
        n_s[...] = jnp.zeros_like(n_s)
        gather(0, 0, _start_split)

    cur = jnp.minimum(i, last_tile)
    nxt = jnp.minimum(i + 1, last_tile)
    gather(cur, slot, _wait)
    gather(nxt, 1 - slot, _start_split)

    ple = jnp.dot(p_ref[...].astype(BF16), wp_s[...], preferred_element_type=F32)
    z = jnp.dot(n_s[...], wg_s[...], preferred_element_type=F32)
    o_ref[...] = h2_s[...] + (1.0 / (1.0 + jnp.exp(-z))) * ple

    route = route_ref[...]
    col = lax.broadcasted_iota(jnp.int32, (tm, SORT_ROWS), 1).astype(F32)
    weights = (jnp.where(col == route[:, 2:3], route[:, 4:5], 0.0)
               + jnp.where(col == route[:, 3:4], route[:, 5:6], 0.0)).astype(BF16)
    y = jnp.dot(weights, gath_s[slot], preferred_element_type=F32)
    h2 = h1_ref[...] + y
    ms = jnp.mean(h2 * h2, axis=-1, keepdims=True)
    h2_s[...] = h2
    n_s[...] = (h2 * lax.rsqrt(ms + EPS) * g_ref[...]).astype(BF16)

    @pl.when(i == n_steps - 1)
    def _():
        gather(nxt, 1 - slot, _wait)


def _combine_ple(chunk_dst, ys, h1, route, p2, g_ple, w_gate, w_proj):
    T = h1.shape[0]
    tm = TM_TOK
    last = T // tm - 1
    cur = lambda w: pl.BlockSpec((tm, w), lambda i, *_: (jnp.minimum(i, last), 0))
    prev = lambda w: pl.BlockSpec((tm, w), lambda i, *_: (jnp.maximum(i - 1, 0), 0))
    const = lambda a, b: pl.BlockSpec((a, b), lambda i, *_: (0, 0))
    grid_spec = pltpu.PrefetchScalarGridSpec(
        num_scalar_prefetch=1,
        grid=(T // tm + 1,),
        in_specs=[pl.BlockSpec(memory_space=pl.ANY), cur(D_MODEL), cur(SUBLANES), prev(PLE_DIM),
                  const(1, D_MODEL), _resident((D_MODEL, D_MODEL)), _resident((PLE_DIM, D_MODEL))],
        out_specs=prev(D_MODEL),
        scratch_shapes=[pltpu.VMEM((2, SORT_ROWS, D_MODEL), BF16), pltpu.SemaphoreType.DMA((2,)),
                        pltpu.VMEM((D_MODEL, D_MODEL), BF16), pltpu.VMEM((PLE_DIM, D_MODEL), BF16),
                        pltpu.VMEM((tm, D_MODEL), F32), pltpu.VMEM((tm, D_MODEL), BF16)],
    )
    return pl.pallas_call(
        _combine_kernel,
        grid_spec=grid_spec,
        out_shape=jax.ShapeDtypeStruct((T, D_MODEL), F32),
        compiler_params=_cparams("arbitrary"),
        name="combine_ple",
    )(chunk_dst, ys, h1, route, p2, g_ple, w_gate, w_proj)


def _rope_tables(positions):
    half = ROT_DIM // 2
    inv = ROPE_THETA ** (-jnp.arange(0, ROT_DIM, 2, dtype=F32) / ROT_DIM)
    ang = positions.astype(F32)[:, None, :] * inv[None, :, None]
    lane = jnp.arange(LANES) % HEAD_DIM
    freq = jnp.arange(half)[:, None]
    rot = (lane[None, :] < ROT_DIM) & (lane[None, :] % half == freq)
    e_cos = rot.astype(BF16)
    e_sin = jnp.where(rot, jnp.where(lane[None, :] < half, -1.0, 1.0), 0.0).astype(BF16)

    def expand(tab, sel):
        hi = tab.astype(BF16)
        lo = (tab - hi.astype(F32)).astype(BF16)
        pick = functools.partial(jnp.einsum, "bfs,fl->bsl", preferred_element_type=F32)
        return pick(hi, sel) + pick(lo, sel)

    return expand(jnp.cos(ang), e_cos) + (lane >= ROT_DIM).astype(F32), expand(jnp.sin(ang), e_sin)


def _layer(i, h, p, cos_t, sin_t, g_mix, w_in, b_f, qn_a, kn_a, qn_b, kn_b, w_o, g_ffn, w_rg, b_rg,
           w_re, b_re, w_up, w_down, g_ple, w_ple_gate, w_ple_proj):
    B, S, _ = h.shape
    T = B * S
    x2 = h.reshape(T, D_MODEL)

    bf = jnp.zeros((1, LANES), F32).at[0, :N_HEADS].set(b_f[i])
    pair = lambda g: jnp.tile(g, 2).reshape(1, LANES)
    w_r = jnp.zeros((D_MODEL, LANES), F32)
    w_r = w_r.at[:, :N_GROUPS].set(w_rg[i])
    w_r = w_r.at[:, ROUTER_LANE0:ROUTER_LANE0 + N_EXPERTS].set(
        jnp.transpose(w_re[i], (1, 0, 2)).reshape(D_MODEL, N_EXPERTS))
    b_r = jnp.zeros((1, LANES), F32).at[0, :N_GROUPS].set(b_rg[i])
    b_r = b_r.at[0, ROUTER_LANE0:ROUTER_LANE0 + N_EXPERTS].set(b_re[i].reshape(-1))

    qkv, f = _in_proj(x2, g_mix[i].reshape(1, -1), w_in, i)
    kbias = _forget_scan(f, bf, B, S)
    qkv3 = qkv.reshape(B, S, N_QKV)
    oa = _dilated_attn(qkv3, cos_t, sin_t, pair(qn_a[i]), pair(kn_a[i]))
    ob = _fox_attn(qkv3, kbias, pair(qn_b[i]), pair(kn_b[i]))

    h1, msort, route, cnt = _out_router(oa.reshape(T, D_SEC), ob.reshape(T, D_SEC), x2, w_o[i],
                                        g_ffn[i].reshape(1, -1), w_r.astype(BF16), b_r)

    n_tok_tiles = T // TM_TOK
    counts = cnt[:, 0, ROUTER_LANE0:ROUTER_LANE0 + N_EXPERTS].astype(jnp.int32)
    seg_rows = (counts + SEG_ALIGN - 1) // SEG_ALIGN * SEG_ALIGN
    seg_local = jnp.cumsum(seg_rows, axis=1) - seg_rows
    rows_e = jnp.sum(seg_rows, axis=0)
    tile_end = jnp.cumsum((rows_e + TM_MOE - 1) // TM_MOE)
    off = jnp.concatenate([jnp.zeros((1,), jnp.int32), tile_end * TM_MOE]).astype(jnp.int32)
    seg_global = off[None, :N_EXPERTS] + jnp.cumsum(seg_rows, axis=0) - seg_rows
    n_tiles = (2 * T + n_tok_tiles * N_EXPERTS * (SEG_ALIGN - 1)) // TM_MOE + N_EXPERTS
    nused = tile_end[-1:].astype(jnp.int32)
    tile_ids = jnp.minimum(jnp.arange(n_tiles, dtype=jnp.int32), nused[0] - 1)
    tile_e = jnp.sum((tile_ids[:, None] >= tile_end[None, :]).astype(jnp.int32), axis=1).astype(jnp.int32)
    chunk_row = jnp.arange(SORT_CHUNKS, dtype=jnp.int32)[None, :, None] * SEG_ALIGN
    lo, hi = seg_local[:, None, :], (seg_local + seg_rows)[:, None, :]
    chunk_dst = jnp.sum(jnp.where((chunk_row >= lo) & (chunk_row < hi), seg_global[:, None, :] + chunk_row - lo, 0),
                        axis=2).reshape(-1).astype(jnp.int32)
    sorted_row = jnp.arange(n_tiles * MOE_CHUNKS, dtype=jnp.int32)[:, None] * SEG_ALIGN
    seg_lo = seg_global.reshape(1, -1)
    seg_src = (jnp.arange(n_tok_tiles, dtype=jnp.int32)[:, None] * SORT_ROWS + seg_local).reshape(1, -1)
    in_seg = (sorted_row >= seg_lo) & (sorted_row < seg_lo + seg_rows.reshape(1, -1))
    src_chunk = jnp.where(jnp.any(in_seg, axis=1),
                          jnp.sum(jnp.where(in_seg, seg_src + sorted_row - seg_lo, 0), axis=1),
                          SORT_ROWS - SEG_ALIGN).astype(jnp.int32)
    e_ids = jnp.arange(N_EXPERTS, dtype=jnp.int32)
    active = rows_e > 0
    later_active = jnp.where(active[None, :] & (e_ids[None, :] > e_ids[:, None]), e_ids[None, :], N_EXPERTS)
    next_e = jnp.min(later_active, axis=1)
    next_e = jnp.where(next_e == N_EXPERTS, -1, next_e).astype(jnp.int32)
    slot_e = ((jnp.cumsum(active.astype(jnp.int32)) - 1) % 2).astype(jnp.int32)

    ys = _moe_experts(tile_e, nused, src_chunk, next_e, slot_e, msort, w_up[i], w_down[i])
    out = _combine_ple(chunk_dst, ys, h1, route, p[i].reshape(T, PLE_DIM),
                       g_ple[i].reshape(1, -1), w_ple_gate[i], w_ple_proj[i])
    return out.reshape(B, S, D_MODEL)


def kernel(x, p, positions, g_mix, w_in, b_f, qn_a, kn_a, qn_b, kn_b, w_o, g_ffn, w_rg, b_rg, w_re, b_re,
           w_up, w_down, g_ple, w_ple_gate, w_ple_proj):
    cos_t, sin_t = _rope_tables(positions)
    h = x
    for i in range(p.shape[0]):
        h = _layer(i, h, p, cos_t, sin_t, g_mix, w_in, b_f, qn_a, kn_a, qn_b, kn_b, w_o, g_ffn, w_rg, b_rg,
                   w_re, b_re, w_up, w_down, g_ple, w_ple_gate, w_ple_proj)
    return h
```
